```python
import math
import jax
import jax.numpy as jnp
from jax import lax
import numpy as np

D_MODEL = 1024
BATCH = 16
SEQ = 2048
DEPTH = 2

CTX_LEN = 256
GRID_W = 64
N_MOD = 9
RMS_EPS = 1e-6

HEAD_DIM = 64
N_Q_HEADS = 8
N_KV_HEADS = 2
Q_PER_KV = N_Q_HEADS // N_KV_HEADS
ATTN_DIM = N_Q_HEADS * HEAD_DIM
KV_DIM = N_KV_HEADS * HEAD_DIM
WINDOW = 128
ATTN_BLOCK = 128
ROPE_THETA = 10000.0
ROPE_AXIS_DIM = HEAD_DIM // 2

SSD_HEADS = 16
SSD_HEAD_DIM = 64
D_SSM = SSD_HEADS * SSD_HEAD_DIM
SSD_GROUPS = 2
SSD_HEADS_PER_GROUP = SSD_HEADS // SSD_GROUPS
SSD_STATE = 128
SSD_CONV = 7
SSD_CHUNK = 128
BC_DIM = SSD_GROUPS * SSD_STATE
XBC_DIM = D_SSM + 4 * BC_DIM
MIX_IN = ATTN_DIM + 2 * KV_DIM + D_SSM + XBC_DIM + 2 * SSD_HEADS
MIX_OUT = ATTN_DIM + D_SSM

HYENA_DIM = D_MODEL
HYENA_SHORT = 3
HYENA_BANDS = 8
HYENA_EMB = 1 + 2 * HYENA_BANDS
HYENA_FILTER_WIDTH = 64
HYENA_INNER = 2
HYENA_FAST_DECAY = 0.3
HYENA_SLOW_DECAY = 1.5
HYENA_TARGET = 1e-2

D_FF = 2816

kernel_name = 'hybrid_swa_ssd_hyena_dit'


def rms_norm(x, g, eps=RMS_EPS):
    xf = x.astype(jnp.float32)
    y = xf * lax.rsqrt(jnp.mean(xf * xf, axis=-1, keepdims=True) + eps)
    return (y * g.astype(jnp.float32)).astype(x.dtype)


def swiglu(h, w13, w2):
    a, b = jnp.split(h @ w13, 2, axis=-1)
    return (jax.nn.silu(a) * b) @ w2


def depthwise_conv(x, w, b):
    width = w.shape[0]
    y = lax.conv_general_dilated(x, w[:, None, :].astype(x.dtype), window_strides=(1,),
                                 padding=[(width // 2, width // 2)],
                                 dimension_numbers=('NWC', 'WIO', 'NWC'),
                                 feature_group_count=x.shape[-1])
    return y + b.astype(x.dtype)


def ada_mods(cond, w, b):
    m = jax.nn.silu(cond) @ w + b
    return m.reshape(cond.shape[0], N_MOD, 1, D_MODEL)


def adaln_in(h, g, m, s):
    return rms_norm(h, g) * (1.0 + m[:, 3 * s + 1]) + m[:, 3 * s]


def macaron_ffn(h, g, m, s, w13, w2):
    return h + 0.5 * m[:, 3 * s + 2] * swiglu(adaln_in(h, g, m, s), w13, w2)


def axial_rope_tables(n_tok):
    rows = n_tok // GRID_W
    r, col = jnp.meshgrid(jnp.arange(rows, dtype=jnp.float32), jnp.arange(GRID_W, dtype=jnp.float32), indexing='ij')
    inv = ROPE_THETA ** (-jnp.arange(0, ROPE_AXIS_DIM, 2, dtype=jnp.float32) / ROPE_AXIS_DIM)
    ang_r = r.reshape(-1)[:, None] * inv
    ang_c = col.reshape(-1)[:, None] * inv
    return (jnp.cos(ang_r), jnp.sin(ang_r), jnp.cos(ang_c), jnp.sin(ang_c))


def _rotate_half(x, cos, sin):
    x1, x2 = jnp.split(x, 2, axis=-1)
    return jnp.concatenate([x1 * cos - x2 * sin, x1 * sin + x2 * cos], axis=-1)


def apply_axial_rope(x, tables):
    cr, sr, cc, sc = [t.reshape((1, t.shape[0]) + (1,) * (x.ndim - 3) + (t.shape[1],)).astype(x.dtype) for t in tables]
    xr, xc = jnp.split(x, 2, axis=-1)
    return jnp.concatenate([_rotate_half(xr, cr, sr), _rotate_half(xc, cc, sc)], axis=-1)


def sink_softmax(scores, sink):
    sk = jnp.broadcast_to(sink.astype(jnp.float32)[None, :, :, None, None], scores[0].shape[:-1] + (1,))
    probs = jax.nn.softmax(jnp.concatenate(list(scores) + [sk], axis=-1), axis=-1)
    cuts = np.cumsum([s.shape[-1] for s in scores])[:-1].tolist()
    return jnp.split(probs[..., :-1], cuts, axis=-1)


def context_attention(q, k, v, sink):
    s = jnp.einsum('bqhgd,bkhd->bhgqk', q, k).astype(jnp.float32) * HEAD_DIM ** -0.5
    (p,) = sink_softmax([s], sink)
    o = jnp.einsum('bhgqk,bkhd->bqhgd', p.astype(v.dtype), v)
    return o.reshape(q.shape[0], q.shape[1], ATTN_DIM)


def window_attention(q, k, v, k_ctx, v_ctx, sink):
    n, n_tok = q.shape[:2]
    nblk = n_tok // ATTN_BLOCK
    band = ATTN_BLOCK + 2 * WINDOW
    pad = ((0, 0), (WINDOW, WINDOW), (0, 0), (0, 0))
    kp, vp = jnp.pad(k, pad), jnp.pad(v, pad)
    qb = jnp.swapaxes(q.reshape(n, nblk, ATTN_BLOCK, N_KV_HEADS, Q_PER_KV, HEAD_DIM), 0, 1)
    scale = HEAD_DIM ** -0.5

    def one_block(args):
        qj, j = args
        start = j * ATTN_BLOCK
        kj = lax.dynamic_slice_in_dim(kp, start, band, axis=1)
        vj = lax.dynamic_slice_in_dim(vp, start, band, axis=1)
        qpos = start + jnp.arange(ATTN_BLOCK)
        kpos = start - WINDOW + jnp.arange(band)
        ok = (jnp.abs(qpos[:, None] - kpos[None, :]) <= WINDOW) & (kpos[None, :] >= 0) & (kpos[None, :] < n_tok)
        s_band = jnp.einsum('bqhgd,bkhd->bhgqk', qj, kj).astype(jnp.float32) * scale
        s_band = jnp.where(ok, s_band, -jnp.inf)
        s_ctx = jnp.einsum('bqhgd,bkhd->bhgqk', qj, k_ctx).astype(jnp.float32) * scale
        p_band, p_ctx = sink_softmax([s_band, s_ctx], sink)
        return (jnp.einsum('bhgqk,bkhd->bqhgd', p_band.astype(v.dtype), vj)
                + jnp.einsum('bhgqk,bkhd->bqhgd', p_ctx.astype(v.dtype), v_ctx))

    out = lax.map(one_block, (qb, jnp.arange(nblk)))
    return jnp.swapaxes(out, 0, 1).reshape(n, n_tok, ATTN_DIM)


def ssd_scan(X, a, Bm, Cm, h0):
    n, n_tok = X.shape[:2]
    nc = n_tok // SSD_CHUNK

    def chunks(t):
        return jnp.moveaxis(t.reshape((n, nc, SSD_CHUNK) + t.shape[2:]), 1, 0)

    tri = jnp.tril(jnp.ones((SSD_CHUNK, SSD_CHUNK), dtype=bool))[None, :, :, None, None]

    def step(h, inp):
        xc, ac, bc, cc = inp
        cs = jnp.cumsum(ac, axis=1)
        seg = cs[:, :, None] - cs[:, None, :]
        decay = jnp.exp(jnp.where(tri, seg, -jnp.inf))
        cb = jnp.einsum('blgn,bsgn->blsg', cc, bc)
        y = jnp.einsum('blsg,blsgr,bsgrp->blgrp', cb, decay, xc)
        y = y + jnp.einsum('blgn,bgrpn->blgrp', cc, h) * jnp.exp(cs)[..., None]
        tail = jnp.exp(cs[:, -1:] - cs)
        h = h * jnp.exp(cs[:, -1])[..., None, None] + jnp.einsum('bsgn,bsgr,bsgrp->bgrpn', bc, tail, xc)
        return h, y

    h, ys = lax.scan(step, h0, (chunks(X), chunks(a), chunks(Bm), chunks(Cm)))
    return jnp.moveaxis(ys, 0, 1).reshape(X.shape), h


def ssd_branch(z, xbc, dt_raw, conv_w, conv_b, dt_bias, a_log, d_skip, norm_w, h0_f, h0_b):
    n, n_tok = z.shape[:2]
    f32 = jnp.float32
    xbc = jax.nn.silu(depthwise_conv(xbc, conv_w, conv_b))
    xs, bf, bb, cf, cb = jnp.split(xbc, [D_SSM, D_SSM + BC_DIM, D_SSM + 2 * BC_DIM, D_SSM + 3 * BC_DIM], axis=-1)
    gshape = (n, n_tok, SSD_GROUPS, SSD_STATE)
    bf, bb, cf, cb = [t.reshape(gshape).astype(f32) for t in (bf, bb, cf, cb)]
    xh = xs.reshape(n, n_tok, SSD_GROUPS, SSD_HEADS_PER_GROUP, SSD_HEAD_DIM).astype(f32)
    dt = jax.nn.softplus(dt_raw.astype(f32).reshape(n, n_tok, 2, SSD_GROUPS, SSD_HEADS_PER_GROUP)
                         + dt_bias.astype(f32).reshape(2, SSD_GROUPS, SSD_HEADS_PER_GROUP))
    A = -jnp.exp(a_log.astype(f32)).reshape(2, SSD_GROUPS, SSD_HEADS_PER_GROUP)
    dt_f, dt_b = dt[:, :, 0], dt[:, :, 1]
    y_f, h_f = ssd_scan(xh * dt_f[..., None], dt_f * A[0], bf, cf, h0_f)
    flip = lambda t: jnp.flip(t, axis=1)
    y_b, h_b = ssd_scan(flip(xh * dt_b[..., None]), flip(dt_b * A[1]), flip(bb), flip(cb), h0_b)
    y = y_f + flip(y_b) + d_skip.astype(f32).reshape(SSD_GROUPS, SSD_HEADS_PER_GROUP)[:, :, None] * xh
    y = y.reshape(n, n_tok, D_SSM) * jax.nn.silu(z.astype(f32))
    yg = y.reshape(n, n_tok, SSD_GROUPS, D_SSM // SSD_GROUPS)
    yg = yg * lax.rsqrt(jnp.mean(yg * yg, axis=-1, keepdims=True) + RMS_EPS)
    return (yg.reshape(n, n_tok, D_SSM) * norm_w.astype(f32)).astype(z.dtype), h_f, h_b


def attn_ssd_mixer(u_lat, u_ctx, rope, w_in, w_out, q_g, k_g, sink, conv_w, conv_b, dt_bias, a_log, d_skip, norm_w):
    cuts = [ATTN_DIM, ATTN_DIM + KV_DIM, ATTN_DIM + 2 * KV_DIM, ATTN_DIM + 2 * KV_DIM + D_SSM,
            ATTN_DIM + 2 * KV_DIM + D_SSM + XBC_DIM]
    q_l, k_l, v_l, z_l, xbc_l, dt_l = jnp.split(u_lat @ w_in, cuts, axis=-1)
    q_c, k_c, v_c, z_c, xbc_c, dt_c = jnp.split(u_ctx @ w_in, cuts, axis=-1)

    def qkv_heads(q, k, v):
        n, n_tok = q.shape[:2]
        q = rms_norm(q.reshape(n, n_tok, N_KV_HEADS, Q_PER_KV, HEAD_DIM), q_g)
        k = rms_norm(k.reshape(n, n_tok, N_KV_HEADS, HEAD_DIM), k_g)
        return q, k, v.reshape(n, n_tok, N_KV_HEADS, HEAD_DIM)

    sink2 = sink.reshape(N_KV_HEADS, Q_PER_KV)
    qc, kc, vc = qkv_heads(q_c, k_c, v_c)
    ql, kl, vl = qkv_heads(q_l, k_l, v_l)
    ql, kl = apply_axial_rope(ql, rope), apply_axial_rope(kl, rope)
    a_ctx = context_attention(qc, kc, vc, sink2)
    a_lat = window_attention(ql, kl, vl, kc, vc, sink2)

    h0 = jnp.zeros((u_ctx.shape[0], SSD_GROUPS, SSD_HEADS_PER_GROUP, SSD_HEAD_DIM, SSD_STATE), jnp.float32)
    s_ctx, hf_ctx, hb_ctx = ssd_branch(z_c, xbc_c, dt_c, conv_w, conv_b, dt_bias, a_log, d_skip, norm_w, h0, h0)
    s_lat, _, _ = ssd_branch(z_l, xbc_l, dt_l, conv_w, conv_b, dt_bias, a_log, d_skip, norm_w, hf_ctx, hb_ctx)

    y_lat = jnp.concatenate([a_lat, s_lat], axis=-1) @ w_out
    y_ctx = jnp.concatenate([a_ctx, s_ctx], axis=-1) @ w_out
    return y_lat, y_ctx


def hyena_filter(n_tok, f_w1, f_b1, f_wh, f_bh, f_wout, freq):
    f32 = jnp.float32
    t = jnp.arange(n_tok, dtype=f32)
    t_norm = t / (n_tok - 1)
    bands = jnp.linspace(1e-4, HYENA_BANDS - 1, HYENA_BANDS, dtype=f32)
    ang = (2.0 * math.pi / n_tok) * t[:, None] * bands
    z = jnp.concatenate([t_norm[:, None], jnp.cos(ang), -jnp.sin(ang)], axis=-1)
    freq = freq.astype(f32)
    h = jnp.sin(freq * (z @ f_w1.astype(f32) + f_b1.astype(f32)))
    for n in range(HYENA_INNER):
        h = jnp.sin(freq * (h @ f_wh[n].astype(f32) + f_bh[n].astype(f32)))
    h = h @ f_wout.astype(f32)
    deltas = jnp.abs(jnp.linspace(math.log(HYENA_TARGET) / HYENA_SLOW_DECAY,
                                  math.log(HYENA_TARGET) / HYENA_FAST_DECAY, HYENA_DIM, dtype=f32))
    window = jnp.exp(-t_norm[:, None] * deltas)
    h_fwd = h[:, :HYENA_DIM] * window
    h_bwd = h[:, HYENA_DIM:] * window
    k = jnp.concatenate([h_fwd, jnp.zeros((1, HYENA_DIM), f32), jnp.flip(h_bwd[1:], axis=0)], axis=0)
    return k / jnp.sum(jnp.abs(k), axis=0, keepdims=True)


def hyena_mixer(u, w_in, conv_w, conv_b, f_w1, f_b1, f_wh, f_bh, f_wout, freq, bias, w_out):
    n_tok = u.shape[1]
    p = depthwise_conv(u @ w_in, conv_w, conv_b)
    x0, x1, v = jnp.split(p, 3, axis=-1)
    v = (v * x1).astype(jnp.float32)
    k = hyena_filter(n_tok, f_w1, f_b1, f_wh, f_bh, f_wout, freq)
    y = jnp.fft.irfft(jnp.fft.rfft(v, n=2 * n_tok, axis=1) * jnp.fft.rfft(k, axis=0)[None],
                      n=2 * n_tok, axis=1)[:, :n_tok]
    y = y + v * bias.astype(jnp.float32)
    return (y.astype(u.dtype) * x0) @ w_out


def setup_inputs(seed: int = 0) -> dict:
    key = jax.random.key(seed)
    k = jax.random.split(key, 32)
    f32 = jnp.float32

    def nrm(kk, shape, scale):
        return jax.random.normal(kk, shape, f32) * scale

    def gain(kk, shape):
        return 1.0 + 0.02 * jax.random.normal(kk, shape, f32)

    n_even = (DEPTH + 1) // 2
    n_odd = DEPTH // 2
    fw = HYENA_FILTER_WIDTH
    dt0 = jnp.exp(jax.random.uniform(k[16], (n_even, 2, SSD_HEADS), f32, math.log(1e-3), math.log(1e-1)))
    return {
        'x': nrm(k[0], (BATCH, SEQ, D_MODEL), 1.0),
        'c': nrm(k[1], (BATCH, D_MODEL), 1.0),
        'ctx': nrm(k[2], (BATCH, CTX_LEN, D_MODEL), 1.0),
        'c_ctx': nrm(k[3], (D_MODEL,), 1.0),
        'w_ada': nrm(k[4], (DEPTH, D_MODEL, N_MOD * D_MODEL), 0.5 * D_MODEL ** -0.5),
        'b_ada': nrm(k[5], (DEPTH, N_MOD * D_MODEL), 0.02),
        'norm_g': gain(k[6], (DEPTH, 3, D_MODEL)),
        'ffn_w13': nrm(k[7], (DEPTH, 2, D_MODEL, 2 * D_FF), D_MODEL ** -0.5),
        'ffn_w2': nrm(k[8], (DEPTH, 2, D_FF, D_MODEL), D_FF ** -0.5),
        'mix_w_in': nrm(k[9], (n_even, D_MODEL, MIX_IN), D_MODEL ** -0.5),
        'mix_w_out': nrm(k[10], (n_even, MIX_OUT, D_MODEL), MIX_OUT ** -0.5),
        'q_norm': gain(k[11], (n_even, HEAD_DIM)),
        'k_norm': gain(k[12], (n_even, HEAD_DIM)),
        'attn_sink': nrm(k[13], (n_even, N_Q_HEADS), 0.5),
        'ssd_conv_w': nrm(k[14], (n_even, SSD_CONV, XBC_DIM), SSD_CONV ** -0.5),
        'ssd_conv_b': nrm(k[15], (n_even, XBC_DIM), 0.02),
        'ssd_dt_bias': dt0 + jnp.log(-jnp.expm1(-dt0)),
        'ssd_a_log': jnp.log(jax.random.uniform(k[17], (n_even, 2, SSD_HEADS), f32, 1.0, 16.0)),
        'ssd_d': gain(k[18], (n_even, SSD_HEADS)),
        'ssd_norm': gain(k[19], (n_even, D_SSM)),
        'hy_w_in': nrm(k[20], (n_odd, D_MODEL, 3 * HYENA_DIM), D_MODEL ** -0.5),
        'hy_conv_w': nrm(k[21], (n_odd, HYENA_SHORT, 3 * HYENA_DIM), HYENA_SHORT ** -0.5),
        'hy_conv_b': nrm(k[22], (n_odd, 3 * HYENA_DIM), 0.02),
        'hy_f_w1': nrm(k[23], (n_odd, HYENA_EMB, fw), HYENA_EMB ** -0.5),
        'hy_f_b1': nrm(k[24], (n_odd, fw), 0.1),
        'hy_f_wh': nrm(k[25], (n_odd, HYENA_INNER, fw, fw), fw ** -0.5),
        'hy_f_bh': nrm(k[26], (n_odd, HYENA_INNER, fw), 0.1),
        'hy_f_wout': nrm(k[27], (n_odd, fw, 2 * HYENA_DIM), fw ** -0.5),
        'hy_freq': gain(k[28], (n_odd, fw)),
        'hy_bias': nrm(k[29], (n_odd, HYENA_DIM), 0.1),
        'hy_w_out': nrm(k[30], (n_odd, HYENA_DIM, D_MODEL), HYENA_DIM ** -0.5),
    }


def reference(x, c, ctx, c_ctx, w_ada, b_ada, norm_g, ffn_w13, ffn_w2, mix_w_in, mix_w_out, q_norm, k_norm,
              attn_sink, ssd_conv_w, ssd_conv_b, ssd_dt_bias, ssd_a_log, ssd_d, ssd_norm, hy_w_in, hy_conv_w,
              hy_conv_b, hy_f_w1, hy_f_b1, hy_f_wh, hy_f_bh, hy_f_wout, hy_freq, hy_bias, hy_w_out):
    rope = axial_rope_tables(x.shape[1])
    h_lat, h_ctx = x, ctx
    for i in range(DEPTH):
        last = i == DEPTH - 1
        even = i % 2 == 0
        j = i // 2
        ctx_live = even or not last
        m_lat = ada_mods(c, w_ada[i], b_ada[i])
        h_lat = macaron_ffn(h_lat, norm_g[i, 0], m_lat, 0, ffn_w13[i, 0], ffn_w2[i, 0])
        if ctx_live:
            m_ctx = ada_mods(c_ctx[None], w_ada[i], b_ada[i])
            h_ctx = macaron_ffn(h_ctx, norm_g[i, 0], m_ctx, 0, ffn_w13[i, 0], ffn_w2[i, 0])
        u_lat = adaln_in(h_lat, norm_g[i, 1], m_lat, 1)
        if even:
            u_ctx = adaln_in(h_ctx, norm_g[i, 1], m_ctx, 1)
            y_lat, y_ctx = attn_ssd_mixer(u_lat, u_ctx, rope, mix_w_in[j], mix_w_out[j], q_norm[j], k_norm[j],
                                          attn_sink[j], ssd_conv_w[j], ssd_conv_b[j], ssd_dt_bias[j],
                                          ssd_a_log[j], ssd_d[j], ssd_norm[j])
        else:
            hy = (hy_w_in[j], hy_conv_w[j], hy_conv_b[j], hy_f_w1[j], hy_f_b1[j], hy_f_wh[j], hy_f_bh[j],
                  hy_f_wout[j], hy_freq[j], hy_bias[j], hy_w_out[j])
            y_lat = hyena_mixer(u_lat, *hy)
            if not last:
                y_ctx = hyena_mixer(adaln_in(h_ctx, norm_g[i, 1], m_ctx, 1), *hy)
        h_lat = h_lat + m_lat[:, 5] * y_lat
        h_lat = macaron_ffn(h_lat, norm_g[i, 2], m_lat, 2, ffn_w13[i, 1], ffn_w2[i, 1])
        if not last:
            h_ctx = h_ctx + m_ctx[:, 5] * y_ctx
            h_ctx = macaron_ffn(h_ctx, norm_g[i, 2], m_ctx, 2, ffn_w13[i, 1], ffn_w2[i, 1])
    return h_lat
```

```python
import functools
import math

import numpy as np
import jax
import jax.numpy as jnp
from jax import lax
from jax.experimental import pallas as pl
from jax.experimental.pallas import tpu as pltpu

f32 = jnp.float32
bf16 = jnp.bfloat16

D_MODEL = 1024
N_MOD = 9
RMS_EPS = 1e-6
GRID_W = 64

HEAD_DIM = 64
N_Q_HEADS = 8
N_KV_HEADS = 2
ATTN_DIM = N_Q_HEADS * HEAD_DIM
KV_DIM = N_KV_HEADS * HEAD_DIM
WINDOW = 128
ATTN_BLOCK = 128
ROPE_THETA = 10000.0

SSD_HEADS = 16
SSD_HEAD_DIM = 64
D_SSM = SSD_HEADS * SSD_HEAD_DIM
SSD_GROUPS = 2
SSD_STATE = 128
SSD_CONV = 7
SSD_CHUNK = 128
BC_DIM = SSD_GROUPS * SSD_STATE
XBC_DIM = D_SSM + 4 * BC_DIM
GROUP_W = D_SSM // SSD_GROUPS

HYENA_SHORT = 3
HYENA_BANDS = 8
HYENA_FILTER_WIDTH = 64
HYENA_INNER = 2
HYENA_FAST_DECAY = 0.3
HYENA_SLOW_DECAY = 1.5
HYENA_TARGET = 1e-2

D_FF = 2816
LANES = 128

COL_XBC = 0
COL_Z = XBC_DIM
COL_Q = COL_Z + D_SSM
COL_K = COL_Q + ATTN_DIM
COL_V = COL_K + KV_DIM
COL_DT = COL_V + KV_DIM
MIX_COLS = 4096

VMEM_LIMIT = 56 * 1024 * 1024


def _params(*sem):
    return pltpu.CompilerParams(dimension_semantics=sem, vmem_limit_bytes=VMEM_LIMIT)


def _dot(a, b):
    return jnp.dot(a, b, preferred_element_type=f32)


def _dot_nt(a, b):
    return lax.dot_general(a, b, (((1,), (1,)), ((), ())), preferred_element_type=f32)


def _split2(x):
    hi = x.astype(bf16)
    lo = (x - hi.astype(f32)).astype(bf16)
    return hi, lo


def _split3(x):
    hi = x.astype(bf16)
    r = x - hi.astype(f32)
    mid = r.astype(bf16)
    lo = (r - mid.astype(f32)).astype(bf16)
    return hi, mid, lo


def _adaln(h, g, shift, scale):
    ms = jnp.mean(h * h, axis=-1, keepdims=True)
    return (h * lax.rsqrt(ms + RMS_EPS) * g) * (1.0 + scale) + shift


def _silu(x):
    return x * jax.nn.sigmoid(x)


def _mods_kernel(c_ref, w_ref, b_ref, o_ref):
    o_ref[...] = _dot(_silu(c_ref[...]).astype(bf16), w_ref[...].astype(bf16)) + b_ref[...]


def ada_mods(cond, w, b):
    r = cond.shape[0]
    tn = 1024
    out = pl.pallas_call(
        _mods_kernel,
        grid=(w.shape[1] // tn,),
        in_specs=[pl.BlockSpec((r, D_MODEL), lambda j: (0, 0)),
                  pl.BlockSpec((D_MODEL, tn), lambda j: (0, j)),
                  pl.BlockSpec((1, tn), lambda j: (0, j))],
        out_specs=pl.BlockSpec((r, tn), lambda j: (0, j)),
        out_shape=jax.ShapeDtypeStruct((r, w.shape[1]), f32),
        compiler_params=_params("parallel"),
        name="ada_mods",
    )(cond, w, b.reshape(1, -1))
    return out.reshape(r, N_MOD, D_MODEL)


def _ffn_kernel(s, nf, h_ref, mod_ref, g_ref, wa_ref, wb_ref, w2_ref, o_ref, u_ref, acc_ref):
    j = pl.program_id(1)

    @pl.when(j == 0)
    def _():
        u = _adaln(h_ref[...], g_ref[...], mod_ref[0, 3 * s:3 * s + 1, :], mod_ref[0, 3 * s + 1:3 * s + 2, :])
        u_ref[...] = u.astype(bf16)
        acc_ref[...] = jnp.zeros_like(acc_ref)

    u = u_ref[...]
    a = _dot(u, wa_ref[...])
    b = _dot(u, wb_ref[...])
    acc_ref[...] += _dot((_silu(a) * b).astype(bf16), w2_ref[...])

    @pl.when(j == nf - 1)
    def _():
        o_ref[...] = h_ref[...] + 0.5 * mod_ref[0, 3 * s + 2:3 * s + 3, :] * acc_ref[...]


def macaron_ffn(h, mods, rows_per_mod, g, s, w13, w2, tm=512, tf=256):
    m = h.shape[0]
    tm = min(tm, rows_per_mod)
    nf = D_FF // tf
    tiles_per_mod = rows_per_mod // tm
    return pl.pallas_call(
        functools.partial(_ffn_kernel, s, nf),
        grid=(m // tm, nf),
        in_specs=[pl.BlockSpec((tm, D_MODEL), lambda i, j: (i, 0)),
                  pl.BlockSpec((1, N_MOD, D_MODEL), lambda i, j: (i // tiles_per_mod, 0, 0)),
                  pl.BlockSpec((1, D_MODEL), lambda i, j: (0, 0)),
                  pl.BlockSpec((D_MODEL, tf), lambda i, j: (0, j)),
                  pl.BlockSpec((D_MODEL, tf), lambda i, j: (0, j + nf)),
                  pl.BlockSpec((tf, D_MODEL), lambda i, j: (j, 0))],
        out_specs=pl.BlockSpec((tm, D_MODEL), lambda i, j: (i, 0)),
        out_shape=jax.ShapeDtypeStruct((m, D_MODEL), f32),
        scratch_shapes=[pltpu.VMEM((tm, D_MODEL), bf16), pltpu.VMEM((tm, D_MODEL), f32)],
        compiler_params=_params("parallel", "arbitrary"),
        name="macaron_ffn",
    )(h, mods, g.reshape(1, -1), w13, w13, w2)


def _inproj_kernel(s, h_ref, mod_ref, g_ref, w_ref, o_ref, u_ref):
    @pl.when(pl.program_id(1) == 0)
    def _():
        u = _adaln(h_ref[...], g_ref[...], mod_ref[0, 3 * s:3 * s + 1, :], mod_ref[0, 3 * s + 1:3 * s + 2, :])
        u_ref[...] = u.astype(bf16)

    o_ref[...] = _dot(u_ref[...], w_ref[...])


def adaln_proj(h, mods, rows_per_mod, g, s, w, tm=512, tn=512):
    m = h.shape[0]
    n = w.shape[1]
    tm = min(tm, rows_per_mod)
    tiles_per_mod = rows_per_mod // tm
    return pl.pallas_call(
        functools.partial(_inproj_kernel, s),
        grid=(m // tm, n // tn),
        in_specs=[pl.BlockSpec((tm, D_MODEL), lambda i, j: (i, 0)),
                  pl.BlockSpec((1, N_MOD, D_MODEL), lambda i, j: (i // tiles_per_mod, 0, 0)),
                  pl.BlockSpec((1, D_MODEL), lambda i, j: (0, 0)),
                  pl.BlockSpec((D_MODEL, tn), lambda i, j: (0, j))],
        out_specs=pl.BlockSpec((tm, tn), lambda i, j: (i, j)),
        out_shape=jax.ShapeDtypeStruct((m, n), f32),
        scratch_shapes=[pltpu.VMEM((tm, D_MODEL), bf16)],
        compiler_params=_params("parallel", "arbitrary"),
        name="adaln_proj",
    )(h, mods, g.reshape(1, -1), w)


def _outproj_kernel(n_in, gate_row, *refs):
    x_refs = refs[:n_in]
    w_refs = refs[n_in:2 * n_in]
    h_ref, mod_ref, o_ref = refs[2 * n_in:]
    acc = _dot(x_refs[0][...].astype(bf16), w_refs[0][...])
    for x_ref, w_ref in zip(x_refs[1:], w_refs[1:]):
        acc += _dot(x_ref[...].astype(bf16), w_ref[...])
    o_ref[...] = h_ref[...] + mod_ref[0, gate_row:gate_row + 1, :] * acc


def out_proj_residual(xs, ws, h, mods, rows_per_mod, gate_row, tm=512):
    m = h.shape[0]
    tm = min(tm, rows_per_mod)
    tiles_per_mod = rows_per_mod // tm
    n_in = len(xs)
    in_specs = ([pl.BlockSpec((tm, x.shape[1]), lambda i: (i, 0)) for x in xs]
                + [pl.BlockSpec(w.shape, lambda i: (0, 0)) for w in ws]
                + [pl.BlockSpec((tm, D_MODEL), lambda i: (i, 0)),
                   pl.BlockSpec((1, N_MOD, D_MODEL), lambda i: (i // tiles_per_mod, 0, 0))])
    return pl.pallas_call(
        functools.partial(_outproj_kernel, n_in, gate_row),
        grid=(m // tm,),
        in_specs=in_specs,
        out_specs=pl.BlockSpec((tm, D_MODEL), lambda i: (i, 0)),
        out_shape=jax.ShapeDtypeStruct((m, D_MODEL), f32),
        compiler_params=_params("parallel"),
        name="out_proj_residual",
    )(*xs, *ws, h, mods)


def _shifted_taps(pad_ref, x, seq, width):
    c = x.shape[1]
    pad_ref[0:8, :] = jnp.zeros((8, c), f32)
    pad_ref[seq + 8:seq + 16, :] = jnp.zeros((8, c), f32)
    pad_ref[8:seq + 8, :] = x
    half = width // 2

    def tap(row0, rows, k):
        return pad_ref[8 + row0 + k - half:8 + row0 + k - half + rows, :]

    return tap


def _dwconv(pad_ref, x, w, b, seq, width, row_tile, emit):
    tap = _shifted_taps(pad_ref, x, seq, width)
    for r0 in range(0, seq, row_tile):
        acc = b + w[0:1, :] * tap(r0, row_tile, 0)
        for k in range(1, width):
            acc += w[k:k + 1, :] * tap(r0, row_tile, k)
        emit(r0, acc)


def _ssd_conv_kernel(seq, x_ref, w_ref, b_ref, o_ref, pad_ref):
    row_tile = min(seq, 256)

    def emit(r0, acc):
        o_ref[r0:r0 + row_tile, :] = _silu(acc)

    _dwconv(pad_ref, x_ref[...], w_ref[...], b_ref[...], seq, SSD_CONV, row_tile, emit)


def ssd_conv(proj, seq, conv_w, conv_b, tc=256):
    nb = proj.shape[0] // seq
    return pl.pallas_call(
        functools.partial(_ssd_conv_kernel, seq),
        grid=(nb, XBC_DIM // tc),
        in_specs=[pl.BlockSpec((seq, tc), lambda b, j: (b, j)),
                  pl.BlockSpec((SSD_CONV, tc), lambda b, j: (0, j)),
                  pl.BlockSpec((1, tc), lambda b, j: (0, j))],
        out_specs=pl.BlockSpec((seq, tc), lambda b, j: (b, j)),
        out_shape=jax.ShapeDtypeStruct((nb * seq, XBC_DIM), f32),
        scratch_shapes=[pltpu.VMEM((seq + 16, tc), f32)],
        compiler_params=_params("parallel", "parallel"),
        name="ssd_conv",
    )(proj, conv_w, conv_b.reshape(1, -1))


def _hyena_conv_kernel(seq, x0_ref, x1_ref, v_ref, w0_ref, w1_ref, wv_ref, b0_ref, b1_ref, bv_ref,
                       x0_out, vg_out, vgh_out, pad_ref, tmp_ref):
    row_tile = min(seq, 256)

    def emit_x0(r0, acc):
        x0_out[r0:r0 + row_tile, :] = acc

    def emit_x1(r0, acc):
        tmp_ref[r0:r0 + row_tile, :] = acc

    def emit_v(r0, acc):
        vg = acc * tmp_ref[r0:r0 + row_tile, :]
        vg_out[r0:r0 + row_tile, :] = vg
        vgh_out[r0:r0 + row_tile, :] = vg.astype(bf16)

    _dwconv(pad_ref, x0_ref[...], w0_ref[...], b0_ref[...], seq, HYENA_SHORT, row_tile, emit_x0)
    _dwconv(pad_ref, x1_ref[...], w1_ref[...], b1_ref[...], seq, HYENA_SHORT, row_tile, emit_x1)
    _dwconv(pad_ref, v_ref[...], wv_ref[...], bv_ref[...], seq, HYENA_SHORT, row_tile, emit_v)


def hyena_conv(proj, seq, conv_w, conv_b, tc=256):
    nb = proj.shape[0] // seq
    nt = D_MODEL // tc
    b2 = conv_b.reshape(1, -1)
    act = lambda off: pl.BlockSpec((seq, tc), lambda b, j: (b, j + off * nt))
    wsp = lambda off: pl.BlockSpec((HYENA_SHORT, tc), lambda b, j: (0, j + off * nt))
    bsp = lambda off: pl.BlockSpec((1, tc), lambda b, j: (0, j + off * nt))
    osp = pl.BlockSpec((seq, tc), lambda b, j: (b, j))
    return pl.pallas_call(
        functools.partial(_hyena_conv_kernel, seq),
        grid=(nb, nt),
        in_specs=[act(0), act(1), act(2), wsp(0), wsp(1), wsp(2), bsp(0), bsp(1), bsp(2)],
        out_specs=[osp, osp, osp],
        out_shape=[jax.ShapeDtypeStruct((nb * seq, D_MODEL), f32),
                   jax.ShapeDtypeStruct((nb * seq, D_MODEL), f32),
                   jax.ShapeDtypeStruct((nb * seq, D_MODEL), bf16)],
        scratch_shapes=[pltpu.VMEM((seq + 16, tc), f32), pltpu.VMEM((seq, tc), f32)],
        compiler_params=_params("parallel", "parallel"),
        name="hyena_conv",
    )(proj, proj, proj, conv_w, conv_w, conv_w, b2, b2, b2)


def _head_norm(x, gain, bd):
    hi, lo = _split2(x * x)
    ms = _dot(hi, bd) + _dot(lo, bd)
    return x * lax.rsqrt(ms + RMS_EPS) * gain


def _rope(x, cos, sin_signed):
    lane = lax.broadcasted_iota(jnp.int32, x.shape, 1)
    partner = jnp.where((lane & 16) != 0, pltpu.roll(x, 16, 1), pltpu.roll(x, LANES - 16, 1))
    return x * cos + partner * sin_signed


def _attn_kernel(seq, q_ref, k_ref, v_ref, kc_ref, vc_ref, qg_ref, kg_ref, cos_ref, sin_ref, bd_ref,
                 sink_ref, o_ref, q_s, k_s, v_s, kc_s, vc_s):
    j = pl.program_id(1)
    nblk = seq // ATTN_BLOCK
    ctx_len = kc_ref.shape[0]

    @pl.when(j == 0)
    def _():
        bd = bd_ref[...]
        cos, sin = cos_ref[...], sin_ref[...]
        scale = HEAD_DIM ** -0.5
        for p in range(ATTN_DIM // LANES):
            qn = _head_norm(q_ref[:, p * LANES:(p + 1) * LANES], qg_ref[...], bd)
            q_s[:, p * LANES:(p + 1) * LANES] = (_rope(qn, cos, sin) * scale).astype(bf16)
        kn = _head_norm(k_ref[...], kg_ref[...], bd)
        zeros = jnp.zeros((WINDOW, KV_DIM), bf16)
        k_s[0:WINDOW, :] = zeros
        k_s[WINDOW + seq:2 * WINDOW + seq, :] = zeros
        v_s[0:WINDOW, :] = zeros
        v_s[WINDOW + seq:2 * WINDOW + seq, :] = zeros
        k_s[WINDOW:WINDOW + seq, :] = _rope(kn, cos, sin).astype(bf16)
        v_s[WINDOW:WINDOW + seq, :] = v_ref[...].astype(bf16)
        kc_s[...] = _head_norm(kc_ref[...], kg_ref[...], bd).astype(bf16)
        vc_s[...] = vc_ref[...].astype(bf16)

    band = ATTN_BLOCK + 2 * WINDOW
    start = pl.multiple_of(j * ATTN_BLOCK, ATTN_BLOCK)
    kb = k_s[pl.ds(start, band), :]
    vb = v_s[pl.ds(start, band), :]
    kc = kc_s[...]
    vc = vc_s[...]
    row = lax.broadcasted_iota(jnp.int32, (ATTN_BLOCK, band), 0)
    col = lax.broadcasted_iota(jnp.int32, (ATTN_BLOCK, band), 1)
    kpos = start - WINDOW + col
    ok = (jnp.abs(row - (col - WINDOW)) <= WINDOW) & (kpos >= 0) & (kpos < seq)
    lane = lax.broadcasted_iota(jnp.int32, (ATTN_BLOCK, LANES), 1)
    qblk = q_s[pl.ds(start, ATTN_BLOCK), :]
    for p in range(ATTN_DIM // LANES):
        qp = qblk[:, p * LANES:(p + 1) * LANES]
        halves = []
        for hh in range(N_KV_HEADS):
            sink = sink_ref[p + (N_Q_HEADS // N_KV_HEADS) * hh]
            mine = (lane >= hh * HEAD_DIM) & (lane < (hh + 1) * HEAD_DIM)
            qe = jnp.where(mine, qp, jnp.zeros_like(qp))
            sb = jnp.where(ok, _dot_nt(qe, kb), -jnp.inf)
            sc = _dot_nt(qe, kc)
            mx = jnp.maximum(jnp.maximum(jnp.max(sb, axis=-1, keepdims=True),
                                         jnp.max(sc, axis=-1, keepdims=True)), sink)
            pb = jnp.exp(sb - mx)
            pc = jnp.exp(sc - mx)
            den = jnp.sum(pb, axis=-1, keepdims=True) + jnp.sum(pc, axis=-1, keepdims=True) + jnp.exp(sink - mx)
            halves.append((_dot(pb.astype(bf16), vb) + _dot(pc.astype(bf16), vc)) / den)
        o_ref[:, p * LANES:(p + 1) * LANES] = jnp.where(lane < HEAD_DIM, halves[0], halves[1]).astype(bf16)


def window_attention(proj_lat, proj_ctx, seq, ctx_len, q_gain, k_gain, sink, rope_cos, rope_sin):
    nb = proj_lat.shape[0] // seq
    nblk = seq // ATTN_BLOCK
    bd = np.kron(np.eye(LANES // HEAD_DIM), np.ones((HEAD_DIM, HEAD_DIM))) / HEAD_DIM
    gain2 = lambda g: jnp.tile(g, LANES // HEAD_DIM).reshape(1, LANES)
    const = lambda shape: pl.BlockSpec(shape, lambda b, j: (0, 0))
    return pl.pallas_call(
        functools.partial(_attn_kernel, seq),
        grid=(nb, nblk),
        in_specs=[pl.BlockSpec((seq, ATTN_DIM), lambda b, j: (b, COL_Q // ATTN_DIM)),
                  pl.BlockSpec((seq, KV_DIM), lambda b, j: (b, COL_K // KV_DIM)),
                  pl.BlockSpec((seq, KV_DIM), lambda b, j: (b, COL_V // KV_DIM)),
                  pl.BlockSpec((ctx_len, KV_DIM), lambda b, j: (b, COL_K // KV_DIM)),
                  pl.BlockSpec((ctx_len, KV_DIM), lambda b, j: (b, COL_V // KV_DIM)),
                  const((1, LANES)), const((1, LANES)),
                  const((seq, LANES)), const((seq, LANES)), const((LANES, LANES)),
                  pl.BlockSpec(memory_space=pltpu.SMEM)],
        out_specs=pl.BlockSpec((ATTN_BLOCK, ATTN_DIM), lambda b, j: (b * nblk + j, 0)),
        out_shape=jax.ShapeDtypeStruct((nb * seq, ATTN_DIM), bf16),
        scratch_shapes=[pltpu.VMEM((seq, ATTN_DIM), bf16),
                        pltpu.VMEM((seq + 2 * WINDOW, KV_DIM), bf16),
                        pltpu.VMEM((seq + 2 * WINDOW, KV_DIM), bf16),
                        pltpu.VMEM((ctx_len, KV_DIM), bf16),
                        pltpu.VMEM((ctx_len, KV_DIM), bf16)],
        compiler_params=_params("parallel", "arbitrary"),
        name="window_attention",
    )(proj_lat, proj_lat, proj_lat, proj_ctx, proj_ctx, gain2(q_gain), gain2(k_gain),
      rope_cos, rope_sin, jnp.asarray(bd, bf16), sink)


def _rope_tables(seq):
    t = np.arange(seq)
    pos = np.stack([t // GRID_W, t % GRID_W], axis=1).astype(np.float32)
    axis_dim = HEAD_DIM // 2
    inv = (ROPE_THETA ** (-np.arange(0, axis_dim, 2, dtype=np.float32) / axis_dim)).astype(np.float32)
    lane = np.arange(LANES)
    d = lane % HEAD_DIM
    which = d // axis_dim
    ang = (pos[:, which] * inv[d % (axis_dim // 2)][None, :]).astype(np.float32)
    sign = np.where((d % axis_dim) < axis_dim // 2, -1.0, 1.0)
    return jnp.asarray(np.cos(ang), f32), jnp.asarray(np.sin(ang) * sign, f32)


def _softplus(x):
    return jnp.maximum(x, 0.0) + jnp.log1p(jnp.exp(-jnp.abs(x)))


def _expand_heads(v, e):
    hi, lo = _split2(v)
    return _dot(hi, e) + _dot(lo, e)


def _ssd_chunk(rev, lane0, want_y, x, bm, cm, dt_raw, dt_bias, a_neg, expand, state_ref):
    t = x.shape[0]
    dt = _softplus(dt_raw + dt_bias)
    a = dt * a_neg
    r = lax.broadcasted_iota(jnp.int32, (t, t), 0)
    c = lax.broadcasted_iota(jnp.int32, (t, t), 1)
    keep = (r <= c) if rev else (r >= c)
    tri = jnp.where(keep, 1.0, 0.0).astype(bf16)
    cs = sum(_dot(tri, part) for part in _split3(a))
    last = cs[0:1, :] if rev else cs[t - 1:t, :]
    e = jnp.exp(cs)
    w = dt * jnp.exp(last - cs)
    e_x = _expand_heads(e, expand)
    w_x = _expand_heads(w, expand)
    elast_x = e_x[0:1, :] if rev else e_x[t - 1:t, :]

    y = None
    if want_y:
        cs_t = cs.T
        dt_t = dt.T
        lane = lax.broadcasted_iota(jnp.int32, (t, LANES), 1)
        cb = [_dot_nt(cm[:, g * SSD_STATE:(g + 1) * SSD_STATE].astype(bf16),
                      bm[:, g * SSD_STATE:(g + 1) * SSD_STATE].astype(bf16)) for g in range(SSD_GROUPS)]
        pieces = []
        for p in range(SSD_HEADS // 2):
            xp = x[:, p * LANES:(p + 1) * LANES].astype(bf16)
            ms = []
            for q in range(2):
                h = 2 * p + q
                g = h // (SSD_HEADS // SSD_GROUPS)
                seg = cs[:, lane0 + h:lane0 + h + 1] - cs_t[lane0 + h:lane0 + h + 1, :]
                dec = jnp.exp(jnp.where(keep, seg, -jnp.inf))
                ms.append((cb[g] * dec * dt_t[lane0 + h:lane0 + h + 1, :]).astype(bf16))
            zero = jnp.zeros_like(xp)
            xcat = jnp.concatenate([jnp.where(lane < SSD_HEAD_DIM, xp, zero),
                                    jnp.where(lane >= SSD_HEAD_DIM, xp, zero)], axis=0)
            pieces.append(_dot(jnp.concatenate(ms, axis=1), xcat))
        y = jnp.concatenate(pieces, axis=1)

    inter = []
    for g in range(SSD_GROUPS):
        gs = slice(g * GROUP_W, (g + 1) * GROUP_W)
        ss = slice(g * SSD_STATE, (g + 1) * SSD_STATE)
        h_t = state_ref[g]
        if want_y:
            inter.append(_dot(cm[:, ss].astype(bf16), h_t.astype(bf16)) * e_x[:, gs])
        xw = (x[:, gs] * w_x[:, gs]).astype(bf16)
        state_ref[g] = h_t * elast_x[:, gs] + _dot(bm[:, ss].T.astype(bf16), xw)
    if want_y:
        y = y + jnp.concatenate(inter, axis=1)
    return y


def _ssd_ctx_kernel(nchunk, x_ref, bc_ref, dt_ref, bias_ref, alog_ref, ef_ref, eb_ref, hf_ref, hb_ref, sf, sb):
    sf[...] = jnp.zeros_like(sf)
    sb[...] = jnp.zeros_like(sb)
    a_neg = -jnp.exp(alog_ref[...])
    bias = bias_ref[...]
    t = SSD_CHUNK
    for ci in range(nchunk):
        rows = slice(ci * t, (ci + 1) * t)
        _ssd_chunk(False, 0, False, x_ref[rows, :], bc_ref[rows, 0:BC_DIM], None, dt_ref[rows, :],
                   bias, a_neg, ef_ref[...], sf)
        rows = slice((nchunk - 1 - ci) * t, (nchunk - ci) * t)
        _ssd_chunk(True, SSD_HEADS, False, x_ref[rows, :], bc_ref[rows, BC_DIM:2 * BC_DIM], None, dt_ref[rows, :],
                   bias, a_neg, eb_ref[...], sb)
    hf_ref[0] = sf[...]
    hb_ref[0] = sb[...]


def _head_expanders():
    ef = np.zeros((LANES, D_SSM), np.float32)
    eb = np.zeros((LANES, D_SSM), np.float32)
    for h in range(SSD_HEADS):
        ef[h, h * SSD_HEAD_DIM:(h + 1) * SSD_HEAD_DIM] = 1.0
        eb[SSD_HEADS + h, h * SSD_HEAD_DIM:(h + 1) * SSD_HEAD_DIM] = 1.0
    return jnp.asarray(ef, bf16), jnp.asarray(eb, bf16)


def _pad_lanes(v):
    v = v.reshape(1, -1).astype(f32)
    return jnp.pad(v, ((0, 0), (0, LANES - v.shape[1])))


def ssd_ctx_states(xc, proj, seq, dt_bias, a_log):
    nb = xc.shape[0] // seq
    ef, eb = _head_expanders()
    const = lambda shape: pl.BlockSpec(shape, lambda b: (0,) * len(shape))
    st = jax.ShapeDtypeStruct((nb, SSD_GROUPS, SSD_STATE, GROUP_W), f32)
    st_spec = pl.BlockSpec((1, SSD_GROUPS, SSD_STATE, GROUP_W), lambda b: (b, 0, 0, 0))
    return pl.pallas_call(
        functools.partial(_ssd_ctx_kernel, seq // SSD_CHUNK),
        grid=(nb,),
        in_specs=[pl.BlockSpec((seq, D_SSM), lambda b: (b, 0)),
                  pl.BlockSpec((seq, 2 * BC_DIM), lambda b: (b, D_SSM // (2 * BC_DIM))),
                  pl.BlockSpec((seq, LANES), lambda b: (b, COL_DT // LANES)),
                  const((1, LANES)), const((1, LANES)), const((LANES, D_SSM)), const((LANES, D_SSM))],
        out_specs=[st_spec, st_spec],
        out_shape=[st, st],
        scratch_shapes=[pltpu.VMEM((SSD_GROUPS, SSD_STATE, GROUP_W), f32),
                        pltpu.VMEM((SSD_GROUPS, SSD_STATE, GROUP_W), f32)],
        compiler_params=_params("parallel"),
        name="ssd_ctx_states",
    )(xc, xc, proj, _pad_lanes(dt_bias), _pad_lanes(a_log), ef, eb)


def _ssd_lat_kernel(nchunk, xf_ref, xb_ref, bf_ref, bb_ref, cf_ref, cb_ref, dtf_ref, dtb_ref, zf_ref, zb_ref,
                    hf0_ref, hb0_ref, bias_ref, alog_ref, dskip_ref, normw_ref, ef_ref, eb_ref,
                    o_ref, sf, sb, yacc):
    c = pl.program_id(1)
    t = SSD_CHUNK

    @pl.when(c == 0)
    def _():
        sf[...] = hf0_ref[0]
        sb[...] = hb0_ref[0]

    a_neg = -jnp.exp(alog_ref[...])
    bias = bias_ref[...]
    xf = xf_ref[...]
    yf = _ssd_chunk(False, 0, True, xf, bf_ref[...], cf_ref[...], dtf_ref[...], bias, a_neg, ef_ref[...], sf)
    yf = yf + dskip_ref[...] * xf
    yb = _ssd_chunk(True, SSD_HEADS, True, xb_ref[...], bb_ref[...], cb_ref[...], dtb_ref[...], bias, a_neg,
                    eb_ref[...], sb)
    rows_f = pl.ds(pl.multiple_of(c * t, t), t)
    rows_b = pl.ds(pl.multiple_of((nchunk - 1 - c) * t, t), t)

    @pl.when(c < nchunk // 2)
    def _():
        yacc[rows_f, :] = yf
        yacc[rows_b, :] = yb

    def finish(y, z):
        y = y * _silu(z)
        outs = []
        for g in range(SSD_GROUPS):
            yg = y[:, g * GROUP_W:(g + 1) * GROUP_W]
            outs.append(yg * lax.rsqrt(jnp.mean(yg * yg, axis=-1, keepdims=True) + RMS_EPS))
        return (jnp.concatenate(outs, axis=1) * normw_ref[...]).astype(bf16)

    @pl.when(c >= nchunk // 2)
    def _():
        o_ref[rows_f, :] = finish(yacc[rows_f, :] + yf, zf_ref[...])
        o_ref[rows_b, :] = finish(yacc[rows_b, :] + yb, zb_ref[...])


def ssd_latent(xc, proj, seq, hf0, hb0, dt_bias, a_log, d_skip, norm_w):
    nb = xc.shape[0] // seq
    nc = seq // SSD_CHUNK
    half = nc // 2
    ef, eb = _head_expanders()
    t = SSD_CHUNK
    fwd = lambda b, c: b * nc + c
    bwd = lambda b, c: b * nc + nc - 1 - c
    zfw = lambda b, c: b * nc + jnp.maximum(c, half)
    zbw = lambda b, c: b * nc + jnp.minimum(nc - 1 - c, half - 1)
    bc0 = D_SSM // SSD_STATE // SSD_GROUPS
    const = lambda shape: pl.BlockSpec(shape, lambda b, c: (0,) * len(shape))
    st_spec = pl.BlockSpec((1, SSD_GROUPS, SSD_STATE, GROUP_W), lambda b, c: (b, 0, 0, 0))
    dskip = jnp.repeat(d_skip.astype(f32), SSD_HEAD_DIM).reshape(1, D_SSM)
    return pl.pallas_call(
        functools.partial(_ssd_lat_kernel, nc),
        grid=(nb, nc),
        in_specs=[pl.BlockSpec((t, D_SSM), lambda b, c: (fwd(b, c), 0)),
                  pl.BlockSpec((t, D_SSM), lambda b, c: (bwd(b, c), 0)),
                  pl.BlockSpec((t, BC_DIM), lambda b, c: (fwd(b, c), bc0)),
                  pl.BlockSpec((t, BC_DIM), lambda b, c: (bwd(b, c), bc0 + 1)),
                  pl.BlockSpec((t, BC_DIM), lambda b, c: (fwd(b, c), bc0 + 2)),
                  pl.BlockSpec((t, BC_DIM), lambda b, c: (bwd(b, c), bc0 + 3)),
                  pl.BlockSpec((t, LANES), lambda b, c: (fwd(b, c), COL_DT // LANES)),
                  pl.BlockSpec((t, LANES), lambda b, c: (bwd(b, c), COL_DT // LANES)),
                  pl.BlockSpec((t, D_SSM), lambda b, c: (zfw(b, c), COL_Z // D_SSM)),
                  pl.BlockSpec((t, D_SSM), lambda b, c: (zbw(b, c), COL_Z // D_SSM)),
                  st_spec, st_spec,
                  const((1, LANES)), const((1, LANES)), const((1, D_SSM)), const((1, D_SSM)),
                  const((LANES, D_SSM)), const((LANES, D_SSM))],
        out_specs=pl.BlockSpec((seq, D_SSM), lambda b, c: (b, 0)),
        out_shape=jax.ShapeDtypeStruct((nb * seq, D_SSM), bf16),
        scratch_shapes=[pltpu.VMEM((SSD_GROUPS, SSD_STATE, GROUP_W), f32),
                        pltpu.VMEM((SSD_GROUPS, SSD_STATE, GROUP_W), f32),
                        pltpu.VMEM((seq, D_SSM), f32)],
        compiler_params=_params("parallel", "arbitrary"),
        name="ssd_latent",
    )(xc, xc, xc, xc, xc, xc, proj, proj, proj, proj, hf0, hb0,
      _pad_lanes(dt_bias), _pad_lanes(a_log), dskip, norm_w.reshape(1, -1).astype(f32), ef, eb)


def _filter_kernel(z_ref, w1_ref, b1_ref, wh_ref, bh_ref, freq_ref, wf_ref, wb_ref, delta_ref, ks_ref, kd_ref):
    hp = lambda a, b: jnp.dot(a, b, preferred_element_type=f32, precision=lax.Precision.HIGHEST)
    z = z_ref[...]
    freq = freq_ref[...]
    h = jnp.sin(freq * (hp(z, w1_ref[...]) + b1_ref[...]))
    for n in range(HYENA_INNER):
        h = jnp.sin(freq * (hp(h, wh_ref[n]) + bh_ref[n]))
    window = jnp.exp(-z[:, 0:1] * delta_ref[...])
    hf = hp(h, wf_ref[...]) * window
    hb = hp(h, wb_ref[...]) * window
    row = lax.broadcasted_iota(jnp.int32, hb.shape, 0)
    hb = jnp.where(row == 0, 0.0, hb)
    norm = jnp.sum(jnp.abs(hf), axis=0, keepdims=True) + jnp.sum(jnp.abs(hb), axis=0, keepdims=True)
    ks_ref[...] = (hf + hb) / norm
    kd_ref[...] = (hf - hb) / norm


def hyena_filter_taps(seq, f_w1, f_b1, f_wh, f_bh, f_wout, freq, tc=256):
    fw = HYENA_FILTER_WIDTH
    t = np.arange(seq, dtype=np.float32)
    t_norm = t / np.float32(seq - 1)
    bands = np.linspace(1e-4, HYENA_BANDS - 1, HYENA_BANDS, dtype=np.float32)
    ang = np.float32(2.0 * math.pi / seq) * t[:, None] * bands
    z = np.concatenate([t_norm[:, None], np.cos(ang), -np.sin(ang)], axis=-1).astype(np.float32)
    z = np.pad(z, ((0, 0), (0, LANES - z.shape[1])))
    deltas = np.abs(np.linspace(math.log(HYENA_TARGET) / HYENA_SLOW_DECAY, math.log(HYENA_TARGET) / HYENA_FAST_DECAY,
                                D_MODEL, dtype=np.float32)).reshape(1, -1)
    padw = lambda a, r, c: jnp.pad(a.astype(f32), [(0, 0)] * (a.ndim - 2) + [(0, r - a.shape[-2]), (0, c - a.shape[-1])])
    w1 = padw(f_w1, LANES, LANES)
    wh = padw(f_wh, LANES, LANES)
    wout = padw(f_wout, LANES, 2 * D_MODEL)
    b1 = padw(f_b1.reshape(1, fw), 1, LANES)
    bh = padw(f_bh.reshape(HYENA_INNER, 1, fw), 1, LANES)
    fq = padw(freq.reshape(1, fw), 1, LANES)
    nt = D_MODEL // tc
    const = lambda shape: pl.BlockSpec(shape, lambda j: (0,) * len(shape))
    out = jax.ShapeDtypeStruct((seq, D_MODEL), f32)
    osp = pl.BlockSpec((seq, tc), lambda j: (0, j))
    return pl.pallas_call(
        _filter_kernel,
        grid=(nt,),
        in_specs=[const((seq, LANES)), const((LANES, LANES)), const((1, LANES)),
                  const((HYENA_INNER, LANES, LANES)), const((HYENA_INNER, 1, LANES)), const((1, LANES)),
                  pl.BlockSpec((LANES, tc), lambda j: (0, j)),
                  pl.BlockSpec((LANES, tc), lambda j: (0, j + nt)),
                  pl.BlockSpec((1, tc), lambda j: (0, j))],
        out_specs=[osp, osp],
        out_shape=[out, out],
        compiler_params=_params("parallel"),
        name="hyena_filter",
    )(jnp.asarray(z), w1, b1, wh, bh, fq, wout, wout, jnp.asarray(deltas))


def _dft_tables(seq):
    n = 2 * seq
    f = np.arange(seq, dtype=np.int64)[:, None]
    t = np.arange(seq, dtype=np.int64)[None, :]
    theta = (2.0 * math.pi / (2 * n)) * (((2 * f + 1) * t) % (2 * n)).astype(np.float64)
    c, s = np.cos(theta), np.sin(theta)
    scale = 2.0 / n
    return (jnp.asarray(c, bf16), jnp.asarray(s, bf16),
            jnp.asarray(c.T * scale, bf16), jnp.asarray(s.T * scale, bf16))


def _spectrum_kernel(c_ref, s_ref, ks_ref, kd_ref, kre_ref, kb_ref):
    kre_ref[...] = sum(_dot(c_ref[...], part) for part in _split2(ks_ref[...]))
    kb_ref[...] = sum(_dot(s_ref[...], part) for part in _split2(kd_ref[...]))


def filter_spectrum(cmat, smat, ks, kd, tmf=256, tc=256):
    seq = ks.shape[0]
    fsp = pl.BlockSpec((tmf, seq), lambda j, m: (m, 0))
    ksp = pl.BlockSpec((seq, tc), lambda j, m: (0, j))
    osp = pl.BlockSpec((tmf, tc), lambda j, m: (m, j))
    out = jax.ShapeDtypeStruct((seq, D_MODEL), f32)
    return pl.pallas_call(
        _spectrum_kernel,
        grid=(D_MODEL // tc, seq // tmf),
        in_specs=[fsp, fsp, ksp, ksp],
        out_specs=[osp, osp],
        out_shape=[out, out],
        compiler_params=_params("parallel", "parallel"),
        name="filter_spectrum",
    )(cmat, smat, ks, kd)


def _dft_fwd_kernel(c_ref, s_ref, v_ref, kre_ref, kb_ref, p_ref, q_ref):
    v = v_ref[...]
    vre = _dot(c_ref[...], v)
    va = _dot(s_ref[...], v)
    kre, kb = kre_ref[...], kb_ref[...]
    p_ref[...] = (vre * kre - va * kb).astype(bf16)
    q_ref[...] = (vre * kb + va * kre).astype(bf16)


def dft_forward(cmat, smat, vgh, kre, kb, seq, tmf=256):
    nb = vgh.shape[0] // seq
    nm = seq // tmf
    fsp = pl.BlockSpec((tmf, seq), lambda b, m: (m, 0))
    ksp = pl.BlockSpec((tmf, D_MODEL), lambda b, m: (m, 0))
    osp = pl.BlockSpec((tmf, D_MODEL), lambda b, m: (b * nm + m, 0))
    out = jax.ShapeDtypeStruct((nb * seq, D_MODEL), bf16)
    return pl.pallas_call(
        _dft_fwd_kernel,
        grid=(nb, nm),
        in_specs=[fsp, fsp, pl.BlockSpec((seq, D_MODEL), lambda b, m: (b, 0)), ksp, ksp],
        out_specs=[osp, osp],
        out_shape=[out, out],
        compiler_params=_params("parallel", "parallel"),
        name="dft_forward",
    )(cmat, smat, vgh, kre, kb)


def _dft_inv_kernel(gate_row, ct_ref, st_ref, p_ref, q_ref, vg_ref, x0_ref, bias_ref, w_ref, h_ref, mod_ref, o_ref):
    y = _dot(ct_ref[...], p_ref[...]) + _dot(st_ref[...], q_ref[...])
    y = y + vg_ref[...] * bias_ref[...]
    out = _dot((y * x0_ref[...]).astype(bf16), w_ref[...])
    o_ref[...] = h_ref[...] + mod_ref[0, gate_row:gate_row + 1, :] * out


def dft_inverse_out(ctm, stm, p, q, vg, x0, bias, w_out, h, mods, seq, gate_row, tmt=256):
    nb = h.shape[0] // seq
    nm = seq // tmt
    gsp = pl.BlockSpec((tmt, seq), lambda b, m: (m, 0))
    full = pl.BlockSpec((seq, D_MODEL), lambda b, m: (b, 0))
    tile = pl.BlockSpec((tmt, D_MODEL), lambda b, m: (b * nm + m, 0))
    return pl.pallas_call(
        functools.partial(_dft_inv_kernel, gate_row),
        grid=(nb, nm),
        in_specs=[gsp, gsp, full, full, tile, tile,
                  pl.BlockSpec((1, D_MODEL), lambda b, m: (0, 0)),
                  pl.BlockSpec((D_MODEL, D_MODEL), lambda b, m: (0, 0)),
                  tile, pl.BlockSpec((1, N_MOD, D_MODEL), lambda b, m: (b, 0, 0))],
        out_specs=tile,
        out_shape=jax.ShapeDtypeStruct(h.shape, f32),
        compiler_params=_params("parallel", "parallel"),
        name="dft_inverse_out",
    )(ctm, stm, p, q, vg, x0, bias.reshape(1, -1).astype(f32), w_out, h, mods)


_Q_HEAD_ORDER = (0, 4, 1, 5, 2, 6, 3, 7)


def _mixer_in_weight(w_in):
    c = np.cumsum([ATTN_DIM, KV_DIM, KV_DIM, D_SSM, XBC_DIM])
    q, k, v, z, xbc, dt = (w_in[:, :c[0]], w_in[:, c[0]:c[1]], w_in[:, c[1]:c[2]], w_in[:, c[2]:c[3]],
                           w_in[:, c[3]:c[4]], w_in[:, c[4]:])
    q = q.reshape(D_MODEL, N_Q_HEADS, HEAD_DIM)[:, np.array(_Q_HEAD_ORDER), :].reshape(D_MODEL, ATTN_DIM)
    w = jnp.concatenate([xbc, z, q, k, v, dt], axis=1)
    return jnp.pad(w, ((0, 0), (0, MIX_COLS - w.shape[1]))).astype(bf16)


def _mixer_out_weights(w_out):
    wa = w_out[:ATTN_DIM].reshape(N_Q_HEADS, HEAD_DIM, D_MODEL)[np.array(_Q_HEAD_ORDER)].reshape(ATTN_DIM, D_MODEL)
    return wa.astype(bf16), w_out[ATTN_DIM:].astype(bf16)


def kernel(x, c, ctx, c_ctx, w_ada, b_ada, norm_g, ffn_w13, ffn_w2, mix_w_in, mix_w_out, q_norm, k_norm, attn_sink,
           ssd_conv_w, ssd_conv_b, ssd_dt_bias, ssd_a_log, ssd_d, ssd_norm, hy_w_in, hy_conv_w, hy_conv_b,
           hy_f_w1, hy_f_b1, hy_f_wh, hy_f_bh, hy_f_wout, hy_freq, hy_bias, hy_w_out):
    nb, seq, _ = x.shape
    ctx_len = ctx.shape[1]
    depth = w_ada.shape[0]
    h_lat = x.reshape(nb * seq, D_MODEL)
    h_ctx = ctx.reshape(nb * ctx_len, D_MODEL)
    w13 = ffn_w13.astype(bf16)
    w2 = ffn_w2.astype(bf16)
    cond = jnp.concatenate([c, c_ctx[None]], axis=0)
    cond = jnp.pad(cond, ((0, -cond.shape[0] % 8), (0, 0)))

    for i in range(depth):
        last = i == depth - 1
        even = i % 2 == 0
        j = i // 2
        ctx_live = even or not last
        mods = ada_mods(cond, w_ada[i], b_ada[i])
        m_lat, m_ctx = mods[:nb], mods[nb:nb + 1]
        h_lat = macaron_ffn(h_lat, m_lat, seq, norm_g[i, 0], 0, w13[i, 0], w2[i, 0])
        if ctx_live:
            h_ctx = macaron_ffn(h_ctx, m_ctx, nb * ctx_len, norm_g[i, 0], 0, w13[i, 0], w2[i, 0])
        if even:
            w_in = _mixer_in_weight(mix_w_in[j])
            wa, ws = _mixer_out_weights(mix_w_out[j])
            p_lat = adaln_proj(h_lat, m_lat, seq, norm_g[i, 1], 1, w_in)
            p_ctx = adaln_proj(h_ctx, m_ctx, nb * ctx_len, norm_g[i, 1], 1, w_in)
            cos, sin = _rope_tables(seq)
            a_lat = window_attention(p_lat, p_ctx, seq, ctx_len, q_norm[j], k_norm[j], attn_sink[j], cos, sin)
            xc_ctx = ssd_conv(p_ctx, ctx_len, ssd_conv_w[j], ssd_conv_b[j])
            xc_lat = ssd_conv(p_lat, seq, ssd_conv_w[j], ssd_conv_b[j])
            hf0, hb0 = ssd_ctx_states(xc_ctx, p_ctx, ctx_len, ssd_dt_bias[j], ssd_a_log[j])
            s_lat = ssd_latent(xc_lat, p_lat, seq, hf0, hb0, ssd_dt_bias[j], ssd_a_log[j], ssd_d[j], ssd_norm[j])
            h_lat = out_proj_residual([a_lat, s_lat], [wa, ws], h_lat, m_lat, seq, 5)
            if not last and any(k % 2 == 0 or k != depth - 1 for k in range(i + 1, depth)):
                raise NotImplementedError("context stream past the first mixer is only needed for depth > 2")
        else:
            p = adaln_proj(h_lat, m_lat, seq, norm_g[i, 1], 1, hy_w_in[j].astype(bf16))
            x0, vg, vgh = hyena_conv(p, seq, hy_conv_w[j], hy_conv_b[j])
            ks, kd = hyena_filter_taps(seq, hy_f_w1[j], hy_f_b1[j], hy_f_wh[j], hy_f_bh[j], hy_f_wout[j], hy_freq[j])
            cm, sm, ctm, stm = _dft_tables(seq)
            kre, kb = filter_spectrum(cm, sm, ks, kd)
            pp, qq = dft_forward(cm, sm, vgh, kre, kb, seq)
            h_lat = dft_inverse_out(ctm, stm, pp, qq, vg, x0, hy_bias[j], hy_w_out[j].astype(bf16), h_lat, m_lat,
                                    seq, 5)
        h_lat = macaron_ffn(h_lat, m_lat, seq, norm_g[i, 2], 2, w13[i, 1], w2[i, 1])
    return h_lat.reshape(nb, seq, D_MODEL)
```

```python
import functools
import math

import numpy as np
import jax
import jax.numpy as jnp
from jax import lax
from jax.experimental import pallas as pl
from jax.experimental.pallas import tpu as pltpu

f32 = jnp.float32
bf16 = jnp.bfloat16

D_MODEL = 1024
N_MOD = 9
RMS_EPS = 1e-6
GRID_W = 64

HEAD_DIM = 64
N_Q_HEADS = 8
N_KV_HEADS = 2
ATTN_DIM = N_Q_HEADS * HEAD_DIM
KV_DIM = N_KV_HEADS * HEAD_DIM
WINDOW = 128
ATTN_BLOCK = 128
ROPE_THETA = 10000.0

SSD_HEADS = 16
SSD_HEAD_DIM = 64
D_SSM = SSD_HEADS * SSD_HEAD_DIM
SSD_GROUPS = 2
SSD_STATE = 128
SSD_CONV = 7
SSD_CHUNK = 128
BC_DIM = SSD_GROUPS * SSD_STATE
XBC_DIM = D_SSM + 4 * BC_DIM
GROUP_W = D_SSM // SSD_GROUPS

HYENA_SHORT = 3
HYENA_BANDS = 8
HYENA_FILTER_WIDTH = 64
HYENA_INNER = 2
HYENA_FAST_DECAY = 0.3
HYENA_SLOW_DECAY = 1.5
HYENA_TARGET = 1e-2

D_FF = 2816
FFN_TF = 256
LANES = 128

COL_XBC = 0
COL_Z = XBC_DIM
COL_Q = COL_Z + D_SSM
COL_K = COL_Q + ATTN_DIM
COL_V = COL_K + KV_DIM
MIX_COLS = COL_V + KV_DIM

VMEM_LIMIT = 56 * 1024 * 1024


def _params(*sem):
    return pltpu.CompilerParams(dimension_semantics=sem, vmem_limit_bytes=VMEM_LIMIT)


def _dot(a, b):
    return jnp.dot(a, b, preferred_element_type=f32)


def _dot_nt(a, b):
    return lax.dot_general(a, b, (((1,), (1,)), ((), ())), preferred_element_type=f32)


def _split2(x):
    hi = x.astype(bf16)
    lo = (x - hi.astype(f32)).astype(bf16)
    return hi, lo


def _split3(x):
    hi = x.astype(bf16)
    r = x - hi.astype(f32)
    mid = r.astype(bf16)
    lo = (r - mid.astype(f32)).astype(bf16)
    return hi, mid, lo


def _adaln(h, g, shift, scale):
    ms = jnp.mean(h * h, axis=-1, keepdims=True)
    return (h * lax.rsqrt(ms + RMS_EPS) * g) * (1.0 + scale) + shift


def _silu(x):
    return x * jax.nn.sigmoid(x)


def _mods_kernel(c_ref, w_ref, b_ref, o_ref):
    o_ref[...] = _dot(_silu(c_ref[...]).astype(bf16), w_ref[...].astype(bf16)) + b_ref[...]


def ada_mods(cond, w, b):
    r = cond.shape[0]
    tn = 1024
    out = pl.pallas_call(
        _mods_kernel,
        grid=(w.shape[1] // tn,),
        in_specs=[pl.BlockSpec((r, D_MODEL), lambda j: (0, 0)),
                  pl.BlockSpec((D_MODEL, tn), lambda j: (0, j)),
                  pl.BlockSpec((1, tn), lambda j: (0, j))],
        out_specs=pl.BlockSpec((r, tn), lambda j: (0, j)),
        out_shape=jax.ShapeDtypeStruct((r, w.shape[1]), f32),
        compiler_params=_params("parallel"),
        name="ada_mods",
    )(cond, w, b.reshape(1, -1))
    return out.reshape(r, N_MOD, D_MODEL)


def _ffn_kernel(s, nf, h_ref, mod_ref, g_ref, w13_ref, w2_ref, o_ref):
    h = h_ref[...]
    u = _adaln(h, g_ref[...], mod_ref[0, 3 * s:3 * s + 1, :], mod_ref[0, 3 * s + 1:3 * s + 2, :]).astype(bf16)
    acc = None
    for j in range(nf):
        a = _dot(u, w13_ref[j])
        b = _dot(u, w13_ref[nf + j])
        part = _dot((_silu(a) * b).astype(bf16), w2_ref[j])
        acc = part if acc is None else acc + part
    o_ref[...] = h + 0.5 * mod_ref[0, 3 * s + 2:3 * s + 3, :] * acc


def _ffn_weights(w13, w2, tf=FFN_TF):
    w13c = w13.astype(bf16).reshape(D_MODEL, 2 * D_FF // tf, tf).transpose(1, 0, 2)
    return w13c, w2.astype(bf16).reshape(D_FF // tf, tf, D_MODEL)


def macaron_ffn(h, mods, rows_per_mod, g, s, w13c, w2c, tm=512):
    m = h.shape[0]
    tm = min(tm, rows_per_mod)
    nf = w2c.shape[0]
    tiles_per_mod = rows_per_mod // tm
    return pl.pallas_call(
        functools.partial(_ffn_kernel, s, nf),
        grid=(m // tm,),
        in_specs=[pl.BlockSpec((tm, D_MODEL), lambda i: (i, 0)),
                  pl.BlockSpec((1, N_MOD, D_MODEL), lambda i: (i // tiles_per_mod, 0, 0)),
                  pl.BlockSpec((1, D_MODEL), lambda i: (0, 0)),
                  pl.BlockSpec(w13c.shape, lambda i: (0, 0, 0)),
                  pl.BlockSpec(w2c.shape, lambda i: (0, 0, 0))],
        out_specs=pl.BlockSpec((tm, D_MODEL), lambda i: (i, 0)),
        out_shape=jax.ShapeDtypeStruct((m, D_MODEL), f32),
        compiler_params=_params("parallel"),
        name="macaron_ffn",
    )(h, mods, g.reshape(1, -1), w13c, w2c)


def _inproj_kernel(s, tn, has_tail, h_ref, mod_ref, g_ref, w_ref, *rest):
    u = _adaln(h_ref[...], g_ref[...], mod_ref[0, 3 * s:3 * s + 1, :], mod_ref[0, 3 * s + 1:3 * s + 2, :]).astype(bf16)
    o_ref = rest[-2] if has_tail else rest[-1]
    for c0 in range(0, w_ref.shape[1], tn):
        o_ref[:, c0:c0 + tn] = _dot(u, w_ref[:, c0:c0 + tn]).astype(bf16)
    if has_tail:
        rest[-1][...] = _dot(u, rest[0][...])


def adaln_proj(h, mods, rows_per_mod, g, s, w, w_tail=None, tm=512, tn=768):
    m = h.shape[0]
    n = w.shape[1]
    tm = min(tm, rows_per_mod)
    tiles_per_mod = rows_per_mod // tm
    has_tail = w_tail is not None
    in_specs = [pl.BlockSpec((tm, D_MODEL), lambda i: (i, 0)),
                pl.BlockSpec((1, N_MOD, D_MODEL), lambda i: (i // tiles_per_mod, 0, 0)),
                pl.BlockSpec((1, D_MODEL), lambda i: (0, 0)),
                pl.BlockSpec((D_MODEL, n), lambda i: (0, 0))]
    out_specs = [pl.BlockSpec((tm, n), lambda i: (i, 0))]
    out_shape = [jax.ShapeDtypeStruct((m, n), bf16)]
    args = [h, mods, g.reshape(1, -1), w]
    if has_tail:
        in_specs.append(pl.BlockSpec((D_MODEL, LANES), lambda i: (0, 0)))
        out_specs.append(pl.BlockSpec((tm, LANES), lambda i: (i, 0)))
        out_shape.append(jax.ShapeDtypeStruct((m, LANES), f32))
        args.append(w_tail)
    out = pl.pallas_call(
        functools.partial(_inproj_kernel, s, tn, has_tail),
        grid=(m // tm,),
        in_specs=in_specs,
        out_specs=out_specs,
        out_shape=out_shape,
        compiler_params=_params("parallel"),
        name="adaln_proj",
    )(*args)
    return out if has_tail else out[0]


def _outproj_kernel(n_in, gate_row, *refs):
    x_refs = refs[:n_in]
    w_refs = refs[n_in:2 * n_in]
    h_ref, mod_ref, o_ref = refs[2 * n_in:]
    acc = _dot(x_refs[0][...].astype(bf16), w_refs[0][...])
    for x_ref, w_ref in zip(x_refs[1:], w_refs[1:]):
        acc += _dot(x_ref[...].astype(bf16), w_ref[...])
    o_ref[...] = h_ref[...] + mod_ref[0, gate_row:gate_row + 1, :] * acc


def out_proj_residual(xs, ws, h, mods, rows_per_mod, gate_row, tm=512):
    m = h.shape[0]
    tm = min(tm, rows_per_mod)
    tiles_per_mod = rows_per_mod // tm
    n_in = len(xs)
    in_specs = ([pl.BlockSpec((tm, x.shape[1]), lambda i: (i, 0)) for x in xs]
                + [pl.BlockSpec(w.shape, lambda i: (0, 0)) for w in ws]
                + [pl.BlockSpec((tm, D_MODEL), lambda i: (i, 0)),
                   pl.BlockSpec((1, N_MOD, D_MODEL), lambda i: (i // tiles_per_mod, 0, 0))])
    return pl.pallas_call(
        functools.partial(_outproj_kernel, n_in, gate_row),
        grid=(m // tm,),
        in_specs=in_specs,
        out_specs=pl.BlockSpec((tm, D_MODEL), lambda i: (i, 0)),
        out_shape=jax.ShapeDtypeStruct((m, D_MODEL), f32),
        compiler_params=_params("parallel"),
        name="out_proj_residual",
    )(*xs, *ws, h, mods)


def _shifted_taps(pad_ref, x, seq, width):
    c = x.shape[1]
    pad_ref[0:8, :] = jnp.zeros((8, c), f32)
    pad_ref[seq + 8:seq + 16, :] = jnp.zeros((8, c), f32)
    pad_ref[8:seq + 8, :] = x
    half = width // 2

    def tap(row0, rows, k):
        return pad_ref[8 + row0 + k - half:8 + row0 + k - half + rows, :]

    return tap


def _dwconv(pad_ref, x, w, b, seq, width, row_tile, emit):
    tap = _shifted_taps(pad_ref, x, seq, width)
    for r0 in range(0, seq, row_tile):
        acc = b + w[0:1, :] * tap(r0, row_tile, 0)
        for k in range(1, width):
            acc += w[k:k + 1, :] * tap(r0, row_tile, k)
        emit(r0, acc)


def _ssd_conv_kernel(seq, x_ref, w_ref, b_ref, o_ref, pad_ref):
    row_tile = min(seq, 256)

    def emit(r0, acc):
        o_ref[r0:r0 + row_tile, :] = _silu(acc).astype(bf16)

    _dwconv(pad_ref, x_ref[...].astype(f32), w_ref[...], b_ref[...], seq, SSD_CONV, row_tile, emit)


def ssd_conv(proj, seq, conv_w, conv_b, tc=256):
    nb = proj.shape[0] // seq
    return pl.pallas_call(
        functools.partial(_ssd_conv_kernel, seq),
        grid=(nb, XBC_DIM // tc),
        in_specs=[pl.BlockSpec((seq, tc), lambda b, j: (b, j)),
                  pl.BlockSpec((SSD_CONV, tc), lambda b, j: (0, j)),
                  pl.BlockSpec((1, tc), lambda b, j: (0, j))],
        out_specs=pl.BlockSpec((seq, tc), lambda b, j: (b, j)),
        out_shape=jax.ShapeDtypeStruct((nb * seq, XBC_DIM), bf16),
        scratch_shapes=[pltpu.VMEM((seq + 16, tc), f32)],
        compiler_params=_params("parallel", "parallel"),
        name="ssd_conv",
    )(proj, conv_w, conv_b.reshape(1, -1))


def _hyena_conv_kernel(seq, x0_ref, x1_ref, v_ref, w0_ref, w1_ref, wv_ref, b0_ref, b1_ref, bv_ref,
                       x0_out, vg_out, pad_ref, tmp_ref):
    row_tile = min(seq, 256)

    def emit_x0(r0, acc):
        x0_out[r0:r0 + row_tile, :] = acc.astype(bf16)

    def emit_x1(r0, acc):
        tmp_ref[r0:r0 + row_tile, :] = acc

    def emit_v(r0, acc):
        vg_out[r0:r0 + row_tile, :] = (acc * tmp_ref[r0:r0 + row_tile, :]).astype(bf16)

    _dwconv(pad_ref, x0_ref[...].astype(f32), w0_ref[...], b0_ref[...], seq, HYENA_SHORT, row_tile, emit_x0)
    _dwconv(pad_ref, x1_ref[...].astype(f32), w1_ref[...], b1_ref[...], seq, HYENA_SHORT, row_tile, emit_x1)
    _dwconv(pad_ref, v_ref[...].astype(f32), wv_ref[...], bv_ref[...], seq, HYENA_SHORT, row_tile, emit_v)


def hyena_conv(proj, seq, conv_w, conv_b, tc=256):
    nb = proj.shape[0] // seq
    nt = D_MODEL // tc
    b2 = conv_b.reshape(1, -1)
    act = lambda off: pl.BlockSpec((seq, tc), lambda b, j: (b, j + off * nt))
    wsp = lambda off: pl.BlockSpec((HYENA_SHORT, tc), lambda b, j: (0, j + off * nt))
    bsp = lambda off: pl.BlockSpec((1, tc), lambda b, j: (0, j + off * nt))
    osp = pl.BlockSpec((seq, tc), lambda b, j: (b, j))
    return pl.pallas_call(
        functools.partial(_hyena_conv_kernel, seq),
        grid=(nb, nt),
        in_specs=[act(0), act(1), act(2), wsp(0), wsp(1), wsp(2), bsp(0), bsp(1), bsp(2)],
        out_specs=[osp, osp],
        out_shape=[jax.ShapeDtypeStruct((nb * seq, D_MODEL), bf16),
                   jax.ShapeDtypeStruct((nb * seq, D_MODEL), bf16)],
        scratch_shapes=[pltpu.VMEM((seq + 16, tc), f32), pltpu.VMEM((seq, tc), f32)],
        compiler_params=_params("parallel", "parallel"),
        name="hyena_conv",
    )(proj, proj, proj, conv_w, conv_w, conv_w, b2, b2, b2)


def _head_norm(x, gain, bd):
    hi, lo = _split2(x * x)
    ms = _dot(hi, bd) + _dot(lo, bd)
    return x * lax.rsqrt(ms + RMS_EPS) * gain


def _rope(x, cos, sin_signed):
    lane = lax.broadcasted_iota(jnp.int32, x.shape, 1)
    partner = jnp.where((lane & 16) != 0, pltpu.roll(x, 16, 1), pltpu.roll(x, LANES - 16, 1))
    return x * cos + partner * sin_signed


def _attn_kernel(seq, q_ref, k_ref, v_ref, kc_ref, vc_ref, qg_ref, kg_ref, cos_ref, sin_ref, bd_ref,
                 sink_ref, o_ref, q_s, k_s, v_s, kc_s, vc_s):
    j = pl.program_id(1)
    nblk = seq // ATTN_BLOCK
    ctx_len = kc_ref.shape[0]

    @pl.when(j == 0)
    def _():
        bd = bd_ref[...]
        cos, sin = cos_ref[...], sin_ref[...]
        scale = HEAD_DIM ** -0.5
        for p in range(ATTN_DIM // LANES):
            qn = _head_norm(q_ref[:, p * LANES:(p + 1) * LANES].astype(f32), qg_ref[...], bd)
            q_s[:, p * LANES:(p + 1) * LANES] = (_rope(qn, cos, sin) * scale).astype(bf16)
        kn = _head_norm(k_ref[...].astype(f32), kg_ref[...], bd)
        zeros = jnp.zeros((WINDOW, KV_DIM), bf16)
        k_s[0:WINDOW, :] = zeros
        k_s[WINDOW + seq:2 * WINDOW + seq, :] = zeros
        v_s[0:WINDOW, :] = zeros
        v_s[WINDOW + seq:2 * WINDOW + seq, :] = zeros
        k_s[WINDOW:WINDOW + seq, :] = _rope(kn, cos, sin).astype(bf16)
        v_s[WINDOW:WINDOW + seq, :] = v_ref[...].astype(bf16)
        kc_s[...] = _head_norm(kc_ref[...].astype(f32), kg_ref[...], bd).astype(bf16)
        vc_s[...] = vc_ref[...].astype(bf16)

    band = ATTN_BLOCK + 2 * WINDOW
    nslab = ATTN_DIM // LANES
    rows = N_Q_HEADS * ATTN_BLOCK
    start = pl.multiple_of(j * ATTN_BLOCK, ATTN_BLOCK)
    kb = k_s[pl.ds(start, band), :]
    vb = v_s[pl.ds(start, band), :]
    lane = lax.broadcasted_iota(jnp.int32, (ATTN_BLOCK, LANES), 1)
    qblk = q_s[pl.ds(start, ATTN_BLOCK), :]
    blocks, sinks = [], []
    for p in range(nslab):
        qp = qblk[:, p * LANES:(p + 1) * LANES]
        for hh in range(N_KV_HEADS):
            mine = (lane >= hh * HEAD_DIM) & (lane < (hh + 1) * HEAD_DIM)
            blocks.append(jnp.where(mine, qp, jnp.zeros_like(qp)))
            sinks.append(jnp.full((ATTN_BLOCK, 1), sink_ref[p + (N_Q_HEADS // N_KV_HEADS) * hh], f32))
    qs = jnp.concatenate(blocks, axis=0)
    sink = jnp.concatenate(sinks, axis=0)
    row = lax.broadcasted_iota(jnp.int32, (rows, band), 0) & (ATTN_BLOCK - 1)
    col = lax.broadcasted_iota(jnp.int32, (rows, band), 1)
    kpos = start - WINDOW + col
    ok = (jnp.abs(row - (col - WINDOW)) <= WINDOW) & (kpos >= 0) & (kpos < seq)
    sb = jnp.where(ok, _dot_nt(qs, kb), -jnp.inf)
    sc = _dot_nt(qs, kc_s[...])
    mx = jnp.maximum(jnp.maximum(jnp.max(sb, axis=-1, keepdims=True), jnp.max(sc, axis=-1, keepdims=True)), sink)
    pb = jnp.exp(sb - mx)
    pc = jnp.exp(sc - mx)
    den = jnp.sum(pb, axis=-1, keepdims=True) + jnp.sum(pc, axis=-1, keepdims=True) + jnp.exp(sink - mx)
    o = (_dot(pb.astype(bf16), vb) + _dot(pc.astype(bf16), vc_s[...])) / den
    for p in range(nslab):
        lo = o[(2 * p) * ATTN_BLOCK:(2 * p + 1) * ATTN_BLOCK, :]
        hi = o[(2 * p + 1) * ATTN_BLOCK:(2 * p + 2) * ATTN_BLOCK, :]
        o_ref[:, p * LANES:(p + 1) * LANES] = jnp.where(lane < HEAD_DIM, lo, hi).astype(bf16)


def window_attention(proj_lat, proj_ctx, seq, ctx_len, q_gain, k_gain, sink, rope_cos, rope_sin):
    nb = proj_lat.shape[0] // seq
    nblk = seq // ATTN_BLOCK
    bd = np.kron(np.eye(LANES // HEAD_DIM), np.ones((HEAD_DIM, HEAD_DIM))) / HEAD_DIM
    gain2 = lambda g: jnp.tile(g, LANES // HEAD_DIM).reshape(1, LANES)
    const = lambda shape: pl.BlockSpec(shape, lambda b, j: (0, 0))
    return pl.pallas_call(
        functools.partial(_attn_kernel, seq),
        grid=(nb, nblk),
        in_specs=[pl.BlockSpec((seq, ATTN_DIM), lambda b, j: (b, COL_Q // ATTN_DIM)),
                  pl.BlockSpec((seq, KV_DIM), lambda b, j: (b, COL_K // KV_DIM)),
                  pl.BlockSpec((seq, KV_DIM), lambda b, j: (b, COL_V // KV_DIM)),
                  pl.BlockSpec((ctx_len, KV_DIM), lambda b, j: (b, COL_K // KV_DIM)),
                  pl.BlockSpec((ctx_len, KV_DIM), lambda b, j: (b, COL_V // KV_DIM)),
                  const((1, LANES)), const((1, LANES)),
                  const((seq, LANES)), const((seq, LANES)), const((LANES, LANES)),
                  pl.BlockSpec(memory_space=pltpu.SMEM)],
        out_specs=pl.BlockSpec((ATTN_BLOCK, ATTN_DIM), lambda b, j: (b * nblk + j, 0)),
        out_shape=jax.ShapeDtypeStruct((nb * seq, ATTN_DIM), bf16),
        scratch_shapes=[pltpu.VMEM((seq, ATTN_DIM), bf16),
                        pltpu.VMEM((seq + 2 * WINDOW, KV_DIM), bf16),
                        pltpu.VMEM((seq + 2 * WINDOW, KV_DIM), bf16),
                        pltpu.VMEM((ctx_len, KV_DIM), bf16),
                        pltpu.VMEM((ctx_len, KV_DIM), bf16)],
        compiler_params=_params("parallel", "arbitrary"),
        name="window_attention",
    )(proj_lat, proj_lat, proj_lat, proj_ctx, proj_ctx, gain2(q_gain), gain2(k_gain),
      rope_cos, rope_sin, jnp.asarray(bd, bf16), sink)


def _rope_tables(seq):
    t = np.arange(seq)
    pos = np.stack([t // GRID_W, t % GRID_W], axis=1).astype(np.float32)
    axis_dim = HEAD_DIM // 2
    inv = (ROPE_THETA ** (-np.arange(0, axis_dim, 2, dtype=np.float32) / axis_dim)).astype(np.float32)
    lane = np.arange(LANES)
    d = lane % HEAD_DIM
    which = d // axis_dim
    ang = (pos[:, which] * inv[d % (axis_dim // 2)][None, :]).astype(np.float32)
    sign = np.where((d % axis_dim) < axis_dim // 2, -1.0, 1.0)
    return jnp.asarray(np.cos(ang), f32), jnp.asarray(np.sin(ang) * sign, f32)


def _softplus(x):
    return jnp.maximum(x, 0.0) + jnp.log1p(jnp.exp(-jnp.abs(x)))


def _expand_heads(v, e):
    return _dot(v.astype(bf16), e)


def _ssd_chunk(rev, lane0, want_y, x, bm, cm, dt_raw, dt_bias, a_neg, expand, state_ref):
    t = x.shape[0]
    dt = _softplus(dt_raw + dt_bias)
    a = dt * a_neg
    r = lax.broadcasted_iota(jnp.int32, (t, t), 0)
    c = lax.broadcasted_iota(jnp.int32, (t, t), 1)
    keep = (r <= c) if rev else (r >= c)
    tri = jnp.where(keep, 1.0, 0.0).astype(bf16)
    cs = sum(_dot(tri, part) for part in _split3(a))
    last = cs[0:1, :] if rev else cs[t - 1:t, :]
    e = jnp.exp(cs)
    w = dt * jnp.exp(last - cs)
    e_x = _expand_heads(e, expand)
    w_x = _expand_heads(w, expand)
    elast_x = e_x[0:1, :] if rev else e_x[t - 1:t, :]

    y = None
    if want_y:
        cs_t = cs.T
        dt_t = dt.T
        lane = lax.broadcasted_iota(jnp.int32, (t, LANES), 1)
        cb = [_dot_nt(cm[:, g * SSD_STATE:(g + 1) * SSD_STATE],
                      bm[:, g * SSD_STATE:(g + 1) * SSD_STATE]) for g in range(SSD_GROUPS)]
        pieces = []
        for p in range(SSD_HEADS // 2):
            xp = x[:, p * LANES:(p + 1) * LANES]
            ms = []
            for q in range(2):
                h = 2 * p + q
                g = h // (SSD_HEADS // SSD_GROUPS)
                seg = cs[:, lane0 + h:lane0 + h + 1] - cs_t[lane0 + h:lane0 + h + 1, :]
                dec = jnp.exp(jnp.where(keep, seg, -jnp.inf))
                ms.append((cb[g] * dec * dt_t[lane0 + h:lane0 + h + 1, :]).astype(bf16))
            zero = jnp.zeros_like(xp)
            xcat = jnp.concatenate([jnp.where(lane < SSD_HEAD_DIM, xp, zero),
                                    jnp.where(lane >= SSD_HEAD_DIM, xp, zero)], axis=0)
            pieces.append(_dot(jnp.concatenate(ms, axis=1), xcat))
        y = jnp.concatenate(pieces, axis=1)

    inter = []
    for g in range(SSD_GROUPS):
        gs = slice(g * GROUP_W, (g + 1) * GROUP_W)
        ss = slice(g * SSD_STATE, (g + 1) * SSD_STATE)
        h_t = state_ref[g]
        if want_y:
            inter.append(_dot(cm[:, ss], h_t.astype(bf16)) * e_x[:, gs])
        xw = (x[:, gs].astype(f32) * w_x[:, gs]).astype(bf16)
        state_ref[g] = h_t * elast_x[:, gs] + _dot(bm[:, ss].astype(f32).T.astype(bf16), xw)
    if want_y:
        y = y + jnp.concatenate(inter, axis=1)
    return y


def _ssd_ctx_kernel(nchunk, x_ref, bc_ref, dt_ref, bias_ref, alog_ref, ef_ref, eb_ref, hf_ref, hb_ref, sf, sb):
    sf[...] = jnp.zeros_like(sf)
    sb[...] = jnp.zeros_like(sb)
    a_neg = -jnp.exp(alog_ref[...])
    bias = bias_ref[...]
    t = SSD_CHUNK
    for ci in range(nchunk):
        rows = slice(ci * t, (ci + 1) * t)
        _ssd_chunk(False, 0, False, x_ref[rows, :], bc_ref[rows, 0:BC_DIM], None, dt_ref[rows, :],
                   bias, a_neg, ef_ref[...], sf)
        rows = slice((nchunk - 1 - ci) * t, (nchunk - ci) * t)
        _ssd_chunk(True, SSD_HEADS, False, x_ref[rows, :], bc_ref[rows, BC_DIM:2 * BC_DIM], None, dt_ref[rows, :],
                   bias, a_neg, eb_ref[...], sb)
    hf_ref[0] = sf[...]
    hb_ref[0] = sb[...]


def _head_expanders():
    ef = np.zeros((LANES, D_SSM), np.float32)
    eb = np.zeros((LANES, D_SSM), np.float32)
    for h in range(SSD_HEADS):
        ef[h, h * SSD_HEAD_DIM:(h + 1) * SSD_HEAD_DIM] = 1.0
        eb[SSD_HEADS + h, h * SSD_HEAD_DIM:(h + 1) * SSD_HEAD_DIM] = 1.0
    return jnp.asarray(ef, bf16), jnp.asarray(eb, bf16)


def _pad_lanes(v):
    v = v.reshape(1, -1).astype(f32)
    return jnp.pad(v, ((0, 0), (0, LANES - v.shape[1])))


def ssd_ctx_states(xc, dt_raw, seq, dt_bias, a_log):
    nb = xc.shape[0] // seq
    ef, eb = _head_expanders()
    const = lambda shape: pl.BlockSpec(shape, lambda b: (0,) * len(shape))
    st = jax.ShapeDtypeStruct((nb, SSD_GROUPS, SSD_STATE, GROUP_W), f32)
    st_spec = pl.BlockSpec((1, SSD_GROUPS, SSD_STATE, GROUP_W), lambda b: (b, 0, 0, 0))
    return pl.pallas_call(
        functools.partial(_ssd_ctx_kernel, seq // SSD_CHUNK),
        grid=(nb,),
        in_specs=[pl.BlockSpec((seq, D_SSM), lambda b: (b, 0)),
                  pl.BlockSpec((seq, 2 * BC_DIM), lambda b: (b, D_SSM // (2 * BC_DIM))),
                  pl.BlockSpec((seq, LANES), lambda b: (b, 0)),
                  const((1, LANES)), const((1, LANES)), const((LANES, D_SSM)), const((LANES, D_SSM))],
        out_specs=[st_spec, st_spec],
        out_shape=[st, st],
        scratch_shapes=[pltpu.VMEM((SSD_GROUPS, SSD_STATE, GROUP_W), f32),
                        pltpu.VMEM((SSD_GROUPS, SSD_STATE, GROUP_W), f32)],
        compiler_params=_params("parallel"),
        name="ssd_ctx_states",
    )(xc, xc, dt_raw, _pad_lanes(dt_bias), _pad_lanes(a_log), ef, eb)


def _ssd_lat_kernel(nchunk, xf_ref, xb_ref, bf_ref, bb_ref, cf_ref, cb_ref, dtf_ref, dtb_ref, zf_ref, zb_ref,
                    hf0_ref, hb0_ref, bias_ref, alog_ref, dskip_ref, normw_ref, ef_ref, eb_ref,
                    o_ref, sf, sb, yacc):
    c = pl.program_id(1)
    t = SSD_CHUNK

    @pl.when(c == 0)
    def _():
        sf[...] = hf0_ref[0]
        sb[...] = hb0_ref[0]

    a_neg = -jnp.exp(alog_ref[...])
    bias = bias_ref[...]
    xf = xf_ref[...]
    yf = _ssd_chunk(False, 0, True, xf, bf_ref[...], cf_ref[...], dtf_ref[...], bias, a_neg, ef_ref[...], sf)
    yf = yf + dskip_ref[...] * xf.astype(f32)
    yb = _ssd_chunk(True, SSD_HEADS, True, xb_ref[...], bb_ref[...], cb_ref[...], dtb_ref[...], bias, a_neg,
                    eb_ref[...], sb)
    rows_f = pl.ds(pl.multiple_of(c * t, t), t)
    rows_b = pl.ds(pl.multiple_of((nchunk - 1 - c) * t, t), t)

    @pl.when(c < nchunk // 2)
    def _():
        yacc[rows_f, :] = yf
        yacc[rows_b, :] = yb

    def finish(y, z):
        y = y * _silu(z.astype(f32))
        outs = []
        for g in range(SSD_GROUPS):
            yg = y[:, g * GROUP_W:(g + 1) * GROUP_W]
            outs.append(yg * lax.rsqrt(jnp.mean(yg * yg, axis=-1, keepdims=True) + RMS_EPS))
        return (jnp.concatenate(outs, axis=1) * normw_ref[...]).astype(bf16)

    @pl.when(c >= nchunk // 2)
    def _():
        o_ref[rows_f, :] = finish(yacc[rows_f, :] + yf, zf_ref[...])
        o_ref[rows_b, :] = finish(yacc[rows_b, :] + yb, zb_ref[...])


def ssd_latent(xc, proj, dt_raw, seq, hf0, hb0, dt_bias, a_log, d_skip, norm_w):
    nb = xc.shape[0] // seq
    nc = seq // SSD_CHUNK
    half = nc // 2
    ef, eb = _head_expanders()
    t = SSD_CHUNK
    fwd = lambda b, c: b * nc + c
    bwd = lambda b, c: b * nc + nc - 1 - c
    zfw = lambda b, c: b * nc + jnp.maximum(c, half)
    zbw = lambda b, c: b * nc + jnp.minimum(nc - 1 - c, half - 1)
    bc0 = D_SSM // SSD_STATE // SSD_GROUPS
    const = lambda shape: pl.BlockSpec(shape, lambda b, c: (0,) * len(shape))
    st_spec = pl.BlockSpec((1, SSD_GROUPS, SSD_STATE, GROUP_W), lambda b, c: (b, 0, 0, 0))
    dskip = jnp.repeat(d_skip.astype(f32), SSD_HEAD_DIM).reshape(1, D_SSM)
    return pl.pallas_call(
        functools.partial(_ssd_lat_kernel, nc),
        grid=(nb, nc),
        in_specs=[pl.BlockSpec((t, D_SSM), lambda b, c: (fwd(b, c), 0)),
                  pl.BlockSpec((t, D_SSM), lambda b, c: (bwd(b, c), 0)),
                  pl.BlockSpec((t, BC_DIM), lambda b, c: (fwd(b, c), bc0)),
                  pl.BlockSpec((t, BC_DIM), lambda b, c: (bwd(b, c), bc0 + 1)),
                  pl.BlockSpec((t, BC_DIM), lambda b, c: (fwd(b, c), bc0 + 2)),
                  pl.BlockSpec((t, BC_DIM), lambda b, c: (bwd(b, c), bc0 + 3)),
                  pl.BlockSpec((t, LANES), lambda b, c: (fwd(b, c), 0)),
                  pl.BlockSpec((t, LANES), lambda b, c: (bwd(b, c), 0)),
                  pl.BlockSpec((t, D_SSM), lambda b, c: (zfw(b, c), COL_Z // D_SSM)),
                  pl.BlockSpec((t, D_SSM), lambda b, c: (zbw(b, c), COL_Z // D_SSM)),
                  st_spec, st_spec,
                  const((1, LANES)), const((1, LANES)), const((1, D_SSM)), const((1, D_SSM)),
                  const((LANES, D_SSM)), const((LANES, D_SSM))],
        out_specs=pl.BlockSpec((seq, D_SSM), lambda b, c: (b, 0)),
        out_shape=jax.ShapeDtypeStruct((nb * seq, D_SSM), bf16),
        scratch_shapes=[pltpu.VMEM((SSD_GROUPS, SSD_STATE, GROUP_W), f32),
                        pltpu.VMEM((SSD_GROUPS, SSD_STATE, GROUP_W), f32),
                        pltpu.VMEM((seq, D_SSM), f32)],
        compiler_params=_params("parallel", "arbitrary"),
        name="ssd_latent",
    )(xc, xc, xc, xc, xc, xc, dt_raw, dt_raw, proj, proj, hf0, hb0,
      _pad_lanes(dt_bias), _pad_lanes(a_log), dskip, norm_w.reshape(1, -1).astype(f32), ef, eb)


def _filter_kernel(z_ref, w1_ref, b1_ref, wh_ref, bh_ref, freq_ref, wf_ref, wb_ref, delta_ref, ks_ref, kd_ref):
    hp = lambda a, b: jnp.dot(a, b, preferred_element_type=f32, precision=lax.Precision.HIGHEST)
    z = z_ref[...]
    freq = freq_ref[...]
    h = jnp.sin(freq * (hp(z, w1_ref[...]) + b1_ref[...]))
    for n in range(HYENA_INNER):
        h = jnp.sin(freq * (hp(h, wh_ref[n]) + bh_ref[n]))
    window = jnp.exp(-z[:, 0:1] * delta_ref[...])
    hf = hp(h, wf_ref[...]) * window
    hb = hp(h, wb_ref[...]) * window
    row = lax.broadcasted_iota(jnp.int32, hb.shape, 0)
    hb = jnp.where(row == 0, 0.0, hb)
    norm = jnp.sum(jnp.abs(hf), axis=0, keepdims=True) + jnp.sum(jnp.abs(hb), axis=0, keepdims=True)
    ks_ref[...] = (hf + hb) / norm
    kd_ref[...] = (hf - hb) / norm


def hyena_filter_taps(seq, f_w1, f_b1, f_wh, f_bh, f_wout, freq, tc=256):
    fw = HYENA_FILTER_WIDTH
    t = np.arange(seq, dtype=np.float32)
    t_norm = t / np.float32(seq - 1)
    bands = np.linspace(1e-4, HYENA_BANDS - 1, HYENA_BANDS, dtype=np.float32)
    ang = np.float32(2.0 * math.pi / seq) * t[:, None] * bands
    z = np.concatenate([t_norm[:, None], np.cos(ang), -np.sin(ang)], axis=-1).astype(np.float32)
    z = np.pad(z, ((0, 0), (0, LANES - z.shape[1])))
    deltas = np.abs(np.linspace(math.log(HYENA_TARGET) / HYENA_SLOW_DECAY, math.log(HYENA_TARGET) / HYENA_FAST_DECAY,
                                D_MODEL, dtype=np.float32)).reshape(1, -1)
    padw = lambda a, r, c: jnp.pad(a.astype(f32), [(0, 0)] * (a.ndim - 2) + [(0, r - a.shape[-2]), (0, c - a.shape[-1])])
    w1 = padw(f_w1, LANES, LANES)
    wh = padw(f_wh, LANES, LANES)
    wout = padw(f_wout, LANES, 2 * D_MODEL)
    b1 = padw(f_b1.reshape(1, fw), 1, LANES)
    bh = padw(f_bh.reshape(HYENA_INNER, 1, fw), 1, LANES)
    fq = padw(freq.reshape(1, fw), 1, LANES)
    nt = D_MODEL // tc
    const = lambda shape: pl.BlockSpec(shape, lambda j: (0,) * len(shape))
    out = jax.ShapeDtypeStruct((seq, D_MODEL), f32)
    osp = pl.BlockSpec((seq, tc), lambda j: (0, j))
    return pl.pallas_call(
        _filter_kernel,
        grid=(nt,),
        in_specs=[const((seq, LANES)), const((LANES, LANES)), const((1, LANES)),
                  const((HYENA_INNER, LANES, LANES)), const((HYENA_INNER, 1, LANES)), const((1, LANES)),
                  pl.BlockSpec((LANES, tc), lambda j: (0, j)),
                  pl.BlockSpec((LANES, tc), lambda j: (0, j + nt)),
                  pl.BlockSpec((1, tc), lambda j: (0, j))],
        out_specs=[osp, osp],
        out_shape=[out, out],
        compiler_params=_params("parallel"),
        name="hyena_filter",
    )(jnp.asarray(z), w1, b1, wh, bh, fq, wout, wout, jnp.asarray(deltas))


def _dft_tables(seq):
    n = 2 * seq
    f = np.arange(seq, dtype=np.int64)[:, None]
    t = np.arange(seq, dtype=np.int64)[None, :]
    theta = (2.0 * math.pi / (2 * n)) * (((2 * f + 1) * t) % (2 * n)).astype(np.float64)
    c, s = np.cos(theta), np.sin(theta)
    scale = 2.0 / n
    return (jnp.asarray(c, bf16), jnp.asarray(s, bf16),
            jnp.asarray(c.T * scale, bf16), jnp.asarray(s.T * scale, bf16))


def _spectrum_kernel(c_ref, s_ref, ks_ref, kd_ref, kre_ref, kb_ref):
    kre_ref[...] = sum(_dot(c_ref[...], part) for part in _split2(ks_ref[...]))
    kb_ref[...] = sum(_dot(s_ref[...], part) for part in _split2(kd_ref[...]))


def filter_spectrum(cmat, smat, ks, kd, tmf=256, tc=256):
    seq = ks.shape[0]
    fsp = pl.BlockSpec((tmf, seq), lambda j, m: (m, 0))
    ksp = pl.BlockSpec((seq, tc), lambda j, m: (0, j))
    osp = pl.BlockSpec((tmf, tc), lambda j, m: (m, j))
    out = jax.ShapeDtypeStruct((seq, D_MODEL), f32)
    return pl.pallas_call(
        _spectrum_kernel,
        grid=(D_MODEL // tc, seq // tmf),
        in_specs=[fsp, fsp, ksp, ksp],
        out_specs=[osp, osp],
        out_shape=[out, out],
        compiler_params=_params("parallel", "parallel"),
        name="filter_spectrum",
    )(cmat, smat, ks, kd)


def _dft_fwd_kernel(c_ref, s_ref, v_ref, kre_ref, kb_ref, p_ref, q_ref):
    v = v_ref[...]
    vre = _dot(c_ref[...], v)
    va = _dot(s_ref[...], v)
    kre, kb = kre_ref[...], kb_ref[...]
    p_ref[...] = (vre * kre - va * kb).astype(bf16)
    q_ref[...] = (vre * kb + va * kre).astype(bf16)


def dft_forward(cmat, smat, vgh, kre, kb, seq, tmf=256):
    nb = vgh.shape[0] // seq
    nm = seq // tmf
    fsp = pl.BlockSpec((tmf, seq), lambda b, m: (m, 0))
    ksp = pl.BlockSpec((tmf, D_MODEL), lambda b, m: (m, 0))
    osp = pl.BlockSpec((tmf, D_MODEL), lambda b, m: (b * nm + m, 0))
    out = jax.ShapeDtypeStruct((nb * seq, D_MODEL), bf16)
    return pl.pallas_call(
        _dft_fwd_kernel,
        grid=(nb, nm),
        in_specs=[fsp, fsp, pl.BlockSpec((seq, D_MODEL), lambda b, m: (b, 0)), ksp, ksp],
        out_specs=[osp, osp],
        out_shape=[out, out],
        compiler_params=_params("parallel", "parallel"),
        name="dft_forward",
    )(cmat, smat, vgh, kre, kb)


def _dft_inv_kernel(gate_row, ct_ref, st_ref, p_ref, q_ref, vg_ref, x0_ref, bias_ref, w_ref, h_ref, mod_ref, o_ref):
    y = _dot(ct_ref[...], p_ref[...]) + _dot(st_ref[...], q_ref[...])
    y = y + vg_ref[...].astype(f32) * bias_ref[...]
    out = _dot((y * x0_ref[...].astype(f32)).astype(bf16), w_ref[...])
    o_ref[...] = h_ref[...] + mod_ref[0, gate_row:gate_row + 1, :] * out


def dft_inverse_out(ctm, stm, p, q, vg, x0, bias, w_out, h, mods, seq, gate_row, tmt=256):
    nb = h.shape[0] // seq
    nm = seq // tmt
    gsp = pl.BlockSpec((tmt, seq), lambda b, m: (m, 0))
    full = pl.BlockSpec((seq, D_MODEL), lambda b, m: (b, 0))
    tile = pl.BlockSpec((tmt, D_MODEL), lambda b, m: (b * nm + m, 0))
    return pl.pallas_call(
        functools.partial(_dft_inv_kernel, gate_row),
        grid=(nb, nm),
        in_specs=[gsp, gsp, full, full, tile, tile,
                  pl.BlockSpec((1, D_MODEL), lambda b, m: (0, 0)),
                  pl.BlockSpec((D_MODEL, D_MODEL), lambda b, m: (0, 0)),
                  tile, pl.BlockSpec((1, N_MOD, D_MODEL), lambda b, m: (b, 0, 0))],
        out_specs=tile,
        out_shape=jax.ShapeDtypeStruct(h.shape, f32),
        compiler_params=_params("parallel", "parallel"),
        name="dft_inverse_out",
    )(ctm, stm, p, q, vg, x0, bias.reshape(1, -1).astype(f32), w_out, h, mods)


_Q_HEAD_ORDER = (0, 4, 1, 5, 2, 6, 3, 7)


def _mixer_in_weight(w_in):
    c = np.cumsum([ATTN_DIM, KV_DIM, KV_DIM, D_SSM, XBC_DIM])
    q, k, v, z, xbc, dt = (w_in[:, :c[0]], w_in[:, c[0]:c[1]], w_in[:, c[1]:c[2]], w_in[:, c[2]:c[3]],
                           w_in[:, c[3]:c[4]], w_in[:, c[4]:])
    q = q.reshape(D_MODEL, N_Q_HEADS, HEAD_DIM)[:, np.array(_Q_HEAD_ORDER), :].reshape(D_MODEL, ATTN_DIM)
    w = jnp.concatenate([xbc, z, q, k, v], axis=1).astype(bf16)
    return w, jnp.pad(dt, ((0, 0), (0, LANES - dt.shape[1]))).astype(bf16)


def _mixer_out_weights(w_out):
    wa = w_out[:ATTN_DIM].reshape(N_Q_HEADS, HEAD_DIM, D_MODEL)[np.array(_Q_HEAD_ORDER)].reshape(ATTN_DIM, D_MODEL)
    return wa.astype(bf16), w_out[ATTN_DIM:].astype(bf16)


def kernel(x, c, ctx, c_ctx, w_ada, b_ada, norm_g, ffn_w13, ffn_w2, mix_w_in, mix_w_out, q_norm, k_norm, attn_sink,
           ssd_conv_w, ssd_conv_b, ssd_dt_bias, ssd_a_log, ssd_d, ssd_norm, hy_w_in, hy_conv_w, hy_conv_b,
           hy_f_w1, hy_f_b1, hy_f_wh, hy_f_bh, hy_f_wout, hy_freq, hy_bias, hy_w_out):
    nb, seq, _ = x.shape
    ctx_len = ctx.shape[1]
    depth = w_ada.shape[0]
    assert depth == 2, "this kernel is written for the two-layer block"
    h_lat = x.reshape(nb * seq, D_MODEL)
    h_ctx = ctx.reshape(nb * ctx_len, D_MODEL)
    cond = jnp.concatenate([c, c_ctx[None]], axis=0)
    cond = jnp.pad(cond, ((0, -cond.shape[0] % 8), (0, 0)))
    ffn_w = [[_ffn_weights(ffn_w13[i, k], ffn_w2[i, k]) for k in range(2)] for i in range(depth)]

    mods = ada_mods(cond, w_ada[0], b_ada[0])
    m_lat, m_ctx = mods[:nb], mods[nb:nb + 1]
    h_lat = macaron_ffn(h_lat, m_lat, seq, norm_g[0, 0], 0, *ffn_w[0][0])
    h_ctx = macaron_ffn(h_ctx, m_ctx, nb * ctx_len, norm_g[0, 0], 0, *ffn_w[0][0])
    w_in, w_dt = _mixer_in_weight(mix_w_in[0])
    wa, ws = _mixer_out_weights(mix_w_out[0])
    p_lat, dt_lat = adaln_proj(h_lat, m_lat, seq, norm_g[0, 1], 1, w_in, w_dt)
    p_ctx, dt_ctx = adaln_proj(h_ctx, m_ctx, nb * ctx_len, norm_g[0, 1], 1, w_in, w_dt)
    cos, sin = _rope_tables(seq)
    a_lat = window_attention(p_lat, p_ctx, seq, ctx_len, q_norm[0], k_norm[0], attn_sink[0], cos, sin)
    xc_ctx = ssd_conv(p_ctx, ctx_len, ssd_conv_w[0], ssd_conv_b[0])
    xc_lat = ssd_conv(p_lat, seq, ssd_conv_w[0], ssd_conv_b[0])
    hf0, hb0 = ssd_ctx_states(xc_ctx, dt_ctx, ctx_len, ssd_dt_bias[0], ssd_a_log[0])
    s_lat = ssd_latent(xc_lat, p_lat, dt_lat, seq, hf0, hb0, ssd_dt_bias[0], ssd_a_log[0], ssd_d[0], ssd_norm[0])
    h_lat = out_proj_residual([a_lat, s_lat], [wa, ws], h_lat, m_lat, seq, 5)
    h_lat = macaron_ffn(h_lat, m_lat, seq, norm_g[0, 2], 2, *ffn_w[0][1])

    m_lat = ada_mods(cond, w_ada[1], b_ada[1])[:nb]
    h_lat = macaron_ffn(h_lat, m_lat, seq, norm_g[1, 0], 0, *ffn_w[1][0])
    p = adaln_proj(h_lat, m_lat, seq, norm_g[1, 1], 1, hy_w_in[0].astype(bf16))
    x0, vg = hyena_conv(p, seq, hy_conv_w[0], hy_conv_b[0])
    ks, kd = hyena_filter_taps(seq, hy_f_w1[0], hy_f_b1[0], hy_f_wh[0], hy_f_bh[0], hy_f_wout[0], hy_freq[0])
    cm, sm, ctm, stm = _dft_tables(seq)
    kre, kb = filter_spectrum(cm, sm, ks, kd)
    pp, qq = dft_forward(cm, sm, vg, kre, kb, seq)
    h_lat = dft_inverse_out(ctm, stm, pp, qq, vg, x0, hy_bias[0], hy_w_out[0].astype(bf16), h_lat, m_lat, seq, 5)
    h_lat = macaron_ffn(h_lat, m_lat, seq, norm_g[1, 2], 2, *ffn_w[1][1])
    return h_lat.reshape(nb, seq, D_MODEL)
```

```python
import functools
import math

import numpy as np
import jax
import jax.numpy as jnp
from jax import lax
from jax.experimental import pallas as pl
from jax.experimental.pallas import tpu as pltpu

f32 = jnp.float32
bf16 = jnp.bfloat16

D_MODEL = 1024
N_MOD = 9
RMS_EPS = 1e-6
GRID_W = 64

HEAD_DIM = 64
N_Q_HEADS = 8
N_KV_HEADS = 2
ATTN_DIM = N_Q_HEADS * HEAD_DIM
KV_DIM = N_KV_HEADS * HEAD_DIM
WINDOW = 128
ATTN_BLOCK = 128
ROPE_THETA = 10000.0

SSD_HEADS = 16
SSD_HEAD_DIM = 64
D_SSM = SSD_HEADS * SSD_HEAD_DIM
SSD_GROUPS = 2
SSD_STATE = 128
SSD_CONV = 7
SSD_CHUNK = 128
BC_DIM = SSD_GROUPS * SSD_STATE
XBC_DIM = D_SSM + 4 * BC_DIM
GROUP_W = D_SSM // SSD_GROUPS

HYENA_SHORT = 3
HYENA_BANDS = 8
HYENA_FILTER_WIDTH = 64
HYENA_INNER = 2
HYENA_FAST_DECAY = 0.3
HYENA_SLOW_DECAY = 1.5
HYENA_TARGET = 1e-2

D_FF = 2816
FFN_TF = 256
LANES = 128

COL_Z = 0
COL_Q = COL_Z + D_SSM
COL_K = COL_Q + ATTN_DIM
COL_V = COL_K + KV_DIM
REST_COLS = COL_V + KV_DIM
HALO = 16
PROJ_TN = 256

VMEM_LIMIT = 56 * 1024 * 1024


def _params(*sem):
    return pltpu.CompilerParams(dimension_semantics=sem, vmem_limit_bytes=VMEM_LIMIT)


def _dot(a, b):
    return jnp.dot(a, b, preferred_element_type=f32)


def _dot_nt(a, b):
    return lax.dot_general(a, b, (((1,), (1,)), ((), ())), preferred_element_type=f32)


def _split2(x):
    hi = x.astype(bf16)
    lo = (x - hi.astype(f32)).astype(bf16)
    return hi, lo


def _split3(x):
    hi = x.astype(bf16)
    r = x - hi.astype(f32)
    mid = r.astype(bf16)
    lo = (r - mid.astype(f32)).astype(bf16)
    return hi, mid, lo


def _adaln(h, g, shift, scale):
    ms = jnp.mean(h * h, axis=-1, keepdims=True)
    return (h * lax.rsqrt(ms + RMS_EPS) * g) * (1.0 + scale) + shift


def _silu(x):
    return x * jax.nn.sigmoid(x)


def _mods_kernel(c_ref, w_ref, b_ref, o_ref):
    o_ref[...] = _dot(_silu(c_ref[...]).astype(bf16), w_ref[...].astype(bf16)) + b_ref[...]


def ada_mods(cond, w, b):
    r = cond.shape[0]
    tn = 1024
    out = pl.pallas_call(
        _mods_kernel,
        grid=(w.shape[1] // tn,),
        in_specs=[pl.BlockSpec((r, D_MODEL), lambda j: (0, 0)),
                  pl.BlockSpec((D_MODEL, tn), lambda j: (0, j)),
                  pl.BlockSpec((1, tn), lambda j: (0, j))],
        out_specs=pl.BlockSpec((r, tn), lambda j: (0, j)),
        out_shape=jax.ShapeDtypeStruct((r, w.shape[1]), f32),
        compiler_params=_params("parallel"),
        name="ada_mods",
    )(cond, w, b.reshape(1, -1))
    return out.reshape(r, N_MOD, D_MODEL)


def _ffn_kernel(s, nf, h_ref, mod_ref, g_ref, w13_ref, w2_ref, o_ref):
    h = h_ref[...]
    u = _adaln(h, g_ref[...], mod_ref[0, 3 * s:3 * s + 1, :], mod_ref[0, 3 * s + 1:3 * s + 2, :]).astype(bf16)
    acc = None
    for j in range(nf):
        a = _dot(u, w13_ref[j])
        b = _dot(u, w13_ref[nf + j])
        part = _dot((_silu(a) * b).astype(bf16), w2_ref[j])
        acc = part if acc is None else acc + part
    o_ref[...] = h + 0.5 * mod_ref[0, 3 * s + 2:3 * s + 3, :] * acc


def _ffn_weights(w13, w2, tf=FFN_TF):
    w13c = w13.astype(bf16).reshape(D_MODEL, 2 * D_FF // tf, tf).transpose(1, 0, 2)
    return w13c, w2.astype(bf16).reshape(D_FF // tf, tf, D_MODEL)


def macaron_ffn(h, mods, rows_per_mod, g, s, w13c, w2c, tm=512):
    m = h.shape[0]
    tm = min(tm, rows_per_mod)
    nf = w2c.shape[0]
    tiles_per_mod = rows_per_mod // tm
    return pl.pallas_call(
        functools.partial(_ffn_kernel, s, nf),
        grid=(m // tm,),
        in_specs=[pl.BlockSpec((tm, D_MODEL), lambda i: (i, 0)),
                  pl.BlockSpec((1, N_MOD, D_MODEL), lambda i: (i // tiles_per_mod, 0, 0)),
                  pl.BlockSpec((1, D_MODEL), lambda i: (0, 0)),
                  pl.BlockSpec(w13c.shape, lambda i: (0, 0, 0)),
                  pl.BlockSpec(w2c.shape, lambda i: (0, 0, 0))],
        out_specs=pl.BlockSpec((tm, D_MODEL), lambda i: (i, 0)),
        out_shape=jax.ShapeDtypeStruct((m, D_MODEL), f32),
        compiler_params=_params("parallel"),
        name="macaron_ffn",
    )(h, mods, g.reshape(1, -1), w13c, w2c)


def _tile_adaln(s, tiles_per_seq, h_ref, hp_ref, hn_ref, mod_ref, g_ref):
    shift, scale = mod_ref[0, 3 * s:3 * s + 1, :], mod_ref[0, 3 * s + 1:3 * s + 2, :]
    g = g_ref[...]
    u = _adaln(h_ref[...], g, shift, scale).astype(bf16)
    uh = _adaln(jnp.concatenate([hp_ref[...], hn_ref[...]], axis=0), g, shift, scale).astype(bf16)
    t = pl.program_id(0) % tiles_per_seq
    row = lax.broadcasted_iota(jnp.int32, (2 * HALO, 1), 0)
    keep_prev = jnp.where(t > 0, 1.0, 0.0)
    keep_next = jnp.where(t < tiles_per_seq - 1, 1.0, 0.0)
    return u, uh, jnp.where(row < HALO, keep_prev, keep_next)


def _conv_cols(pad_ref, u, uh, keep, w, conv_w, conv_b, emit):
    tm = u.shape[0]
    halo = _dot(uh, w) * keep
    pad_ref[0:HALO, :] = halo[0:HALO, :]
    pad_ref[HALO:HALO + tm, :] = _dot(u, w)
    pad_ref[HALO + tm:2 * HALO + tm, :] = halo[HALO:2 * HALO, :]
    width = conv_w.shape[0]
    rows = min(tm, 256)
    for r0 in range(0, tm, rows):
        base = HALO + r0 - width // 2
        acc = conv_b + conv_w[0:1, :] * pad_ref[base:base + rows, :]
        for k in range(1, width):
            acc += conv_w[k:k + 1, :] * pad_ref[base + k:base + k + rows, :]
        emit(r0, rows, acc)


def _mixer_proj_kernel(s, tiles_per_seq, h_ref, hp_ref, hn_ref, mod_ref, g_ref, w_ref, wdt_ref, cw_ref, cb_ref,
                       xc_ref, rest_ref, dt_ref, pad_ref):
    u, uh, keep = _tile_adaln(s, tiles_per_seq, h_ref, hp_ref, hn_ref, mod_ref, g_ref)
    tn = PROJ_TN
    for c0 in range(0, XBC_DIM, tn):
        def emit(r0, rows, acc, c0=c0):
            xc_ref[r0:r0 + rows, c0:c0 + tn] = _silu(acc).astype(bf16)

        _conv_cols(pad_ref, u, uh, keep, w_ref[:, c0:c0 + tn], cw_ref[:, c0:c0 + tn], cb_ref[:, c0:c0 + tn], emit)
    for c0 in range(0, REST_COLS, tn):
        rest_ref[:, c0:c0 + tn] = _dot(u, w_ref[:, XBC_DIM + c0:XBC_DIM + c0 + tn]).astype(bf16)
    dt_ref[...] = _dot(u, wdt_ref[...])


def _hyena_proj_kernel(s, tiles_per_seq, h_ref, hp_ref, hn_ref, mod_ref, g_ref, w_ref, cw_ref, cb_ref,
                       x0_ref, vg_ref, pad_ref, x1_ref):
    u, uh, keep = _tile_adaln(s, tiles_per_seq, h_ref, hp_ref, hn_ref, mod_ref, g_ref)
    tn = PROJ_TN
    for c0 in range(0, D_MODEL, tn):
        def emit_x0(r0, rows, acc, c0=c0):
            x0_ref[r0:r0 + rows, c0:c0 + tn] = acc.astype(bf16)

        def emit_x1(r0, rows, acc):
            x1_ref[r0:r0 + rows, :] = acc

        def emit_v(r0, rows, acc, c0=c0):
            vg_ref[r0:r0 + rows, c0:c0 + tn] = (acc * x1_ref[r0:r0 + rows, :]).astype(bf16)

        for off, emit in ((0, emit_x0), (D_MODEL, emit_x1), (2 * D_MODEL, emit_v)):
            cols = slice(off + c0, off + c0 + tn)
            _conv_cols(pad_ref, u, uh, keep, w_ref[:, cols], cw_ref[:, cols], cb_ref[:, cols], emit)


def _proj_call(body, name, h, mods, rows_per_mod, seq, g, consts, out_widths, out_dtypes, scratch, tm=512):
    m = h.shape[0]
    tm = min(tm, seq)
    tiles_per_seq = seq // tm
    tiles_per_mod = rows_per_mod // tm
    hb = tm // HALO
    in_specs = [pl.BlockSpec((tm, D_MODEL), lambda i: (i, 0)),
                pl.BlockSpec((HALO, D_MODEL), lambda i: (jnp.maximum(i * hb - 1, 0), 0)),
                pl.BlockSpec((HALO, D_MODEL), lambda i: (jnp.minimum((i + 1) * hb, m // HALO - 1), 0)),
                pl.BlockSpec((1, N_MOD, D_MODEL), lambda i: (i // tiles_per_mod, 0, 0)),
                pl.BlockSpec((1, D_MODEL), lambda i: (0, 0))]
    in_specs += [pl.BlockSpec(a.shape, lambda i: (0, 0)) for a in consts]
    return pl.pallas_call(
        functools.partial(body, 1, tiles_per_seq),
        grid=(m // tm,),
        in_specs=in_specs,
        out_specs=[pl.BlockSpec((tm, n), lambda i: (i, 0)) for n in out_widths],
        out_shape=[jax.ShapeDtypeStruct((m, n), dt) for n, dt in zip(out_widths, out_dtypes)],
        scratch_shapes=[pltpu.VMEM((tm + 2 * HALO, PROJ_TN), f32)] + [pltpu.VMEM((tm, PROJ_TN), f32)] * scratch,
        compiler_params=_params("parallel"),
        name=name,
    )(h, h, h, mods, g.reshape(1, -1), *consts)


def mixer_proj(h, mods, rows_per_mod, seq, g, w, w_dt, conv_w, conv_b):
    consts = [w, w_dt, conv_w.astype(f32), conv_b.reshape(1, -1).astype(f32)]
    return _proj_call(_mixer_proj_kernel, "mixer_proj", h, mods, rows_per_mod, seq, g, consts,
                      [XBC_DIM, REST_COLS, LANES], [bf16, bf16, f32], 0)


def hyena_proj(h, mods, rows_per_mod, seq, g, w, conv_w, conv_b):
    consts = [w, conv_w.astype(f32), conv_b.reshape(1, -1).astype(f32)]
    return _proj_call(_hyena_proj_kernel, "hyena_proj", h, mods, rows_per_mod, seq, g, consts,
                      [D_MODEL, D_MODEL], [bf16, bf16], 1)


def _outproj_kernel(n_in, gate_row, *refs):
    x_refs = refs[:n_in]
    w_refs = refs[n_in:2 * n_in]
    h_ref, mod_ref, o_ref = refs[2 * n_in:]
    acc = _dot(x_refs[0][...].astype(bf16), w_refs[0][...])
    for x_ref, w_ref in zip(x_refs[1:], w_refs[1:]):
        acc += _dot(x_ref[...].astype(bf16), w_ref[...])
    o_ref[...] = h_ref[...] + mod_ref[0, gate_row:gate_row + 1, :] * acc


def out_proj_residual(xs, ws, h, mods, rows_per_mod, gate_row, tm=512):
    m = h.shape[0]
    tm = min(tm, rows_per_mod)
    tiles_per_mod = rows_per_mod // tm
    n_in = len(xs)
    in_specs = ([pl.BlockSpec((tm, x.shape[1]), lambda i: (i, 0)) for x in xs]
                + [pl.BlockSpec(w.shape, lambda i: (0, 0)) for w in ws]
                + [pl.BlockSpec((tm, D_MODEL), lambda i: (i, 0)),
                   pl.BlockSpec((1, N_MOD, D_MODEL), lambda i: (i // tiles_per_mod, 0, 0))])
    return pl.pallas_call(
        functools.partial(_outproj_kernel, n_in, gate_row),
        grid=(m // tm,),
        in_specs=in_specs,
        out_specs=pl.BlockSpec((tm, D_MODEL), lambda i: (i, 0)),
        out_shape=jax.ShapeDtypeStruct((m, D_MODEL), f32),
        compiler_params=_params("parallel"),
        name="out_proj_residual",
    )(*xs, *ws, h, mods)


def _head_norm(x, gain, bd):
    hi, lo = _split2(x * x)
    ms = _dot(hi, bd) + _dot(lo, bd)
    return x * lax.rsqrt(ms + RMS_EPS) * gain


def _rope(x, cos, sin_signed):
    lane = lax.broadcasted_iota(jnp.int32, x.shape, 1)
    partner = jnp.where((lane & 16) != 0, pltpu.roll(x, 16, 1), pltpu.roll(x, LANES - 16, 1))
    return x * cos + partner * sin_signed


def _t_bf16(x):
    return x.astype(f32).T.astype(bf16)


def _attn_kernel(seq, q_ref, k_ref, v_ref, kc_ref, vc_ref, qg_ref, kg_ref, cos_ref, sin_ref, bd_ref,
                 sink_ref, o_ref, qt_s, k_s, vt_s, kc_s, vct_s):
    j = pl.program_id(1)
    nblk = seq // ATTN_BLOCK
    nslab = ATTN_DIM // LANES
    blk = ATTN_BLOCK

    @pl.when(j == 0)
    def _():
        bd = bd_ref[...]
        cos, sin = cos_ref[...], sin_ref[...]
        scale = HEAD_DIM ** -0.5
        for p in range(nslab):
            qn = _head_norm(q_ref[:, p * LANES:(p + 1) * LANES].astype(f32), qg_ref[...], bd)
            qr = _rope(qn, cos, sin) * scale
            for jb in range(nblk):
                qt_s[jb, p * LANES:(p + 1) * LANES, :] = _t_bf16(qr[jb * blk:(jb + 1) * blk, :])
        kn = _head_norm(k_ref[...].astype(f32), kg_ref[...], bd)
        zeros = jnp.zeros((WINDOW, KV_DIM), bf16)
        k_s[0:WINDOW, :] = zeros
        k_s[WINDOW + seq:2 * WINDOW + seq, :] = zeros
        k_s[WINDOW:WINDOW + seq, :] = _rope(kn, cos, sin).astype(bf16)
        vt_s[0] = zeros
        vt_s[nblk + 1] = zeros
        for jb in range(nblk):
            vt_s[jb + 1] = _t_bf16(v_ref[jb * blk:(jb + 1) * blk, :])
        kc_s[...] = _head_norm(kc_ref[...].astype(f32), kg_ref[...], bd).astype(bf16)
        vct_s[...] = _t_bf16(vc_ref[...])

    band = blk + 2 * WINDOW
    start = pl.multiple_of(j * blk, blk)
    kb = k_s[pl.ds(start, band), :]
    kc = kc_s[...]
    vtb = jnp.concatenate([vt_s[j], vt_s[j + 1], vt_s[j + 2]], axis=1)
    vtc = vct_s[...]
    qt = qt_s[j]
    key = lax.broadcasted_iota(jnp.int32, (band, 2 * blk), 0)
    qry = lax.broadcasted_iota(jnp.int32, (band, 2 * blk), 1) & (blk - 1)
    kpos = start - WINDOW + key
    ok = (jnp.abs(qry - (key - WINDOW)) <= WINDOW) & (kpos >= 0) & (kpos < seq)
    dim = lax.broadcasted_iota(jnp.int32, (LANES, blk), 0)
    lane2 = lax.broadcasted_iota(jnp.int32, (1, 2 * blk), 1)
    for p in range(nslab):
        qslab = qt[p * LANES:(p + 1) * LANES, :]
        zero = jnp.zeros_like(qslab)
        rhs = jnp.concatenate([jnp.where(dim < HEAD_DIM, qslab, zero), jnp.where(dim >= HEAD_DIM, qslab, zero)], axis=1)
        sb = jnp.where(ok, _dot(kb, rhs), -jnp.inf)
        sc = _dot(kc, rhs)
        sink = jnp.where(lane2 < blk, sink_ref[p], sink_ref[p + N_Q_HEADS // N_KV_HEADS])
        mx = jnp.maximum(jnp.maximum(jnp.max(sb, axis=0, keepdims=True), jnp.max(sc, axis=0, keepdims=True)), sink)
        pb = jnp.exp(sb - mx)
        pc = jnp.exp(sc - mx)
        den = jnp.sum(pb, axis=0, keepdims=True) + jnp.sum(pc, axis=0, keepdims=True) + jnp.exp(sink - mx)
        ot = (_dot(vtb, pb.astype(bf16)) + _dot(vtc, pc.astype(bf16))) / den
        both = jnp.where(dim < HEAD_DIM, ot[:, 0:blk], ot[:, blk:2 * blk])
        o_ref[:, p * LANES:(p + 1) * LANES] = both.T.astype(bf16)


def window_attention(proj_lat, proj_ctx, seq, ctx_len, q_gain, k_gain, sink, rope_cos, rope_sin):
    nb = proj_lat.shape[0] // seq
    nblk = seq // ATTN_BLOCK
    bd = np.kron(np.eye(LANES // HEAD_DIM), np.ones((HEAD_DIM, HEAD_DIM))) / HEAD_DIM
    gain2 = lambda g: jnp.tile(g, LANES // HEAD_DIM).reshape(1, LANES)
    const = lambda shape: pl.BlockSpec(shape, lambda b, j: (0, 0))
    return pl.pallas_call(
        functools.partial(_attn_kernel, seq),
        grid=(nb, nblk),
        in_specs=[pl.BlockSpec((seq, ATTN_DIM), lambda b, j: (b, COL_Q // ATTN_DIM)),
                  pl.BlockSpec((seq, KV_DIM), lambda b, j: (b, COL_K // KV_DIM)),
                  pl.BlockSpec((seq, KV_DIM), lambda b, j: (b, COL_V // KV_DIM)),
                  pl.BlockSpec((ctx_len, KV_DIM), lambda b, j: (b, COL_K // KV_DIM)),
                  pl.BlockSpec((ctx_len, KV_DIM), lambda b, j: (b, COL_V // KV_DIM)),
                  const((1, LANES)), const((1, LANES)),
                  const((seq, LANES)), const((seq, LANES)), const((LANES, LANES)),
                  pl.BlockSpec(memory_space=pltpu.SMEM)],
        out_specs=pl.BlockSpec((ATTN_BLOCK, ATTN_DIM), lambda b, j: (b * nblk + j, 0)),
        out_shape=jax.ShapeDtypeStruct((nb * seq, ATTN_DIM), bf16),
        scratch_shapes=[pltpu.VMEM((nblk, ATTN_DIM, ATTN_BLOCK), bf16),
                        pltpu.VMEM((seq + 2 * WINDOW, KV_DIM), bf16),
                        pltpu.VMEM((nblk + 2, KV_DIM, ATTN_BLOCK), bf16),
                        pltpu.VMEM((ctx_len, KV_DIM), bf16),
                        pltpu.VMEM((KV_DIM, ctx_len), bf16)],
        compiler_params=_params("parallel", "arbitrary"),
        name="window_attention",
    )(proj_lat, proj_lat, proj_lat, proj_ctx, proj_ctx, gain2(q_gain), gain2(k_gain),
      rope_cos, rope_sin, jnp.asarray(bd, bf16), sink)


def _rope_tables(seq):
    t = np.arange(seq)
    pos = np.stack([t // GRID_W, t % GRID_W], axis=1).astype(np.float32)
    axis_dim = HEAD_DIM // 2
    inv = (ROPE_THETA ** (-np.arange(0, axis_dim, 2, dtype=np.float32) / axis_dim)).astype(np.float32)
    lane = np.arange(LANES)
    d = lane % HEAD_DIM
    which = d // axis_dim
    ang = (pos[:, which] * inv[d % (axis_dim // 2)][None, :]).astype(np.float32)
    sign = np.where((d % axis_dim) < axis_dim // 2, -1.0, 1.0)
    return jnp.asarray(np.cos(ang), f32), jnp.asarray(np.sin(ang) * sign, f32)


def _softplus(x):
    return jnp.maximum(x, 0.0) + jnp.log1p(jnp.exp(-jnp.abs(x)))


def _expand_heads(v, e):
    return _dot(v.astype(bf16), e)


def _ssd_chunk(rev, lane0, want_y, x, bm, cm, dt_raw, dt_bias, a_neg, expand, state_ref):
    t = x.shape[0]
    dt = _softplus(dt_raw + dt_bias)
    a = dt * a_neg
    r = lax.broadcasted_iota(jnp.int32, (t, t), 0)
    c = lax.broadcasted_iota(jnp.int32, (t, t), 1)
    keep = (r <= c) if rev else (r >= c)
    tri = jnp.where(keep, 1.0, 0.0).astype(bf16)
    cs = sum(_dot(tri, part) for part in _split3(a))
    last = cs[0:1, :] if rev else cs[t - 1:t, :]
    e = jnp.exp(cs)
    w = dt * jnp.exp(last - cs)
    e_x = _expand_heads(e, expand)
    w_x = _expand_heads(w, expand)
    elast_x = e_x[0:1, :] if rev else e_x[t - 1:t, :]

    y = None
    if want_y:
        cs_t = cs.T
        dt_t = dt.T
        lane = lax.broadcasted_iota(jnp.int32, (t, LANES), 1)
        cb = [_dot_nt(cm[:, g * SSD_STATE:(g + 1) * SSD_STATE],
                      bm[:, g * SSD_STATE:(g + 1) * SSD_STATE]) for g in range(SSD_GROUPS)]
        pieces = []
        for p in range(SSD_HEADS // 2):
            xp = x[:, p * LANES:(p + 1) * LANES]
            ms = []
            for q in range(2):
                h = 2 * p + q
                g = h // (SSD_HEADS // SSD_GROUPS)
                seg = cs[:, lane0 + h:lane0 + h + 1] - cs_t[lane0 + h:lane0 + h + 1, :]
                dec = jnp.exp(jnp.where(keep, seg, -jnp.inf))
                ms.append((cb[g] * dec * dt_t[lane0 + h:lane0 + h + 1, :]).astype(bf16))
            zero = jnp.zeros_like(xp)
            xcat = jnp.concatenate([jnp.where(lane < SSD_HEAD_DIM, xp, zero),
                                    jnp.where(lane >= SSD_HEAD_DIM, xp, zero)], axis=0)
            pieces.append(_dot(jnp.concatenate(ms, axis=1), xcat))
        y = jnp.concatenate(pieces, axis=1)

    inter = []
    for g in range(SSD_GROUPS):
        gs = slice(g * GROUP_W, (g + 1) * GROUP_W)
        ss = slice(g * SSD_STATE, (g + 1) * SSD_STATE)
        h_t = state_ref[g]
        if want_y:
            inter.append(_dot(cm[:, ss], h_t.astype(bf16)) * e_x[:, gs])
        xw = (x[:, gs].astype(f32) * w_x[:, gs]).astype(bf16)
        state_ref[g] = h_t * elast_x[:, gs] + _dot(bm[:, ss].astype(f32).T.astype(bf16), xw)
    if want_y:
        y = y + jnp.concatenate(inter, axis=1)
    return y


def _ssd_ctx_kernel(nchunk, x_ref, bc_ref, dt_ref, bias_ref, alog_ref, ef_ref, eb_ref, hf_ref, hb_ref, sf, sb):
    sf[...] = jnp.zeros_like(sf)
    sb[...] = jnp.zeros_like(sb)
    a_neg = -jnp.exp(alog_ref[...])
    bias = bias_ref[...]
    t = SSD_CHUNK
    for ci in range(nchunk):
        rows = slice(ci * t, (ci + 1) * t)
        _ssd_chunk(False, 0, False, x_ref[rows, :], bc_ref[rows, 0:BC_DIM], None, dt_ref[rows, :],
                   bias, a_neg, ef_ref[...], sf)
        rows = slice((nchunk - 1 - ci) * t, (nchunk - ci) * t)
        _ssd_chunk(True, SSD_HEADS, False, x_ref[rows, :], bc_ref[rows, BC_DIM:2 * BC_DIM], None, dt_ref[rows, :],
                   bias, a_neg, eb_ref[...], sb)
    hf_ref[0] = sf[...]
    hb_ref[0] = sb[...]


def _head_expanders():
    ef = np.zeros((LANES, D_SSM), np.float32)
    eb = np.zeros((LANES, D_SSM), np.float32)
    for h in range(SSD_HEADS):
        ef[h, h * SSD_HEAD_DIM:(h + 1) * SSD_HEAD_DIM] = 1.0
        eb[SSD_HEADS + h, h * SSD_HEAD_DIM:(h + 1) * SSD_HEAD_DIM] = 1.0
    return jnp.asarray(ef, bf16), jnp.asarray(eb, bf16)


def _pad_lanes(v):
    v = v.reshape(1, -1).astype(f32)
    return jnp.pad(v, ((0, 0), (0, LANES - v.shape[1])))


def ssd_ctx_states(xc, dt_raw, seq, dt_bias, a_log):
    nb = xc.shape[0] // seq
    ef, eb = _head_expanders()
    const = lambda shape: pl.BlockSpec(shape, lambda b: (0,) * len(shape))
    st = jax.ShapeDtypeStruct((nb, SSD_GROUPS, SSD_STATE, GROUP_W), f32)
    st_spec = pl.BlockSpec((1, SSD_GROUPS, SSD_STATE, GROUP_W), lambda b: (b, 0, 0, 0))
    return pl.pallas_call(
        functools.partial(_ssd_ctx_kernel, seq // SSD_CHUNK),
        grid=(nb,),
        in_specs=[pl.BlockSpec((seq, D_SSM), lambda b: (b, 0)),
                  pl.BlockSpec((seq, 2 * BC_DIM), lambda b: (b, D_SSM // (2 * BC_DIM))),
                  pl.BlockSpec((seq, LANES), lambda b: (b, 0)),
                  const((1, LANES)), const((1, LANES)), const((LANES, D_SSM)), const((LANES, D_SSM))],
        out_specs=[st_spec, st_spec],
        out_shape=[st, st],
        scratch_shapes=[pltpu.VMEM((SSD_GROUPS, SSD_STATE, GROUP_W), f32),
                        pltpu.VMEM((SSD_GROUPS, SSD_STATE, GROUP_W), f32)],
        compiler_params=_params("parallel"),
        name="ssd_ctx_states",
    )(xc, xc, dt_raw, _pad_lanes(dt_bias), _pad_lanes(a_log), ef, eb)


def _ssd_lat_kernel(nchunk, xf_ref, xb_ref, bf_ref, bb_ref, cf_ref, cb_ref, dtf_ref, dtb_ref, zf_ref, zb_ref,
                    hf0_ref, hb0_ref, bias_ref, alog_ref, dskip_ref, normw_ref, ef_ref, eb_ref,
                    o_ref, sf, sb, yacc):
    c = pl.program_id(1)
    t = SSD_CHUNK

    @pl.when(c == 0)
    def _():
        sf[...] = hf0_ref[0]
        sb[...] = hb0_ref[0]

    a_neg = -jnp.exp(alog_ref[...])
    bias = bias_ref[...]
    xf = xf_ref[...]
    yf = _ssd_chunk(False, 0, True, xf, bf_ref[...], cf_ref[...], dtf_ref[...], bias, a_neg, ef_ref[...], sf)
    yf = yf + dskip_ref[...] * xf.astype(f32)
    yb = _ssd_chunk(True, SSD_HEADS, True, xb_ref[...], bb_ref[...], cb_ref[...], dtb_ref[...], bias, a_neg,
                    eb_ref[...], sb)
    rows_f = pl.ds(pl.multiple_of(c * t, t), t)
    rows_b = pl.ds(pl.multiple_of((nchunk - 1 - c) * t, t), t)

    @pl.when(c < nchunk // 2)
    def _():
        yacc[rows_f, :] = yf
        yacc[rows_b, :] = yb

    def finish(y, z):
        y = y * _silu(z.astype(f32))
        outs = []
        for g in range(SSD_GROUPS):
            yg = y[:, g * GROUP_W:(g + 1) * GROUP_W]
            outs.append(yg * lax.rsqrt(jnp.mean(yg * yg, axis=-1, keepdims=True) + RMS_EPS))
        return (jnp.concatenate(outs, axis=1) * normw_ref[...]).astype(bf16)

    @pl.when(c >= nchunk // 2)
    def _():
        o_ref[rows_f, :] = finish(yacc[rows_f, :] + yf, zf_ref[...])
        o_ref[rows_b, :] = finish(yacc[rows_b, :] + yb, zb_ref[...])


def ssd_latent(xc, proj, dt_raw, seq, hf0, hb0, dt_bias, a_log, d_skip, norm_w):
    nb = xc.shape[0] // seq
    nc = seq // SSD_CHUNK
    half = nc // 2
    ef, eb = _head_expanders()
    t = SSD_CHUNK
    fwd = lambda b, c: b * nc + c
    bwd = lambda b, c: b * nc + nc - 1 - c
    zfw = lambda b, c: b * nc + jnp.maximum(c, half)
    zbw = lambda b, c: b * nc + jnp.minimum(nc - 1 - c, half - 1)
    bc0 = D_SSM // SSD_STATE // SSD_GROUPS
    const = lambda shape: pl.BlockSpec(shape, lambda b, c: (0,) * len(shape))
    st_spec = pl.BlockSpec((1, SSD_GROUPS, SSD_STATE, GROUP_W), lambda b, c: (b, 0, 0, 0))
    dskip = jnp.repeat(d_skip.astype(f32), SSD_HEAD_DIM).reshape(1, D_SSM)
    return pl.pallas_call(
        functools.partial(_ssd_lat_kernel, nc),
        grid=(nb, nc),
        in_specs=[pl.BlockSpec((t, D_SSM), lambda b, c: (fwd(b, c), 0)),
                  pl.BlockSpec((t, D_SSM), lambda b, c: (bwd(b, c), 0)),
                  pl.BlockSpec((t, BC_DIM), lambda b, c: (fwd(b, c), bc0)),
                  pl.BlockSpec((t, BC_DIM), lambda b, c: (bwd(b, c), bc0 + 1)),
                  pl.BlockSpec((t, BC_DIM), lambda b, c: (fwd(b, c), bc0 + 2)),
                  pl.BlockSpec((t, BC_DIM), lambda b, c: (bwd(b, c), bc0 + 3)),
                  pl.BlockSpec((t, LANES), lambda b, c: (fwd(b, c), 0)),
                  pl.BlockSpec((t, LANES), lambda b, c: (bwd(b, c), 0)),
                  pl.BlockSpec((t, D_SSM), lambda b, c: (zfw(b, c), COL_Z // D_SSM)),
                  pl.BlockSpec((t, D_SSM), lambda b, c: (zbw(b, c), COL_Z // D_SSM)),
                  st_spec, st_spec,
                  const((1, LANES)), const((1, LANES)), const((1, D_SSM)), const((1, D_SSM)),
                  const((LANES, D_SSM)), const((LANES, D_SSM))],
        out_specs=pl.BlockSpec((seq, D_SSM), lambda b, c: (b, 0)),
        out_shape=jax.ShapeDtypeStruct((nb * seq, D_SSM), bf16),
        scratch_shapes=[pltpu.VMEM((SSD_GROUPS, SSD_STATE, GROUP_W), f32),
                        pltpu.VMEM((SSD_GROUPS, SSD_STATE, GROUP_W), f32),
                        pltpu.VMEM((seq, D_SSM), f32)],
        compiler_params=_params("parallel", "arbitrary"),
        name="ssd_latent",
    )(xc, xc, xc, xc, xc, xc, dt_raw, dt_raw, proj, proj, hf0, hb0,
      _pad_lanes(dt_bias), _pad_lanes(a_log), dskip, norm_w.reshape(1, -1).astype(f32), ef, eb)


def _filter_kernel(z_ref, w1_ref, b1_ref, wh_ref, bh_ref, freq_ref, wf_ref, wb_ref, delta_ref, ks_ref, kd_ref):
    hp = lambda a, b: jnp.dot(a, b, preferred_element_type=f32, precision=lax.Precision.HIGHEST)
    z = z_ref[...]
    freq = freq_ref[...]
    h = jnp.sin(freq * (hp(z, w1_ref[...]) + b1_ref[...]))
    for n in range(HYENA_INNER):
        h = jnp.sin(freq * (hp(h, wh_ref[n]) + bh_ref[n]))
    window = jnp.exp(-z[:, 0:1] * delta_ref[...])
    hf = hp(h, wf_ref[...]) * window
    hb = hp(h, wb_ref[...]) * window
    row = lax.broadcasted_iota(jnp.int32, hb.shape, 0)
    hb = jnp.where(row == 0, 0.0, hb)
    norm = jnp.sum(jnp.abs(hf), axis=0, keepdims=True) + jnp.sum(jnp.abs(hb), axis=0, keepdims=True)
    ks_ref[...] = (hf + hb) / norm
    kd_ref[...] = (hf - hb) / norm


def hyena_filter_taps(seq, f_w1, f_b1, f_wh, f_bh, f_wout, freq, tc=256):
    fw = HYENA_FILTER_WIDTH
    t = np.arange(seq, dtype=np.float32)
    t_norm = t / np.float32(seq - 1)
    bands = np.linspace(1e-4, HYENA_BANDS - 1, HYENA_BANDS, dtype=np.float32)
    ang = np.float32(2.0 * math.pi / seq) * t[:, None] * bands
    z = np.concatenate([t_norm[:, None], np.cos(ang), -np.sin(ang)], axis=-1).astype(np.float32)
    z = np.pad(z, ((0, 0), (0, LANES - z.shape[1])))
    deltas = np.abs(np.linspace(math.log(HYENA_TARGET) / HYENA_SLOW_DECAY, math.log(HYENA_TARGET) / HYENA_FAST_DECAY,
                                D_MODEL, dtype=np.float32)).reshape(1, -1)
    padw = lambda a, r, c: jnp.pad(a.astype(f32), [(0, 0)] * (a.ndim - 2) + [(0, r - a.shape[-2]), (0, c - a.shape[-1])])
    w1 = padw(f_w1, LANES, LANES)
    wh = padw(f_wh, LANES, LANES)
    wout = padw(f_wout, LANES, 2 * D_MODEL)
    b1 = padw(f_b1.reshape(1, fw), 1, LANES)
    bh = padw(f_bh.reshape(HYENA_INNER, 1, fw), 1, LANES)
    fq = padw(freq.reshape(1, fw), 1, LANES)
    nt = D_MODEL // tc
    const = lambda shape: pl.BlockSpec(shape, lambda j: (0,) * len(shape))
    out = jax.ShapeDtypeStruct((seq, D_MODEL), f32)
    osp = pl.BlockSpec((seq, tc), lambda j: (0, j))
    return pl.pallas_call(
        _filter_kernel,
        grid=(nt,),
        in_specs=[const((seq, LANES)), const((LANES, LANES)), const((1, LANES)),
                  const((HYENA_INNER, LANES, LANES)), const((HYENA_INNER, 1, LANES)), const((1, LANES)),
                  pl.BlockSpec((LANES, tc), lambda j: (0, j)),
                  pl.BlockSpec((LANES, tc), lambda j: (0, j + nt)),
                  pl.BlockSpec((1, tc), lambda j: (0, j))],
        out_specs=[osp, osp],
        out_shape=[out, out],
        compiler_params=_params("parallel"),
        name="hyena_filter",
    )(jnp.asarray(z), w1, b1, wh, bh, fq, wout, wout, jnp.asarray(deltas))


def _dft_tables(seq):
    n = 2 * seq
    f = np.arange(seq, dtype=np.int64)[:, None]
    t = np.arange(seq, dtype=np.int64)[None, :]
    theta = (2.0 * math.pi / (2 * n)) * (((2 * f + 1) * t) % (2 * n)).astype(np.float64)
    c, s = np.cos(theta), np.sin(theta)
    scale = 2.0 / n
    return (jnp.asarray(c, bf16), jnp.asarray(s, bf16),
            jnp.asarray(c.T * scale, bf16), jnp.asarray(s.T * scale, bf16))


def _spectrum_kernel(c_ref, s_ref, ks_ref, kd_ref, kre_ref, kb_ref):
    kre_ref[...] = sum(_dot(c_ref[...], part) for part in _split2(ks_ref[...]))
    kb_ref[...] = sum(_dot(s_ref[...], part) for part in _split2(kd_ref[...]))


def filter_spectrum(cmat, smat, ks, kd, tmf=256, tc=256):
    seq = ks.shape[0]
    fsp = pl.BlockSpec((tmf, seq), lambda j, m: (m, 0))
    ksp = pl.BlockSpec((seq, tc), lambda j, m: (0, j))
    osp = pl.BlockSpec((tmf, tc), lambda j, m: (m, j))
    out = jax.ShapeDtypeStruct((seq, D_MODEL), f32)
    return pl.pallas_call(
        _spectrum_kernel,
        grid=(D_MODEL // tc, seq // tmf),
        in_specs=[fsp, fsp, ksp, ksp],
        out_specs=[osp, osp],
        out_shape=[out, out],
        compiler_params=_params("parallel", "parallel"),
        name="filter_spectrum",
    )(cmat, smat, ks, kd)


def _dft_fwd_kernel(c_ref, s_ref, v_ref, kre_ref, kb_ref, p_ref, q_ref):
    v = v_ref[...]
    vre = _dot(c_ref[...], v)
    va = _dot(s_ref[...], v)
    kre, kb = kre_ref[...], kb_ref[...]
    p_ref[...] = (vre * kre - va * kb).astype(bf16)
    q_ref[...] = (vre * kb + va * kre).astype(bf16)


def dft_forward(cmat, smat, vgh, kre, kb, seq, tmf=256):
    nb = vgh.shape[0] // seq
    nm = seq // tmf
    fsp = pl.BlockSpec((tmf, seq), lambda b, m: (m, 0))
    ksp = pl.BlockSpec((tmf, D_MODEL), lambda b, m: (m, 0))
    osp = pl.BlockSpec((tmf, D_MODEL), lambda b, m: (b * nm + m, 0))
    out = jax.ShapeDtypeStruct((nb * seq, D_MODEL), bf16)
    return pl.pallas_call(
        _dft_fwd_kernel,
        grid=(nb, nm),
        in_specs=[fsp, fsp, pl.BlockSpec((seq, D_MODEL), lambda b, m: (b, 0)), ksp, ksp],
        out_specs=[osp, osp],
        out_shape=[out, out],
        compiler_params=_params("parallel", "parallel"),
        name="dft_forward",
    )(cmat, smat, vgh, kre, kb)


def _dft_inv_kernel(gate_row, ct_ref, st_ref, p_ref, q_ref, vg_ref, x0_ref, bias_ref, w_ref, h_ref, mod_ref, o_ref):
    y = _dot(ct_ref[...], p_ref[...]) + _dot(st_ref[...], q_ref[...])
    y = y + vg_ref[...].astype(f32) * bias_ref[...]
    out = _dot((y * x0_ref[...].astype(f32)).astype(bf16), w_ref[...])
    o_ref[...] = h_ref[...] + mod_ref[0, gate_row:gate_row + 1, :] * out


def dft_inverse_out(ctm, stm, p, q, vg, x0, bias, w_out, h, mods, seq, gate_row, tmt=256):
    nb = h.shape[0] // seq
    nm = seq // tmt
    gsp = pl.BlockSpec((tmt, seq), lambda b, m: (m, 0))
    full = pl.BlockSpec((seq, D_MODEL), lambda b, m: (b, 0))
    tile = pl.BlockSpec((tmt, D_MODEL), lambda b, m: (b * nm + m, 0))
    return pl.pallas_call(
        functools.partial(_dft_inv_kernel, gate_row),
        grid=(nb, nm),
        in_specs=[gsp, gsp, full, full, tile, tile,
                  pl.BlockSpec((1, D_MODEL), lambda b, m: (0, 0)),
                  pl.BlockSpec((D_MODEL, D_MODEL), lambda b, m: (0, 0)),
                  tile, pl.BlockSpec((1, N_MOD, D_MODEL), lambda b, m: (b, 0, 0))],
        out_specs=tile,
        out_shape=jax.ShapeDtypeStruct(h.shape, f32),
        compiler_params=_params("parallel", "parallel"),
        name="dft_inverse_out",
    )(ctm, stm, p, q, vg, x0, bias.reshape(1, -1).astype(f32), w_out, h, mods)


_Q_HEAD_ORDER = (0, 4, 1, 5, 2, 6, 3, 7)


def _mixer_in_weight(w_in):
    c = np.cumsum([ATTN_DIM, KV_DIM, KV_DIM, D_SSM, XBC_DIM])
    q, k, v, z, xbc, dt = (w_in[:, :c[0]], w_in[:, c[0]:c[1]], w_in[:, c[1]:c[2]], w_in[:, c[2]:c[3]],
                           w_in[:, c[3]:c[4]], w_in[:, c[4]:])
    q = q.reshape(D_MODEL, N_Q_HEADS, HEAD_DIM)[:, np.array(_Q_HEAD_ORDER), :].reshape(D_MODEL, ATTN_DIM)
    w = jnp.concatenate([xbc, z, q, k, v], axis=1).astype(bf16)
    return w, jnp.pad(dt, ((0, 0), (0, LANES - dt.shape[1]))).astype(bf16)


def _mixer_out_weights(w_out):
    wa = w_out[:ATTN_DIM].reshape(N_Q_HEADS, HEAD_DIM, D_MODEL)[np.array(_Q_HEAD_ORDER)].reshape(ATTN_DIM, D_MODEL)
    return wa.astype(bf16), w_out[ATTN_DIM:].astype(bf16)


def kernel(x, c, ctx, c_ctx, w_ada, b_ada, norm_g, ffn_w13, ffn_w2, mix_w_in, mix_w_out, q_norm, k_norm, attn_sink,
           ssd_conv_w, ssd_conv_b, ssd_dt_bias, ssd_a_log, ssd_d, ssd_norm, hy_w_in, hy_conv_w, hy_conv_b,
           hy_f_w1, hy_f_b1, hy_f_wh, hy_f_bh, hy_f_wout, hy_freq, hy_bias, hy_w_out):
    nb, seq, _ = x.shape
    ctx_len = ctx.shape[1]
    depth = w_ada.shape[0]
    assert depth == 2, "this kernel is written for the two-layer block"
    h_lat = x.reshape(nb * seq, D_MODEL)
    h_ctx = ctx.reshape(nb * ctx_len, D_MODEL)
    cond = jnp.concatenate([c, c_ctx[None]], axis=0)
    cond = jnp.pad(cond, ((0, -cond.shape[0] % 8), (0, 0)))
    ffn_w = [[_ffn_weights(ffn_w13[i, k], ffn_w2[i, k]) for k in range(2)] for i in range(depth)]

    mods = ada_mods(cond, w_ada[0], b_ada[0])
    m_lat, m_ctx = mods[:nb], mods[nb:nb + 1]
    h_lat = macaron_ffn(h_lat, m_lat, seq, norm_g[0, 0], 0, *ffn_w[0][0])
    h_ctx = macaron_ffn(h_ctx, m_ctx, nb * ctx_len, norm_g[0, 0], 0, *ffn_w[0][0])
    w_in, w_dt = _mixer_in_weight(mix_w_in[0])
    wa, ws = _mixer_out_weights(mix_w_out[0])
    xc_lat, p_lat, dt_lat = mixer_proj(h_lat, m_lat, seq, seq, norm_g[0, 1], w_in, w_dt, ssd_conv_w[0], ssd_conv_b[0])
    xc_ctx, p_ctx, dt_ctx = mixer_proj(h_ctx, m_ctx, nb * ctx_len, ctx_len, norm_g[0, 1], w_in, w_dt,
                                       ssd_conv_w[0], ssd_conv_b[0])
    cos, sin = _rope_tables(seq)
    a_lat = window_attention(p_lat, p_ctx, seq, ctx_len, q_norm[0], k_norm[0], attn_sink[0], cos, sin)
    hf0, hb0 = ssd_ctx_states(xc_ctx, dt_ctx, ctx_len, ssd_dt_bias[0], ssd_a_log[0])
    s_lat = ssd_latent(xc_lat, p_lat, dt_lat, seq, hf0, hb0, ssd_dt_bias[0], ssd_a_log[0], ssd_d[0], ssd_norm[0])
    h_lat = out_proj_residual([a_lat, s_lat], [wa, ws], h_lat, m_lat, seq, 5)
    h_lat = macaron_ffn(h_lat, m_lat, seq, norm_g[0, 2], 2, *ffn_w[0][1])

    m_lat = ada_mods(cond, w_ada[1], b_ada[1])[:nb]
    h_lat = macaron_ffn(h_lat, m_lat, seq, norm_g[1, 0], 0, *ffn_w[1][0])
    x0, vg = hyena_proj(h_lat, m_lat, seq, seq, norm_g[1, 1], hy_w_in[0].astype(bf16), hy_conv_w[0], hy_conv_b[0])
    ks, kd = hyena_filter_taps(seq, hy_f_w1[0], hy_f_b1[0], hy_f_wh[0], hy_f_bh[0], hy_f_wout[0], hy_freq[0])
    cm, sm, ctm, stm = _dft_tables(seq)
    kre, kb = filter_spectrum(cm, sm, ks, kd)
    pp, qq = dft_forward(cm, sm, vg, kre, kb, seq)
    h_lat = dft_inverse_out(ctm, stm, pp, qq, vg, x0, hy_bias[0], hy_w_out[0].astype(bf16), h_lat, m_lat, seq, 5)
    h_lat = macaron_ffn(h_lat, m_lat, seq, norm_g[1, 2], 2, *ffn_w[1][1])
    return h_lat.reshape(nb, seq, D_MODEL)
```

```python
import functools
import math

import numpy as np
import jax
import jax.numpy as jnp
from jax import lax
from jax.experimental import pallas as pl
from jax.experimental.pallas import tpu as pltpu

f32 = jnp.float32
bf16 = jnp.bfloat16

D_MODEL = 1024
N_MOD = 9
RMS_EPS = 1e-6
GRID_W = 64

HEAD_DIM = 64
N_Q_HEADS = 8
N_KV_HEADS = 2
ATTN_DIM = N_Q_HEADS * HEAD_DIM
KV_DIM = N_KV_HEADS * HEAD_DIM
WINDOW = 128
ATTN_BLOCK = 128
ROPE_THETA = 10000.0

SSD_HEADS = 16
SSD_HEAD_DIM = 64
D_SSM = SSD_HEADS * SSD_HEAD_DIM
SSD_GROUPS = 2
SSD_STATE = 128
SSD_CONV = 7
SSD_CHUNK = 128
BC_DIM = SSD_GROUPS * SSD_STATE
XBC_DIM = D_SSM + 4 * BC_DIM
GROUP_W = D_SSM // SSD_GROUPS

HYENA_SHORT = 3
HYENA_BANDS = 8
HYENA_FILTER_WIDTH = 64
HYENA_INNER = 2
HYENA_FAST_DECAY = 0.3
HYENA_SLOW_DECAY = 1.5
HYENA_TARGET = 1e-2

D_FF = 2816
FFN_TF = 256
LANES = 128

COL_Z = 0
COL_Q = COL_Z + D_SSM
COL_K = COL_Q + ATTN_DIM
COL_V = COL_K + KV_DIM
REST_COLS = COL_V + KV_DIM
HALO = 16
PROJ_TN = 256
CONV_ROWS = 32

VMEM_LIMIT = 56 * 1024 * 1024


def _params(*sem):
    return pltpu.CompilerParams(dimension_semantics=sem, vmem_limit_bytes=VMEM_LIMIT)


def _dot(a, b):
    return jnp.dot(a, b, preferred_element_type=f32)


def _dot_nt(a, b):
    return lax.dot_general(a, b, (((1,), (1,)), ((), ())), preferred_element_type=f32)


def _split2(x):
    hi = x.astype(bf16)
    lo = (x - hi.astype(f32)).astype(bf16)
    return hi, lo


def _split3(x):
    hi = x.astype(bf16)
    r = x - hi.astype(f32)
    mid = r.astype(bf16)
    lo = (r - mid.astype(f32)).astype(bf16)
    return hi, mid, lo


def _adaln(h, g, shift, scale):
    ms = jnp.mean(h * h, axis=-1, keepdims=True)
    return (h * lax.rsqrt(ms + RMS_EPS) * g) * (1.0 + scale) + shift


def _silu(x):
    return x * jax.nn.sigmoid(x)


def _mods_kernel(c_ref, w_ref, b_ref, o_ref):
    o_ref[...] = _dot(_silu(c_ref[...]).astype(bf16), w_ref[...].astype(bf16)) + b_ref[...]


def ada_mods(cond, w, b):
    r = cond.shape[0]
    tn = 1024
    out = pl.pallas_call(
        _mods_kernel,
        grid=(w.shape[1] // tn,),
        in_specs=[pl.BlockSpec((r, D_MODEL), lambda j: (0, 0)),
                  pl.BlockSpec((D_MODEL, tn), lambda j: (0, j)),
                  pl.BlockSpec((1, tn), lambda j: (0, j))],
        out_specs=pl.BlockSpec((r, tn), lambda j: (0, j)),
        out_shape=jax.ShapeDtypeStruct((r, w.shape[1]), f32),
        compiler_params=_params("parallel"),
        name="ada_mods",
    )(cond, w, b.reshape(1, -1))
    return out.reshape(r, N_MOD, D_MODEL)


def _ffn_kernel(s, nf, h_ref, mod_ref, g_ref, w13_ref, w2_ref, o_ref):
    h = h_ref[...]
    u = _adaln(h, g_ref[...], mod_ref[0, 3 * s:3 * s + 1, :], mod_ref[0, 3 * s + 1:3 * s + 2, :]).astype(bf16)
    acc = None
    tf = w2_ref.shape[1]
    for j in range(nf):
        a = _dot(u, w13_ref[:, j * tf:(j + 1) * tf])
        b = _dot(u, w13_ref[:, D_FF + j * tf:D_FF + (j + 1) * tf])
        part = _dot((_silu(a) * b).astype(bf16), w2_ref[j])
        acc = part if acc is None else acc + part
    o_ref[...] = h + 0.5 * mod_ref[0, 3 * s + 2:3 * s + 3, :] * acc


def _ffn_weights(w13, w2, tf=FFN_TF):
    return w13.astype(bf16), w2.astype(bf16).reshape(D_FF // tf, tf, D_MODEL)


def macaron_ffn(h, mods, rows_per_mod, g, s, w13c, w2c, tm=512):
    m = h.shape[0]
    tm = min(tm, rows_per_mod)
    nf = w2c.shape[0]
    tiles_per_mod = rows_per_mod // tm
    return pl.pallas_call(
        functools.partial(_ffn_kernel, s, nf),
        grid=(m // tm,),
        in_specs=[pl.BlockSpec((tm, D_MODEL), lambda i: (i, 0)),
                  pl.BlockSpec((1, N_MOD, D_MODEL), lambda i: (i // tiles_per_mod, 0, 0)),
                  pl.BlockSpec((1, D_MODEL), lambda i: (0, 0)),
                  pl.BlockSpec(w13c.shape, lambda i: (0, 0)),
                  pl.BlockSpec(w2c.shape, lambda i: (0, 0, 0))],
        out_specs=pl.BlockSpec((tm, D_MODEL), lambda i: (i, 0)),
        out_shape=jax.ShapeDtypeStruct((m, D_MODEL), f32),
        compiler_params=_params("parallel"),
        name="macaron_ffn",
    )(h, mods, g.reshape(1, -1), w13c, w2c)


def _tile_adaln(s, tiles_per_seq, h_ref, hp_ref, hn_ref, mod_ref, g_ref, u_ref, uh_ref):
    shift, scale = mod_ref[0, 3 * s:3 * s + 1, :], mod_ref[0, 3 * s + 1:3 * s + 2, :]
    g = g_ref[...]
    u_ref[...] = _adaln(h_ref[...], g, shift, scale).astype(bf16)
    uh_ref[...] = _adaln(jnp.concatenate([hp_ref[...], hn_ref[...]], axis=0), g, shift, scale).astype(bf16)
    t = pl.program_id(0) % tiles_per_seq
    row = lax.broadcasted_iota(jnp.int32, (2 * HALO, 1), 0)
    keep_prev = jnp.where(t > 0, 1.0, 0.0)
    keep_next = jnp.where(t < tiles_per_seq - 1, 1.0, 0.0)
    return jnp.where(row < HALO, keep_prev, keep_next)


def _project_padded(pad_ref, u_ref, uh_ref, keep, w):
    tm = u_ref.shape[0]
    halo = _dot(uh_ref[...], w) * keep
    pad_ref[0:HALO, :] = halo[0:HALO, :]
    pad_ref[HALO:HALO + tm, :] = _dot(u_ref[...], w)
    pad_ref[HALO + tm:2 * HALO + tm, :] = halo[HALO:2 * HALO, :]


def _conv_padded(pad_ref, conv_w, conv_b, emit):
    tm = pad_ref.shape[0] - 2 * HALO
    width = conv_w.shape[0]
    rows = CONV_ROWS
    for l0 in range(0, pad_ref.shape[1], LANES):
        lanes = slice(l0, l0 + LANES)
        taps = [conv_w[k:k + 1, lanes] for k in range(width)]
        bias = conv_b[:, lanes]
        half = width // 2
        for r0 in range(0, tm, rows):
            win = pad_ref[HALO + r0 - 8:HALO + r0 + rows + 8, lanes]
            acc = bias + taps[half] * win[8:8 + rows, :]
            for k in range(width):
                if k != half:
                    acc += taps[k] * pltpu.roll(win, (half - k) % (rows + 16), 0)[8:8 + rows, :]
            emit(r0, rows, l0, acc)


def _mixer_proj_kernel(s, tiles_per_seq, h_ref, hp_ref, hn_ref, mod_ref, g_ref, w_ref, wdt_ref, cw_ref, cb_ref,
                       xc_ref, rest_ref, dt_ref, pad_ref, u_ref, uh_ref):
    keep = _tile_adaln(s, tiles_per_seq, h_ref, hp_ref, hn_ref, mod_ref, g_ref, u_ref, uh_ref)
    tn = PROJ_TN
    n_conv = XBC_DIM // tn
    n_rest = REST_COLS // tn

    def project(c):
        _project_padded(pad_ref.at[c % 2], u_ref, uh_ref, keep, w_ref[:, c * tn:(c + 1) * tn])

    def plain(c):
        if c < n_rest:
            rest_ref[:, c * tn:(c + 1) * tn] = _dot(u_ref[...], w_ref[:, XBC_DIM + c * tn:XBC_DIM + (c + 1) * tn]).astype(bf16)
        elif c == n_rest:
            dt_ref[...] = _dot(u_ref[...], wdt_ref[...])

    project(0)
    for c in range(n_conv):
        if c + 1 < n_conv:
            project(c + 1)
        plain(c)

        def emit(r0, rows, l0, acc, c0=c * tn):
            xc_ref[r0:r0 + rows, c0 + l0:c0 + l0 + LANES] = _silu(acc).astype(bf16)

        _conv_padded(pad_ref.at[c % 2], cw_ref[:, c * tn:(c + 1) * tn], cb_ref[:, c * tn:(c + 1) * tn], emit)
    for c in range(n_conv, n_rest + 1):
        plain(c)


def _hyena_proj_kernel(s, tiles_per_seq, h_ref, hp_ref, hn_ref, mod_ref, g_ref, w_ref, cw_ref, cb_ref,
                       x0_ref, vg_ref, pad_ref, u_ref, uh_ref, x1_ref):
    keep = _tile_adaln(s, tiles_per_seq, h_ref, hp_ref, hn_ref, mod_ref, g_ref, u_ref, uh_ref)
    tn = PROJ_TN
    cols = [part * D_MODEL + c0 for c0 in range(0, D_MODEL, tn) for part in range(3)]

    def project(i):
        _project_padded(pad_ref.at[i % 2], u_ref, uh_ref, keep, w_ref[:, cols[i]:cols[i] + tn])

    project(0)
    for i, col in enumerate(cols):
        if i + 1 < len(cols):
            project(i + 1)
        c0 = col % D_MODEL

        def emit_x0(r0, rows, l0, acc, c0=c0):
            x0_ref[r0:r0 + rows, c0 + l0:c0 + l0 + LANES] = acc.astype(bf16)

        def emit_x1(r0, rows, l0, acc):
            x1_ref[r0:r0 + rows, l0:l0 + LANES] = acc

        def emit_v(r0, rows, l0, acc, c0=c0):
            vg = acc * x1_ref[r0:r0 + rows, l0:l0 + LANES]
            vg_ref[r0:r0 + rows, c0 + l0:c0 + l0 + LANES] = vg.astype(bf16)

        emit = (emit_x0, emit_x1, emit_v)[col // D_MODEL]
        _conv_padded(pad_ref.at[i % 2], cw_ref[:, col:col + tn], cb_ref[:, col:col + tn], emit)


def _proj_call(body, name, h, mods, rows_per_mod, seq, g, consts, out_widths, out_dtypes, scratch, tm=512):
    m = h.shape[0]
    tm = min(tm, seq)
    tiles_per_seq = seq // tm
    tiles_per_mod = rows_per_mod // tm
    hb = tm // HALO
    in_specs = [pl.BlockSpec((tm, D_MODEL), lambda i: (i, 0)),
                pl.BlockSpec((HALO, D_MODEL), lambda i: (jnp.maximum(i * hb - 1, 0), 0)),
                pl.BlockSpec((HALO, D_MODEL), lambda i: (jnp.minimum((i + 1) * hb, m // HALO - 1), 0)),
                pl.BlockSpec((1, N_MOD, D_MODEL), lambda i: (i // tiles_per_mod, 0, 0)),
                pl.BlockSpec((1, D_MODEL), lambda i: (0, 0))]
    in_specs += [pl.BlockSpec(a.shape, lambda i: (0, 0)) for a in consts]
    return pl.pallas_call(
        functools.partial(body, 1, tiles_per_seq),
        grid=(m // tm,),
        in_specs=in_specs,
        out_specs=[pl.BlockSpec((tm, n), lambda i: (i, 0)) for n in out_widths],
        out_shape=[jax.ShapeDtypeStruct((m, n), dt) for n, dt in zip(out_widths, out_dtypes)],
        scratch_shapes=[pltpu.VMEM((2, tm + 2 * HALO, PROJ_TN), f32), pltpu.VMEM((tm, D_MODEL), bf16),
                        pltpu.VMEM((2 * HALO, D_MODEL), bf16)] + [pltpu.VMEM((tm, PROJ_TN), f32)] * scratch,
        compiler_params=_params("parallel"),
        name=name,
    )(h, h, h, mods, g.reshape(1, -1), *consts)


def mixer_proj(h, mods, rows_per_mod, seq, g, w, w_dt, conv_w, conv_b):
    consts = [w, w_dt, conv_w.astype(f32), conv_b.reshape(1, -1).astype(f32)]
    return _proj_call(_mixer_proj_kernel, "mixer_proj", h, mods, rows_per_mod, seq, g, consts,
                      [XBC_DIM, REST_COLS, LANES], [bf16, bf16, f32], 0)


def hyena_proj(h, mods, rows_per_mod, seq, g, w, conv_w, conv_b):
    consts = [w, conv_w.astype(f32), conv_b.reshape(1, -1).astype(f32)]
    return _proj_call(_hyena_proj_kernel, "hyena_proj", h, mods, rows_per_mod, seq, g, consts,
                      [D_MODEL, D_MODEL], [bf16, bf16], 1)


def _outproj_kernel(n_in, gate_row, *refs):
    x_refs = refs[:n_in]
    w_refs = refs[n_in:2 * n_in]
    h_ref, mod_ref, o_ref = refs[2 * n_in:]
    acc = _dot(x_refs[0][...].astype(bf16), w_refs[0][...])
    for x_ref, w_ref in zip(x_refs[1:], w_refs[1:]):
        acc += _dot(x_ref[...].astype(bf16), w_ref[...])
    o_ref[...] = h_ref[...] + mod_ref[0, gate_row:gate_row + 1, :] * acc


def out_proj_residual(xs, ws, h, mods, rows_per_mod, gate_row, tm=512):
    m = h.shape[0]
    tm = min(tm, rows_per_mod)
    tiles_per_mod = rows_per_mod // tm
    n_in = len(xs)
    in_specs = ([pl.BlockSpec((tm, x.shape[1]), lambda i: (i, 0)) for x in xs]
                + [pl.BlockSpec(w.shape, lambda i: (0, 0)) for w in ws]
                + [pl.BlockSpec((tm, D_MODEL), lambda i: (i, 0)),
                   pl.BlockSpec((1, N_MOD, D_MODEL), lambda i: (i // tiles_per_mod, 0, 0))])
    return pl.pallas_call(
        functools.partial(_outproj_kernel, n_in, gate_row),
        grid=(m // tm,),
        in_specs=in_specs,
        out_specs=pl.BlockSpec((tm, D_MODEL), lambda i: (i, 0)),
        out_shape=jax.ShapeDtypeStruct((m, D_MODEL), f32),
        compiler_params=_params("parallel"),
        name="out_proj_residual",
    )(*xs, *ws, h, mods)


def _head_norm(x, gain, bd):
    hi, lo = _split2(x * x)
    ms = _dot(hi, bd) + _dot(lo, bd)
    return x * lax.rsqrt(ms + RMS_EPS) * gain


def _rope(x, cos, sin_signed):
    lane = lax.broadcasted_iota(jnp.int32, x.shape, 1)
    partner = jnp.where((lane & 16) != 0, pltpu.roll(x, 16, 1), pltpu.roll(x, LANES - 16, 1))
    return x * cos + partner * sin_signed


def _t_bf16(x):
    return x.astype(f32).T.astype(bf16)


def _attn_kernel(seq, q_ref, k_ref, v_ref, kc_ref, vc_ref, qg_ref, kg_ref, cos_ref, sin_ref, bd_ref,
                 sink_ref, o_ref, qt_s, k_s, vt_s, kc_s, vct_s):
    j = pl.program_id(1)
    nblk = seq // ATTN_BLOCK
    nslab = ATTN_DIM // LANES
    blk = ATTN_BLOCK

    @pl.when(j == 0)
    def _():
        bd = bd_ref[...]
        cos, sin = cos_ref[...], sin_ref[...]
        scale = HEAD_DIM ** -0.5
        for p in range(nslab):
            qn = _head_norm(q_ref[:, p * LANES:(p + 1) * LANES].astype(f32), qg_ref[...], bd)
            qr = _rope(qn, cos, sin) * scale
            for jb in range(nblk):
                qt_s[jb, p * LANES:(p + 1) * LANES, :] = _t_bf16(qr[jb * blk:(jb + 1) * blk, :])
        kn = _head_norm(k_ref[...].astype(f32), kg_ref[...], bd)
        zeros = jnp.zeros((WINDOW, KV_DIM), bf16)
        k_s[0:WINDOW, :] = zeros
        k_s[WINDOW + seq:2 * WINDOW + seq, :] = zeros
        k_s[WINDOW:WINDOW + seq, :] = _rope(kn, cos, sin).astype(bf16)
        vt_s[0] = zeros
        vt_s[nblk + 1] = zeros
        for jb in range(nblk):
            vt_s[jb + 1] = _t_bf16(v_ref[jb * blk:(jb + 1) * blk, :])
        kc_s[...] = _head_norm(kc_ref[...].astype(f32), kg_ref[...], bd).astype(bf16)
        vct_s[...] = _t_bf16(vc_ref[...])

    band = blk + 2 * WINDOW
    start = pl.multiple_of(j * blk, blk)
    kb = k_s[pl.ds(start, band), :]
    kc = kc_s[...]
    vtb = jnp.concatenate([vt_s[j], vt_s[j + 1], vt_s[j + 2]], axis=1)
    vtc = vct_s[...]
    qt = qt_s[j]
    key = lax.broadcasted_iota(jnp.int32, (blk, 2 * blk), 0)
    qry = lax.broadcasted_iota(jnp.int32, (blk, 2 * blk), 1) & (blk - 1)
    ok_lo = (jnp.abs(qry - (key - WINDOW)) <= WINDOW) & (start - WINDOW + key >= 0)
    ok_hi = (jnp.abs(qry - (key + blk)) <= WINDOW) & (start + blk + key < seq)
    dim = lax.broadcasted_iota(jnp.int32, (LANES, blk), 0)
    lane2 = lax.broadcasted_iota(jnp.int32, (1, 2 * blk), 1)
    ones_b = jnp.ones((16, band), bf16)
    ones_c = jnp.ones((16, kc.shape[0]), bf16)
    for p in range(nslab):
        qslab = qt[p * LANES:(p + 1) * LANES, :]
        zero = jnp.zeros_like(qslab)
        rhs = jnp.concatenate([jnp.where(dim < HEAD_DIM, qslab, zero), jnp.where(dim >= HEAD_DIM, qslab, zero)], axis=1)
        sb = _dot(kb, rhs)
        s_lo = jnp.where(ok_lo, sb[0:blk, :], -jnp.inf)
        s_mid = sb[blk:2 * blk, :]
        s_hi = jnp.where(ok_hi, sb[2 * blk:3 * blk, :], -jnp.inf)
        sc = _dot(kc, rhs)
        sink = jnp.where(lane2 < blk, sink_ref[p], sink_ref[p + N_Q_HEADS // N_KV_HEADS])
        colmax = lambda s: jnp.max(s, axis=0, keepdims=True)
        mx = jnp.maximum(jnp.maximum(jnp.maximum(colmax(s_lo), colmax(s_mid)), jnp.maximum(colmax(s_hi), colmax(sc))),
                         sink)
        pb = jnp.concatenate([jnp.exp(s_lo - mx).astype(bf16), jnp.exp(s_mid - mx).astype(bf16),
                              jnp.exp(s_hi - mx).astype(bf16)], axis=0)
        pc = jnp.exp(sc - mx).astype(bf16)
        den = (_dot(ones_b, pb) + _dot(ones_c, pc))[0:1, :] + jnp.exp(sink - mx)
        ot = (_dot(vtb, pb) + _dot(vtc, pc)) / den
        both = jnp.where(dim < HEAD_DIM, ot[:, 0:blk], ot[:, blk:2 * blk])
        o_ref[:, p * LANES:(p + 1) * LANES] = both.T.astype(bf16)


def window_attention(proj_lat, proj_ctx, seq, ctx_len, q_gain, k_gain, sink, rope_cos, rope_sin):
    nb = proj_lat.shape[0] // seq
    nblk = seq // ATTN_BLOCK
    bd = np.kron(np.eye(LANES // HEAD_DIM), np.ones((HEAD_DIM, HEAD_DIM))) / HEAD_DIM
    gain2 = lambda g: jnp.tile(g, LANES // HEAD_DIM).reshape(1, LANES)
    const = lambda shape: pl.BlockSpec(shape, lambda b, j: (0, 0))
    return pl.pallas_call(
        functools.partial(_attn_kernel, seq),
        grid=(nb, nblk),
        in_specs=[pl.BlockSpec((seq, ATTN_DIM), lambda b, j: (b, COL_Q // ATTN_DIM)),
                  pl.BlockSpec((seq, KV_DIM), lambda b, j: (b, COL_K // KV_DIM)),
                  pl.BlockSpec((seq, KV_DIM), lambda b, j: (b, COL_V // KV_DIM)),
                  pl.BlockSpec((ctx_len, KV_DIM), lambda b, j: (b, COL_K // KV_DIM)),
                  pl.BlockSpec((ctx_len, KV_DIM), lambda b, j: (b, COL_V // KV_DIM)),
                  const((1, LANES)), const((1, LANES)),
                  const((seq, LANES)), const((seq, LANES)), const((LANES, LANES)),
                  pl.BlockSpec(memory_space=pltpu.SMEM)],
        out_specs=pl.BlockSpec((ATTN_BLOCK, ATTN_DIM), lambda b, j: (b * nblk + j, 0)),
        out_shape=jax.ShapeDtypeStruct((nb * seq, ATTN_DIM), bf16),
        scratch_shapes=[pltpu.VMEM((nblk, ATTN_DIM, ATTN_BLOCK), bf16),
                        pltpu.VMEM((seq + 2 * WINDOW, KV_DIM), bf16),
                        pltpu.VMEM((nblk + 2, KV_DIM, ATTN_BLOCK), bf16),
                        pltpu.VMEM((ctx_len, KV_DIM), bf16),
                        pltpu.VMEM((KV_DIM, ctx_len), bf16)],
        compiler_params=_params("parallel", "arbitrary"),
        name="window_attention",
    )(proj_lat, proj_lat, proj_lat, proj_ctx, proj_ctx, gain2(q_gain), gain2(k_gain),
      rope_cos, rope_sin, jnp.asarray(bd, bf16), sink)


def _rope_tables(seq):
    t = np.arange(seq)
    pos = np.stack([t // GRID_W, t % GRID_W], axis=1).astype(np.float32)
    axis_dim = HEAD_DIM // 2
    inv = (ROPE_THETA ** (-np.arange(0, axis_dim, 2, dtype=np.float32) / axis_dim)).astype(np.float32)
    lane = np.arange(LANES)
    d = lane % HEAD_DIM
    which = d // axis_dim
    ang = (pos[:, which] * inv[d % (axis_dim // 2)][None, :]).astype(np.float32)
    sign = np.where((d % axis_dim) < axis_dim // 2, -1.0, 1.0)
    return jnp.asarray(np.cos(ang), f32), jnp.asarray(np.sin(ang) * sign, f32)


def _softplus(x):
    return jnp.maximum(x, 0.0) + jnp.log1p(jnp.exp(-jnp.abs(x)))


def _expand_heads(v, e):
    return _dot(v.astype(bf16), e)


def _ssd_chunk(rev, lane0, want_y, x, bm, cm, dt_raw, dt_bias, a_neg, expand, state_ref):
    t = x.shape[0]
    dt = _softplus(dt_raw + dt_bias)
    a = dt * a_neg
    r = lax.broadcasted_iota(jnp.int32, (t, t), 0)
    c = lax.broadcasted_iota(jnp.int32, (t, t), 1)
    keep = (r <= c) if rev else (r >= c)
    tri = jnp.where(keep, 1.0, 0.0).astype(bf16)
    cs = sum(_dot(tri, part) for part in _split3(a))
    last = cs[0:1, :] if rev else cs[t - 1:t, :]
    e = jnp.exp(cs)
    w = dt * jnp.exp(last - cs)
    e_x = _expand_heads(e, expand)
    w_x = _expand_heads(w, expand)
    elast_x = e_x[0:1, :] if rev else e_x[t - 1:t, :]

    y = None
    if want_y:
        cs_t = cs.T
        dt_t = dt.T
        lane = lax.broadcasted_iota(jnp.int32, (t, LANES), 1)
        cb = [_dot_nt(cm[:, g * SSD_STATE:(g + 1) * SSD_STATE],
                      bm[:, g * SSD_STATE:(g + 1) * SSD_STATE]) for g in range(SSD_GROUPS)]
        pieces = []
        for p in range(SSD_HEADS // 2):
            xp = x[:, p * LANES:(p + 1) * LANES]
            ms = []
            for q in range(2):
                h = 2 * p + q
                g = h // (SSD_HEADS // SSD_GROUPS)
                seg = cs[:, lane0 + h:lane0 + h + 1] - cs_t[lane0 + h:lane0 + h + 1, :]
                dec = jnp.exp(jnp.where(keep, seg, -jnp.inf))
                ms.append((cb[g] * dec * dt_t[lane0 + h:lane0 + h + 1, :]).astype(bf16))
            zero = jnp.zeros_like(xp)
            xcat = jnp.concatenate([jnp.where(lane < SSD_HEAD_DIM, xp, zero),
                                    jnp.where(lane >= SSD_HEAD_DIM, xp, zero)], axis=0)
            pieces.append(_dot(jnp.concatenate(ms, axis=1), xcat))
        y = jnp.concatenate(pieces, axis=1)

    inter = []
    for g in range(SSD_GROUPS):
        gs = slice(g * GROUP_W, (g + 1) * GROUP_W)
        ss = slice(g * SSD_STATE, (g + 1) * SSD_STATE)
        h_t = state_ref[g]
        if want_y:
            inter.append(_dot(cm[:, ss], h_t.astype(bf16)) * e_x[:, gs])
        xw = (x[:, gs].astype(f32) * w_x[:, gs]).astype(bf16)
        state_ref[g] = h_t * elast_x[:, gs] + _dot(bm[:, ss].astype(f32).T.astype(bf16), xw)
    if want_y:
        y = y + jnp.concatenate(inter, axis=1)
    return y


def _ssd_ctx_kernel(nchunk, x_ref, bc_ref, dt_ref, bias_ref, alog_ref, ef_ref, eb_ref, hf_ref, hb_ref, sf, sb):
    sf[...] = jnp.zeros_like(sf)
    sb[...] = jnp.zeros_like(sb)
    a_neg = -jnp.exp(alog_ref[...])
    bias = bias_ref[...]
    t = SSD_CHUNK
    for ci in range(nchunk):
        rows = slice(ci * t, (ci + 1) * t)
        _ssd_chunk(False, 0, False, x_ref[rows, :], bc_ref[rows, 0:BC_DIM], None, dt_ref[rows, :],
                   bias, a_neg, ef_ref[...], sf)
        rows = slice((nchunk - 1 - ci) * t, (nchunk - ci) * t)
        _ssd_chunk(True, SSD_HEADS, False, x_ref[rows, :], bc_ref[rows, BC_DIM:2 * BC_DIM], None, dt_ref[rows, :],
                   bias, a_neg, eb_ref[...], sb)
    hf_ref[0] = sf[...]
    hb_ref[0] = sb[...]


def _head_expanders():
    ef = np.zeros((LANES, D_SSM), np.float32)
    eb = np.zeros((LANES, D_SSM), np.float32)
    for h in range(SSD_HEADS):
        ef[h, h * SSD_HEAD_DIM:(h + 1) * SSD_HEAD_DIM] = 1.0
        eb[SSD_HEADS + h, h * SSD_HEAD_DIM:(h + 1) * SSD_HEAD_DIM] = 1.0
    return jnp.asarray(ef, bf16), jnp.asarray(eb, bf16)


def _pad_lanes(v):
    v = v.reshape(1, -1).astype(f32)
    return jnp.pad(v, ((0, 0), (0, LANES - v.shape[1])))


def ssd_ctx_states(xc, dt_raw, seq, dt_bias, a_log):
    nb = xc.shape[0] // seq
    ef, eb = _head_expanders()
    const = lambda shape: pl.BlockSpec(shape, lambda b: (0,) * len(shape))
    st = jax.ShapeDtypeStruct((nb, SSD_GROUPS, SSD_STATE, GROUP_W), f32)
    st_spec = pl.BlockSpec((1, SSD_GROUPS, SSD_STATE, GROUP_W), lambda b: (b, 0, 0, 0))
    return pl.pallas_call(
        functools.partial(_ssd_ctx_kernel, seq // SSD_CHUNK),
        grid=(nb,),
        in_specs=[pl.BlockSpec((seq, D_SSM), lambda b: (b, 0)),
                  pl.BlockSpec((seq, 2 * BC_DIM), lambda b: (b, D_SSM // (2 * BC_DIM))),
                  pl.BlockSpec((seq, LANES), lambda b: (b, 0)),
                  const((1, LANES)), const((1, LANES)), const((LANES, D_SSM)), const((LANES, D_SSM))],
        out_specs=[st_spec, st_spec],
        out_shape=[st, st],
        scratch_shapes=[pltpu.VMEM((SSD_GROUPS, SSD_STATE, GROUP_W), f32),
                        pltpu.VMEM((SSD_GROUPS, SSD_STATE, GROUP_W), f32)],
        compiler_params=_params("parallel"),
        name="ssd_ctx_states",
    )(xc, xc, dt_raw, _pad_lanes(dt_bias), _pad_lanes(a_log), ef, eb)


def _ssd_lat_kernel(nchunk, xf_ref, xb_ref, bf_ref, bb_ref, cf_ref, cb_ref, dtf_ref, dtb_ref, zf_ref, zb_ref,
                    hf0_ref, hb0_ref, bias_ref, alog_ref, dskip_ref, normw_ref, ef_ref, eb_ref,
                    o_ref, sf, sb, yacc):
    c = pl.program_id(1)
    t = SSD_CHUNK

    @pl.when(c == 0)
    def _():
        sf[...] = hf0_ref[0]
        sb[...] = hb0_ref[0]

    a_neg = -jnp.exp(alog_ref[...])
    bias = bias_ref[...]
    xf = xf_ref[...]
    yf = _ssd_chunk(False, 0, True, xf, bf_ref[...], cf_ref[...], dtf_ref[...], bias, a_neg, ef_ref[...], sf)
    yf = yf + dskip_ref[...] * xf.astype(f32)
    yb = _ssd_chunk(True, SSD_HEADS, True, xb_ref[...], bb_ref[...], cb_ref[...], dtb_ref[...], bias, a_neg,
                    eb_ref[...], sb)
    rows_f = pl.ds(pl.multiple_of(c * t, t), t)
    rows_b = pl.ds(pl.multiple_of((nchunk - 1 - c) * t, t), t)

    @pl.when(c < nchunk // 2)
    def _():
        yacc[rows_f, :] = yf
        yacc[rows_b, :] = yb

    def finish(y, z):
        y = y * _silu(z.astype(f32))
        outs = []
        for g in range(SSD_GROUPS):
            yg = y[:, g * GROUP_W:(g + 1) * GROUP_W]
            outs.append(yg * lax.rsqrt(jnp.mean(yg * yg, axis=-1, keepdims=True) + RMS_EPS))
        return (jnp.concatenate(outs, axis=1) * normw_ref[...]).astype(bf16)

    @pl.when(c >= nchunk // 2)
    def _():
        o_ref[rows_f, :] = finish(yacc[rows_f, :] + yf, zf_ref[...])
        o_ref[rows_b, :] = finish(yacc[rows_b, :] + yb, zb_ref[...])


def ssd_latent(xc, proj, dt_raw, seq, hf0, hb0, dt_bias, a_log, d_skip, norm_w):
    nb = xc.shape[0] // seq
    nc = seq // SSD_CHUNK
    half = nc // 2
    ef, eb = _head_expanders()
    t = SSD_CHUNK
    fwd = lambda b, c: b * nc + c
    bwd = lambda b, c: b * nc + nc - 1 - c
    zfw = lambda b, c: b * nc + jnp.maximum(c, half)
    zbw = lambda b, c: b * nc + jnp.minimum(nc - 1 - c, half - 1)
    bc0 = D_SSM // SSD_STATE // SSD_GROUPS
    const = lambda shape: pl.BlockSpec(shape, lambda b, c: (0,) * len(shape))
    st_spec = pl.BlockSpec((1, SSD_GROUPS, SSD_STATE, GROUP_W), lambda b, c: (b, 0, 0, 0))
    dskip = jnp.repeat(d_skip.astype(f32), SSD_HEAD_DIM).reshape(1, D_SSM)
    return pl.pallas_call(
        functools.partial(_ssd_lat_kernel, nc),
        grid=(nb, nc),
        in_specs=[pl.BlockSpec((t, D_SSM), lambda b, c: (fwd(b, c), 0)),
                  pl.BlockSpec((t, D_SSM), lambda b, c: (bwd(b, c), 0)),
                  pl.BlockSpec((t, BC_DIM), lambda b, c: (fwd(b, c), bc0)),
                  pl.BlockSpec((t, BC_DIM), lambda b, c: (bwd(b, c), bc0 + 1)),
                  pl.BlockSpec((t, BC_DIM), lambda b, c: (fwd(b, c), bc0 + 2)),
                  pl.BlockSpec((t, BC_DIM), lambda b, c: (bwd(b, c), bc0 + 3)),
                  pl.BlockSpec((t, LANES), lambda b, c: (fwd(b, c), 0)),
                  pl.BlockSpec((t, LANES), lambda b, c: (bwd(b, c), 0)),
                  pl.BlockSpec((t, D_SSM), lambda b, c: (zfw(b, c), COL_Z // D_SSM)),
                  pl.BlockSpec((t, D_SSM), lambda b, c: (zbw(b, c), COL_Z // D_SSM)),
                  st_spec, st_spec,
                  const((1, LANES)), const((1, LANES)), const((1, D_SSM)), const((1, D_SSM)),
                  const((LANES, D_SSM)), const((LANES, D_SSM))],
        out_specs=pl.BlockSpec((seq, D_SSM), lambda b, c: (b, 0)),
        out_shape=jax.ShapeDtypeStruct((nb * seq, D_SSM), bf16),
        scratch_shapes=[pltpu.VMEM((SSD_GROUPS, SSD_STATE, GROUP_W), f32),
                        pltpu.VMEM((SSD_GROUPS, SSD_STATE, GROUP_W), f32),
                        pltpu.VMEM((seq, D_SSM), f32)],
        compiler_params=_params("parallel", "arbitrary"),
        name="ssd_latent",
    )(xc, xc, xc, xc, xc, xc, dt_raw, dt_raw, proj, proj, hf0, hb0,
      _pad_lanes(dt_bias), _pad_lanes(a_log), dskip, norm_w.reshape(1, -1).astype(f32), ef, eb)


def _filter_kernel(z_ref, w1_ref, b1_ref, wh_ref, bh_ref, freq_ref, wf_ref, wb_ref, delta_ref, ks_ref, kd_ref):
    hp = lambda a, b: jnp.dot(a, b, preferred_element_type=f32, precision=lax.Precision.HIGHEST)
    z = z_ref[...]
    freq = freq_ref[...]
    h = jnp.sin(freq * (hp(z, w1_ref[...]) + b1_ref[...]))
    for n in range(HYENA_INNER):
        h = jnp.sin(freq * (hp(h, wh_ref[n]) + bh_ref[n]))
    window = jnp.exp(-z[:, 0:1] * delta_ref[...])
    hf = hp(h, wf_ref[...]) * window
    hb = hp(h, wb_ref[...]) * window
    row = lax.broadcasted_iota(jnp.int32, hb.shape, 0)
    hb = jnp.where(row == 0, 0.0, hb)
    norm = jnp.sum(jnp.abs(hf), axis=0, keepdims=True) + jnp.sum(jnp.abs(hb), axis=0, keepdims=True)
    ks_ref[...] = (hf + hb) / norm
    kd_ref[...] = (hf - hb) / norm


def hyena_filter_taps(seq, f_w1, f_b1, f_wh, f_bh, f_wout, freq, tc=256):
    fw = HYENA_FILTER_WIDTH
    t = np.arange(seq, dtype=np.float32)
    t_norm = t / np.float32(seq - 1)
    bands = np.linspace(1e-4, HYENA_BANDS - 1, HYENA_BANDS, dtype=np.float32)
    ang = np.float32(2.0 * math.pi / seq) * t[:, None] * bands
    z = np.concatenate([t_norm[:, None], np.cos(ang), -np.sin(ang)], axis=-1).astype(np.float32)
    z = np.pad(z, ((0, 0), (0, LANES - z.shape[1])))
    deltas = np.abs(np.linspace(math.log(HYENA_TARGET) / HYENA_SLOW_DECAY, math.log(HYENA_TARGET) / HYENA_FAST_DECAY,
                                D_MODEL, dtype=np.float32)).reshape(1, -1)
    padw = lambda a, r, c: jnp.pad(a.astype(f32), [(0, 0)] * (a.ndim - 2) + [(0, r - a.shape[-2]), (0, c - a.shape[-1])])
    w1 = padw(f_w1, LANES, LANES)
    wh = padw(f_wh, LANES, LANES)
    wout = padw(f_wout, LANES, 2 * D_MODEL)
    b1 = padw(f_b1.reshape(1, fw), 1, LANES)
    bh = padw(f_bh.reshape(HYENA_INNER, 1, fw), 1, LANES)
    fq = padw(freq.reshape(1, fw), 1, LANES)
    nt = D_MODEL // tc
    const = lambda shape: pl.BlockSpec(shape, lambda j: (0,) * len(shape))
    out = jax.ShapeDtypeStruct((seq, D_MODEL), f32)
    osp = pl.BlockSpec((seq, tc), lambda j: (0, j))
    return pl.pallas_call(
        _filter_kernel,
        grid=(nt,),
        in_specs=[const((seq, LANES)), const((LANES, LANES)), const((1, LANES)),
                  const((HYENA_INNER, LANES, LANES)), const((HYENA_INNER, 1, LANES)), const((1, LANES)),
                  pl.BlockSpec((LANES, tc), lambda j: (0, j)),
                  pl.BlockSpec((LANES, tc), lambda j: (0, j + nt)),
                  pl.BlockSpec((1, tc), lambda j: (0, j))],
        out_specs=[osp, osp],
        out_shape=[out, out],
        compiler_params=_params("parallel"),
        name="hyena_filter",
    )(jnp.asarray(z), w1, b1, wh, bh, fq, wout, wout, jnp.asarray(deltas))


def _dft_tables(seq):
    n = 2 * seq
    half = seq // 2

    def theta(f, t):
        return (2.0 * math.pi / (2 * n)) * (((2 * f[:, None] + 1) * t[None, :]) % (2 * n)).astype(np.float64)

    f_low = np.arange(half, dtype=np.int64)
    order = np.concatenate([f_low, seq - 1 - f_low])
    th_full = theta(order, np.arange(seq, dtype=np.int64))
    tp = np.arange(half, dtype=np.int64)
    th_e, th_o = theta(f_low, 2 * tp), theta(f_low, 2 * tp + 1)
    fwd = [np.cos(th_e), np.cos(th_o), np.sin(th_e), np.sin(th_o)]
    as_bf16 = lambda a: jnp.asarray(a, bf16)
    return (as_bf16(np.cos(th_full)), as_bf16(np.sin(th_full)),
            [as_bf16(a) for a in fwd], [as_bf16(a.T * (2.0 / n)) for a in fwd])


def _spectrum_kernel(c_ref, s_ref, ks_ref, kd_ref, kre_ref, kb_ref):
    kre_ref[...] = sum(_dot(c_ref[...], part) for part in _split2(ks_ref[...]))
    kb_ref[...] = sum(_dot(s_ref[...], part) for part in _split2(kd_ref[...]))


def filter_spectrum(cmat, smat, ks, kd, tmf=256, tc=256):
    seq = ks.shape[0]
    fsp = pl.BlockSpec((tmf, seq), lambda j, m: (m, 0))
    ksp = pl.BlockSpec((seq, tc), lambda j, m: (0, j))
    osp = pl.BlockSpec((tmf, tc), lambda j, m: (m, j))
    out = jax.ShapeDtypeStruct((seq, D_MODEL), f32)
    return pl.pallas_call(
        _spectrum_kernel,
        grid=(D_MODEL // tc, seq // tmf),
        in_specs=[fsp, fsp, ksp, ksp],
        out_specs=[osp, osp],
        out_shape=[out, out],
        compiler_params=_params("parallel", "parallel"),
        name="filter_spectrum",
    )(cmat, smat, ks, kd)


def _dft_fwd_kernel(ce_ref, co_ref, se_ref, so_ref, ve_ref, vo_ref, kre_ref, kb_ref, krem_ref, kbm_ref,
                    pp_ref, pm_ref, qp_ref, qm_ref):
    ve, vo = ve_ref[...], vo_ref[...]
    ec, oc = _dot(ce_ref[...], ve), _dot(co_ref[...], vo)
    es, os_ = _dot(se_ref[...], ve), _dot(so_ref[...], vo)

    def times_filter(vre, va, kre, kb):
        return vre * kre - va * kb, vre * kb + va * kre

    p, q = times_filter(ec + oc, es + os_, kre_ref[...], kb_ref[...])
    p_m, q_m = times_filter(ec - oc, os_ - es, krem_ref[...], kbm_ref[...])
    pp_ref[...] = (p + p_m).astype(bf16)
    pm_ref[...] = (p - p_m).astype(bf16)
    qp_ref[...] = (q + q_m).astype(bf16)
    qm_ref[...] = (q - q_m).astype(bf16)


def dft_forward(fwd_tables, vg, kre, kb, seq, tmf=512):
    half = seq // 2
    nb = vg.shape[0] // seq
    tmf = min(tmf, half)
    nm = half // tmf
    v2 = vg.reshape(nb * half, 2 * D_MODEL)
    fsp = pl.BlockSpec((tmf, half), lambda b, m: (m, 0))
    ksp = pl.BlockSpec((tmf, D_MODEL), lambda b, m: (m, 0))
    kmsp = pl.BlockSpec((tmf, D_MODEL), lambda b, m: (m + nm, 0))
    osp = pl.BlockSpec((tmf, D_MODEL), lambda b, m: (b * nm + m, 0))
    out = jax.ShapeDtypeStruct((nb * half, D_MODEL), bf16)
    return pl.pallas_call(
        _dft_fwd_kernel,
        grid=(nb, nm),
        in_specs=[fsp, fsp, fsp, fsp,
                  pl.BlockSpec((half, D_MODEL), lambda b, m: (b, 0)),
                  pl.BlockSpec((half, D_MODEL), lambda b, m: (b, 1)),
                  ksp, ksp, kmsp, kmsp],
        out_specs=[osp] * 4,
        out_shape=[out] * 4,
        compiler_params=_params("parallel", "parallel"),
        name="dft_forward",
    )(*fwd_tables, v2, v2, kre, kb, kre, kb)


def _dft_inv_kernel(gate_row, cet_ref, cot_ref, set_ref, sot_ref, pp_ref, pm_ref, qp_ref, qm_ref,
                    vg_ref, x0_ref, bias_ref, w_ref, h_ref, mod_ref, o_ref):
    gate = mod_ref[0, gate_row:gate_row + 1, :]

    def finish(y, lanes):
        y = y + vg_ref[:, lanes].astype(f32) * bias_ref[...]
        out = _dot((y * x0_ref[:, lanes].astype(f32)).astype(bf16), w_ref[...])
        o_ref[:, lanes] = h_ref[:, lanes] + gate * out

    finish(_dot(cet_ref[...], pp_ref[...]) + _dot(set_ref[...], qm_ref[...]), slice(0, D_MODEL))
    finish(_dot(cot_ref[...], pm_ref[...]) + _dot(sot_ref[...], qp_ref[...]), slice(D_MODEL, 2 * D_MODEL))


def dft_inverse_out(inv_tables, folded, vg, x0, bias, w_out, h, mods, seq, gate_row, tmt=256):
    half = seq // 2
    nb = h.shape[0] // seq
    tmt = min(tmt, half)
    nm = half // tmt
    pair = lambda a: a.reshape(nb * half, 2 * D_MODEL)
    gsp = pl.BlockSpec((tmt, half), lambda b, m: (m, 0))
    full = pl.BlockSpec((half, D_MODEL), lambda b, m: (b, 0))
    tile = pl.BlockSpec((tmt, 2 * D_MODEL), lambda b, m: (b * nm + m, 0))
    out = pl.pallas_call(
        functools.partial(_dft_inv_kernel, gate_row),
        grid=(nb, nm),
        in_specs=[gsp, gsp, gsp, gsp, full, full, full, full, tile, tile,
                  pl.BlockSpec((1, D_MODEL), lambda b, m: (0, 0)),
                  pl.BlockSpec((D_MODEL, D_MODEL), lambda b, m: (0, 0)),
                  tile, pl.BlockSpec((1, N_MOD, D_MODEL), lambda b, m: (b, 0, 0))],
        out_specs=tile,
        out_shape=jax.ShapeDtypeStruct((nb * half, 2 * D_MODEL), f32),
        compiler_params=_params("parallel", "parallel"),
        name="dft_inverse_out",
    )(*inv_tables, *folded, pair(vg), pair(x0), bias.reshape(1, -1).astype(f32), w_out, pair(h), mods)
    return out.reshape(h.shape)


_Q_HEAD_ORDER = (0, 4, 1, 5, 2, 6, 3, 7)


def _mixer_in_weight(w_in):
    c = np.cumsum([ATTN_DIM, KV_DIM, KV_DIM, D_SSM, XBC_DIM])
    q, k, v, z, xbc, dt = (w_in[:, :c[0]], w_in[:, c[0]:c[1]], w_in[:, c[1]:c[2]], w_in[:, c[2]:c[3]],
                           w_in[:, c[3]:c[4]], w_in[:, c[4]:])
    q = q.reshape(D_MODEL, N_Q_HEADS, HEAD_DIM)[:, np.array(_Q_HEAD_ORDER), :].reshape(D_MODEL, ATTN_DIM)
    w = jnp.concatenate([xbc, z, q, k, v], axis=1).astype(bf16)
    return w, jnp.pad(dt, ((0, 0), (0, LANES - dt.shape[1]))).astype(bf16)


def _mixer_out_weights(w_out):
    wa = w_out[:ATTN_DIM].reshape(N_Q_HEADS, HEAD_DIM, D_MODEL)[np.array(_Q_HEAD_ORDER)].reshape(ATTN_DIM, D_MODEL)
    return wa.astype(bf16), w_out[ATTN_DIM:].astype(bf16)


def kernel(x, c, ctx, c_ctx, w_ada, b_ada, norm_g, ffn_w13, ffn_w2, mix_w_in, mix_w_out, q_norm, k_norm, attn_sink,
           ssd_conv_w, ssd_conv_b, ssd_dt_bias, ssd_a_log, ssd_d, ssd_norm, hy_w_in, hy_conv_w, hy_conv_b,
           hy_f_w1, hy_f_b1, hy_f_wh, hy_f_bh, hy_f_wout, hy_freq, hy_bias, hy_w_out):
    nb, seq, _ = x.shape
    ctx_len = ctx.shape[1]
    depth = w_ada.shape[0]
    assert depth == 2, "this kernel is written for the two-layer block"
    h_lat = x.reshape(nb * seq, D_MODEL)
    h_ctx = ctx.reshape(nb * ctx_len, D_MODEL)
    cond = jnp.concatenate([c, c_ctx[None]], axis=0)
    cond = jnp.pad(cond, ((0, -cond.shape[0] % 8), (0, 0)))
    ffn_w = [[_ffn_weights(ffn_w13[i, k], ffn_w2[i, k]) for k in range(2)] for i in range(depth)]

    mods = ada_mods(cond, w_ada[0], b_ada[0])
    m_lat, m_ctx = mods[:nb], mods[nb:nb + 1]
    h_lat = macaron_ffn(h_lat, m_lat, seq, norm_g[0, 0], 0, *ffn_w[0][0])
    h_ctx = macaron_ffn(h_ctx, m_ctx, nb * ctx_len, norm_g[0, 0], 0, *ffn_w[0][0])
    w_in, w_dt = _mixer_in_weight(mix_w_in[0])
    wa, ws = _mixer_out_weights(mix_w_out[0])
    xc_lat, p_lat, dt_lat = mixer_proj(h_lat, m_lat, seq, seq, norm_g[0, 1], w_in, w_dt, ssd_conv_w[0], ssd_conv_b[0])
    xc_ctx, p_ctx, dt_ctx = mixer_proj(h_ctx, m_ctx, nb * ctx_len, ctx_len, norm_g[0, 1], w_in, w_dt,
                                       ssd_conv_w[0], ssd_conv_b[0])
    cos, sin = _rope_tables(seq)
    a_lat = window_attention(p_lat, p_ctx, seq, ctx_len, q_norm[0], k_norm[0], attn_sink[0], cos, sin)
    hf0, hb0 = ssd_ctx_states(xc_ctx, dt_ctx, ctx_len, ssd_dt_bias[0], ssd_a_log[0])
    s_lat = ssd_latent(xc_lat, p_lat, dt_lat, seq, hf0, hb0, ssd_dt_bias[0], ssd_a_log[0], ssd_d[0], ssd_norm[0])
    h_lat = out_proj_residual([a_lat, s_lat], [wa, ws], h_lat, m_lat, seq, 5)
    h_lat = macaron_ffn(h_lat, m_lat, seq, norm_g[0, 2], 2, *ffn_w[0][1])

    m_lat = ada_mods(cond, w_ada[1], b_ada[1])[:nb]
    h_lat = macaron_ffn(h_lat, m_lat, seq, norm_g[1, 0], 0, *ffn_w[1][0])
    x0, vg = hyena_proj(h_lat, m_lat, seq, seq, norm_g[1, 1], hy_w_in[0].astype(bf16), hy_conv_w[0], hy_conv_b[0])
    ks, kd = hyena_filter_taps(seq, hy_f_w1[0], hy_f_b1[0], hy_f_wh[0], hy_f_bh[0], hy_f_wout[0], hy_freq[0])
    c_full, s_full, fwd_tables, inv_tables = _dft_tables(seq)
    kre, kb = filter_spectrum(c_full, s_full, ks, kd)
    folded = dft_forward(fwd_tables, vg, kre, kb, seq)
    h_lat = dft_inverse_out(inv_tables, folded, vg, x0, hy_bias[0], hy_w_out[0].astype(bf16), h_lat, m_lat, seq, 5)
    h_lat = macaron_ffn(h_lat, m_lat, seq, norm_g[1, 2], 2, *ffn_w[1][1])
    return h_lat.reshape(nb, seq, D_MODEL)
```

```python
import functools
import math

import numpy as np
import jax
import jax.numpy as jnp
from jax import lax
from jax.experimental import pallas as pl
from jax.experimental.pallas import tpu as pltpu

f32 = jnp.float32
bf16 = jnp.bfloat16

D_MODEL = 1024
N_MOD = 9
RMS_EPS = 1e-6
GRID_W = 64

HEAD_DIM = 64
N_Q_HEADS = 8
N_KV_HEADS = 2
ATTN_DIM = N_Q_HEADS * HEAD_DIM
KV_DIM = N_KV_HEADS * HEAD_DIM
WINDOW = 128
ATTN_BLOCK = 128
ROPE_THETA = 10000.0

SSD_HEADS = 16
SSD_HEAD_DIM = 64
D_SSM = SSD_HEADS * SSD_HEAD_DIM
SSD_GROUPS = 2
SSD_STATE = 128
SSD_CONV = 7
SSD_CHUNK = 128
BC_DIM = SSD_GROUPS * SSD_STATE
XBC_DIM = D_SSM + 4 * BC_DIM
GROUP_W = D_SSM // SSD_GROUPS

HYENA_SHORT = 3
HYENA_BANDS = 8
HYENA_FILTER_WIDTH = 64
HYENA_INNER = 2
HYENA_FAST_DECAY = 0.3
HYENA_SLOW_DECAY = 1.5
HYENA_TARGET = 1e-2

D_FF = 2816
FFN_TF = 256
LANES = 128

COL_Z = 0
COL_Q = COL_Z + D_SSM
COL_K = COL_Q + ATTN_DIM
COL_V = COL_K + KV_DIM
REST_COLS = COL_V + KV_DIM
HALO = 16
PROJ_TN = 256
CONV_ROWS = 32

VMEM_LIMIT = 56 * 1024 * 1024


def _params(*sem):
    return pltpu.CompilerParams(dimension_semantics=sem, vmem_limit_bytes=VMEM_LIMIT)


def _dot(a, b):
    return jnp.dot(a, b, preferred_element_type=f32)


def _dot_nt(a, b):
    return lax.dot_general(a, b, (((1,), (1,)), ((), ())), preferred_element_type=f32)


def _split2(x):
    hi = x.astype(bf16)
    lo = (x - hi.astype(f32)).astype(bf16)
    return hi, lo


def _split3(x):
    hi = x.astype(bf16)
    r = x - hi.astype(f32)
    mid = r.astype(bf16)
    lo = (r - mid.astype(f32)).astype(bf16)
    return hi, mid, lo


def _adaln(h, g, shift, scale):
    ms = jnp.mean(h * h, axis=-1, keepdims=True)
    return (h * lax.rsqrt(ms + RMS_EPS) * g) * (1.0 + scale) + shift


def _silu(x):
    return x * jax.nn.sigmoid(x)


def _mods_kernel(c_ref, w_ref, b_ref, o_ref):
    o_ref[...] = _dot(_silu(c_ref[...]).astype(bf16), w_ref[...].astype(bf16)) + b_ref[...]


def ada_mods(cond, w, b):
    r = cond.shape[0]
    tn = 1024
    out = pl.pallas_call(
        _mods_kernel,
        grid=(w.shape[1] // tn,),
        in_specs=[pl.BlockSpec((r, D_MODEL), lambda j: (0, 0)),
                  pl.BlockSpec((D_MODEL, tn), lambda j: (0, j)),
                  pl.BlockSpec((1, tn), lambda j: (0, j))],
        out_specs=pl.BlockSpec((r, tn), lambda j: (0, j)),
        out_shape=jax.ShapeDtypeStruct((r, w.shape[1]), f32),
        compiler_params=_params("parallel"),
        name="ada_mods",
    )(cond, w, b.reshape(1, -1))
    return out.reshape(r, N_MOD, D_MODEL)


def _ffn_kernel(s, nf, h_ref, mod_ref, g_ref, w13_ref, w2_ref, o_ref):
    h = h_ref[...]
    u = _adaln(h, g_ref[...], mod_ref[0, 3 * s:3 * s + 1, :], mod_ref[0, 3 * s + 1:3 * s + 2, :]).astype(bf16)
    acc = None
    tf = w2_ref.shape[1]
    for j in range(nf):
        a = _dot(u, w13_ref[:, j * tf:(j + 1) * tf])
        b = _dot(u, w13_ref[:, D_FF + j * tf:D_FF + (j + 1) * tf])
        part = _dot((_silu(a) * b).astype(bf16), w2_ref[j])
        acc = part if acc is None else acc + part
    o_ref[...] = h + 0.5 * mod_ref[0, 3 * s + 2:3 * s + 3, :] * acc


def _ffn_weights(w13, w2, tf=FFN_TF):
    return w13.astype(bf16), w2.astype(bf16).reshape(D_FF // tf, tf, D_MODEL)


def macaron_ffn(h, mods, rows_per_mod, g, s, w13c, w2c, tm=512):
    m = h.shape[0]
    tm = min(tm, rows_per_mod)
    nf = w2c.shape[0]
    tiles_per_mod = rows_per_mod // tm
    return pl.pallas_call(
        functools.partial(_ffn_kernel, s, nf),
        grid=(m // tm,),
        in_specs=[pl.BlockSpec((tm, D_MODEL), lambda i: (i, 0)),
                  pl.BlockSpec((1, N_MOD, D_MODEL), lambda i: (i // tiles_per_mod, 0, 0)),
                  pl.BlockSpec((1, D_MODEL), lambda i: (0, 0)),
                  pl.BlockSpec(w13c.shape, lambda i: (0, 0)),
                  pl.BlockSpec(w2c.shape, lambda i: (0, 0, 0))],
        out_specs=pl.BlockSpec((tm, D_MODEL), lambda i: (i, 0)),
        out_shape=jax.ShapeDtypeStruct((m, D_MODEL), f32),
        compiler_params=_params("parallel"),
        name="macaron_ffn",
    )(h, mods, g.reshape(1, -1), w13c, w2c)


def _tile_adaln(s, tiles_per_seq, h_ref, hp_ref, hn_ref, mod_ref, g_ref, u_ref, uh_ref):
    shift, scale = mod_ref[0, 3 * s:3 * s + 1, :], mod_ref[0, 3 * s + 1:3 * s + 2, :]
    g = g_ref[...]
    u_ref[...] = _adaln(h_ref[...], g, shift, scale).astype(bf16)
    uh_ref[...] = _adaln(jnp.concatenate([hp_ref[...], hn_ref[...]], axis=0), g, shift, scale).astype(bf16)
    t = pl.program_id(0) % tiles_per_seq
    row = lax.broadcasted_iota(jnp.int32, (2 * HALO, 1), 0)
    keep_prev = jnp.where(t > 0, 1.0, 0.0)
    keep_next = jnp.where(t < tiles_per_seq - 1, 1.0, 0.0)
    return jnp.where(row < HALO, keep_prev, keep_next)


def _project_padded(pad_ref, u_ref, uh_ref, keep, w):
    tm = u_ref.shape[0]
    halo = _dot(uh_ref[...], w) * keep
    pad_ref[0:HALO, :] = halo[0:HALO, :]
    pad_ref[HALO:HALO + tm, :] = _dot(u_ref[...], w)
    pad_ref[HALO + tm:2 * HALO + tm, :] = halo[HALO:2 * HALO, :]


def _conv_padded(pad_ref, conv_w, conv_b, emit):
    tm = pad_ref.shape[0] - 2 * HALO
    width = conv_w.shape[0]
    rows = CONV_ROWS
    for l0 in range(0, pad_ref.shape[1], LANES):
        lanes = slice(l0, l0 + LANES)
        taps = [conv_w[k:k + 1, lanes] for k in range(width)]
        bias = conv_b[:, lanes]
        half = width // 2
        for r0 in range(0, tm, rows):
            win = pad_ref[HALO + r0 - 8:HALO + r0 + rows + 8, lanes]
            acc = bias + taps[half] * win[8:8 + rows, :]
            for k in range(width):
                if k != half:
                    acc += taps[k] * pltpu.roll(win, (half - k) % (rows + 16), 0)[8:8 + rows, :]
            emit(r0, rows, l0, acc)


def _mixer_proj_kernel(s, tiles_per_seq, h_ref, hp_ref, hn_ref, mod_ref, g_ref, w_ref, wdt_ref, cw_ref, cb_ref,
                       xc_ref, rest_ref, dt_ref, pad_ref, u_ref, uh_ref):
    keep = _tile_adaln(s, tiles_per_seq, h_ref, hp_ref, hn_ref, mod_ref, g_ref, u_ref, uh_ref)
    tn = PROJ_TN
    n_conv = XBC_DIM // tn
    n_rest = REST_COLS // tn

    def project(c):
        _project_padded(pad_ref.at[c % 2], u_ref, uh_ref, keep, w_ref[:, c * tn:(c + 1) * tn])

    def plain(c):
        if c < n_rest:
            rest_ref[:, c * tn:(c + 1) * tn] = _dot(u_ref[...], w_ref[:, XBC_DIM + c * tn:XBC_DIM + (c + 1) * tn]).astype(bf16)
        elif c == n_rest:
            dt_ref[...] = _dot(u_ref[...], wdt_ref[...])

    project(0)
    for c in range(n_conv):
        if c + 1 < n_conv:
            project(c + 1)
        plain(c)

        def emit(r0, rows, l0, acc, c0=c * tn):
            xc_ref[r0:r0 + rows, c0 + l0:c0 + l0 + LANES] = _silu(acc).astype(bf16)

        _conv_padded(pad_ref.at[c % 2], cw_ref[:, c * tn:(c + 1) * tn], cb_ref[:, c * tn:(c + 1) * tn], emit)
    for c in range(n_conv, n_rest + 1):
        plain(c)


def _hyena_proj_kernel(s, tiles_per_seq, h_ref, hp_ref, hn_ref, mod_ref, g_ref, w_ref, cw_ref, cb_ref,
                       x0e_ref, x0o_ref, vge_ref, vgo_ref, pad_ref, u_ref, uh_ref, x1_ref, split_ref):
    keep = _tile_adaln(s, tiles_per_seq, h_ref, hp_ref, hn_ref, mod_ref, g_ref, u_ref, uh_ref)
    tn = PROJ_TN
    cols = [part * D_MODEL + c0 for c0 in range(0, D_MODEL, tn) for part in range(3)]

    def project(i):
        _project_padded(pad_ref.at[i % 2], u_ref, uh_ref, keep, w_ref[:, cols[i]:cols[i] + tn])

    project(0)
    for i, col in enumerate(cols):
        if i + 1 < len(cols):
            project(i + 1)
        c0 = col % D_MODEL

        def split_tokens(value, even_ref, odd_ref, r0, rows, lanes):
            split_ref[...] = value
            dst = slice(r0 // 2, (r0 + rows) // 2)
            even_ref[dst, lanes] = split_ref[pl.ds(0, rows // 2, stride=2), :].astype(bf16)
            odd_ref[dst, lanes] = split_ref[pl.ds(1, rows // 2, stride=2), :].astype(bf16)

        def emit_x0(r0, rows, l0, acc, c0=c0):
            split_tokens(acc, x0e_ref, x0o_ref, r0, rows, slice(c0 + l0, c0 + l0 + LANES))

        def emit_x1(r0, rows, l0, acc):
            x1_ref[r0:r0 + rows, l0:l0 + LANES] = acc

        def emit_v(r0, rows, l0, acc, c0=c0):
            vg = acc * x1_ref[r0:r0 + rows, l0:l0 + LANES]
            split_tokens(vg, vge_ref, vgo_ref, r0, rows, slice(c0 + l0, c0 + l0 + LANES))

        emit = (emit_x0, emit_x1, emit_v)[col // D_MODEL]
        _conv_padded(pad_ref.at[i % 2], cw_ref[:, col:col + tn], cb_ref[:, col:col + tn], emit)


def _proj_call(body, name, h, mods, rows_per_mod, seq, g, consts, outs, extra_scratch, tm=512):
    m = h.shape[0]
    tm = min(tm, seq)
    tiles_per_seq = seq // tm
    tiles_per_mod = rows_per_mod // tm
    hb = tm // HALO
    in_specs = [pl.BlockSpec((tm, D_MODEL), lambda i: (i, 0)),
                pl.BlockSpec((HALO, D_MODEL), lambda i: (jnp.maximum(i * hb - 1, 0), 0)),
                pl.BlockSpec((HALO, D_MODEL), lambda i: (jnp.minimum((i + 1) * hb, m // HALO - 1), 0)),
                pl.BlockSpec((1, N_MOD, D_MODEL), lambda i: (i // tiles_per_mod, 0, 0)),
                pl.BlockSpec((1, D_MODEL), lambda i: (0, 0))]
    in_specs += [pl.BlockSpec(a.shape, lambda i: (0, 0)) for a in consts]
    return pl.pallas_call(
        functools.partial(body, 1, tiles_per_seq),
        grid=(m // tm,),
        in_specs=in_specs,
        out_specs=[pl.BlockSpec((tm // div, n), lambda i: (i, 0)) for div, n, _ in outs],
        out_shape=[jax.ShapeDtypeStruct((m // div, n), dt) for div, n, dt in outs],
        scratch_shapes=[pltpu.VMEM((2, tm + 2 * HALO, PROJ_TN), f32), pltpu.VMEM((tm, D_MODEL), bf16),
                        pltpu.VMEM((2 * HALO, D_MODEL), bf16)] + extra_scratch(tm),
        compiler_params=_params("parallel"),
        name=name,
    )(h, h, h, mods, g.reshape(1, -1), *consts)


def mixer_proj(h, mods, rows_per_mod, seq, g, w, w_dt, conv_w, conv_b):
    consts = [w, w_dt, conv_w.astype(f32), conv_b.reshape(1, -1).astype(f32)]
    return _proj_call(_mixer_proj_kernel, "mixer_proj", h, mods, rows_per_mod, seq, g, consts,
                      [(1, XBC_DIM, bf16), (1, REST_COLS, bf16), (1, LANES, f32)], lambda tm: [])


def hyena_proj(h, mods, rows_per_mod, seq, g, w, conv_w, conv_b):
    consts = [w, conv_w.astype(f32), conv_b.reshape(1, -1).astype(f32)]
    scratch = lambda tm: [pltpu.VMEM((tm, PROJ_TN), f32), pltpu.VMEM((CONV_ROWS, LANES), f32)]
    return _proj_call(_hyena_proj_kernel, "hyena_proj", h, mods, rows_per_mod, seq, g, consts,
                      [(2, D_MODEL, bf16)] * 4, scratch)


def _outproj_kernel(n_in, gate_row, *refs):
    x_refs = refs[:n_in]
    w_refs = refs[n_in:2 * n_in]
    h_ref, mod_ref, o_ref = refs[2 * n_in:]
    acc = _dot(x_refs[0][...].astype(bf16), w_refs[0][...])
    for x_ref, w_ref in zip(x_refs[1:], w_refs[1:]):
        acc += _dot(x_ref[...].astype(bf16), w_ref[...])
    o_ref[...] = h_ref[...] + mod_ref[0, gate_row:gate_row + 1, :] * acc


def out_proj_residual(xs, ws, h, mods, rows_per_mod, gate_row, tm=512):
    m = h.shape[0]
    tm = min(tm, rows_per_mod)
    tiles_per_mod = rows_per_mod // tm
    n_in = len(xs)
    in_specs = ([pl.BlockSpec((tm, x.shape[1]), lambda i: (i, 0)) for x in xs]
                + [pl.BlockSpec(w.shape, lambda i: (0, 0)) for w in ws]
                + [pl.BlockSpec((tm, D_MODEL), lambda i: (i, 0)),
                   pl.BlockSpec((1, N_MOD, D_MODEL), lambda i: (i // tiles_per_mod, 0, 0))])
    return pl.pallas_call(
        functools.partial(_outproj_kernel, n_in, gate_row),
        grid=(m // tm,),
        in_specs=in_specs,
        out_specs=pl.BlockSpec((tm, D_MODEL), lambda i: (i, 0)),
        out_shape=jax.ShapeDtypeStruct((m, D_MODEL), f32),
        compiler_params=_params("parallel"),
        name="out_proj_residual",
    )(*xs, *ws, h, mods)


def _head_norm(x, gain, bd):
    hi, lo = _split2(x * x)
    ms = _dot(hi, bd) + _dot(lo, bd)
    return x * lax.rsqrt(ms + RMS_EPS) * gain


def _rope(x, cos, sin_signed):
    lane = lax.broadcasted_iota(jnp.int32, x.shape, 1)
    partner = jnp.where((lane & 16) != 0, pltpu.roll(x, 16, 1), pltpu.roll(x, LANES - 16, 1))
    return x * cos + partner * sin_signed


def _t_bf16(x):
    return x.astype(f32).T.astype(bf16)


def _attn_kernel(seq, q_ref, k_ref, v_ref, kc_ref, vc_ref, qg_ref, kg_ref, cos_ref, sin_ref, bd_ref,
                 sink_ref, o_ref, qt_s, k_s, vt_s, kc_s, vct_s):
    j = pl.program_id(1)
    nblk = seq // ATTN_BLOCK
    nslab = ATTN_DIM // LANES
    blk = ATTN_BLOCK

    @pl.when(j == 0)
    def _():
        bd = bd_ref[...]
        cos, sin = cos_ref[...], sin_ref[...]
        scale = HEAD_DIM ** -0.5
        for p in range(nslab):
            qn = _head_norm(q_ref[:, p * LANES:(p + 1) * LANES].astype(f32), qg_ref[...], bd)
            qr = _rope(qn, cos, sin) * scale
            for jb in range(nblk):
                qt_s[jb, p * LANES:(p + 1) * LANES, :] = _t_bf16(qr[jb * blk:(jb + 1) * blk, :])
        kn = _head_norm(k_ref[...].astype(f32), kg_ref[...], bd)
        zeros = jnp.zeros((WINDOW, KV_DIM), bf16)
        k_s[0:WINDOW, :] = zeros
        k_s[WINDOW + seq:2 * WINDOW + seq, :] = zeros
        k_s[WINDOW:WINDOW + seq, :] = _rope(kn, cos, sin).astype(bf16)
        vt_s[0] = zeros
        vt_s[nblk + 1] = zeros
        for jb in range(nblk):
            vt_s[jb + 1] = _t_bf16(v_ref[jb * blk:(jb + 1) * blk, :])
        kc_s[...] = _head_norm(kc_ref[...].astype(f32), kg_ref[...], bd).astype(bf16)
        vct_s[...] = _t_bf16(vc_ref[...])

    band = blk + 2 * WINDOW
    start = pl.multiple_of(j * blk, blk)
    kb = k_s[pl.ds(start, band), :]
    kc = kc_s[...]
    vtb = jnp.concatenate([vt_s[j], vt_s[j + 1], vt_s[j + 2]], axis=1)
    vtc = vct_s[...]
    qt = qt_s[j]
    key = lax.broadcasted_iota(jnp.int32, (blk, 2 * blk), 0)
    qry = lax.broadcasted_iota(jnp.int32, (blk, 2 * blk), 1) & (blk - 1)
    ok_lo = (jnp.abs(qry - (key - WINDOW)) <= WINDOW) & (start - WINDOW + key >= 0)
    ok_hi = (jnp.abs(qry - (key + blk)) <= WINDOW) & (start + blk + key < seq)
    dim = lax.broadcasted_iota(jnp.int32, (LANES, blk), 0)
    lane2 = lax.broadcasted_iota(jnp.int32, (1, 2 * blk), 1)
    ones_b = jnp.ones((16, band), bf16)
    ones_c = jnp.ones((16, kc.shape[0]), bf16)
    for p in range(nslab):
        qslab = qt[p * LANES:(p + 1) * LANES, :]
        zero = jnp.zeros_like(qslab)
        rhs = jnp.concatenate([jnp.where(dim < HEAD_DIM, qslab, zero), jnp.where(dim >= HEAD_DIM, qslab, zero)], axis=1)
        sb = _dot(kb, rhs)
        s_lo = jnp.where(ok_lo, sb[0:blk, :], -jnp.inf)
        s_mid = sb[blk:2 * blk, :]
        s_hi = jnp.where(ok_hi, sb[2 * blk:3 * blk, :], -jnp.inf)
        sc = _dot(kc, rhs)
        sink = jnp.where(lane2 < blk, sink_ref[p], sink_ref[p + N_Q_HEADS // N_KV_HEADS])
        colmax = lambda s: jnp.max(s, axis=0, keepdims=True)
        mx = jnp.maximum(jnp.maximum(jnp.maximum(colmax(s_lo), colmax(s_mid)), jnp.maximum(colmax(s_hi), colmax(sc))),
                         sink)
        pb = jnp.concatenate([jnp.exp(s_lo - mx).astype(bf16), jnp.exp(s_mid - mx).astype(bf16),
                              jnp.exp(s_hi - mx).astype(bf16)], axis=0)
        pc = jnp.exp(sc - mx).astype(bf16)
        den = (_dot(ones_b, pb) + _dot(ones_c, pc))[0:1, :] + jnp.exp(sink - mx)
        ot = (_dot(vtb, pb) + _dot(vtc, pc)) / den
        both = jnp.where(dim < HEAD_DIM, ot[:, 0:blk], ot[:, blk:2 * blk])
        o_ref[:, p * LANES:(p + 1) * LANES] = both.T.astype(bf16)


def window_attention(proj_lat, proj_ctx, seq, ctx_len, q_gain, k_gain, sink, rope_cos, rope_sin):
    nb = proj_lat.shape[0] // seq
    nblk = seq // ATTN_BLOCK
    bd = np.kron(np.eye(LANES // HEAD_DIM), np.ones((HEAD_DIM, HEAD_DIM))) / HEAD_DIM
    gain2 = lambda g: jnp.tile(g, LANES // HEAD_DIM).reshape(1, LANES)
    const = lambda shape: pl.BlockSpec(shape, lambda b, j: (0, 0))
    return pl.pallas_call(
        functools.partial(_attn_kernel, seq),
        grid=(nb, nblk),
        in_specs=[pl.BlockSpec((seq, ATTN_DIM), lambda b, j: (b, COL_Q // ATTN_DIM)),
                  pl.BlockSpec((seq, KV_DIM), lambda b, j: (b, COL_K // KV_DIM)),
                  pl.BlockSpec((seq, KV_DIM), lambda b, j: (b, COL_V // KV_DIM)),
                  pl.BlockSpec((ctx_len, KV_DIM), lambda b, j: (b, COL_K // KV_DIM)),
                  pl.BlockSpec((ctx_len, KV_DIM), lambda b, j: (b, COL_V // KV_DIM)),
                  const((1, LANES)), const((1, LANES)),
                  const((seq, LANES)), const((seq, LANES)), const((LANES, LANES)),
                  pl.BlockSpec(memory_space=pltpu.SMEM)],
        out_specs=pl.BlockSpec((ATTN_BLOCK, ATTN_DIM), lambda b, j: (b * nblk + j, 0)),
        out_shape=jax.ShapeDtypeStruct((nb * seq, ATTN_DIM), bf16),
        scratch_shapes=[pltpu.VMEM((nblk, ATTN_DIM, ATTN_BLOCK), bf16),
                        pltpu.VMEM((seq + 2 * WINDOW, KV_DIM), bf16),
                        pltpu.VMEM((nblk + 2, KV_DIM, ATTN_BLOCK), bf16),
                        pltpu.VMEM((ctx_len, KV_DIM), bf16),
                        pltpu.VMEM((KV_DIM, ctx_len), bf16)],
        compiler_params=_params("parallel", "arbitrary"),
        name="window_attention",
    )(proj_lat, proj_lat, proj_lat, proj_ctx, proj_ctx, gain2(q_gain), gain2(k_gain),
      rope_cos, rope_sin, jnp.asarray(bd, bf16), sink)


def _rope_tables(seq):
    t = np.arange(seq)
    pos = np.stack([t // GRID_W, t % GRID_W], axis=1).astype(np.float32)
    axis_dim = HEAD_DIM // 2
    inv = (ROPE_THETA ** (-np.arange(0, axis_dim, 2, dtype=np.float32) / axis_dim)).astype(np.float32)
    lane = np.arange(LANES)
    d = lane % HEAD_DIM
    which = d // axis_dim
    ang = (pos[:, which] * inv[d % (axis_dim // 2)][None, :]).astype(np.float32)
    sign = np.where((d % axis_dim) < axis_dim // 2, -1.0, 1.0)
    return jnp.asarray(np.cos(ang), f32), jnp.asarray(np.sin(ang) * sign, f32)


def _softplus(x):
    return jnp.maximum(x, 0.0) + jnp.log1p(jnp.exp(-jnp.abs(x)))


def _expand_heads(v, e):
    return _dot(v.astype(bf16), e)


def _ssd_chunk(rev, lane0, want_y, x, bm, cm, dt_raw, dt_bias, a_neg, expand, state_ref):
    t = x.shape[0]
    dt = _softplus(dt_raw + dt_bias)
    a = dt * a_neg
    r = lax.broadcasted_iota(jnp.int32, (t, t), 0)
    c = lax.broadcasted_iota(jnp.int32, (t, t), 1)
    keep = (r <= c) if rev else (r >= c)
    tri = jnp.where(keep, 1.0, 0.0).astype(bf16)
    cs = sum(_dot(tri, part) for part in _split3(a))
    last = cs[0:1, :] if rev else cs[t - 1:t, :]
    e = jnp.exp(cs)
    w = dt * jnp.exp(last - cs)
    e_x = _expand_heads(e, expand)
    w_x = _expand_heads(w, expand)
    elast_x = e_x[0:1, :] if rev else e_x[t - 1:t, :]

    y = None
    if want_y:
        cs_t = cs.T
        dt_t = dt.T
        lane = lax.broadcasted_iota(jnp.int32, (t, LANES), 1)
        cb = [_dot_nt(cm[:, g * SSD_STATE:(g + 1) * SSD_STATE],
                      bm[:, g * SSD_STATE:(g + 1) * SSD_STATE]) for g in range(SSD_GROUPS)]
        pieces = []
        for p in range(SSD_HEADS // 2):
            xp = x[:, p * LANES:(p + 1) * LANES]
            ms = []
            for q in range(2):
                h = 2 * p + q
                g = h // (SSD_HEADS // SSD_GROUPS)
                seg = cs[:, lane0 + h:lane0 + h + 1] - cs_t[lane0 + h:lane0 + h + 1, :]
                dec = jnp.exp(jnp.where(keep, seg, -jnp.inf))
                ms.append((cb[g] * dec * dt_t[lane0 + h:lane0 + h + 1, :]).astype(bf16))
            zero = jnp.zeros_like(xp)
            xcat = jnp.concatenate([jnp.where(lane < SSD_HEAD_DIM, xp, zero),
                                    jnp.where(lane >= SSD_HEAD_DIM, xp, zero)], axis=0)
            pieces.append(_dot(jnp.concatenate(ms, axis=1), xcat))
        y = jnp.concatenate(pieces, axis=1)

    inter = []
    for g in range(SSD_GROUPS):
        gs = slice(g * GROUP_W, (g + 1) * GROUP_W)
        ss = slice(g * SSD_STATE, (g + 1) * SSD_STATE)
        h_t = state_ref[g]
        if want_y:
            inter.append(_dot(cm[:, ss], h_t.astype(bf16)) * e_x[:, gs])
        xw = (x[:, gs].astype(f32) * w_x[:, gs]).astype(bf16)
        state_ref[g] = h_t * elast_x[:, gs] + _dot(bm[:, ss].astype(f32).T.astype(bf16), xw)
    if want_y:
        y = y + jnp.concatenate(inter, axis=1)
    return y


def _ssd_ctx_kernel(nchunk, x_ref, bc_ref, dt_ref, bias_ref, alog_ref, ef_ref, eb_ref, hf_ref, hb_ref, sf, sb):
    sf[...] = jnp.zeros_like(sf)
    sb[...] = jnp.zeros_like(sb)
    a_neg = -jnp.exp(alog_ref[...])
    bias = bias_ref[...]
    t = SSD_CHUNK
    for ci in range(nchunk):
        rows = slice(ci * t, (ci + 1) * t)
        _ssd_chunk(False, 0, False, x_ref[rows, :], bc_ref[rows, 0:BC_DIM], None, dt_ref[rows, :],
                   bias, a_neg, ef_ref[...], sf)
        rows = slice((nchunk - 1 - ci) * t, (nchunk - ci) * t)
        _ssd_chunk(True, SSD_HEADS, False, x_ref[rows, :], bc_ref[rows, BC_DIM:2 * BC_DIM], None, dt_ref[rows, :],
                   bias, a_neg, eb_ref[...], sb)
    hf_ref[0] = sf[...]
    hb_ref[0] = sb[...]


def _head_expanders():
    ef = np.zeros((LANES, D_SSM), np.float32)
    eb = np.zeros((LANES, D_SSM), np.float32)
    for h in range(SSD_HEADS):
        ef[h, h * SSD_HEAD_DIM:(h + 1) * SSD_HEAD_DIM] = 1.0
        eb[SSD_HEADS + h, h * SSD_HEAD_DIM:(h + 1) * SSD_HEAD_DIM] = 1.0
    return jnp.asarray(ef, bf16), jnp.asarray(eb, bf16)


def _pad_lanes(v):
    v = v.reshape(1, -1).astype(f32)
    return jnp.pad(v, ((0, 0), (0, LANES - v.shape[1])))


def ssd_ctx_states(xc, dt_raw, seq, dt_bias, a_log):
    nb = xc.shape[0] // seq
    ef, eb = _head_expanders()
    const = lambda shape: pl.BlockSpec(shape, lambda b: (0,) * len(shape))
    st = jax.ShapeDtypeStruct((nb, SSD_GROUPS, SSD_STATE, GROUP_W), f32)
    st_spec = pl.BlockSpec((1, SSD_GROUPS, SSD_STATE, GROUP_W), lambda b: (b, 0, 0, 0))
    return pl.pallas_call(
        functools.partial(_ssd_ctx_kernel, seq // SSD_CHUNK),
        grid=(nb,),
        in_specs=[pl.BlockSpec((seq, D_SSM), lambda b: (b, 0)),
                  pl.BlockSpec((seq, 2 * BC_DIM), lambda b: (b, D_SSM // (2 * BC_DIM))),
                  pl.BlockSpec((seq, LANES), lambda b: (b, 0)),
                  const((1, LANES)), const((1, LANES)), const((LANES, D_SSM)), const((LANES, D_SSM))],
        out_specs=[st_spec, st_spec],
        out_shape=[st, st],
        scratch_shapes=[pltpu.VMEM((SSD_GROUPS, SSD_STATE, GROUP_W), f32),
                        pltpu.VMEM((SSD_GROUPS, SSD_STATE, GROUP_W), f32)],
        compiler_params=_params("parallel"),
        name="ssd_ctx_states",
    )(xc, xc, dt_raw, _pad_lanes(dt_bias), _pad_lanes(a_log), ef, eb)


def _ssd_lat_kernel(nchunk, xf_ref, xb_ref, bf_ref, bb_ref, cf_ref, cb_ref, dtf_ref, dtb_ref, zf_ref, zb_ref,
                    hf0_ref, hb0_ref, bias_ref, alog_ref, dskip_ref, normw_ref, ef_ref, eb_ref,
                    o_ref, sf, sb, yacc):
    c = pl.program_id(1)
    t = SSD_CHUNK

    @pl.when(c == 0)
    def _():
        sf[...] = hf0_ref[0]
        sb[...] = hb0_ref[0]

    a_neg = -jnp.exp(alog_ref[...])
    bias = bias_ref[...]
    xf = xf_ref[...]
    yf = _ssd_chunk(False, 0, True, xf, bf_ref[...], cf_ref[...], dtf_ref[...], bias, a_neg, ef_ref[...], sf)
    yf = yf + dskip_ref[...] * xf.astype(f32)
    yb = _ssd_chunk(True, SSD_HEADS, True, xb_ref[...], bb_ref[...], cb_ref[...], dtb_ref[...], bias, a_neg,
                    eb_ref[...], sb)
    rows_f = pl.ds(pl.multiple_of(c * t, t), t)
    rows_b = pl.ds(pl.multiple_of((nchunk - 1 - c) * t, t), t)

    @pl.when(c < nchunk // 2)
    def _():
        yacc[rows_f, :] = yf
        yacc[rows_b, :] = yb

    def finish(y, z):
        y = y * _silu(z.astype(f32))
        outs = []
        for g in range(SSD_GROUPS):
            yg = y[:, g * GROUP_W:(g + 1) * GROUP_W]
            outs.append(yg * lax.rsqrt(jnp.mean(yg * yg, axis=-1, keepdims=True) + RMS_EPS))
        return (jnp.concatenate(outs, axis=1) * normw_ref[...]).astype(bf16)

    @pl.when(c >= nchunk // 2)
    def _():
        o_ref[rows_f, :] = finish(yacc[rows_f, :] + yf, zf_ref[...])
        o_ref[rows_b, :] = finish(yacc[rows_b, :] + yb, zb_ref[...])


def ssd_latent(xc, proj, dt_raw, seq, hf0, hb0, dt_bias, a_log, d_skip, norm_w):
    nb = xc.shape[0] // seq
    nc = seq // SSD_CHUNK
    half = nc // 2
    ef, eb = _head_expanders()
    t = SSD_CHUNK
    fwd = lambda b, c: b * nc + c
    bwd = lambda b, c: b * nc + nc - 1 - c
    zfw = lambda b, c: b * nc + jnp.maximum(c, half)
    zbw = lambda b, c: b * nc + jnp.minimum(nc - 1 - c, half - 1)
    bc0 = D_SSM // SSD_STATE // SSD_GROUPS
    const = lambda shape: pl.BlockSpec(shape, lambda b, c: (0,) * len(shape))
    st_spec = pl.BlockSpec((1, SSD_GROUPS, SSD_STATE, GROUP_W), lambda b, c: (b, 0, 0, 0))
    dskip = jnp.repeat(d_skip.astype(f32), SSD_HEAD_DIM).reshape(1, D_SSM)
    return pl.pallas_call(
        functools.partial(_ssd_lat_kernel, nc),
        grid=(nb, nc),
        in_specs=[pl.BlockSpec((t, D_SSM), lambda b, c: (fwd(b, c), 0)),
                  pl.BlockSpec((t, D_SSM), lambda b, c: (bwd(b, c), 0)),
                  pl.BlockSpec((t, BC_DIM), lambda b, c: (fwd(b, c), bc0)),
                  pl.BlockSpec((t, BC_DIM), lambda b, c: (bwd(b, c), bc0 + 1)),
                  pl.BlockSpec((t, BC_DIM), lambda b, c: (fwd(b, c), bc0 + 2)),
                  pl.BlockSpec((t, BC_DIM), lambda b, c: (bwd(b, c), bc0 + 3)),
                  pl.BlockSpec((t, LANES), lambda b, c: (fwd(b, c), 0)),
                  pl.BlockSpec((t, LANES), lambda b, c: (bwd(b, c), 0)),
                  pl.BlockSpec((t, D_SSM), lambda b, c: (zfw(b, c), COL_Z // D_SSM)),
                  pl.BlockSpec((t, D_SSM), lambda b, c: (zbw(b, c), COL_Z // D_SSM)),
                  st_spec, st_spec,
                  const((1, LANES)), const((1, LANES)), const((1, D_SSM)), const((1, D_SSM)),
                  const((LANES, D_SSM)), const((LANES, D_SSM))],
        out_specs=pl.BlockSpec((seq, D_SSM), lambda b, c: (b, 0)),
        out_shape=jax.ShapeDtypeStruct((nb * seq, D_SSM), bf16),
        scratch_shapes=[pltpu.VMEM((SSD_GROUPS, SSD_STATE, GROUP_W), f32),
                        pltpu.VMEM((SSD_GROUPS, SSD_STATE, GROUP_W), f32),
                        pltpu.VMEM((seq, D_SSM), f32)],
        compiler_params=_params("parallel", "arbitrary"),
        name="ssd_latent",
    )(xc, xc, xc, xc, xc, xc, dt_raw, dt_raw, proj, proj, hf0, hb0,
      _pad_lanes(dt_bias), _pad_lanes(a_log), dskip, norm_w.reshape(1, -1).astype(f32), ef, eb)


def _filter_kernel(z_ref, w1_ref, b1_ref, wh_ref, bh_ref, freq_ref, wf_ref, wb_ref, delta_ref, ks_ref, kd_ref):
    hp = lambda a, b: jnp.dot(a, b, preferred_element_type=f32, precision=lax.Precision.HIGHEST)
    z = z_ref[...]
    freq = freq_ref[...]
    h = jnp.sin(freq * (hp(z, w1_ref[...]) + b1_ref[...]))
    for n in range(HYENA_INNER):
        h = jnp.sin(freq * (hp(h, wh_ref[n]) + bh_ref[n]))
    window = jnp.exp(-z[:, 0:1] * delta_ref[...])
    hf = hp(h, wf_ref[...]) * window
    hb = hp(h, wb_ref[...]) * window
    row = lax.broadcasted_iota(jnp.int32, hb.shape, 0)
    hb = jnp.where(row == 0, 0.0, hb)
    norm = jnp.sum(jnp.abs(hf), axis=0, keepdims=True) + jnp.sum(jnp.abs(hb), axis=0, keepdims=True)
    ks_ref[...] = (hf + hb) / norm
    kd_ref[...] = (hf - hb) / norm


def hyena_filter_taps(seq, f_w1, f_b1, f_wh, f_bh, f_wout, freq, tc=256):
    fw = HYENA_FILTER_WIDTH
    t = np.arange(seq, dtype=np.float32)
    t_norm = t / np.float32(seq - 1)
    bands = np.linspace(1e-4, HYENA_BANDS - 1, HYENA_BANDS, dtype=np.float32)
    ang = np.float32(2.0 * math.pi / seq) * t[:, None] * bands
    z = np.concatenate([t_norm[:, None], np.cos(ang), -np.sin(ang)], axis=-1).astype(np.float32)
    z = np.pad(z, ((0, 0), (0, LANES - z.shape[1])))
    deltas = np.abs(np.linspace(math.log(HYENA_TARGET) / HYENA_SLOW_DECAY, math.log(HYENA_TARGET) / HYENA_FAST_DECAY,
                                D_MODEL, dtype=np.float32)).reshape(1, -1)
    padw = lambda a, r, c: jnp.pad(a.astype(f32), [(0, 0)] * (a.ndim - 2) + [(0, r - a.shape[-2]), (0, c - a.shape[-1])])
    w1 = padw(f_w1, LANES, LANES)
    wh = padw(f_wh, LANES, LANES)
    wout = padw(f_wout, LANES, 2 * D_MODEL)
    b1 = padw(f_b1.reshape(1, fw), 1, LANES)
    bh = padw(f_bh.reshape(HYENA_INNER, 1, fw), 1, LANES)
    fq = padw(freq.reshape(1, fw), 1, LANES)
    nt = D_MODEL // tc
    const = lambda shape: pl.BlockSpec(shape, lambda j: (0,) * len(shape))
    out = jax.ShapeDtypeStruct((seq, D_MODEL), f32)
    osp = pl.BlockSpec((seq, tc), lambda j: (0, j))
    return pl.pallas_call(
        _filter_kernel,
        grid=(nt,),
        in_specs=[const((seq, LANES)), const((LANES, LANES)), const((1, LANES)),
                  const((HYENA_INNER, LANES, LANES)), const((HYENA_INNER, 1, LANES)), const((1, LANES)),
                  pl.BlockSpec((LANES, tc), lambda j: (0, j)),
                  pl.BlockSpec((LANES, tc), lambda j: (0, j + nt)),
                  pl.BlockSpec((1, tc), lambda j: (0, j))],
        out_specs=[osp, osp],
        out_shape=[out, out],
        compiler_params=_params("parallel"),
        name="hyena_filter",
    )(jnp.asarray(z), w1, b1, wh, bh, fq, wout, wout, jnp.asarray(deltas))


def _dft_tables(seq):
    n = 2 * seq
    half = seq // 2

    def theta(f, t):
        return (2.0 * math.pi / (2 * n)) * (((2 * f[:, None] + 1) * t[None, :]) % (2 * n)).astype(np.float64)

    f_low = np.arange(half, dtype=np.int64)
    order = np.concatenate([f_low, seq - 1 - f_low])
    th_full = theta(order, np.arange(seq, dtype=np.int64))
    tp = np.arange(half, dtype=np.int64)
    th_e, th_o = theta(f_low, 2 * tp), theta(f_low, 2 * tp + 1)
    fwd = [np.cos(th_e), np.cos(th_o), np.sin(th_e), np.sin(th_o)]
    as_bf16 = lambda a: jnp.asarray(a, bf16)
    return (as_bf16(np.cos(th_full)), as_bf16(np.sin(th_full)),
            [as_bf16(a) for a in fwd], [as_bf16(a.T * (2.0 / n)) for a in fwd])


def _spectrum_kernel(c_ref, s_ref, ks_ref, kd_ref, kre_ref, kb_ref):
    kre_ref[...] = sum(_dot(c_ref[...], part) for part in _split2(ks_ref[...]))
    kb_ref[...] = sum(_dot(s_ref[...], part) for part in _split2(kd_ref[...]))


def filter_spectrum(cmat, smat, ks, kd, tmf=256, tc=256):
    seq = ks.shape[0]
    fsp = pl.BlockSpec((tmf, seq), lambda j, m: (m, 0))
    ksp = pl.BlockSpec((seq, tc), lambda j, m: (0, j))
    osp = pl.BlockSpec((tmf, tc), lambda j, m: (m, j))
    out = jax.ShapeDtypeStruct((seq, D_MODEL), f32)
    return pl.pallas_call(
        _spectrum_kernel,
        grid=(D_MODEL // tc, seq // tmf),
        in_specs=[fsp, fsp, ksp, ksp],
        out_specs=[osp, osp],
        out_shape=[out, out],
        compiler_params=_params("parallel", "parallel"),
        name="filter_spectrum",
    )(cmat, smat, ks, kd)


def _dft_fwd_kernel(ce_ref, co_ref, se_ref, so_ref, ve_ref, vo_ref, kre_ref, kb_ref, krem_ref, kbm_ref,
                    pp_ref, pm_ref, qp_ref, qm_ref):
    ve, vo = ve_ref[...], vo_ref[...]
    ec, oc = _dot(ce_ref[...], ve), _dot(co_ref[...], vo)
    es, os_ = _dot(se_ref[...], ve), _dot(so_ref[...], vo)

    def times_filter(vre, va, kre, kb):
        return vre * kre - va * kb, vre * kb + va * kre

    p, q = times_filter(ec + oc, es + os_, kre_ref[...], kb_ref[...])
    p_m, q_m = times_filter(ec - oc, os_ - es, krem_ref[...], kbm_ref[...])
    pp_ref[...] = (p + p_m).astype(bf16)
    pm_ref[...] = (p - p_m).astype(bf16)
    qp_ref[...] = (q + q_m).astype(bf16)
    qm_ref[...] = (q - q_m).astype(bf16)


def dft_forward(fwd_tables, vg_even, vg_odd, kre, kb, seq, tmf=512):
    half = seq // 2
    nb = vg_even.shape[0] // half
    tmf = min(tmf, half)
    nm = half // tmf
    fsp = pl.BlockSpec((tmf, half), lambda b, m: (m, 0))
    ksp = pl.BlockSpec((tmf, D_MODEL), lambda b, m: (m, 0))
    kmsp = pl.BlockSpec((tmf, D_MODEL), lambda b, m: (m + nm, 0))
    osp = pl.BlockSpec((tmf, D_MODEL), lambda b, m: (b * nm + m, 0))
    out = jax.ShapeDtypeStruct((nb * half, D_MODEL), bf16)
    return pl.pallas_call(
        _dft_fwd_kernel,
        grid=(nb, nm),
        in_specs=[fsp, fsp, fsp, fsp,
                  pl.BlockSpec((half, D_MODEL), lambda b, m: (b, 0)),
                  pl.BlockSpec((half, D_MODEL), lambda b, m: (b, 0)),
                  ksp, ksp, kmsp, kmsp],
        out_specs=[osp] * 4,
        out_shape=[out] * 4,
        compiler_params=_params("parallel", "parallel"),
        name="dft_forward",
    )(*fwd_tables, vg_even, vg_odd, kre, kb, kre, kb)


def _dft_inv_kernel(gate_row, cet_ref, cot_ref, set_ref, sot_ref, pp_ref, pm_ref, qp_ref, qm_ref,
                    vge_ref, vgo_ref, x0e_ref, x0o_ref, bias_ref, w_ref, h_ref, mod_ref, o_ref, stage_ref):
    gate = mod_ref[0, gate_row:gate_row + 1, :]
    tmt = vge_ref.shape[0]
    nlane = D_MODEL // LANES
    for c in range(nlane):
        stage_ref[c] = h_ref[:, c * LANES:(c + 1) * LANES]

    def finish(y, parity, vg_ref, x0_ref):
        y = y + vg_ref[...].astype(f32) * bias_ref[...]
        upd = gate * _dot((y * x0_ref[...].astype(f32)).astype(bf16), w_ref[...])
        rows = pl.ds(parity, tmt, stride=2)
        for c in range(nlane):
            chunk = stage_ref.at[c]
            chunk[rows, :] = chunk[rows, :] + upd[:, c * LANES:(c + 1) * LANES]

    finish(_dot(cet_ref[...], pp_ref[...]) + _dot(set_ref[...], qm_ref[...]), 0, vge_ref, x0e_ref)
    finish(_dot(cot_ref[...], pm_ref[...]) + _dot(sot_ref[...], qp_ref[...]), 1, vgo_ref, x0o_ref)
    for c in range(nlane):
        o_ref[:, c * LANES:(c + 1) * LANES] = stage_ref[c]


def dft_inverse_out(inv_tables, folded, vg_eo, x0_eo, bias, w_out, h, mods, seq, gate_row, tmt=256):
    half = seq // 2
    nb = h.shape[0] // seq
    tmt = min(tmt, half)
    nm = half // tmt
    gsp = pl.BlockSpec((tmt, half), lambda b, m: (m, 0))
    full = pl.BlockSpec((half, D_MODEL), lambda b, m: (b, 0))
    part = pl.BlockSpec((tmt, D_MODEL), lambda b, m: (b * nm + m, 0))
    tile = pl.BlockSpec((2 * tmt, D_MODEL), lambda b, m: (b * nm + m, 0))
    return pl.pallas_call(
        functools.partial(_dft_inv_kernel, gate_row),
        grid=(nb, nm),
        in_specs=[gsp, gsp, gsp, gsp, full, full, full, full, part, part, part, part,
                  pl.BlockSpec((1, D_MODEL), lambda b, m: (0, 0)),
                  pl.BlockSpec((D_MODEL, D_MODEL), lambda b, m: (0, 0)),
                  tile, pl.BlockSpec((1, N_MOD, D_MODEL), lambda b, m: (b, 0, 0))],
        out_specs=tile,
        out_shape=jax.ShapeDtypeStruct(h.shape, f32),
        scratch_shapes=[pltpu.VMEM((D_MODEL // LANES, 2 * tmt, LANES), f32)],
        compiler_params=_params("parallel", "parallel"),
        name="dft_inverse_out",
    )(*inv_tables, *folded, *vg_eo, *x0_eo, bias.reshape(1, -1).astype(f32), w_out, h, mods)


_Q_HEAD_ORDER = (0, 4, 1, 5, 2, 6, 3, 7)


def _mixer_in_weight(w_in):
    c = np.cumsum([ATTN_DIM, KV_DIM, KV_DIM, D_SSM, XBC_DIM])
    q, k, v, z, xbc, dt = (w_in[:, :c[0]], w_in[:, c[0]:c[1]], w_in[:, c[1]:c[2]], w_in[:, c[2]:c[3]],
                           w_in[:, c[3]:c[4]], w_in[:, c[4]:])
    q = q.reshape(D_MODEL, N_Q_HEADS, HEAD_DIM)[:, np.array(_Q_HEAD_ORDER), :].reshape(D_MODEL, ATTN_DIM)
    w = jnp.concatenate([xbc, z, q, k, v], axis=1).astype(bf16)
    return w, jnp.pad(dt, ((0, 0), (0, LANES - dt.shape[1]))).astype(bf16)


def _mixer_out_weights(w_out):
    wa = w_out[:ATTN_DIM].reshape(N_Q_HEADS, HEAD_DIM, D_MODEL)[np.array(_Q_HEAD_ORDER)].reshape(ATTN_DIM, D_MODEL)
    return wa.astype(bf16), w_out[ATTN_DIM:].astype(bf16)


def kernel(x, c, ctx, c_ctx, w_ada, b_ada, norm_g, ffn_w13, ffn_w2, mix_w_in, mix_w_out, q_norm, k_norm, attn_sink,
           ssd_conv_w, ssd_conv_b, ssd_dt_bias, ssd_a_log, ssd_d, ssd_norm, hy_w_in, hy_conv_w, hy_conv_b,
           hy_f_w1, hy_f_b1, hy_f_wh, hy_f_bh, hy_f_wout, hy_freq, hy_bias, hy_w_out):
    nb, seq, _ = x.shape
    ctx_len = ctx.shape[1]
    depth = w_ada.shape[0]
    assert depth == 2, "this kernel is written for the two-layer block"
    h_lat = x.reshape(nb * seq, D_MODEL)
    h_ctx = ctx.reshape(nb * ctx_len, D_MODEL)
    cond = jnp.concatenate([c, c_ctx[None]], axis=0)
    cond = jnp.pad(cond, ((0, -cond.shape[0] % 8), (0, 0)))
    ffn_w = [[_ffn_weights(ffn_w13[i, k], ffn_w2[i, k]) for k in range(2)] for i in range(depth)]

    mods = ada_mods(cond, w_ada[0], b_ada[0])
    m_lat, m_ctx = mods[:nb], mods[nb:nb + 1]
    h_lat = macaron_ffn(h_lat, m_lat, seq, norm_g[0, 0], 0, *ffn_w[0][0])
    h_ctx = macaron_ffn(h_ctx, m_ctx, nb * ctx_len, norm_g[0, 0], 0, *ffn_w[0][0])
    w_in, w_dt = _mixer_in_weight(mix_w_in[0])
    wa, ws = _mixer_out_weights(mix_w_out[0])
    xc_lat, p_lat, dt_lat = mixer_proj(h_lat, m_lat, seq, seq, norm_g[0, 1], w_in, w_dt, ssd_conv_w[0], ssd_conv_b[0])
    xc_ctx, p_ctx, dt_ctx = mixer_proj(h_ctx, m_ctx, nb * ctx_len, ctx_len, norm_g[0, 1], w_in, w_dt,
                                       ssd_conv_w[0], ssd_conv_b[0])
    cos, sin = _rope_tables(seq)
    a_lat = window_attention(p_lat, p_ctx, seq, ctx_len, q_norm[0], k_norm[0], attn_sink[0], cos, sin)
    hf0, hb0 = ssd_ctx_states(xc_ctx, dt_ctx, ctx_len, ssd_dt_bias[0], ssd_a_log[0])
    s_lat = ssd_latent(xc_lat, p_lat, dt_lat, seq, hf0, hb0, ssd_dt_bias[0], ssd_a_log[0], ssd_d[0], ssd_norm[0])
    h_lat = out_proj_residual([a_lat, s_lat], [wa, ws], h_lat, m_lat, seq, 5)
    h_lat = macaron_ffn(h_lat, m_lat, seq, norm_g[0, 2], 2, *ffn_w[0][1])

    m_lat = ada_mods(cond, w_ada[1], b_ada[1])[:nb]
    h_lat = macaron_ffn(h_lat, m_lat, seq, norm_g[1, 0], 0, *ffn_w[1][0])
    x0e, x0o, vge, vgo = hyena_proj(h_lat, m_lat, seq, seq, norm_g[1, 1], hy_w_in[0].astype(bf16),
                                    hy_conv_w[0], hy_conv_b[0])
    ks, kd = hyena_filter_taps(seq, hy_f_w1[0], hy_f_b1[0], hy_f_wh[0], hy_f_bh[0], hy_f_wout[0], hy_freq[0])
    c_full, s_full, fwd_tables, inv_tables = _dft_tables(seq)
    kre, kb = filter_spectrum(c_full, s_full, ks, kd)
    folded = dft_forward(fwd_tables, vge, vgo, kre, kb, seq)
    h_lat = dft_inverse_out(inv_tables, folded, (vge, vgo), (x0e, x0o), hy_bias[0], hy_w_out[0].astype(bf16),
                            h_lat, m_lat, seq, 5)
    h_lat = macaron_ffn(h_lat, m_lat, seq, norm_g[1, 2], 2, *ffn_w[1][1])
    return h_lat.reshape(nb, seq, D_MODEL)
```

```python
import functools
import math

import numpy as np
import jax
import jax.numpy as jnp
from jax import lax
from jax.experimental import pallas as pl
from jax.experimental.pallas import tpu as pltpu

f32 = jnp.float32
bf16 = jnp.bfloat16

D_MODEL = 1024
N_MOD = 9
RMS_EPS = 1e-6
GRID_W = 64

HEAD_DIM = 64
N_Q_HEADS = 8
N_KV_HEADS = 2
ATTN_DIM = N_Q_HEADS * HEAD_DIM
KV_DIM = N_KV_HEADS * HEAD_DIM
WINDOW = 128
ATTN_BLOCK = 128
ROPE_THETA = 10000.0

SSD_HEADS = 16
SSD_HEAD_DIM = 64
D_SSM = SSD_HEADS * SSD_HEAD_DIM
SSD_GROUPS = 2
SSD_STATE = 128
SSD_CONV = 7
SSD_CHUNK = 128
BC_DIM = SSD_GROUPS * SSD_STATE
XBC_DIM = D_SSM + 4 * BC_DIM
GROUP_W = D_SSM // SSD_GROUPS

HYENA_SHORT = 3
HYENA_BANDS = 8
HYENA_FILTER_WIDTH = 64
HYENA_INNER = 2
HYENA_FAST_DECAY = 0.3
HYENA_SLOW_DECAY = 1.5
HYENA_TARGET = 1e-2

D_FF = 2816
FFN_TF = 256
LANES = 128

COL_Z = 0
COL_Q = COL_Z + D_SSM
COL_K = COL_Q + ATTN_DIM
COL_V = COL_K + KV_DIM
REST_COLS = COL_V + KV_DIM
HALO = 16
PROJ_TN = 256
PROJ_TM = 256
CONV_ROWS = 64

VMEM_LIMIT = 56 * 1024 * 1024


def _params(*sem):
    return pltpu.CompilerParams(dimension_semantics=sem, vmem_limit_bytes=VMEM_LIMIT)


def _dot(a, b):
    return jnp.dot(a, b, preferred_element_type=f32)


def _dot_nt(a, b):
    return lax.dot_general(a, b, (((1,), (1,)), ((), ())), preferred_element_type=f32)


def _split2(x):
    hi = x.astype(bf16)
    lo = (x - hi.astype(f32)).astype(bf16)
    return hi, lo


def _split3(x):
    hi = x.astype(bf16)
    r = x - hi.astype(f32)
    mid = r.astype(bf16)
    lo = (r - mid.astype(f32)).astype(bf16)
    return hi, mid, lo


def _adaln(h, g, shift, scale):
    ms = jnp.mean(h * h, axis=-1, keepdims=True)
    return (h * lax.rsqrt(ms + RMS_EPS) * g) * (1.0 + scale) + shift


def _silu(x):
    return x * jax.nn.sigmoid(x)


def _mods_kernel(c_ref, w_ref, b_ref, o_ref):
    o_ref[...] = _dot(_silu(c_ref[...]).astype(bf16), w_ref[...].astype(bf16)) + b_ref[...]


def ada_mods(cond, w, b):
    r = cond.shape[0]
    tn = 1024
    out = pl.pallas_call(
        _mods_kernel,
        grid=(w.shape[1] // tn,),
        in_specs=[pl.BlockSpec((r, D_MODEL), lambda j: (0, 0)),
                  pl.BlockSpec((D_MODEL, tn), lambda j: (0, j)),
                  pl.BlockSpec((1, tn), lambda j: (0, j))],
        out_specs=pl.BlockSpec((r, tn), lambda j: (0, j)),
        out_shape=jax.ShapeDtypeStruct((r, w.shape[1]), f32),
        compiler_params=_params("parallel"),
        name="ada_mods",
    )(cond, w, b.reshape(1, -1))
    return out.reshape(r, N_MOD, D_MODEL)


def _ffn_kernel(s, nf, h_ref, mod_ref, g_ref, w13_ref, w2_ref, o_ref):
    h = h_ref[...]
    u = _adaln(h, g_ref[...], mod_ref[0, 3 * s:3 * s + 1, :], mod_ref[0, 3 * s + 1:3 * s + 2, :]).astype(bf16)
    acc = None
    tf = w2_ref.shape[1]
    for j in range(nf):
        a = _dot(u, w13_ref[:, j * tf:(j + 1) * tf])
        b = _dot(u, w13_ref[:, D_FF + j * tf:D_FF + (j + 1) * tf])
        part = _dot((_silu(a) * b).astype(bf16), w2_ref[j])
        acc = part if acc is None else acc + part
    o_ref[...] = h + 0.5 * mod_ref[0, 3 * s + 2:3 * s + 3, :] * acc


def _ffn_weights(w13, w2, tf=FFN_TF):
    return w13.astype(bf16), w2.astype(bf16).reshape(D_FF // tf, tf, D_MODEL)


def macaron_ffn(h, mods, rows_per_mod, g, s, w13c, w2c, tm=512):
    m = h.shape[0]
    tm = min(tm, rows_per_mod)
    nf = w2c.shape[0]
    tiles_per_mod = rows_per_mod // tm
    return pl.pallas_call(
        functools.partial(_ffn_kernel, s, nf),
        grid=(m // tm,),
        in_specs=[pl.BlockSpec((tm, D_MODEL), lambda i: (i, 0)),
                  pl.BlockSpec((1, N_MOD, D_MODEL), lambda i: (i // tiles_per_mod, 0, 0)),
                  pl.BlockSpec((1, D_MODEL), lambda i: (0, 0)),
                  pl.BlockSpec(w13c.shape, lambda i: (0, 0)),
                  pl.BlockSpec(w2c.shape, lambda i: (0, 0, 0))],
        out_specs=pl.BlockSpec((tm, D_MODEL), lambda i: (i, 0)),
        out_shape=jax.ShapeDtypeStruct((m, D_MODEL), f32),
        compiler_params=_params("parallel"),
        name="macaron_ffn",
    )(h, mods, g.reshape(1, -1), w13c, w2c)


def _tile_adaln(s, tiles_per_seq, h_ref, hp_ref, hn_ref, mod_ref, g_ref, u_ref, uh_ref):
    shift, scale = mod_ref[0, 3 * s:3 * s + 1, :], mod_ref[0, 3 * s + 1:3 * s + 2, :]
    g = g_ref[...]
    u_ref[...] = _adaln(h_ref[...], g, shift, scale).astype(bf16)
    uh_ref[...] = _adaln(jnp.concatenate([hp_ref[...], hn_ref[...]], axis=0), g, shift, scale).astype(bf16)
    t = pl.program_id(0) % tiles_per_seq
    row = lax.broadcasted_iota(jnp.int32, (2 * HALO, 1), 0)
    keep_prev = jnp.where(t > 0, 1.0, 0.0)
    keep_next = jnp.where(t < tiles_per_seq - 1, 1.0, 0.0)
    return jnp.where(row < HALO, keep_prev, keep_next)


def _project_padded(pad_ref, u_ref, uh_ref, keep, w):
    tm = u_ref.shape[0]

    def halo_rows():
        halo = _dot(uh_ref[...], w) * keep
        pad_ref[0:HALO, :] = halo[0:HALO, :]
        pad_ref[HALO + tm:2 * HALO + tm, :] = halo[HALO:2 * HALO, :]

    def tile_rows(r0):
        pad_ref[HALO + r0:HALO + r0 + PROJ_TM, :] = _dot(u_ref[r0:r0 + PROJ_TM, :], w)

    return [halo_rows] + [functools.partial(tile_rows, r0) for r0 in range(0, tm, PROJ_TM)]


def _conv_padded(pad_ref, conv_w, conv_b, emit, between=()):
    tm = pad_ref.shape[0] - 2 * HALO
    width = conv_w.shape[0]
    rows = CONV_ROWS
    steps = [(l0, r0) for l0 in range(0, pad_ref.shape[1], LANES) for r0 in range(0, tm, rows)]
    every = max(1, len(steps) // max(1, len(between)))
    pending = list(between)
    half = width // 2
    for n, (l0, r0) in enumerate(steps):
        if pending and n % every == 0:
            pending.pop(0)()
        lanes = slice(l0, l0 + LANES)
        win = pad_ref[HALO + r0 - 8:HALO + r0 + rows + 8, lanes]
        acc = conv_b[:, lanes] + conv_w[half:half + 1, lanes] * win[8:8 + rows, :]
        for k in range(width):
            if k != half:
                acc += conv_w[k:k + 1, lanes] * pltpu.roll(win, (half - k) % (rows + 16), 0)[8:8 + rows, :]
        emit(r0, rows, l0, acc)
    for thunk in pending:
        thunk()


def _mixer_proj_kernel(s, tiles_per_seq, h_ref, hp_ref, hn_ref, mod_ref, g_ref, w_ref, wdt_ref, cw_ref, cb_ref,
                       xc_ref, rest_ref, dt_ref, pad_ref, u_ref, uh_ref):
    keep = _tile_adaln(s, tiles_per_seq, h_ref, hp_ref, hn_ref, mod_ref, g_ref, u_ref, uh_ref)
    tn = PROJ_TN
    n_conv = XBC_DIM // tn
    n_rest = REST_COLS // tn

    tm = u_ref.shape[0]

    def project(c):
        return _project_padded(pad_ref.at[c % 2], u_ref, uh_ref, keep, w_ref[:, c * tn:(c + 1) * tn])

    def plain_rows(c, r0):
        rows = slice(r0, r0 + PROJ_TM)
        if c < n_rest:
            w = w_ref[:, XBC_DIM + c * tn:XBC_DIM + (c + 1) * tn]
            rest_ref[rows, c * tn:(c + 1) * tn] = _dot(u_ref[rows, :], w).astype(bf16)
        elif c == n_rest:
            dt_ref[rows, :] = _dot(u_ref[rows, :], wdt_ref[...])

    def plain(c):
        return [functools.partial(plain_rows, c, r0) for r0 in range(0, tm, PROJ_TM)]

    for thunk in project(0):
        thunk()
    for c in range(n_conv):
        def emit(r0, rows, l0, acc, c0=c * tn):
            xc_ref[r0:r0 + rows, c0 + l0:c0 + l0 + LANES] = _silu(acc).astype(bf16)

        between = (project(c + 1) if c + 1 < n_conv else []) + plain(c)
        _conv_padded(pad_ref.at[c % 2], cw_ref[:, c * tn:(c + 1) * tn], cb_ref[:, c * tn:(c + 1) * tn], emit, between)
    for c in range(n_conv, n_rest + 1):
        for thunk in plain(c):
            thunk()


def _hyena_proj_kernel(s, tiles_per_seq, h_ref, hp_ref, hn_ref, mod_ref, g_ref, w_ref, cw_ref, cb_ref,
                       x0e_ref, x0o_ref, vge_ref, vgo_ref, pad_ref, u_ref, uh_ref, x1_ref, split_ref):
    keep = _tile_adaln(s, tiles_per_seq, h_ref, hp_ref, hn_ref, mod_ref, g_ref, u_ref, uh_ref)
    tn = PROJ_TN
    cols = [part * D_MODEL + c0 for c0 in range(0, D_MODEL, tn) for part in range(3)]

    def project(i):
        return _project_padded(pad_ref.at[i % 2], u_ref, uh_ref, keep, w_ref[:, cols[i]:cols[i] + tn])

    for thunk in project(0):
        thunk()
    for i, col in enumerate(cols):
        between = project(i + 1) if i + 1 < len(cols) else []
        c0 = col % D_MODEL

        def split_tokens(value, even_ref, odd_ref, r0, rows, lanes):
            split_ref[...] = value
            dst = slice(r0 // 2, (r0 + rows) // 2)
            even_ref[dst, lanes] = split_ref[pl.ds(0, rows // 2, stride=2), :].astype(bf16)
            odd_ref[dst, lanes] = split_ref[pl.ds(1, rows // 2, stride=2), :].astype(bf16)

        def emit_x0(r0, rows, l0, acc, c0=c0):
            split_tokens(acc, x0e_ref, x0o_ref, r0, rows, slice(c0 + l0, c0 + l0 + LANES))

        def emit_x1(r0, rows, l0, acc):
            x1_ref[r0:r0 + rows, l0:l0 + LANES] = acc

        def emit_v(r0, rows, l0, acc, c0=c0):
            vg = acc * x1_ref[r0:r0 + rows, l0:l0 + LANES]
            split_tokens(vg, vge_ref, vgo_ref, r0, rows, slice(c0 + l0, c0 + l0 + LANES))

        emit = (emit_x0, emit_x1, emit_v)[col // D_MODEL]
        _conv_padded(pad_ref.at[i % 2], cw_ref[:, col:col + tn], cb_ref[:, col:col + tn], emit, between)


def _proj_call(body, name, h, mods, rows_per_mod, seq, g, consts, outs, extra_scratch, tm=512):
    m = h.shape[0]
    tm = min(tm, seq)
    tiles_per_seq = seq // tm
    tiles_per_mod = rows_per_mod // tm
    hb = tm // HALO
    in_specs = [pl.BlockSpec((tm, D_MODEL), lambda i: (i, 0)),
                pl.BlockSpec((HALO, D_MODEL), lambda i: (jnp.maximum(i * hb - 1, 0), 0)),
                pl.BlockSpec((HALO, D_MODEL), lambda i: (jnp.minimum((i + 1) * hb, m // HALO - 1), 0)),
                pl.BlockSpec((1, N_MOD, D_MODEL), lambda i: (i // tiles_per_mod, 0, 0)),
                pl.BlockSpec((1, D_MODEL), lambda i: (0, 0))]
    in_specs += [pl.BlockSpec(a.shape, lambda i: (0, 0)) for a in consts]
    return pl.pallas_call(
        functools.partial(body, 1, tiles_per_seq),
        grid=(m // tm,),
        in_specs=in_specs,
        out_specs=[pl.BlockSpec((tm // div, n), lambda i: (i, 0)) for div, n, _ in outs],
        out_shape=[jax.ShapeDtypeStruct((m // div, n), dt) for div, n, dt in outs],
        scratch_shapes=[pltpu.VMEM((2, tm + 2 * HALO, PROJ_TN), f32), pltpu.VMEM((tm, D_MODEL), bf16),
                        pltpu.VMEM((2 * HALO, D_MODEL), bf16)] + extra_scratch(tm),
        compiler_params=_params("parallel"),
        name=name,
    )(h, h, h, mods, g.reshape(1, -1), *consts)


def mixer_proj(h, mods, rows_per_mod, seq, g, w, w_dt, conv_w, conv_b):
    consts = [w, w_dt, conv_w.astype(f32), conv_b.reshape(1, -1).astype(f32)]
    return _proj_call(_mixer_proj_kernel, "mixer_proj", h, mods, rows_per_mod, seq, g, consts,
                      [(1, XBC_DIM, bf16), (1, REST_COLS, bf16), (1, LANES, f32)], lambda tm: [])


def hyena_proj(h, mods, rows_per_mod, seq, g, w, conv_w, conv_b):
    consts = [w, conv_w.astype(f32), conv_b.reshape(1, -1).astype(f32)]
    scratch = lambda tm: [pltpu.VMEM((tm, PROJ_TN), f32), pltpu.VMEM((CONV_ROWS, LANES), f32)]
    return _proj_call(_hyena_proj_kernel, "hyena_proj", h, mods, rows_per_mod, seq, g, consts,
                      [(2, D_MODEL, bf16)] * 4, scratch)


def _outproj_kernel(n_in, gate_row, *refs):
    x_refs = refs[:n_in]
    w_refs = refs[n_in:2 * n_in]
    h_ref, mod_ref, o_ref = refs[2 * n_in:]
    acc = _dot(x_refs[0][...].astype(bf16), w_refs[0][...])
    for x_ref, w_ref in zip(x_refs[1:], w_refs[1:]):
        acc += _dot(x_ref[...].astype(bf16), w_ref[...])
    o_ref[...] = h_ref[...] + mod_ref[0, gate_row:gate_row + 1, :] * acc


def out_proj_residual(xs, ws, h, mods, rows_per_mod, gate_row, tm=512):
    m = h.shape[0]
    tm = min(tm, rows_per_mod)
    tiles_per_mod = rows_per_mod // tm
    n_in = len(xs)
    in_specs = ([pl.BlockSpec((tm, x.shape[1]), lambda i: (i, 0)) for x in xs]
                + [pl.BlockSpec(w.shape, lambda i: (0, 0)) for w in ws]
                + [pl.BlockSpec((tm, D_MODEL), lambda i: (i, 0)),
                   pl.BlockSpec((1, N_MOD, D_MODEL), lambda i: (i // tiles_per_mod, 0, 0))])
    return pl.pallas_call(
        functools.partial(_outproj_kernel, n_in, gate_row),
        grid=(m // tm,),
        in_specs=in_specs,
        out_specs=pl.BlockSpec((tm, D_MODEL), lambda i: (i, 0)),
        out_shape=jax.ShapeDtypeStruct((m, D_MODEL), f32),
        compiler_params=_params("parallel"),
        name="out_proj_residual",
    )(*xs, *ws, h, mods)


def _head_norm(x, gain, bd):
    hi, lo = _split2(x * x)
    ms = _dot(hi, bd) + _dot(lo, bd)
    return x * lax.rsqrt(ms + RMS_EPS) * gain


def _rope(x, cos, sin_signed):
    lane = lax.broadcasted_iota(jnp.int32, x.shape, 1)
    partner = jnp.where((lane & 16) != 0, pltpu.roll(x, 16, 1), pltpu.roll(x, LANES - 16, 1))
    return x * cos + partner * sin_signed


def _t_bf16(x):
    return x.astype(f32).T.astype(bf16)


def _attn_kernel(seq, q_ref, k_ref, v_ref, kc_ref, vc_ref, qg_ref, kg_ref, cos_ref, sin_ref, bd_ref,
                 sink_ref, o_ref, qt_s, k_s, vt_s, kc_s, vct_s):
    j = pl.program_id(1)
    nblk = seq // ATTN_BLOCK
    nslab = ATTN_DIM // LANES
    blk = ATTN_BLOCK

    @pl.when(j == 0)
    def _():
        bd = bd_ref[...]
        cos, sin = cos_ref[...], sin_ref[...]
        scale = HEAD_DIM ** -0.5
        for p in range(nslab):
            qn = _head_norm(q_ref[:, p * LANES:(p + 1) * LANES].astype(f32), qg_ref[...], bd)
            qr = _rope(qn, cos, sin) * scale
            for jb in range(nblk):
                qt_s[jb, p * LANES:(p + 1) * LANES, :] = _t_bf16(qr[jb * blk:(jb + 1) * blk, :])
        kn = _head_norm(k_ref[...].astype(f32), kg_ref[...], bd)
        zeros = jnp.zeros((WINDOW, KV_DIM), bf16)
        k_s[0:WINDOW, :] = zeros
        k_s[WINDOW + seq:2 * WINDOW + seq, :] = zeros
        k_s[WINDOW:WINDOW + seq, :] = _rope(kn, cos, sin).astype(bf16)
        vt_s[0] = zeros
        vt_s[nblk + 1] = zeros
        for jb in range(nblk):
            vt_s[jb + 1] = _t_bf16(v_ref[jb * blk:(jb + 1) * blk, :])
        kc_s[...] = _head_norm(kc_ref[...].astype(f32), kg_ref[...], bd).astype(bf16)
        vct_s[...] = _t_bf16(vc_ref[...])

    band = blk + 2 * WINDOW
    start = pl.multiple_of(j * blk, blk)
    kb = k_s[pl.ds(start, band), :]
    kc = kc_s[...]
    vtb = jnp.concatenate([vt_s[j], vt_s[j + 1], vt_s[j + 2]], axis=1)
    vtc = vct_s[...]
    qt = qt_s[j]
    key = lax.broadcasted_iota(jnp.int32, (blk, 2 * blk), 0)
    qry = lax.broadcasted_iota(jnp.int32, (blk, 2 * blk), 1) & (blk - 1)
    ok_lo = (jnp.abs(qry - (key - WINDOW)) <= WINDOW) & (start - WINDOW + key >= 0)
    ok_hi = (jnp.abs(qry - (key + blk)) <= WINDOW) & (start + blk + key < seq)
    dim = lax.broadcasted_iota(jnp.int32, (LANES, blk), 0)
    lane2 = lax.broadcasted_iota(jnp.int32, (1, 2 * blk), 1)
    ones_b = jnp.ones((16, band), bf16)
    ones_c = jnp.ones((16, kc.shape[0]), bf16)
    def scores(p):
        qslab = qt[p * LANES:(p + 1) * LANES, :]
        zero = jnp.zeros_like(qslab)
        rhs = jnp.concatenate([jnp.where(dim < HEAD_DIM, qslab, zero), jnp.where(dim >= HEAD_DIM, qslab, zero)], axis=1)
        return _dot(kb, rhs), _dot(kc, rhs)

    def softmax(p, sb, sc):
        s_lo = jnp.where(ok_lo, sb[0:blk, :], -jnp.inf)
        s_mid = sb[blk:2 * blk, :]
        s_hi = jnp.where(ok_hi, sb[2 * blk:3 * blk, :], -jnp.inf)
        sink = jnp.where(lane2 < blk, sink_ref[p], sink_ref[p + N_Q_HEADS // N_KV_HEADS])
        colmax = lambda s: jnp.max(s, axis=0, keepdims=True)
        mx = jnp.maximum(jnp.maximum(jnp.maximum(colmax(s_lo), colmax(s_mid)), jnp.maximum(colmax(s_hi), colmax(sc))),
                         sink)
        pb = jnp.concatenate([jnp.exp(s_lo - mx).astype(bf16), jnp.exp(s_mid - mx).astype(bf16),
                              jnp.exp(s_hi - mx).astype(bf16)], axis=0)
        return pb, jnp.exp(sc - mx).astype(bf16), jnp.exp(sink - mx)

    def values(p, pb, pc, sink_term):
        den = (_dot(ones_b, pb) + _dot(ones_c, pc))[0:1, :] + sink_term
        ot = (_dot(vtb, pb) + _dot(vtc, pc)) / den
        both = jnp.where(dim < HEAD_DIM, ot[:, 0:blk], ot[:, blk:2 * blk])
        o_ref[:, p * LANES:(p + 1) * LANES] = both.T.astype(bf16)

    s_next = scores(0)
    for p in range(nslab):
        s_cur = s_next
        if p + 1 < nslab:
            s_next = scores(p + 1)
        values(p, *softmax(p, *s_cur))


def window_attention(proj_lat, proj_ctx, seq, ctx_len, q_gain, k_gain, sink, rope_cos, rope_sin):
    nb = proj_lat.shape[0] // seq
    nblk = seq // ATTN_BLOCK
    bd = np.kron(np.eye(LANES // HEAD_DIM), np.ones((HEAD_DIM, HEAD_DIM))) / HEAD_DIM
    gain2 = lambda g: jnp.tile(g, LANES // HEAD_DIM).reshape(1, LANES)
    const = lambda shape: pl.BlockSpec(shape, lambda b, j: (0, 0))
    return pl.pallas_call(
        functools.partial(_attn_kernel, seq),
        grid=(nb, nblk),
        in_specs=[pl.BlockSpec((seq, ATTN_DIM), lambda b, j: (b, COL_Q // ATTN_DIM)),
                  pl.BlockSpec((seq, KV_DIM), lambda b, j: (b, COL_K // KV_DIM)),
                  pl.BlockSpec((seq, KV_DIM), lambda b, j: (b, COL_V // KV_DIM)),
                  pl.BlockSpec((ctx_len, KV_DIM), lambda b, j: (b, COL_K // KV_DIM)),
                  pl.BlockSpec((ctx_len, KV_DIM), lambda b, j: (b, COL_V // KV_DIM)),
                  const((1, LANES)), const((1, LANES)),
                  const((seq, LANES)), const((seq, LANES)), const((LANES, LANES)),
                  pl.BlockSpec(memory_space=pltpu.SMEM)],
        out_specs=pl.BlockSpec((ATTN_BLOCK, ATTN_DIM), lambda b, j: (b * nblk + j, 0)),
        out_shape=jax.ShapeDtypeStruct((nb * seq, ATTN_DIM), bf16),
        scratch_shapes=[pltpu.VMEM((nblk, ATTN_DIM, ATTN_BLOCK), bf16),
                        pltpu.VMEM((seq + 2 * WINDOW, KV_DIM), bf16),
                        pltpu.VMEM((nblk + 2, KV_DIM, ATTN_BLOCK), bf16),
                        pltpu.VMEM((ctx_len, KV_DIM), bf16),
                        pltpu.VMEM((KV_DIM, ctx_len), bf16)],
        compiler_params=_params("parallel", "arbitrary"),
        name="window_attention",
    )(proj_lat, proj_lat, proj_lat, proj_ctx, proj_ctx, gain2(q_gain), gain2(k_gain),
      rope_cos, rope_sin, jnp.asarray(bd, bf16), sink)


def _rope_tables(seq):
    t = np.arange(seq)
    pos = np.stack([t // GRID_W, t % GRID_W], axis=1).astype(np.float32)
    axis_dim = HEAD_DIM // 2
    inv = (ROPE_THETA ** (-np.arange(0, axis_dim, 2, dtype=np.float32) / axis_dim)).astype(np.float32)
    lane = np.arange(LANES)
    d = lane % HEAD_DIM
    which = d // axis_dim
    ang = (pos[:, which] * inv[d % (axis_dim // 2)][None, :]).astype(np.float32)
    sign = np.where((d % axis_dim) < axis_dim // 2, -1.0, 1.0)
    return jnp.asarray(np.cos(ang), f32), jnp.asarray(np.sin(ang) * sign, f32)


def _softplus(x):
    return jnp.maximum(x, 0.0) + jnp.log1p(jnp.exp(-jnp.abs(x)))


def _expand_heads(v, e):
    return _dot(v.astype(bf16), e)


def _ssd_chunk(rev, lane0, want_y, x, bm, cm, dt_raw, dt_bias, a_neg, expand, state_ref, result):
    t = x.shape[0]
    dt = _softplus(dt_raw + dt_bias)
    a = dt * a_neg
    r = lax.broadcasted_iota(jnp.int32, (t, t), 0)
    c = lax.broadcasted_iota(jnp.int32, (t, t), 1)
    keep = (r <= c) if rev else (r >= c)
    tri = jnp.where(keep, 1.0, 0.0).astype(bf16)
    cs = sum(_dot(tri, part) for part in _split3(a))
    last = cs[0:1, :] if rev else cs[t - 1:t, :]
    e = jnp.exp(cs)
    w = dt * jnp.exp(last - cs)
    e_x = _expand_heads(e, expand)
    w_x = _expand_heads(w, expand)
    elast_x = e_x[0:1, :] if rev else e_x[t - 1:t, :]
    yield

    y = None
    if want_y:
        cs_t = cs.T
        dt_t = dt.T
        lane = lax.broadcasted_iota(jnp.int32, (t, LANES), 1)
        cb = [_dot_nt(cm[:, g * SSD_STATE:(g + 1) * SSD_STATE],
                      bm[:, g * SSD_STATE:(g + 1) * SSD_STATE]) for g in range(SSD_GROUPS)]
        yield
        pieces = []
        for p in range(SSD_HEADS // 2):
            xp = x[:, p * LANES:(p + 1) * LANES]
            ms = []
            for q in range(2):
                h = 2 * p + q
                g = h // (SSD_HEADS // SSD_GROUPS)
                seg = cs[:, lane0 + h:lane0 + h + 1] - cs_t[lane0 + h:lane0 + h + 1, :]
                dec = jnp.exp(jnp.where(keep, seg, -jnp.inf))
                ms.append((cb[g] * dec * dt_t[lane0 + h:lane0 + h + 1, :]).astype(bf16))
            zero = jnp.zeros_like(xp)
            xcat = jnp.concatenate([jnp.where(lane < SSD_HEAD_DIM, xp, zero),
                                    jnp.where(lane >= SSD_HEAD_DIM, xp, zero)], axis=0)
            pieces.append(_dot(jnp.concatenate(ms, axis=1), xcat))
            yield
        y = jnp.concatenate(pieces, axis=1)

    inter = []
    for g in range(SSD_GROUPS):
        gs = slice(g * GROUP_W, (g + 1) * GROUP_W)
        ss = slice(g * SSD_STATE, (g + 1) * SSD_STATE)
        h_t = state_ref[g]
        if want_y:
            inter.append(_dot(cm[:, ss], h_t.astype(bf16)) * e_x[:, gs])
        xw = (x[:, gs].astype(f32) * w_x[:, gs]).astype(bf16)
        state_ref[g] = h_t * elast_x[:, gs] + _dot(bm[:, ss].astype(f32).T.astype(bf16), xw)
        yield
    if want_y:
        result.append(y + jnp.concatenate(inter, axis=1))


def _interleave(*stage_generators):
    active = list(stage_generators)
    while active:
        for gen in list(active):
            if next(gen, StopIteration) is StopIteration:
                active.remove(gen)


def _ssd_ctx_kernel(nchunk, x_ref, bc_ref, dt_ref, bias_ref, alog_ref, ef_ref, eb_ref, hf_ref, hb_ref, sf, sb):
    sf[...] = jnp.zeros_like(sf)
    sb[...] = jnp.zeros_like(sb)
    a_neg = -jnp.exp(alog_ref[...])
    bias = bias_ref[...]
    t = SSD_CHUNK
    for ci in range(nchunk):
        rows = slice(ci * t, (ci + 1) * t)
        fwd = _ssd_chunk(False, 0, False, x_ref[rows, :], bc_ref[rows, 0:BC_DIM], None, dt_ref[rows, :],
                         bias, a_neg, ef_ref[...], sf, None)
        rows = slice((nchunk - 1 - ci) * t, (nchunk - ci) * t)
        bwd = _ssd_chunk(True, SSD_HEADS, False, x_ref[rows, :], bc_ref[rows, BC_DIM:2 * BC_DIM], None,
                         dt_ref[rows, :], bias, a_neg, eb_ref[...], sb, None)
        _interleave(fwd, bwd)
    hf_ref[0] = sf[...]
    hb_ref[0] = sb[...]


def _head_expanders():
    ef = np.zeros((LANES, D_SSM), np.float32)
    eb = np.zeros((LANES, D_SSM), np.float32)
    for h in range(SSD_HEADS):
        ef[h, h * SSD_HEAD_DIM:(h + 1) * SSD_HEAD_DIM] = 1.0
        eb[SSD_HEADS + h, h * SSD_HEAD_DIM:(h + 1) * SSD_HEAD_DIM] = 1.0
    return jnp.asarray(ef, bf16), jnp.asarray(eb, bf16)


def _pad_lanes(v):
    v = v.reshape(1, -1).astype(f32)
    return jnp.pad(v, ((0, 0), (0, LANES - v.shape[1])))


def ssd_ctx_states(xc, dt_raw, seq, dt_bias, a_log):
    nb = xc.shape[0] // seq
    ef, eb = _head_expanders()
    const = lambda shape: pl.BlockSpec(shape, lambda b: (0,) * len(shape))
    st = jax.ShapeDtypeStruct((nb, SSD_GROUPS, SSD_STATE, GROUP_W), f32)
    st_spec = pl.BlockSpec((1, SSD_GROUPS, SSD_STATE, GROUP_W), lambda b: (b, 0, 0, 0))
    return pl.pallas_call(
        functools.partial(_ssd_ctx_kernel, seq // SSD_CHUNK),
        grid=(nb,),
        in_specs=[pl.BlockSpec((seq, D_SSM), lambda b: (b, 0)),
                  pl.BlockSpec((seq, 2 * BC_DIM), lambda b: (b, D_SSM // (2 * BC_DIM))),
                  pl.BlockSpec((seq, LANES), lambda b: (b, 0)),
                  const((1, LANES)), const((1, LANES)), const((LANES, D_SSM)), const((LANES, D_SSM))],
        out_specs=[st_spec, st_spec],
        out_shape=[st, st],
        scratch_shapes=[pltpu.VMEM((SSD_GROUPS, SSD_STATE, GROUP_W), f32),
                        pltpu.VMEM((SSD_GROUPS, SSD_STATE, GROUP_W), f32)],
        compiler_params=_params("parallel"),
        name="ssd_ctx_states",
    )(xc, xc, dt_raw, _pad_lanes(dt_bias), _pad_lanes(a_log), ef, eb)


def _ssd_lat_kernel(nchunk, xf_ref, xb_ref, bf_ref, bb_ref, cf_ref, cb_ref, dtf_ref, dtb_ref, zf_ref, zb_ref,
                    hf0_ref, hb0_ref, bias_ref, alog_ref, dskip_ref, normw_ref, ef_ref, eb_ref,
                    o_ref, sf, sb, yacc):
    c = pl.program_id(1)
    t = SSD_CHUNK

    @pl.when(c == 0)
    def _():
        sf[...] = hf0_ref[0]
        sb[...] = hb0_ref[0]

    a_neg = -jnp.exp(alog_ref[...])
    bias = bias_ref[...]
    xf = xf_ref[...]
    yf, yb = [], []
    _interleave(
        _ssd_chunk(False, 0, True, xf, bf_ref[...], cf_ref[...], dtf_ref[...], bias, a_neg, ef_ref[...], sf, yf),
        _ssd_chunk(True, SSD_HEADS, True, xb_ref[...], bb_ref[...], cb_ref[...], dtb_ref[...], bias, a_neg,
                   eb_ref[...], sb, yb))
    yf = yf[0] + dskip_ref[...] * xf.astype(f32)
    yb = yb[0]
    rows_f = pl.ds(pl.multiple_of(c * t, t), t)
    rows_b = pl.ds(pl.multiple_of((nchunk - 1 - c) * t, t), t)

    @pl.when(c < nchunk // 2)
    def _():
        yacc[rows_f, :] = yf
        yacc[rows_b, :] = yb

    def finish(y, z):
        y = y * _silu(z.astype(f32))
        outs = []
        for g in range(SSD_GROUPS):
            yg = y[:, g * GROUP_W:(g + 1) * GROUP_W]
            outs.append(yg * lax.rsqrt(jnp.mean(yg * yg, axis=-1, keepdims=True) + RMS_EPS))
        return (jnp.concatenate(outs, axis=1) * normw_ref[...]).astype(bf16)

    @pl.when(c >= nchunk // 2)
    def _():
        o_ref[rows_f, :] = finish(yacc[rows_f, :] + yf, zf_ref[...])
        o_ref[rows_b, :] = finish(yacc[rows_b, :] + yb, zb_ref[...])


def ssd_latent(xc, proj, dt_raw, seq, hf0, hb0, dt_bias, a_log, d_skip, norm_w):
    nb = xc.shape[0] // seq
    nc = seq // SSD_CHUNK
    half = nc // 2
    ef, eb = _head_expanders()
    t = SSD_CHUNK
    fwd = lambda b, c: b * nc + c
    bwd = lambda b, c: b * nc + nc - 1 - c
    zfw = lambda b, c: b * nc + jnp.maximum(c, half)
    zbw = lambda b, c: b * nc + jnp.minimum(nc - 1 - c, half - 1)
    bc0 = D_SSM // SSD_STATE // SSD_GROUPS
    const = lambda shape: pl.BlockSpec(shape, lambda b, c: (0,) * len(shape))
    st_spec = pl.BlockSpec((1, SSD_GROUPS, SSD_STATE, GROUP_W), lambda b, c: (b, 0, 0, 0))
    dskip = jnp.repeat(d_skip.astype(f32), SSD_HEAD_DIM).reshape(1, D_SSM)
    return pl.pallas_call(
        functools.partial(_ssd_lat_kernel, nc),
        grid=(nb, nc),
        in_specs=[pl.BlockSpec((t, D_SSM), lambda b, c: (fwd(b, c), 0)),
                  pl.BlockSpec((t, D_SSM), lambda b, c: (bwd(b, c), 0)),
                  pl.BlockSpec((t, BC_DIM), lambda b, c: (fwd(b, c), bc0)),
                  pl.BlockSpec((t, BC_DIM), lambda b, c: (bwd(b, c), bc0 + 1)),
                  pl.BlockSpec((t, BC_DIM), lambda b, c: (fwd(b, c), bc0 + 2)),
                  pl.BlockSpec((t, BC_DIM), lambda b, c: (bwd(b, c), bc0 + 3)),
                  pl.BlockSpec((t, LANES), lambda b, c: (fwd(b, c), 0)),
                  pl.BlockSpec((t, LANES), lambda b, c: (bwd(b, c), 0)),
                  pl.BlockSpec((t, D_SSM), lambda b, c: (zfw(b, c), COL_Z // D_SSM)),
                  pl.BlockSpec((t, D_SSM), lambda b, c: (zbw(b, c), COL_Z // D_SSM)),
                  st_spec, st_spec,
                  const((1, LANES)), const((1, LANES)), const((1, D_SSM)), const((1, D_SSM)),
                  const((LANES, D_SSM)), const((LANES, D_SSM))],
        out_specs=pl.BlockSpec((seq, D_SSM), lambda b, c: (b, 0)),
        out_shape=jax.ShapeDtypeStruct((nb * seq, D_SSM), bf16),
        scratch_shapes=[pltpu.VMEM((SSD_GROUPS, SSD_STATE, GROUP_W), f32),
                        pltpu.VMEM((SSD_GROUPS, SSD_STATE, GROUP_W), f32),
                        pltpu.VMEM((seq, D_SSM), f32)],
        compiler_params=_params("parallel", "arbitrary"),
        name="ssd_latent",
    )(xc, xc, xc, xc, xc, xc, dt_raw, dt_raw, proj, proj, hf0, hb0,
      _pad_lanes(dt_bias), _pad_lanes(a_log), dskip, norm_w.reshape(1, -1).astype(f32), ef, eb)


def _filter_kernel(z_ref, w1_ref, b1_ref, wh_ref, bh_ref, freq_ref, wf_ref, wb_ref, delta_ref, ks_ref, kd_ref):
    hp = lambda a, b: jnp.dot(a, b, preferred_element_type=f32, precision=lax.Precision.HIGHEST)
    z = z_ref[...]
    freq = freq_ref[...]
    h = jnp.sin(freq * (hp(z, w1_ref[...]) + b1_ref[...]))
    for n in range(HYENA_INNER):
        h = jnp.sin(freq * (hp(h, wh_ref[n]) + bh_ref[n]))
    window = jnp.exp(-z[:, 0:1] * delta_ref[...])
    hf = hp(h, wf_ref[...]) * window
    hb = hp(h, wb_ref[...]) * window
    row = lax.broadcasted_iota(jnp.int32, hb.shape, 0)
    hb = jnp.where(row == 0, 0.0, hb)
    norm = jnp.sum(jnp.abs(hf), axis=0, keepdims=True) + jnp.sum(jnp.abs(hb), axis=0, keepdims=True)
    ks_ref[...] = (hf + hb) / norm
    kd_ref[...] = (hf - hb) / norm


def hyena_filter_taps(seq, f_w1, f_b1, f_wh, f_bh, f_wout, freq, tc=256):
    fw = HYENA_FILTER_WIDTH
    t = np.arange(seq, dtype=np.float32)
    t_norm = t / np.float32(seq - 1)
    bands = np.linspace(1e-4, HYENA_BANDS - 1, HYENA_BANDS, dtype=np.float32)
    ang = np.float32(2.0 * math.pi / seq) * t[:, None] * bands
    z = np.concatenate([t_norm[:, None], np.cos(ang), -np.sin(ang)], axis=-1).astype(np.float32)
    z = np.pad(z, ((0, 0), (0, LANES - z.shape[1])))
    deltas = np.abs(np.linspace(math.log(HYENA_TARGET) / HYENA_SLOW_DECAY, math.log(HYENA_TARGET) / HYENA_FAST_DECAY,
                                D_MODEL, dtype=np.float32)).reshape(1, -1)
    padw = lambda a, r, c: jnp.pad(a.astype(f32), [(0, 0)] * (a.ndim - 2) + [(0, r - a.shape[-2]), (0, c - a.shape[-1])])
    w1 = padw(f_w1, LANES, LANES)
    wh = padw(f_wh, LANES, LANES)
    wout = padw(f_wout, LANES, 2 * D_MODEL)
    b1 = padw(f_b1.reshape(1, fw), 1, LANES)
    bh = padw(f_bh.reshape(HYENA_INNER, 1, fw), 1, LANES)
    fq = padw(freq.reshape(1, fw), 1, LANES)
    nt = D_MODEL // tc
    const = lambda shape: pl.BlockSpec(shape, lambda j: (0,) * len(shape))
    out = jax.ShapeDtypeStruct((seq, D_MODEL), f32)
    osp = pl.BlockSpec((seq, tc), lambda j: (0, j))
    return pl.pallas_call(
        _filter_kernel,
        grid=(nt,),
        in_specs=[const((seq, LANES)), const((LANES, LANES)), const((1, LANES)),
                  const((HYENA_INNER, LANES, LANES)), const((HYENA_INNER, 1, LANES)), const((1, LANES)),
                  pl.BlockSpec((LANES, tc), lambda j: (0, j)),
                  pl.BlockSpec((LANES, tc), lambda j: (0, j + nt)),
                  pl.BlockSpec((1, tc), lambda j: (0, j))],
        out_specs=[osp, osp],
        out_shape=[out, out],
        compiler_params=_params("parallel"),
        name="hyena_filter",
    )(jnp.asarray(z), w1, b1, wh, bh, fq, wout, wout, jnp.asarray(deltas))


def _dft_tables(seq):
    n = 2 * seq
    half = seq // 2

    def theta(f, t):
        return (2.0 * math.pi / (2 * n)) * (((2 * f[:, None] + 1) * t[None, :]) % (2 * n)).astype(np.float64)

    f_low = np.arange(half, dtype=np.int64)
    order = np.concatenate([f_low, seq - 1 - f_low])
    th_full = theta(order, np.arange(seq, dtype=np.int64))
    tp = np.arange(half, dtype=np.int64)
    th_e, th_o = theta(f_low, 2 * tp), theta(f_low, 2 * tp + 1)
    fwd = [np.cos(th_e), np.cos(th_o), np.sin(th_e), np.sin(th_o)]
    as_bf16 = lambda a: jnp.asarray(a, bf16)
    return (as_bf16(np.cos(th_full)), as_bf16(np.sin(th_full)),
            [as_bf16(a) for a in fwd], [as_bf16(a.T * (2.0 / n)) for a in fwd])


def _spectrum_kernel(c_ref, s_ref, ks_ref, kd_ref, kre_ref, kb_ref):
    kre_ref[...] = sum(_dot(c_ref[...], part) for part in _split2(ks_ref[...]))
    kb_ref[...] = sum(_dot(s_ref[...], part) for part in _split2(kd_ref[...]))


def filter_spectrum(cmat, smat, ks, kd, tmf=256, tc=256):
    seq = ks.shape[0]
    fsp = pl.BlockSpec((tmf, seq), lambda j, m: (m, 0))
    ksp = pl.BlockSpec((seq, tc), lambda j, m: (0, j))
    osp = pl.BlockSpec((tmf, tc), lambda j, m: (m, j))
    out = jax.ShapeDtypeStruct((seq, D_MODEL), f32)
    return pl.pallas_call(
        _spectrum_kernel,
        grid=(D_MODEL // tc, seq // tmf),
        in_specs=[fsp, fsp, ksp, ksp],
        out_specs=[osp, osp],
        out_shape=[out, out],
        compiler_params=_params("parallel", "parallel"),
        name="filter_spectrum",
    )(cmat, smat, ks, kd)


def _dft_fwd_kernel(ce_ref, co_ref, se_ref, so_ref, ve_ref, vo_ref, kre_ref, kb_ref, krem_ref, kbm_ref,
                    pp_ref, pm_ref, qp_ref, qm_ref):
    ve, vo = ve_ref[...], vo_ref[...]
    ec, oc = _dot(ce_ref[...], ve), _dot(co_ref[...], vo)
    es, os_ = _dot(se_ref[...], ve), _dot(so_ref[...], vo)

    def times_filter(vre, va, kre, kb):
        return vre * kre - va * kb, vre * kb + va * kre

    p, q = times_filter(ec + oc, es + os_, kre_ref[...], kb_ref[...])
    p_m, q_m = times_filter(ec - oc, os_ - es, krem_ref[...], kbm_ref[...])
    pp_ref[...] = (p + p_m).astype(bf16)
    pm_ref[...] = (p - p_m).astype(bf16)
    qp_ref[...] = (q + q_m).astype(bf16)
    qm_ref[...] = (q - q_m).astype(bf16)


def dft_forward(fwd_tables, vg_even, vg_odd, kre, kb, seq, tmf=512):
    half = seq // 2
    nb = vg_even.shape[0] // half
    tmf = min(tmf, half)
    nm = half // tmf
    fsp = pl.BlockSpec((tmf, half), lambda b, m: (m, 0))
    ksp = pl.BlockSpec((tmf, D_MODEL), lambda b, m: (m, 0))
    kmsp = pl.BlockSpec((tmf, D_MODEL), lambda b, m: (m + nm, 0))
    osp = pl.BlockSpec((tmf, D_MODEL), lambda b, m: (b * nm + m, 0))
    out = jax.ShapeDtypeStruct((nb * half, D_MODEL), bf16)
    return pl.pallas_call(
        _dft_fwd_kernel,
        grid=(nb, nm),
        in_specs=[fsp, fsp, fsp, fsp,
                  pl.BlockSpec((half, D_MODEL), lambda b, m: (b, 0)),
                  pl.BlockSpec((half, D_MODEL), lambda b, m: (b, 0)),
                  ksp, ksp, kmsp, kmsp],
        out_specs=[osp] * 4,
        out_shape=[out] * 4,
        compiler_params=_params("parallel", "parallel"),
        name="dft_forward",
    )(*fwd_tables, vg_even, vg_odd, kre, kb, kre, kb)


def _dft_inv_kernel(gate_row, cet_ref, cot_ref, set_ref, sot_ref, pp_ref, pm_ref, qp_ref, qm_ref,
                    vge_ref, vgo_ref, x0e_ref, x0o_ref, bias_ref, w_ref, h_ref, mod_ref, o_ref, stage_ref):
    gate = mod_ref[0, gate_row:gate_row + 1, :]
    tmt = vge_ref.shape[0]
    nlane = D_MODEL // LANES
    for c in range(nlane):
        stage_ref[c] = h_ref[:, c * LANES:(c + 1) * LANES]

    def finish(y, parity, vg_ref, x0_ref):
        y = y + vg_ref[...].astype(f32) * bias_ref[...]
        upd = gate * _dot((y * x0_ref[...].astype(f32)).astype(bf16), w_ref[...])
        rows = pl.ds(parity, tmt, stride=2)
        for c in range(nlane):
            chunk = stage_ref.at[c]
            chunk[rows, :] = chunk[rows, :] + upd[:, c * LANES:(c + 1) * LANES]

    finish(_dot(cet_ref[...], pp_ref[...]) + _dot(set_ref[...], qm_ref[...]), 0, vge_ref, x0e_ref)
    finish(_dot(cot_ref[...], pm_ref[...]) + _dot(sot_ref[...], qp_ref[...]), 1, vgo_ref, x0o_ref)
    for c in range(nlane):
        o_ref[:, c * LANES:(c + 1) * LANES] = stage_ref[c]


def dft_inverse_out(inv_tables, folded, vg_eo, x0_eo, bias, w_out, h, mods, seq, gate_row, tmt=256):
    half = seq // 2
    nb = h.shape[0] // seq
    tmt = min(tmt, half)
    nm = half // tmt
    gsp = pl.BlockSpec((tmt, half), lambda b, m: (m, 0))
    full = pl.BlockSpec((half, D_MODEL), lambda b, m: (b, 0))
    part = pl.BlockSpec((tmt, D_MODEL), lambda b, m: (b * nm + m, 0))
    tile = pl.BlockSpec((2 * tmt, D_MODEL), lambda b, m: (b * nm + m, 0))
    return pl.pallas_call(
        functools.partial(_dft_inv_kernel, gate_row),
        grid=(nb, nm),
        in_specs=[gsp, gsp, gsp, gsp, full, full, full, full, part, part, part, part,
                  pl.BlockSpec((1, D_MODEL), lambda b, m: (0, 0)),
                  pl.BlockSpec((D_MODEL, D_MODEL), lambda b, m: (0, 0)),
                  tile, pl.BlockSpec((1, N_MOD, D_MODEL), lambda b, m: (b, 0, 0))],
        out_specs=tile,
        out_shape=jax.ShapeDtypeStruct(h.shape, f32),
        scratch_shapes=[pltpu.VMEM((D_MODEL // LANES, 2 * tmt, LANES), f32)],
        compiler_params=_params("parallel", "parallel"),
        name="dft_inverse_out",
    )(*inv_tables, *folded, *vg_eo, *x0_eo, bias.reshape(1, -1).astype(f32), w_out, h, mods)


_Q_HEAD_ORDER = (0, 4, 1, 5, 2, 6, 3, 7)


def _mixer_in_weight(w_in):
    c = np.cumsum([ATTN_DIM, KV_DIM, KV_DIM, D_SSM, XBC_DIM])
    q, k, v, z, xbc, dt = (w_in[:, :c[0]], w_in[:, c[0]:c[1]], w_in[:, c[1]:c[2]], w_in[:, c[2]:c[3]],
                           w_in[:, c[3]:c[4]], w_in[:, c[4]:])
    q = q.reshape(D_MODEL, N_Q_HEADS, HEAD_DIM)[:, np.array(_Q_HEAD_ORDER), :].reshape(D_MODEL, ATTN_DIM)
    w = jnp.concatenate([xbc, z, q, k, v], axis=1).astype(bf16)
    return w, jnp.pad(dt, ((0, 0), (0, LANES - dt.shape[1]))).astype(bf16)


def _mixer_out_weights(w_out):
    wa = w_out[:ATTN_DIM].reshape(N_Q_HEADS, HEAD_DIM, D_MODEL)[np.array(_Q_HEAD_ORDER)].reshape(ATTN_DIM, D_MODEL)
    return wa.astype(bf16), w_out[ATTN_DIM:].astype(bf16)


def kernel(x, c, ctx, c_ctx, w_ada, b_ada, norm_g, ffn_w13, ffn_w2, mix_w_in, mix_w_out, q_norm, k_norm, attn_sink,
           ssd_conv_w, ssd_conv_b, ssd_dt_bias, ssd_a_log, ssd_d, ssd_norm, hy_w_in, hy_conv_w, hy_conv_b,
           hy_f_w1, hy_f_b1, hy_f_wh, hy_f_bh, hy_f_wout, hy_freq, hy_bias, hy_w_out):
    nb, seq, _ = x.shape
    ctx_len = ctx.shape[1]
    depth = w_ada.shape[0]
    assert depth == 2, "this kernel is written for the two-layer block"
    h_lat = x.reshape(nb * seq, D_MODEL)
    h_ctx = ctx.reshape(nb * ctx_len, D_MODEL)
    cond = jnp.concatenate([c, c_ctx[None]], axis=0)
    cond = jnp.pad(cond, ((0, -cond.shape[0] % 8), (0, 0)))
    ffn_w = [[_ffn_weights(ffn_w13[i, k], ffn_w2[i, k]) for k in range(2)] for i in range(depth)]

    mods = ada_mods(cond, w_ada[0], b_ada[0])
    m_lat, m_ctx = mods[:nb], mods[nb:nb + 1]
    h_lat = macaron_ffn(h_lat, m_lat, seq, norm_g[0, 0], 0, *ffn_w[0][0])
    h_ctx = macaron_ffn(h_ctx, m_ctx, nb * ctx_len, norm_g[0, 0], 0, *ffn_w[0][0])
    w_in, w_dt = _mixer_in_weight(mix_w_in[0])
    wa, ws = _mixer_out_weights(mix_w_out[0])
    xc_lat, p_lat, dt_lat = mixer_proj(h_lat, m_lat, seq, seq, norm_g[0, 1], w_in, w_dt, ssd_conv_w[0], ssd_conv_b[0])
    xc_ctx, p_ctx, dt_ctx = mixer_proj(h_ctx, m_ctx, nb * ctx_len, ctx_len, norm_g[0, 1], w_in, w_dt,
                                       ssd_conv_w[0], ssd_conv_b[0])
    cos, sin = _rope_tables(seq)
    a_lat = window_attention(p_lat, p_ctx, seq, ctx_len, q_norm[0], k_norm[0], attn_sink[0], cos, sin)
    hf0, hb0 = ssd_ctx_states(xc_ctx, dt_ctx, ctx_len, ssd_dt_bias[0], ssd_a_log[0])
    s_lat = ssd_latent(xc_lat, p_lat, dt_lat, seq, hf0, hb0, ssd_dt_bias[0], ssd_a_log[0], ssd_d[0], ssd_norm[0])
    h_lat = out_proj_residual([a_lat, s_lat], [wa, ws], h_lat, m_lat, seq, 5)
    h_lat = macaron_ffn(h_lat, m_lat, seq, norm_g[0, 2], 2, *ffn_w[0][1])

    m_lat = ada_mods(cond, w_ada[1], b_ada[1])[:nb]
    h_lat = macaron_ffn(h_lat, m_lat, seq, norm_g[1, 0], 0, *ffn_w[1][0])
    x0e, x0o, vge, vgo = hyena_proj(h_lat, m_lat, seq, seq, norm_g[1, 1], hy_w_in[0].astype(bf16),
                                    hy_conv_w[0], hy_conv_b[0])
    ks, kd = hyena_filter_taps(seq, hy_f_w1[0], hy_f_b1[0], hy_f_wh[0], hy_f_bh[0], hy_f_wout[0], hy_freq[0])
    c_full, s_full, fwd_tables, inv_tables = _dft_tables(seq)
    kre, kb = filter_spectrum(c_full, s_full, ks, kd)
    folded = dft_forward(fwd_tables, vge, vgo, kre, kb, seq)
    h_lat = dft_inverse_out(inv_tables, folded, (vge, vgo), (x0e, x0o), hy_bias[0], hy_w_out[0].astype(bf16),
                            h_lat, m_lat, seq, 5)
    h_lat = macaron_ffn(h_lat, m_lat, seq, norm_g[1, 2], 2, *ffn_w[1][1])
    return h_lat.reshape(nb, seq, D_MODEL)
```

```python
import functools
import math

import numpy as np
import jax
import jax.numpy as jnp
from jax import lax
from jax.experimental import pallas as pl
from jax.experimental.pallas import tpu as pltpu

f32 = jnp.float32
bf16 = jnp.bfloat16

D_MODEL = 1024
N_MOD = 9
RMS_EPS = 1e-6
GRID_W = 64

HEAD_DIM = 64
N_Q_HEADS = 8
N_KV_HEADS = 2
ATTN_DIM = N_Q_HEADS * HEAD_DIM
KV_DIM = N_KV_HEADS * HEAD_DIM
WINDOW = 128
ATTN_BLOCK = 128
ROPE_THETA = 10000.0

SSD_HEADS = 16
SSD_HEAD_DIM = 64
D_SSM = SSD_HEADS * SSD_HEAD_DIM
SSD_GROUPS = 2
SSD_STATE = 128
SSD_CONV = 7
SSD_CHUNK = 128
BC_DIM = SSD_GROUPS * SSD_STATE
XBC_DIM = D_SSM + 4 * BC_DIM
GROUP_W = D_SSM // SSD_GROUPS

HYENA_SHORT = 3
HYENA_BANDS = 8
HYENA_FILTER_WIDTH = 64
HYENA_INNER = 2
HYENA_FAST_DECAY = 0.3
HYENA_SLOW_DECAY = 1.5
HYENA_TARGET = 1e-2

D_FF = 2816
FFN_TF = 256
LANES = 128

COL_Z = 0
COL_Q = COL_Z + D_SSM
COL_K = COL_Q + ATTN_DIM
COL_V = COL_K + KV_DIM
REST_COLS = COL_V + KV_DIM
HALO = 16
PROJ_TN = 256
PROJ_TM = 256
CONV_ROWS = 64

VMEM_LIMIT = 56 * 1024 * 1024


def _params(*sem):
    return pltpu.CompilerParams(dimension_semantics=sem, vmem_limit_bytes=VMEM_LIMIT)


def _dot(a, b):
    return jnp.dot(a, b, preferred_element_type=f32)


def _dot_nt(a, b):
    return lax.dot_general(a, b, (((1,), (1,)), ((), ())), preferred_element_type=f32)


def _split2(x):
    hi = x.astype(bf16)
    lo = (x - hi.astype(f32)).astype(bf16)
    return hi, lo


def _split3(x):
    hi = x.astype(bf16)
    r = x - hi.astype(f32)
    mid = r.astype(bf16)
    lo = (r - mid.astype(f32)).astype(bf16)
    return hi, mid, lo


def _adaln(h, g, shift, scale):
    ms = jnp.mean(h * h, axis=-1, keepdims=True)
    return (h * lax.rsqrt(ms + RMS_EPS) * g) * (1.0 + scale) + shift


def _silu(x):
    return x * jax.nn.sigmoid(x)


def _mods_kernel(c_ref, w_ref, b_ref, o_ref):
    o_ref[...] = _dot(_silu(c_ref[...]).astype(bf16), w_ref[...].astype(bf16)) + b_ref[...]


def ada_mods(cond, w, b):
    r = cond.shape[0]
    tn = 1024
    out = pl.pallas_call(
        _mods_kernel,
        grid=(w.shape[1] // tn,),
        in_specs=[pl.BlockSpec((r, D_MODEL), lambda j: (0, 0)),
                  pl.BlockSpec((D_MODEL, tn), lambda j: (0, j)),
                  pl.BlockSpec((1, tn), lambda j: (0, j))],
        out_specs=pl.BlockSpec((r, tn), lambda j: (0, j)),
        out_shape=jax.ShapeDtypeStruct((r, w.shape[1]), f32),
        compiler_params=_params("parallel"),
        name="ada_mods",
    )(cond, w, b.reshape(1, -1))
    return out.reshape(r, N_MOD, D_MODEL)


def _ffn_kernel(s, nf, h_ref, mod_ref, g_ref, w13_ref, w2_ref, o_ref):
    h = h_ref[...]
    u = _adaln(h, g_ref[...], mod_ref[0, 3 * s:3 * s + 1, :], mod_ref[0, 3 * s + 1:3 * s + 2, :]).astype(bf16)
    acc = None
    tf = w2_ref.shape[1]
    for j in range(nf):
        a = _dot(u, w13_ref[:, j * tf:(j + 1) * tf])
        b = _dot(u, w13_ref[:, D_FF + j * tf:D_FF + (j + 1) * tf])
        part = _dot((_silu(a) * b).astype(bf16), w2_ref[j])
        acc = part if acc is None else acc + part
    o_ref[...] = h + 0.5 * mod_ref[0, 3 * s + 2:3 * s + 3, :] * acc


def _ffn_weights(w13, w2, tf=FFN_TF):
    return w13.astype(bf16), w2.astype(bf16).reshape(D_FF // tf, tf, D_MODEL)


def macaron_ffn(h, mods, rows_per_mod, g, s, w13c, w2c, tm=512):
    m = h.shape[0]
    tm = min(tm, rows_per_mod)
    nf = w2c.shape[0]
    tiles_per_mod = rows_per_mod // tm
    return pl.pallas_call(
        functools.partial(_ffn_kernel, s, nf),
        grid=(m // tm,),
        in_specs=[pl.BlockSpec((tm, D_MODEL), lambda i: (i, 0)),
                  pl.BlockSpec((1, N_MOD, D_MODEL), lambda i: (i // tiles_per_mod, 0, 0)),
                  pl.BlockSpec((1, D_MODEL), lambda i: (0, 0)),
                  pl.BlockSpec(w13c.shape, lambda i: (0, 0)),
                  pl.BlockSpec(w2c.shape, lambda i: (0, 0, 0))],
        out_specs=pl.BlockSpec((tm, D_MODEL), lambda i: (i, 0)),
        out_shape=jax.ShapeDtypeStruct((m, D_MODEL), f32),
        compiler_params=_params("parallel"),
        name="macaron_ffn",
    )(h, mods, g.reshape(1, -1), w13c, w2c)


def _tile_adaln(s, tiles_per_seq, h_ref, hp_ref, hn_ref, mod_ref, g_ref, u_ref, uh_ref):
    shift, scale = mod_ref[0, 3 * s:3 * s + 1, :], mod_ref[0, 3 * s + 1:3 * s + 2, :]
    g = g_ref[...]
    u_ref[...] = _adaln(h_ref[...], g, shift, scale).astype(bf16)
    uh_ref[...] = _adaln(jnp.concatenate([hp_ref[...], hn_ref[...]], axis=0), g, shift, scale).astype(bf16)
    t = pl.program_id(0) % tiles_per_seq
    row = lax.broadcasted_iota(jnp.int32, (2 * HALO, 1), 0)
    keep_prev = jnp.where(t > 0, 1.0, 0.0)
    keep_next = jnp.where(t < tiles_per_seq - 1, 1.0, 0.0)
    return jnp.where(row < HALO, keep_prev, keep_next)


def _project_padded(pad_ref, u_ref, uh_ref, keep, w):
    tm = u_ref.shape[0]

    def halo_rows():
        halo = _dot(uh_ref[...], w) * keep
        pad_ref[0:HALO, :] = halo[0:HALO, :]
        pad_ref[HALO + tm:2 * HALO + tm, :] = halo[HALO:2 * HALO, :]

    def tile_rows(r0):
        pad_ref[HALO + r0:HALO + r0 + PROJ_TM, :] = _dot(u_ref[r0:r0 + PROJ_TM, :], w)

    return [halo_rows] + [functools.partial(tile_rows, r0) for r0 in range(0, tm, PROJ_TM)]


def _conv_padded(pad_ref, conv_w, conv_b, emit, between=()):
    tm = pad_ref.shape[0] - 2 * HALO
    width = conv_w.shape[0]
    rows = CONV_ROWS
    steps = [(l0, r0) for l0 in range(0, pad_ref.shape[1], LANES) for r0 in range(0, tm, rows)]
    every = max(1, len(steps) // max(1, len(between)))
    pending = list(between)
    half = width // 2
    for n, (l0, r0) in enumerate(steps):
        if pending and n % every == 0:
            pending.pop(0)()
        lanes = slice(l0, l0 + LANES)
        win = pad_ref[HALO + r0 - 8:HALO + r0 + rows + 8, lanes]
        acc = conv_b[:, lanes] + conv_w[half:half + 1, lanes] * win[8:8 + rows, :]
        for k in range(width):
            if k != half:
                acc += conv_w[k:k + 1, lanes] * pltpu.roll(win, (half - k) % (rows + 16), 0)[8:8 + rows, :]
        emit(r0, rows, l0, acc)
    for thunk in pending:
        thunk()


def _mixer_proj_kernel(s, tiles_per_seq, h_ref, hp_ref, hn_ref, mod_ref, g_ref, w_ref, wdt_ref, cw_ref, cb_ref,
                       perm_ref, xc_ref, rest_ref, dt_ref, pad_ref, u_ref, uh_ref, up_ref, ystage_ref):
    tm = h_ref.shape[0]
    grp = tm // 8
    shift, scale = mod_ref[0, 3 * s:3 * s + 1, :], mod_ref[0, 3 * s + 1:3 * s + 2, :]
    gain = g_ref[...]
    u_ref[...] = _adaln(h_ref[...], gain, shift, scale).astype(bf16)
    uh_ref[...] = _adaln(jnp.concatenate([hp_ref[...], hn_ref[...]], axis=0), gain, shift, scale).astype(bf16)
    up_ref[...] = _dot(perm_ref[...], u_ref[...]).astype(bf16)
    t = pl.program_id(0) % tiles_per_seq
    keep_prev = jnp.where(t > 0, 1.0, 0.0)
    keep_next = jnp.where(t < tiles_per_seq - 1, 1.0, 0.0)
    keep = jnp.where(lax.broadcasted_iota(jnp.int32, (2 * HALO, 1), 0) < HALO, keep_prev, keep_next)

    tn = PROJ_TN
    n_conv = XBC_DIM // tn
    n_rest = REST_COLS // tn
    half = SSD_CONV // 2
    ext = 8 * half
    sub = lax.broadcasted_iota(jnp.int32, (8, tn), 0)

    def project(c):
        w = w_ref[:, c * tn:(c + 1) * tn]
        pad = pad_ref.at[c % 2]

        def tile_rows(r0):
            pad[ext + r0:ext + r0 + PROJ_TM, :] = _dot(up_ref[r0:r0 + PROJ_TM, :], w)

        def boundary_tiles():
            halo = _dot(uh_ref[...], w) * keep
            for j in range(half):
                src = ext + 8 * (grp - half + j)
                pad[8 * j:8 * j + 8, :] = jnp.where(sub == 0, halo[HALO - half + j:HALO - half + j + 1, :],
                                                    pltpu.roll(pad[src:src + 8, :], 1, 0))
                src = ext + 8 * j
                dst = ext + tm + 8 * j
                pad[dst:dst + 8, :] = jnp.where(sub == 7, halo[HALO + j:HALO + j + 1, :],
                                                pltpu.roll(pad[src:src + 8, :], 7, 0))

        return [functools.partial(tile_rows, r0) for r0 in range(0, tm, PROJ_TM)] + [boundary_tiles]

    def conv(c, between):
        pad = pad_ref.at[c % 2]
        cols = slice(c * tn, (c + 1) * tn)
        rows = CONV_ROWS
        steps = [(l0, r0) for l0 in range(0, tn, LANES) for r0 in range(0, tm, rows)]
        every = max(1, len(steps) // max(1, len(between)))
        pending = list(between)
        for n, (l0, r0) in enumerate(steps):
            if pending and n % every == 0:
                pending.pop(0)()
            lanes = slice(l0, l0 + LANES)
            wcol = slice(c * tn + l0, c * tn + l0 + LANES)
            acc = cb_ref[:, wcol]
            for k in range(SSD_CONV):
                acc = acc + cw_ref[k:k + 1, wcol] * pad[8 * k + r0:8 * k + r0 + rows, lanes]
            ystage_ref[l0 // LANES, r0:r0 + rows, :] = _silu(acc)
        for thunk in pending:
            thunk()
        per = grp // 8
        for l0 in range(0, tn, LANES):
            stage = ystage_ref.at[l0 // LANES]
            for r0 in range(0, tm, rows):
                tiles = [stage[pl.ds(64 * (j % per) + j // per, 8, stride=8), :] for j in range(r0 // 8, (r0 + rows) // 8)]
                xc_ref[r0:r0 + rows, c * tn + l0:c * tn + l0 + LANES] = jnp.concatenate(tiles, axis=0).astype(bf16)

    def plain_rows(c, r0):
        rows = slice(r0, r0 + PROJ_TM)
        if c < n_rest:
            w = w_ref[:, XBC_DIM + c * tn:XBC_DIM + (c + 1) * tn]
            rest_ref[rows, c * tn:(c + 1) * tn] = _dot(u_ref[rows, :], w).astype(bf16)
        elif c == n_rest:
            dt_ref[rows, :] = _dot(u_ref[rows, :], wdt_ref[...])

    def plain(c):
        return [functools.partial(plain_rows, c, r0) for r0 in range(0, tm, PROJ_TM)]

    for thunk in project(0):
        thunk()
    for c in range(n_conv):
        conv(c, (project(c + 1) if c + 1 < n_conv else []) + plain(c))
    for c in range(n_conv, n_rest + 1):
        for thunk in plain(c):
            thunk()


def _hyena_proj_kernel(s, tiles_per_seq, h_ref, hp_ref, hn_ref, mod_ref, g_ref, w_ref, cw_ref, cb_ref,
                       x0e_ref, x0o_ref, vge_ref, vgo_ref, pad_ref, u_ref, uh_ref, x1_ref, split_ref):
    keep = _tile_adaln(s, tiles_per_seq, h_ref, hp_ref, hn_ref, mod_ref, g_ref, u_ref, uh_ref)
    tn = PROJ_TN
    cols = [part * D_MODEL + c0 for c0 in range(0, D_MODEL, tn) for part in range(3)]

    def project(i):
        return _project_padded(pad_ref.at[i % 2], u_ref, uh_ref, keep, w_ref[:, cols[i]:cols[i] + tn])

    for thunk in project(0):
        thunk()
    for i, col in enumerate(cols):
        between = project(i + 1) if i + 1 < len(cols) else []
        c0 = col % D_MODEL

        def split_tokens(value, even_ref, odd_ref, r0, rows, lanes):
            split_ref[...] = value
            dst = slice(r0 // 2, (r0 + rows) // 2)
            even_ref[dst, lanes] = split_ref[pl.ds(0, rows // 2, stride=2), :].astype(bf16)
            odd_ref[dst, lanes] = split_ref[pl.ds(1, rows // 2, stride=2), :].astype(bf16)

        def emit_x0(r0, rows, l0, acc, c0=c0):
            split_tokens(acc, x0e_ref, x0o_ref, r0, rows, slice(c0 + l0, c0 + l0 + LANES))

        def emit_x1(r0, rows, l0, acc):
            x1_ref[r0:r0 + rows, l0:l0 + LANES] = acc

        def emit_v(r0, rows, l0, acc, c0=c0):
            vg = acc * x1_ref[r0:r0 + rows, l0:l0 + LANES]
            split_tokens(vg, vge_ref, vgo_ref, r0, rows, slice(c0 + l0, c0 + l0 + LANES))

        emit = (emit_x0, emit_x1, emit_v)[col // D_MODEL]
        _conv_padded(pad_ref.at[i % 2], cw_ref[:, col:col + tn], cb_ref[:, col:col + tn], emit, between)


def _proj_call(body, name, h, mods, rows_per_mod, seq, g, consts, outs, pad_margin, extra_scratch, tm=512):
    m = h.shape[0]
    tm = min(tm, seq)
    tiles_per_seq = seq // tm
    tiles_per_mod = rows_per_mod // tm
    hb = tm // HALO
    in_specs = [pl.BlockSpec((tm, D_MODEL), lambda i: (i, 0)),
                pl.BlockSpec((HALO, D_MODEL), lambda i: (jnp.maximum(i * hb - 1, 0), 0)),
                pl.BlockSpec((HALO, D_MODEL), lambda i: (jnp.minimum((i + 1) * hb, m // HALO - 1), 0)),
                pl.BlockSpec((1, N_MOD, D_MODEL), lambda i: (i // tiles_per_mod, 0, 0)),
                pl.BlockSpec((1, D_MODEL), lambda i: (0, 0))]
    in_specs += [pl.BlockSpec(a.shape, lambda i: (0, 0)) for a in consts]
    return pl.pallas_call(
        functools.partial(body, 1, tiles_per_seq),
        grid=(m // tm,),
        in_specs=in_specs,
        out_specs=[pl.BlockSpec((tm // div, n), lambda i: (i, 0)) for div, n, _ in outs],
        out_shape=[jax.ShapeDtypeStruct((m // div, n), dt) for div, n, dt in outs],
        scratch_shapes=[pltpu.VMEM((2, tm + 2 * pad_margin, PROJ_TN), f32), pltpu.VMEM((tm, D_MODEL), bf16),
                        pltpu.VMEM((2 * HALO, D_MODEL), bf16)] + extra_scratch(tm),
        compiler_params=_params("parallel"),
        name=name,
    )(h, h, h, mods, g.reshape(1, -1), *consts)


def mixer_proj(h, mods, rows_per_mod, seq, g, w, w_dt, conv_w, conv_b):
    tm = min(512, seq)
    r = np.arange(tm)
    perm = np.zeros((tm, tm), np.float32)
    perm[r, (r % 8) * (tm // 8) + r // 8] = 1.0
    consts = [w, w_dt, conv_w.astype(f32), conv_b.reshape(1, -1).astype(f32), jnp.asarray(perm, bf16)]
    scratch = lambda tm: [pltpu.VMEM((tm, D_MODEL), bf16), pltpu.VMEM((PROJ_TN // LANES, tm, LANES), f32)]
    return _proj_call(_mixer_proj_kernel, "mixer_proj", h, mods, rows_per_mod, seq, g, consts,
                      [(1, XBC_DIM, bf16), (1, REST_COLS, bf16), (1, LANES, f32)], 8 * (SSD_CONV // 2), scratch)


def hyena_proj(h, mods, rows_per_mod, seq, g, w, conv_w, conv_b):
    consts = [w, conv_w.astype(f32), conv_b.reshape(1, -1).astype(f32)]
    scratch = lambda tm: [pltpu.VMEM((tm, PROJ_TN), f32), pltpu.VMEM((CONV_ROWS, LANES), f32)]
    return _proj_call(_hyena_proj_kernel, "hyena_proj", h, mods, rows_per_mod, seq, g, consts,
                      [(2, D_MODEL, bf16)] * 4, HALO, scratch)


def _outproj_kernel(n_in, gate_row, *refs):
    x_refs = refs[:n_in]
    w_refs = refs[n_in:2 * n_in]
    h_ref, mod_ref, o_ref = refs[2 * n_in:]
    acc = _dot(x_refs[0][...].astype(bf16), w_refs[0][...])
    for x_ref, w_ref in zip(x_refs[1:], w_refs[1:]):
        acc += _dot(x_ref[...].astype(bf16), w_ref[...])
    o_ref[...] = h_ref[...] + mod_ref[0, gate_row:gate_row + 1, :] * acc


def out_proj_residual(xs, ws, h, mods, rows_per_mod, gate_row, tm=512):
    m = h.shape[0]
    tm = min(tm, rows_per_mod)
    tiles_per_mod = rows_per_mod // tm
    n_in = len(xs)
    in_specs = ([pl.BlockSpec((tm, x.shape[1]), lambda i: (i, 0)) for x in xs]
                + [pl.BlockSpec(w.shape, lambda i: (0, 0)) for w in ws]
                + [pl.BlockSpec((tm, D_MODEL), lambda i: (i, 0)),
                   pl.BlockSpec((1, N_MOD, D_MODEL), lambda i: (i // tiles_per_mod, 0, 0))])
    return pl.pallas_call(
        functools.partial(_outproj_kernel, n_in, gate_row),
        grid=(m // tm,),
        in_specs=in_specs,
        out_specs=pl.BlockSpec((tm, D_MODEL), lambda i: (i, 0)),
        out_shape=jax.ShapeDtypeStruct((m, D_MODEL), f32),
        compiler_params=_params("parallel"),
        name="out_proj_residual",
    )(*xs, *ws, h, mods)


def _head_norm(x, gain, bd):
    hi, lo = _split2(x * x)
    ms = _dot(hi, bd) + _dot(lo, bd)
    return x * lax.rsqrt(ms + RMS_EPS) * gain


def _rope(x, cos, sin_signed):
    lane = lax.broadcasted_iota(jnp.int32, x.shape, 1)
    partner = jnp.where((lane & 16) != 0, pltpu.roll(x, 16, 1), pltpu.roll(x, LANES - 16, 1))
    return x * cos + partner * sin_signed


def _t_bf16(x):
    return x.astype(f32).T.astype(bf16)


def _attn_kernel(seq, q_ref, k_ref, v_ref, kc_ref, vc_ref, qg_ref, kg_ref, cos_ref, sin_ref, bd_ref,
                 sink_ref, o_ref, qt_s, k_s, vt_s, kc_s, vct_s):
    j = pl.program_id(1)
    nblk = seq // ATTN_BLOCK
    nslab = ATTN_DIM // LANES
    blk = ATTN_BLOCK

    @pl.when(j == 0)
    def _():
        bd = bd_ref[...]
        cos, sin = cos_ref[...], sin_ref[...]
        scale = HEAD_DIM ** -0.5
        for p in range(nslab):
            qn = _head_norm(q_ref[:, p * LANES:(p + 1) * LANES].astype(f32), qg_ref[...], bd)
            qr = _rope(qn, cos, sin) * scale
            for jb in range(nblk):
                qt_s[jb, p * LANES:(p + 1) * LANES, :] = _t_bf16(qr[jb * blk:(jb + 1) * blk, :])
        kn = _head_norm(k_ref[...].astype(f32), kg_ref[...], bd)
        zeros = jnp.zeros((WINDOW, KV_DIM), bf16)
        k_s[0:WINDOW, :] = zeros
        k_s[WINDOW + seq:2 * WINDOW + seq, :] = zeros
        k_s[WINDOW:WINDOW + seq, :] = _rope(kn, cos, sin).astype(bf16)
        vt_s[0] = zeros
        vt_s[nblk + 1] = zeros
        for jb in range(nblk):
            vt_s[jb + 1] = _t_bf16(v_ref[jb * blk:(jb + 1) * blk, :])
        kc_s[...] = _head_norm(kc_ref[...].astype(f32), kg_ref[...], bd).astype(bf16)
        vct_s[...] = _t_bf16(vc_ref[...])

    band = blk + 2 * WINDOW
    start = pl.multiple_of(j * blk, blk)
    kb = k_s[pl.ds(start, band), :]
    kc = kc_s[...]
    vtb = jnp.concatenate([vt_s[j], vt_s[j + 1], vt_s[j + 2]], axis=1)
    vtc = vct_s[...]
    qt = qt_s[j]
    key = lax.broadcasted_iota(jnp.int32, (blk, 2 * blk), 0)
    qry = lax.broadcasted_iota(jnp.int32, (blk, 2 * blk), 1) & (blk - 1)
    ok_lo = (jnp.abs(qry - (key - WINDOW)) <= WINDOW) & (start - WINDOW + key >= 0)
    ok_hi = (jnp.abs(qry - (key + blk)) <= WINDOW) & (start + blk + key < seq)
    dim = lax.broadcasted_iota(jnp.int32, (LANES, blk), 0)
    lane2 = lax.broadcasted_iota(jnp.int32, (1, 2 * blk), 1)
    ones_b = jnp.ones((16, band), bf16)
    ones_c = jnp.ones((16, kc.shape[0]), bf16)
    def scores(p):
        qslab = qt[p * LANES:(p + 1) * LANES, :]
        zero = jnp.zeros_like(qslab)
        rhs = jnp.concatenate([jnp.where(dim < HEAD_DIM, qslab, zero), jnp.where(dim >= HEAD_DIM, qslab, zero)], axis=1)
        return _dot(kb, rhs), _dot(kc, rhs)

    def softmax(p, sb, sc):
        s_lo = jnp.where(ok_lo, sb[0:blk, :], -jnp.inf)
        s_mid = sb[blk:2 * blk, :]
        s_hi = jnp.where(ok_hi, sb[2 * blk:3 * blk, :], -jnp.inf)
        sink = jnp.where(lane2 < blk, sink_ref[p], sink_ref[p + N_Q_HEADS // N_KV_HEADS])
        colmax = lambda s: jnp.max(s, axis=0, keepdims=True)
        mx = jnp.maximum(jnp.maximum(jnp.maximum(colmax(s_lo), colmax(s_mid)), jnp.maximum(colmax(s_hi), colmax(sc))),
                         sink)
        pb = jnp.concatenate([jnp.exp(s_lo - mx).astype(bf16), jnp.exp(s_mid - mx).astype(bf16),
                              jnp.exp(s_hi - mx).astype(bf16)], axis=0)
        return pb, jnp.exp(sc - mx).astype(bf16), jnp.exp(sink - mx)

    def values(p, pb, pc, sink_term):
        den = (_dot(ones_b, pb) + _dot(ones_c, pc))[0:1, :] + sink_term
        ot = (_dot(vtb, pb) + _dot(vtc, pc)) / den
        both = jnp.where(dim < HEAD_DIM, ot[:, 0:blk], ot[:, blk:2 * blk])
        o_ref[:, p * LANES:(p + 1) * LANES] = both.T.astype(bf16)

    s_next = scores(0)
    for p in range(nslab):
        s_cur = s_next
        if p + 1 < nslab:
            s_next = scores(p + 1)
        values(p, *softmax(p, *s_cur))


def window_attention(proj_lat, proj_ctx, seq, ctx_len, q_gain, k_gain, sink, rope_cos, rope_sin):
    nb = proj_lat.shape[0] // seq
    nblk = seq // ATTN_BLOCK
    bd = np.kron(np.eye(LANES // HEAD_DIM), np.ones((HEAD_DIM, HEAD_DIM))) / HEAD_DIM
    gain2 = lambda g: jnp.tile(g, LANES // HEAD_DIM).reshape(1, LANES)
    const = lambda shape: pl.BlockSpec(shape, lambda b, j: (0, 0))
    return pl.pallas_call(
        functools.partial(_attn_kernel, seq),
        grid=(nb, nblk),
        in_specs=[pl.BlockSpec((seq, ATTN_DIM), lambda b, j: (b, COL_Q // ATTN_DIM)),
                  pl.BlockSpec((seq, KV_DIM), lambda b, j: (b, COL_K // KV_DIM)),
                  pl.BlockSpec((seq, KV_DIM), lambda b, j: (b, COL_V // KV_DIM)),
                  pl.BlockSpec((ctx_len, KV_DIM), lambda b, j: (b, COL_K // KV_DIM)),
                  pl.BlockSpec((ctx_len, KV_DIM), lambda b, j: (b, COL_V // KV_DIM)),
                  const((1, LANES)), const((1, LANES)),
                  const((seq, LANES)), const((seq, LANES)), const((LANES, LANES)),
                  pl.BlockSpec(memory_space=pltpu.SMEM)],
        out_specs=pl.BlockSpec((ATTN_BLOCK, ATTN_DIM), lambda b, j: (b * nblk + j, 0)),
        out_shape=jax.ShapeDtypeStruct((nb * seq, ATTN_DIM), bf16),
        scratch_shapes=[pltpu.VMEM((nblk, ATTN_DIM, ATTN_BLOCK), bf16),
                        pltpu.VMEM((seq + 2 * WINDOW, KV_DIM), bf16),
                        pltpu.VMEM((nblk + 2, KV_DIM, ATTN_BLOCK), bf16),
                        pltpu.VMEM((ctx_len, KV_DIM), bf16),
                        pltpu.VMEM((KV_DIM, ctx_len), bf16)],
        compiler_params=_params("parallel", "arbitrary"),
        name="window_attention",
    )(proj_lat, proj_lat, proj_lat, proj_ctx, proj_ctx, gain2(q_gain), gain2(k_gain),
      rope_cos, rope_sin, jnp.asarray(bd, bf16), sink)


def _rope_tables(seq):
    t = np.arange(seq)
    pos = np.stack([t // GRID_W, t % GRID_W], axis=1).astype(np.float32)
    axis_dim = HEAD_DIM // 2
    inv = (ROPE_THETA ** (-np.arange(0, axis_dim, 2, dtype=np.float32) / axis_dim)).astype(np.float32)
    lane = np.arange(LANES)
    d = lane % HEAD_DIM
    which = d // axis_dim
    ang = (pos[:, which] * inv[d % (axis_dim // 2)][None, :]).astype(np.float32)
    sign = np.where((d % axis_dim) < axis_dim // 2, -1.0, 1.0)
    return jnp.asarray(np.cos(ang), f32), jnp.asarray(np.sin(ang) * sign, f32)


def _softplus(x):
    return jnp.maximum(x, 0.0) + jnp.log1p(jnp.exp(-jnp.abs(x)))


def _expand_heads(v, e):
    return _dot(v.astype(bf16), e)


def _ssd_chunk(rev, lane0, want_y, x, bm, cm, dt_raw, dt_bias, a_neg, expand, state_ref, result):
    t = x.shape[0]
    dt = _softplus(dt_raw + dt_bias)
    a = dt * a_neg
    r = lax.broadcasted_iota(jnp.int32, (t, t), 0)
    c = lax.broadcasted_iota(jnp.int32, (t, t), 1)
    keep = (r <= c) if rev else (r >= c)
    tri = jnp.where(keep, 1.0, 0.0).astype(bf16)
    cs = sum(_dot(tri, part) for part in _split3(a))
    last = cs[0:1, :] if rev else cs[t - 1:t, :]
    e = jnp.exp(cs)
    w = dt * jnp.exp(last - cs)
    e_x = _expand_heads(e, expand)
    w_x = _expand_heads(w, expand)
    elast_x = e_x[0:1, :] if rev else e_x[t - 1:t, :]
    yield

    y = None
    if want_y:
        cs_t = cs.T
        dt_t = dt.T
        lane = lax.broadcasted_iota(jnp.int32, (t, LANES), 1)
        cb = [_dot_nt(cm[:, g * SSD_STATE:(g + 1) * SSD_STATE],
                      bm[:, g * SSD_STATE:(g + 1) * SSD_STATE]) for g in range(SSD_GROUPS)]
        yield
        pieces = []
        for p in range(SSD_HEADS // 2):
            xp = x[:, p * LANES:(p + 1) * LANES]
            ms = []
            for q in range(2):
                h = 2 * p + q
                g = h // (SSD_HEADS // SSD_GROUPS)
                seg = cs[:, lane0 + h:lane0 + h + 1] - cs_t[lane0 + h:lane0 + h + 1, :]
                dec = jnp.exp(jnp.where(keep, seg, -jnp.inf))
                ms.append((cb[g] * dec * dt_t[lane0 + h:lane0 + h + 1, :]).astype(bf16))
            zero = jnp.zeros_like(xp)
            xcat = jnp.concatenate([jnp.where(lane < SSD_HEAD_DIM, xp, zero),
                                    jnp.where(lane >= SSD_HEAD_DIM, xp, zero)], axis=0)
            pieces.append(_dot(jnp.concatenate(ms, axis=1), xcat))
            yield
        y = jnp.concatenate(pieces, axis=1)

    inter = []
    for g in range(SSD_GROUPS):
        gs = slice(g * GROUP_W, (g + 1) * GROUP_W)
        ss = slice(g * SSD_STATE, (g + 1) * SSD_STATE)
        h_t = state_ref[g]
        if want_y:
            inter.append(_dot(cm[:, ss], h_t.astype(bf16)) * e_x[:, gs])
        xw = (x[:, gs].astype(f32) * w_x[:, gs]).astype(bf16)
        state_ref[g] = h_t * elast_x[:, gs] + _dot(bm[:, ss].astype(f32).T.astype(bf16), xw)
        yield
    if want_y:
        result.append(y + jnp.concatenate(inter, axis=1))


def _interleave(*stage_generators):
    active = list(stage_generators)
    while active:
        for gen in list(active):
            if next(gen, StopIteration) is StopIteration:
                active.remove(gen)


def _ssd_ctx_kernel(nchunk, x_ref, bc_ref, dt_ref, bias_ref, alog_ref, ef_ref, eb_ref, hf_ref, hb_ref, sf, sb):
    sf[...] = jnp.zeros_like(sf)
    sb[...] = jnp.zeros_like(sb)
    a_neg = -jnp.exp(alog_ref[...])
    bias = bias_ref[...]
    t = SSD_CHUNK
    for ci in range(nchunk):
        rows = slice(ci * t, (ci + 1) * t)
        fwd = _ssd_chunk(False, 0, False, x_ref[rows, :], bc_ref[rows, 0:BC_DIM], None, dt_ref[rows, :],
                         bias, a_neg, ef_ref[...], sf, None)
        rows = slice((nchunk - 1 - ci) * t, (nchunk - ci) * t)
        bwd = _ssd_chunk(True, SSD_HEADS, False, x_ref[rows, :], bc_ref[rows, BC_DIM:2 * BC_DIM], None,
                         dt_ref[rows, :], bias, a_neg, eb_ref[...], sb, None)
        _interleave(fwd, bwd)
    hf_ref[0] = sf[...]
    hb_ref[0] = sb[...]


def _head_expanders():
    ef = np.zeros((LANES, D_SSM), np.float32)
    eb = np.zeros((LANES, D_SSM), np.float32)
    for h in range(SSD_HEADS):
        ef[h, h * SSD_HEAD_DIM:(h + 1) * SSD_HEAD_DIM] = 1.0
        eb[SSD_HEADS + h, h * SSD_HEAD_DIM:(h + 1) * SSD_HEAD_DIM] = 1.0
    return jnp.asarray(ef, bf16), jnp.asarray(eb, bf16)


def _pad_lanes(v):
    v = v.reshape(1, -1).astype(f32)
    return jnp.pad(v, ((0, 0), (0, LANES - v.shape[1])))


def ssd_ctx_states(xc, dt_raw, seq, dt_bias, a_log):
    nb = xc.shape[0] // seq
    ef, eb = _head_expanders()
    const = lambda shape: pl.BlockSpec(shape, lambda b: (0,) * len(shape))
    st = jax.ShapeDtypeStruct((nb, SSD_GROUPS, SSD_STATE, GROUP_W), f32)
    st_spec = pl.BlockSpec((1, SSD_GROUPS, SSD_STATE, GROUP_W), lambda b: (b, 0, 0, 0))
    return pl.pallas_call(
        functools.partial(_ssd_ctx_kernel, seq // SSD_CHUNK),
        grid=(nb,),
        in_specs=[pl.BlockSpec((seq, D_SSM), lambda b: (b, 0)),
                  pl.BlockSpec((seq, 2 * BC_DIM), lambda b: (b, D_SSM // (2 * BC_DIM))),
                  pl.BlockSpec((seq, LANES), lambda b: (b, 0)),
                  const((1, LANES)), const((1, LANES)), const((LANES, D_SSM)), const((LANES, D_SSM))],
        out_specs=[st_spec, st_spec],
        out_shape=[st, st],
        scratch_shapes=[pltpu.VMEM((SSD_GROUPS, SSD_STATE, GROUP_W), f32),
                        pltpu.VMEM((SSD_GROUPS, SSD_STATE, GROUP_W), f32)],
        compiler_params=_params("parallel"),
        name="ssd_ctx_states",
    )(xc, xc, dt_raw, _pad_lanes(dt_bias), _pad_lanes(a_log), ef, eb)


def _ssd_lat_kernel(nchunk, xf_ref, xb_ref, bf_ref, bb_ref, cf_ref, cb_ref, dtf_ref, dtb_ref, zf_ref, zb_ref,
                    hf0_ref, hb0_ref, bias_ref, alog_ref, dskip_ref, normw_ref, ef_ref, eb_ref,
                    o_ref, sf, sb, yacc):
    c = pl.program_id(1)
    t = SSD_CHUNK

    @pl.when(c == 0)
    def _():
        sf[...] = hf0_ref[0]
        sb[...] = hb0_ref[0]

    a_neg = -jnp.exp(alog_ref[...])
    bias = bias_ref[...]
    xf = xf_ref[...]
    yf, yb = [], []
    _interleave(
        _ssd_chunk(False, 0, True, xf, bf_ref[...], cf_ref[...], dtf_ref[...], bias, a_neg, ef_ref[...], sf, yf),
        _ssd_chunk(True, SSD_HEADS, True, xb_ref[...], bb_ref[...], cb_ref[...], dtb_ref[...], bias, a_neg,
                   eb_ref[...], sb, yb))
    yf = yf[0] + dskip_ref[...] * xf.astype(f32)
    yb = yb[0]
    rows_f = pl.ds(pl.multiple_of(c * t, t), t)
    rows_b = pl.ds(pl.multiple_of((nchunk - 1 - c) * t, t), t)

    @pl.when(c < nchunk // 2)
    def _():
        yacc[rows_f, :] = yf
        yacc[rows_b, :] = yb

    def finish(y, z):
        y = y * _silu(z.astype(f32))
        outs = []
        for g in range(SSD_GROUPS):
            yg = y[:, g * GROUP_W:(g + 1) * GROUP_W]
            outs.append(yg * lax.rsqrt(jnp.mean(yg * yg, axis=-1, keepdims=True) + RMS_EPS))
        return (jnp.concatenate(outs, axis=1) * normw_ref[...]).astype(bf16)

    @pl.when(c >= nchunk // 2)
    def _():
        o_ref[rows_f, :] = finish(yacc[rows_f, :] + yf, zf_ref[...])
        o_ref[rows_b, :] = finish(yacc[rows_b, :] + yb, zb_ref[...])


def ssd_latent(xc, proj, dt_raw, seq, hf0, hb0, dt_bias, a_log, d_skip, norm_w):
    nb = xc.shape[0] // seq
    nc = seq // SSD_CHUNK
    half = nc // 2
    ef, eb = _head_expanders()
    t = SSD_CHUNK
    fwd = lambda b, c: b * nc + c
    bwd = lambda b, c: b * nc + nc - 1 - c
    zfw = lambda b, c: b * nc + jnp.maximum(c, half)
    zbw = lambda b, c: b * nc + jnp.minimum(nc - 1 - c, half - 1)
    bc0 = D_SSM // SSD_STATE // SSD_GROUPS
    const = lambda shape: pl.BlockSpec(shape, lambda b, c: (0,) * len(shape))
    st_spec = pl.BlockSpec((1, SSD_GROUPS, SSD_STATE, GROUP_W), lambda b, c: (b, 0, 0, 0))
    dskip = jnp.repeat(d_skip.astype(f32), SSD_HEAD_DIM).reshape(1, D_SSM)
    return pl.pallas_call(
        functools.partial(_ssd_lat_kernel, nc),
        grid=(nb, nc),
        in_specs=[pl.BlockSpec((t, D_SSM), lambda b, c: (fwd(b, c), 0)),
                  pl.BlockSpec((t, D_SSM), lambda b, c: (bwd(b, c), 0)),
                  pl.BlockSpec((t, BC_DIM), lambda b, c: (fwd(b, c), bc0)),
                  pl.BlockSpec((t, BC_DIM), lambda b, c: (bwd(b, c), bc0 + 1)),
                  pl.BlockSpec((t, BC_DIM), lambda b, c: (fwd(b, c), bc0 + 2)),
                  pl.BlockSpec((t, BC_DIM), lambda b, c: (bwd(b, c), bc0 + 3)),
                  pl.BlockSpec((t, LANES), lambda b, c: (fwd(b, c), 0)),
                  pl.BlockSpec((t, LANES), lambda b, c: (bwd(b, c), 0)),
                  pl.BlockSpec((t, D_SSM), lambda b, c: (zfw(b, c), COL_Z // D_SSM)),
                  pl.BlockSpec((t, D_SSM), lambda b, c: (zbw(b, c), COL_Z // D_SSM)),
                  st_spec, st_spec,
                  const((1, LANES)), const((1, LANES)), const((1, D_SSM)), const((1, D_SSM)),
                  const((LANES, D_SSM)), const((LANES, D_SSM))],
        out_specs=pl.BlockSpec((seq, D_SSM), lambda b, c: (b, 0)),
        out_shape=jax.ShapeDtypeStruct((nb * seq, D_SSM), bf16),
        scratch_shapes=[pltpu.VMEM((SSD_GROUPS, SSD_STATE, GROUP_W), f32),
                        pltpu.VMEM((SSD_GROUPS, SSD_STATE, GROUP_W), f32),
                        pltpu.VMEM((seq, D_SSM), f32)],
        compiler_params=_params("parallel", "arbitrary"),
        name="ssd_latent",
    )(xc, xc, xc, xc, xc, xc, dt_raw, dt_raw, proj, proj, hf0, hb0,
      _pad_lanes(dt_bias), _pad_lanes(a_log), dskip, norm_w.reshape(1, -1).astype(f32), ef, eb)


def _filter_kernel(z_ref, w1_ref, b1_ref, wh_ref, bh_ref, freq_ref, wf_ref, wb_ref, delta_ref, ks_ref, kd_ref):
    hp = lambda a, b: jnp.dot(a, b, preferred_element_type=f32, precision=lax.Precision.HIGHEST)
    z = z_ref[...]
    freq = freq_ref[...]
    h = jnp.sin(freq * (hp(z, w1_ref[...]) + b1_ref[...]))
    for n in range(HYENA_INNER):
        h = jnp.sin(freq * (hp(h, wh_ref[n]) + bh_ref[n]))
    window = jnp.exp(-z[:, 0:1] * delta_ref[...])
    hf = hp(h, wf_ref[...]) * window
    hb = hp(h, wb_ref[...]) * window
    row = lax.broadcasted_iota(jnp.int32, hb.shape, 0)
    hb = jnp.where(row == 0, 0.0, hb)
    norm = jnp.sum(jnp.abs(hf), axis=0, keepdims=True) + jnp.sum(jnp.abs(hb), axis=0, keepdims=True)
    ks_ref[...] = (hf + hb) / norm
    kd_ref[...] = (hf - hb) / norm


def hyena_filter_taps(seq, f_w1, f_b1, f_wh, f_bh, f_wout, freq, tc=256):
    fw = HYENA_FILTER_WIDTH
    t = np.arange(seq, dtype=np.float32)
    t_norm = t / np.float32(seq - 1)
    bands = np.linspace(1e-4, HYENA_BANDS - 1, HYENA_BANDS, dtype=np.float32)
    ang = np.float32(2.0 * math.pi / seq) * t[:, None] * bands
    z = np.concatenate([t_norm[:, None], np.cos(ang), -np.sin(ang)], axis=-1).astype(np.float32)
    z = np.pad(z, ((0, 0), (0, LANES - z.shape[1])))
    deltas = np.abs(np.linspace(math.log(HYENA_TARGET) / HYENA_SLOW_DECAY, math.log(HYENA_TARGET) / HYENA_FAST_DECAY,
                                D_MODEL, dtype=np.float32)).reshape(1, -1)
    padw = lambda a, r, c: jnp.pad(a.astype(f32), [(0, 0)] * (a.ndim - 2) + [(0, r - a.shape[-2]), (0, c - a.shape[-1])])
    w1 = padw(f_w1, LANES, LANES)
    wh = padw(f_wh, LANES, LANES)
    wout = padw(f_wout, LANES, 2 * D_MODEL)
    b1 = padw(f_b1.reshape(1, fw), 1, LANES)
    bh = padw(f_bh.reshape(HYENA_INNER, 1, fw), 1, LANES)
    fq = padw(freq.reshape(1, fw), 1, LANES)
    nt = D_MODEL // tc
    const = lambda shape: pl.BlockSpec(shape, lambda j: (0,) * len(shape))
    out = jax.ShapeDtypeStruct((seq, D_MODEL), f32)
    osp = pl.BlockSpec((seq, tc), lambda j: (0, j))
    return pl.pallas_call(
        _filter_kernel,
        grid=(nt,),
        in_specs=[const((seq, LANES)), const((LANES, LANES)), const((1, LANES)),
                  const((HYENA_INNER, LANES, LANES)), const((HYENA_INNER, 1, LANES)), const((1, LANES)),
                  pl.BlockSpec((LANES, tc), lambda j: (0, j)),
                  pl.BlockSpec((LANES, tc), lambda j: (0, j + nt)),
                  pl.BlockSpec((1, tc), lambda j: (0, j))],
        out_specs=[osp, osp],
        out_shape=[out, out],
        compiler_params=_params("parallel"),
        name="hyena_filter",
    )(jnp.asarray(z), w1, b1, wh, bh, fq, wout, wout, jnp.asarray(deltas))


def _dft_tables(seq):
    n = 2 * seq
    half = seq // 2

    def theta(f, t):
        return (2.0 * math.pi / (2 * n)) * (((2 * f[:, None] + 1) * t[None, :]) % (2 * n)).astype(np.float64)

    f_low = np.arange(half, dtype=np.int64)
    order = np.concatenate([f_low, seq - 1 - f_low])
    th_full = theta(order, np.arange(seq, dtype=np.int64))
    tp = np.arange(half, dtype=np.int64)
    th_e, th_o = theta(f_low, 2 * tp), theta(f_low, 2 * tp + 1)
    fwd = [np.cos(th_e), np.cos(th_o), np.sin(th_e), np.sin(th_o)]
    as_bf16 = lambda a: jnp.asarray(a, bf16)
    return (as_bf16(np.cos(th_full)), as_bf16(np.sin(th_full)),
            [as_bf16(a) for a in fwd], [as_bf16(a.T * (2.0 / n)) for a in fwd])


def _spectrum_kernel(c_ref, s_ref, ks_ref, kd_ref, kre_ref, kb_ref):
    kre_ref[...] = sum(_dot(c_ref[...], part) for part in _split2(ks_ref[...]))
    kb_ref[...] = sum(_dot(s_ref[...], part) for part in _split2(kd_ref[...]))


def filter_spectrum(cmat, smat, ks, kd, tmf=256, tc=256):
    seq = ks.shape[0]
    fsp = pl.BlockSpec((tmf, seq), lambda j, m: (m, 0))
    ksp = pl.BlockSpec((seq, tc), lambda j, m: (0, j))
    osp = pl.BlockSpec((tmf, tc), lambda j, m: (m, j))
    out = jax.ShapeDtypeStruct((seq, D_MODEL), f32)
    return pl.pallas_call(
        _spectrum_kernel,
        grid=(D_MODEL // tc, seq // tmf),
        in_specs=[fsp, fsp, ksp, ksp],
        out_specs=[osp, osp],
        out_shape=[out, out],
        compiler_params=_params("parallel", "parallel"),
        name="filter_spectrum",
    )(cmat, smat, ks, kd)


def _dft_fwd_kernel(ce_ref, co_ref, se_ref, so_ref, ve_ref, vo_ref, kre_ref, kb_ref, krem_ref, kbm_ref,
                    pp_ref, pm_ref, qp_ref, qm_ref):
    ve, vo = ve_ref[...], vo_ref[...]
    ec, oc = _dot(ce_ref[...], ve), _dot(co_ref[...], vo)
    es, os_ = _dot(se_ref[...], ve), _dot(so_ref[...], vo)

    def times_filter(vre, va, kre, kb):
        return vre * kre - va * kb, vre * kb + va * kre

    p, q = times_filter(ec + oc, es + os_, kre_ref[...], kb_ref[...])
    p_m, q_m = times_filter(ec - oc, os_ - es, krem_ref[...], kbm_ref[...])
    pp_ref[...] = (p + p_m).astype(bf16)
    pm_ref[...] = (p - p_m).astype(bf16)
    qp_ref[...] = (q + q_m).astype(bf16)
    qm_ref[...] = (q - q_m).astype(bf16)


def dft_forward(fwd_tables, vg_even, vg_odd, kre, kb, seq, tmf=512):
    half = seq // 2
    nb = vg_even.shape[0] // half
    tmf = min(tmf, half)
    nm = half // tmf
    fsp = pl.BlockSpec((tmf, half), lambda b, m: (m, 0))
    ksp = pl.BlockSpec((tmf, D_MODEL), lambda b, m: (m, 0))
    kmsp = pl.BlockSpec((tmf, D_MODEL), lambda b, m: (m + nm, 0))
    osp = pl.BlockSpec((tmf, D_MODEL), lambda b, m: (b * nm + m, 0))
    out = jax.ShapeDtypeStruct((nb * half, D_MODEL), bf16)
    return pl.pallas_call(
        _dft_fwd_kernel,
        grid=(nb, nm),
        in_specs=[fsp, fsp, fsp, fsp,
                  pl.BlockSpec((half, D_MODEL), lambda b, m: (b, 0)),
                  pl.BlockSpec((half, D_MODEL), lambda b, m: (b, 0)),
                  ksp, ksp, kmsp, kmsp],
        out_specs=[osp] * 4,
        out_shape=[out] * 4,
        compiler_params=_params("parallel", "parallel"),
        name="dft_forward",
    )(*fwd_tables, vg_even, vg_odd, kre, kb, kre, kb)


def _dft_inv_kernel(gate_row, cet_ref, cot_ref, set_ref, sot_ref, pp_ref, pm_ref, qp_ref, qm_ref,
                    vge_ref, vgo_ref, x0e_ref, x0o_ref, bias_ref, w_ref, h_ref, mod_ref, o_ref, stage_ref):
    gate = mod_ref[0, gate_row:gate_row + 1, :]
    tmt = vge_ref.shape[0]
    nlane = D_MODEL // LANES
    for c in range(nlane):
        stage_ref[c] = h_ref[:, c * LANES:(c + 1) * LANES]

    def finish(y, parity, vg_ref, x0_ref):
        y = y + vg_ref[...].astype(f32) * bias_ref[...]
        upd = gate * _dot((y * x0_ref[...].astype(f32)).astype(bf16), w_ref[...])
        rows = pl.ds(parity, tmt, stride=2)
        for c in range(nlane):
            chunk = stage_ref.at[c]
            chunk[rows, :] = chunk[rows, :] + upd[:, c * LANES:(c + 1) * LANES]

    finish(_dot(cet_ref[...], pp_ref[...]) + _dot(set_ref[...], qm_ref[...]), 0, vge_ref, x0e_ref)
    finish(_dot(cot_ref[...], pm_ref[...]) + _dot(sot_ref[...], qp_ref[...]), 1, vgo_ref, x0o_ref)
    for c in range(nlane):
        o_ref[:, c * LANES:(c + 1) * LANES] = stage_ref[c]


def dft_inverse_out(inv_tables, folded, vg_eo, x0_eo, bias, w_out, h, mods, seq, gate_row, tmt=256):
    half = seq // 2
    nb = h.shape[0] // seq
    tmt = min(tmt, half)
    nm = half // tmt
    gsp = pl.BlockSpec((tmt, half), lambda b, m: (m, 0))
    full = pl.BlockSpec((half, D_MODEL), lambda b, m: (b, 0))
    part = pl.BlockSpec((tmt, D_MODEL), lambda b, m: (b * nm + m, 0))
    tile = pl.BlockSpec((2 * tmt, D_MODEL), lambda b, m: (b * nm + m, 0))
    return pl.pallas_call(
        functools.partial(_dft_inv_kernel, gate_row),
        grid=(nb, nm),
        in_specs=[gsp, gsp, gsp, gsp, full, full, full, full, part, part, part, part,
                  pl.BlockSpec((1, D_MODEL), lambda b, m: (0, 0)),
                  pl.BlockSpec((D_MODEL, D_MODEL), lambda b, m: (0, 0)),
                  tile, pl.BlockSpec((1, N_MOD, D_MODEL), lambda b, m: (b, 0, 0))],
        out_specs=tile,
        out_shape=jax.ShapeDtypeStruct(h.shape, f32),
        scratch_shapes=[pltpu.VMEM((D_MODEL // LANES, 2 * tmt, LANES), f32)],
        compiler_params=_params("parallel", "parallel"),
        name="dft_inverse_out",
    )(*inv_tables, *folded, *vg_eo, *x0_eo, bias.reshape(1, -1).astype(f32), w_out, h, mods)


_Q_HEAD_ORDER = (0, 4, 1, 5, 2, 6, 3, 7)


def _mixer_in_weight(w_in):
    c = np.cumsum([ATTN_DIM, KV_DIM, KV_DIM, D_SSM, XBC_DIM])
    q, k, v, z, xbc, dt = (w_in[:, :c[0]], w_in[:, c[0]:c[1]], w_in[:, c[1]:c[2]], w_in[:, c[2]:c[3]],
                           w_in[:, c[3]:c[4]], w_in[:, c[4]:])
    q = q.reshape(D_MODEL, N_Q_HEADS, HEAD_DIM)[:, np.array(_Q_HEAD_ORDER), :].reshape(D_MODEL, ATTN_DIM)
    w = jnp.concatenate([xbc, z, q, k, v], axis=1).astype(bf16)
    return w, jnp.pad(dt, ((0, 0), (0, LANES - dt.shape[1]))).astype(bf16)


def _mixer_out_weights(w_out):
    wa = w_out[:ATTN_DIM].reshape(N_Q_HEADS, HEAD_DIM, D_MODEL)[np.array(_Q_HEAD_ORDER)].reshape(ATTN_DIM, D_MODEL)
    return wa.astype(bf16), w_out[ATTN_DIM:].astype(bf16)


def kernel(x, c, ctx, c_ctx, w_ada, b_ada, norm_g, ffn_w13, ffn_w2, mix_w_in, mix_w_out, q_norm, k_norm, attn_sink,
           ssd_conv_w, ssd_conv_b, ssd_dt_bias, ssd_a_log, ssd_d, ssd_norm, hy_w_in, hy_conv_w, hy_conv_b,
           hy_f_w1, hy_f_b1, hy_f_wh, hy_f_bh, hy_f_wout, hy_freq, hy_bias, hy_w_out):
    nb, seq, _ = x.shape
    ctx_len = ctx.shape[1]
    depth = w_ada.shape[0]
    assert depth == 2, "this kernel is written for the two-layer block"
    h_lat = x.reshape(nb * seq, D_MODEL)
    h_ctx = ctx.reshape(nb * ctx_len, D_MODEL)
    cond = jnp.concatenate([c, c_ctx[None]], axis=0)
    cond = jnp.pad(cond, ((0, -cond.shape[0] % 8), (0, 0)))
    ffn_w = [[_ffn_weights(ffn_w13[i, k], ffn_w2[i, k]) for k in range(2)] for i in range(depth)]

    mods = ada_mods(cond, w_ada[0], b_ada[0])
    m_lat, m_ctx = mods[:nb], mods[nb:nb + 1]
    h_lat = macaron_ffn(h_lat, m_lat, seq, norm_g[0, 0], 0, *ffn_w[0][0])
    h_ctx = macaron_ffn(h_ctx, m_ctx, nb * ctx_len, norm_g[0, 0], 0, *ffn_w[0][0])
    w_in, w_dt = _mixer_in_weight(mix_w_in[0])
    wa, ws = _mixer_out_weights(mix_w_out[0])
    xc_lat, p_lat, dt_lat = mixer_proj(h_lat, m_lat, seq, seq, norm_g[0, 1], w_in, w_dt, ssd_conv_w[0], ssd_conv_b[0])
    xc_ctx, p_ctx, dt_ctx = mixer_proj(h_ctx, m_ctx, nb * ctx_len, ctx_len, norm_g[0, 1], w_in, w_dt,
                                       ssd_conv_w[0], ssd_conv_b[0])
    cos, sin = _rope_tables(seq)
    a_lat = window_attention(p_lat, p_ctx, seq, ctx_len, q_norm[0], k_norm[0], attn_sink[0], cos, sin)
    hf0, hb0 = ssd_ctx_states(xc_ctx, dt_ctx, ctx_len, ssd_dt_bias[0], ssd_a_log[0])
    s_lat = ssd_latent(xc_lat, p_lat, dt_lat, seq, hf0, hb0, ssd_dt_bias[0], ssd_a_log[0], ssd_d[0], ssd_norm[0])
    h_lat = out_proj_residual([a_lat, s_lat], [wa, ws], h_lat, m_lat, seq, 5)
    h_lat = macaron_ffn(h_lat, m_lat, seq, norm_g[0, 2], 2, *ffn_w[0][1])

    m_lat = ada_mods(cond, w_ada[1], b_ada[1])[:nb]
    h_lat = macaron_ffn(h_lat, m_lat, seq, norm_g[1, 0], 0, *ffn_w[1][0])
    x0e, x0o, vge, vgo = hyena_proj(h_lat, m_lat, seq, seq, norm_g[1, 1], hy_w_in[0].astype(bf16),
                                    hy_conv_w[0], hy_conv_b[0])
    ks, kd = hyena_filter_taps(seq, hy_f_w1[0], hy_f_b1[0], hy_f_wh[0], hy_f_bh[0], hy_f_wout[0], hy_freq[0])
    c_full, s_full, fwd_tables, inv_tables = _dft_tables(seq)
    kre, kb = filter_spectrum(c_full, s_full, ks, kd)
    folded = dft_forward(fwd_tables, vge, vgo, kre, kb, seq)
    h_lat = dft_inverse_out(inv_tables, folded, (vge, vgo), (x0e, x0o), hy_bias[0], hy_w_out[0].astype(bf16),
                            h_lat, m_lat, seq, 5)
    h_lat = macaron_ffn(h_lat, m_lat, seq, norm_g[1, 2], 2, *ffn_w[1][1])
    return h_lat.reshape(nb, seq, D_MODEL)
```

```python
import functools
import math

import numpy as np
import jax
import jax.numpy as jnp
from jax import lax
from jax.experimental import pallas as pl
from jax.experimental.pallas import tpu as pltpu

f32 = jnp.float32
bf16 = jnp.bfloat16

D_MODEL = 1024
N_MOD = 9
RMS_EPS = 1e-6
GRID_W = 64

HEAD_DIM = 64
N_Q_HEADS = 8
N_KV_HEADS = 2
ATTN_DIM = N_Q_HEADS * HEAD_DIM
KV_DIM = N_KV_HEADS * HEAD_DIM
WINDOW = 128
ATTN_BLOCK = 128
ROPE_THETA = 10000.0

SSD_HEADS = 16
SSD_HEAD_DIM = 64
D_SSM = SSD_HEADS * SSD_HEAD_DIM
SSD_GROUPS = 2
SSD_STATE = 128
SSD_CONV = 7
SSD_CHUNK = 128
BC_DIM = SSD_GROUPS * SSD_STATE
XBC_DIM = D_SSM + 4 * BC_DIM
GROUP_W = D_SSM // SSD_GROUPS

HYENA_SHORT = 3
HYENA_BANDS = 8
HYENA_FILTER_WIDTH = 64
HYENA_INNER = 2
HYENA_FAST_DECAY = 0.3
HYENA_SLOW_DECAY = 1.5
HYENA_TARGET = 1e-2

D_FF = 2816
FFN_TF = 256
LANES = 128

COL_Z = 0
COL_Q = COL_Z + D_SSM
COL_K = COL_Q + ATTN_DIM
COL_V = COL_K + KV_DIM
REST_COLS = COL_V + KV_DIM
HALO = 16
PROJ_TN = 256
PROJ_TM = 256
CONV_ROWS = 64

VMEM_LIMIT = 56 * 1024 * 1024


def _params(*sem):
    return pltpu.CompilerParams(dimension_semantics=sem, vmem_limit_bytes=VMEM_LIMIT)


def _dot(a, b):
    return jnp.dot(a, b, preferred_element_type=f32)


def _dot_nt(a, b):
    return lax.dot_general(a, b, (((1,), (1,)), ((), ())), preferred_element_type=f32)


def _split2(x):
    hi = x.astype(bf16)
    lo = (x - hi.astype(f32)).astype(bf16)
    return hi, lo


def _split3(x):
    hi = x.astype(bf16)
    r = x - hi.astype(f32)
    mid = r.astype(bf16)
    lo = (r - mid.astype(f32)).astype(bf16)
    return hi, mid, lo


def _adaln(h, g, shift, scale):
    ms = jnp.mean(h * h, axis=-1, keepdims=True)
    return (h * lax.rsqrt(ms + RMS_EPS) * g) * (1.0 + scale) + shift


def _silu(x):
    return x * jax.nn.sigmoid(x)


def _mods_kernel(c_ref, w_ref, b_ref, o_ref):
    o_ref[...] = _dot(_silu(c_ref[...]).astype(bf16), w_ref[...].astype(bf16)) + b_ref[...]


def ada_mods(cond, w, b):
    r = cond.shape[0]
    tn = 1024
    out = pl.pallas_call(
        _mods_kernel,
        grid=(w.shape[1] // tn,),
        in_specs=[pl.BlockSpec((r, D_MODEL), lambda j: (0, 0)),
                  pl.BlockSpec((D_MODEL, tn), lambda j: (0, j)),
                  pl.BlockSpec((1, tn), lambda j: (0, j))],
        out_specs=pl.BlockSpec((r, tn), lambda j: (0, j)),
        out_shape=jax.ShapeDtypeStruct((r, w.shape[1]), f32),
        compiler_params=_params("parallel"),
        name="ada_mods",
    )(cond, w, b.reshape(1, -1))
    return out.reshape(r, N_MOD, D_MODEL)


def _ffn_kernel(s, nf, n_mix, h_ref, mod_ref, g_ref, w13_ref, w2_ref, *rest):
    o_ref = rest[-1]
    h = h_ref[...]
    if n_mix:
        mixed = _dot(rest[0][...], rest[n_mix][...])
        for x_ref, w_ref in zip(rest[1:n_mix], rest[n_mix + 1:2 * n_mix]):
            mixed += _dot(x_ref[...], w_ref[...])
        h = h + mod_ref[0, 3 * s - 1:3 * s, :] * mixed
    u = _adaln(h, g_ref[...], mod_ref[0, 3 * s:3 * s + 1, :], mod_ref[0, 3 * s + 1:3 * s + 2, :]).astype(bf16)
    acc = None
    tf = w2_ref.shape[1]
    for j in range(nf):
        a = _dot(u, w13_ref[:, j * tf:(j + 1) * tf])
        b = _dot(u, w13_ref[:, D_FF + j * tf:D_FF + (j + 1) * tf])
        part = _dot((_silu(a) * b).astype(bf16), w2_ref[j])
        acc = part if acc is None else acc + part
    o_ref[...] = h + 0.5 * mod_ref[0, 3 * s + 2:3 * s + 3, :] * acc


def _ffn_weights(w13, w2, tf=FFN_TF):
    return w13.astype(bf16), w2.astype(bf16).reshape(D_FF // tf, tf, D_MODEL)


def macaron_ffn(h, mods, rows_per_mod, g, s, w13c, w2c, mix=(), tm=512):
    m = h.shape[0]
    tm = min(tm, rows_per_mod)
    nf = w2c.shape[0]
    tiles_per_mod = rows_per_mod // tm
    assert not mix or s == 2, "the mixer's gate is the modulation row just before the second FFN's"
    xs = [x for x, _ in mix]
    ws = [w for _, w in mix]
    return pl.pallas_call(
        functools.partial(_ffn_kernel, s, nf, len(mix)),
        grid=(m // tm,),
        in_specs=[pl.BlockSpec((tm, D_MODEL), lambda i: (i, 0)),
                  pl.BlockSpec((1, N_MOD, D_MODEL), lambda i: (i // tiles_per_mod, 0, 0)),
                  pl.BlockSpec((1, D_MODEL), lambda i: (0, 0)),
                  pl.BlockSpec(w13c.shape, lambda i: (0, 0)),
                  pl.BlockSpec(w2c.shape, lambda i: (0, 0, 0))]
        + [pl.BlockSpec((tm, x.shape[1]), lambda i: (i, 0)) for x in xs]
        + [pl.BlockSpec(w.shape, lambda i: (0, 0)) for w in ws],
        out_specs=pl.BlockSpec((tm, D_MODEL), lambda i: (i, 0)),
        out_shape=jax.ShapeDtypeStruct((m, D_MODEL), f32),
        compiler_params=_params("parallel"),
        name="macaron_ffn",
    )(h, mods, g.reshape(1, -1), w13c, w2c, *xs, *ws)


def _tile_adaln(s, tiles_per_seq, h_ref, hp_ref, hn_ref, mod_ref, g_ref, u_ref, uh_ref):
    shift, scale = mod_ref[0, 3 * s:3 * s + 1, :], mod_ref[0, 3 * s + 1:3 * s + 2, :]
    g = g_ref[...]
    u_ref[...] = _adaln(h_ref[...], g, shift, scale).astype(bf16)
    uh_ref[...] = _adaln(jnp.concatenate([hp_ref[...], hn_ref[...]], axis=0), g, shift, scale).astype(bf16)
    t = pl.program_id(0) % tiles_per_seq
    row = lax.broadcasted_iota(jnp.int32, (2 * HALO, 1), 0)
    keep_prev = jnp.where(t > 0, 1.0, 0.0)
    keep_next = jnp.where(t < tiles_per_seq - 1, 1.0, 0.0)
    return jnp.where(row < HALO, keep_prev, keep_next)


def _project_padded(pad_ref, u_ref, uh_ref, keep, w):
    tm = u_ref.shape[0]

    def halo_rows():
        halo = _dot(uh_ref[...], w) * keep
        pad_ref[0:HALO, :] = halo[0:HALO, :]
        pad_ref[HALO + tm:2 * HALO + tm, :] = halo[HALO:2 * HALO, :]

    def tile_rows(r0):
        pad_ref[HALO + r0:HALO + r0 + PROJ_TM, :] = _dot(u_ref[r0:r0 + PROJ_TM, :], w)

    return [halo_rows] + [functools.partial(tile_rows, r0) for r0 in range(0, tm, PROJ_TM)]


def _conv_padded(pad_ref, conv_w, conv_b, emit, between=()):
    tm = pad_ref.shape[0] - 2 * HALO
    width = conv_w.shape[0]
    rows = CONV_ROWS
    steps = [(l0, r0) for l0 in range(0, pad_ref.shape[1], LANES) for r0 in range(0, tm, rows)]
    every = max(1, len(steps) // max(1, len(between)))
    pending = list(between)
    half = width // 2
    for n, (l0, r0) in enumerate(steps):
        if pending and n % every == 0:
            pending.pop(0)()
        lanes = slice(l0, l0 + LANES)
        win = pad_ref[HALO + r0 - 8:HALO + r0 + rows + 8, lanes]
        acc = conv_b[:, lanes] + conv_w[half:half + 1, lanes] * win[8:8 + rows, :]
        for k in range(width):
            if k != half:
                acc += conv_w[k:k + 1, lanes] * pltpu.roll(win, (half - k) % (rows + 16), 0)[8:8 + rows, :]
        emit(r0, rows, l0, acc)
    for thunk in pending:
        thunk()


def _mixer_proj_kernel(s, tiles_per_seq, h_ref, hp_ref, hn_ref, mod_ref, g_ref, w_ref, wdt_ref, cw_ref, cb_ref,
                       perm_ref, xc_ref, rest_ref, dt_ref, pad_ref, u_ref, uh_ref, up_ref, ystage_ref):
    tm = h_ref.shape[0]
    grp = tm // 8
    shift, scale = mod_ref[0, 3 * s:3 * s + 1, :], mod_ref[0, 3 * s + 1:3 * s + 2, :]
    gain = g_ref[...]
    u_ref[...] = _adaln(h_ref[...], gain, shift, scale).astype(bf16)
    uh_ref[...] = _adaln(jnp.concatenate([hp_ref[...], hn_ref[...]], axis=0), gain, shift, scale).astype(bf16)
    up_ref[...] = _dot(perm_ref[...], u_ref[...]).astype(bf16)
    t = pl.program_id(0) % tiles_per_seq
    keep_prev = jnp.where(t > 0, 1.0, 0.0)
    keep_next = jnp.where(t < tiles_per_seq - 1, 1.0, 0.0)
    keep = jnp.where(lax.broadcasted_iota(jnp.int32, (2 * HALO, 1), 0) < HALO, keep_prev, keep_next)

    tn = PROJ_TN
    n_conv = XBC_DIM // tn
    n_rest = REST_COLS // tn
    half = SSD_CONV // 2
    ext = 8 * half
    sub = lax.broadcasted_iota(jnp.int32, (8, tn), 0)

    def project(c):
        w = w_ref[:, c * tn:(c + 1) * tn]
        pad = pad_ref.at[c % 2]

        def tile_rows(r0):
            pad[ext + r0:ext + r0 + PROJ_TM, :] = _dot(up_ref[r0:r0 + PROJ_TM, :], w)

        def boundary_tiles():
            halo = _dot(uh_ref[...], w) * keep
            for j in range(half):
                src = ext + 8 * (grp - half + j)
                pad[8 * j:8 * j + 8, :] = jnp.where(sub == 0, halo[HALO - half + j:HALO - half + j + 1, :],
                                                    pltpu.roll(pad[src:src + 8, :], 1, 0))
                src = ext + 8 * j
                dst = ext + tm + 8 * j
                pad[dst:dst + 8, :] = jnp.where(sub == 7, halo[HALO + j:HALO + j + 1, :],
                                                pltpu.roll(pad[src:src + 8, :], 7, 0))

        return [functools.partial(tile_rows, r0) for r0 in range(0, tm, PROJ_TM)] + [boundary_tiles]

    def conv(c, between):
        pad = pad_ref.at[c % 2]
        cols = slice(c * tn, (c + 1) * tn)
        rows = CONV_ROWS
        steps = [(l0, r0) for l0 in range(0, tn, LANES) for r0 in range(0, tm, rows)]
        every = max(1, len(steps) // max(1, len(between)))
        pending = list(between)
        for n, (l0, r0) in enumerate(steps):
            if pending and n % every == 0:
                pending.pop(0)()
            lanes = slice(l0, l0 + LANES)
            wcol = slice(c * tn + l0, c * tn + l0 + LANES)
            acc = cb_ref[:, wcol]
            for k in range(SSD_CONV):
                acc = acc + cw_ref[k:k + 1, wcol] * pad[8 * k + r0:8 * k + r0 + rows, lanes]
            ystage_ref[l0 // LANES, r0:r0 + rows, :] = _silu(acc)
        for thunk in pending:
            thunk()
        per = grp // 8
        for l0 in range(0, tn, LANES):
            stage = ystage_ref.at[l0 // LANES]
            for r0 in range(0, tm, rows):
                tiles = [stage[pl.ds(64 * (j % per) + j // per, 8, stride=8), :] for j in range(r0 // 8, (r0 + rows) // 8)]
                xc_ref[r0:r0 + rows, c * tn + l0:c * tn + l0 + LANES] = jnp.concatenate(tiles, axis=0).astype(bf16)

    def plain_rows(c, r0):
        rows = slice(r0, r0 + PROJ_TM)
        if c < n_rest:
            w = w_ref[:, XBC_DIM + c * tn:XBC_DIM + (c + 1) * tn]
            rest_ref[rows, c * tn:(c + 1) * tn] = _dot(u_ref[rows, :], w).astype(bf16)
        elif c == n_rest:
            dt_ref[rows, :] = _dot(u_ref[rows, :], wdt_ref[...])

    def plain(c):
        return [functools.partial(plain_rows, c, r0) for r0 in range(0, tm, PROJ_TM)]

    for thunk in project(0):
        thunk()
    for c in range(n_conv):
        conv(c, (project(c + 1) if c + 1 < n_conv else []) + plain(c))
    for c in range(n_conv, n_rest + 1):
        for thunk in plain(c):
            thunk()


def _hyena_proj_kernel(s, tiles_per_seq, h_ref, hp_ref, hn_ref, mod_ref, g_ref, w_ref, cw_ref, cb_ref,
                       x0e_ref, x0o_ref, vge_ref, vgo_ref, pad_ref, u_ref, uh_ref, x1_ref, split_ref):
    keep = _tile_adaln(s, tiles_per_seq, h_ref, hp_ref, hn_ref, mod_ref, g_ref, u_ref, uh_ref)
    tn = PROJ_TN
    cols = [part * D_MODEL + c0 for c0 in range(0, D_MODEL, tn) for part in range(3)]

    def project(i):
        return _project_padded(pad_ref.at[i % 2], u_ref, uh_ref, keep, w_ref[:, cols[i]:cols[i] + tn])

    for thunk in project(0):
        thunk()
    for i, col in enumerate(cols):
        between = project(i + 1) if i + 1 < len(cols) else []
        c0 = col % D_MODEL

        def split_tokens(value, even_ref, odd_ref, r0, rows, lanes):
            split_ref[...] = value
            dst = slice(r0 // 2, (r0 + rows) // 2)
            even_ref[dst, lanes] = split_ref[pl.ds(0, rows // 2, stride=2), :].astype(bf16)
            odd_ref[dst, lanes] = split_ref[pl.ds(1, rows // 2, stride=2), :].astype(bf16)

        def emit_x0(r0, rows, l0, acc, c0=c0):
            split_tokens(acc, x0e_ref, x0o_ref, r0, rows, slice(c0 + l0, c0 + l0 + LANES))

        def emit_x1(r0, rows, l0, acc):
            x1_ref[r0:r0 + rows, l0:l0 + LANES] = acc

        def emit_v(r0, rows, l0, acc, c0=c0):
            vg = acc * x1_ref[r0:r0 + rows, l0:l0 + LANES]
            split_tokens(vg, vge_ref, vgo_ref, r0, rows, slice(c0 + l0, c0 + l0 + LANES))

        emit = (emit_x0, emit_x1, emit_v)[col // D_MODEL]
        _conv_padded(pad_ref.at[i % 2], cw_ref[:, col:col + tn], cb_ref[:, col:col + tn], emit, between)


def _proj_call(body, name, h, mods, rows_per_mod, seq, g, consts, outs, pad_margin, extra_scratch, tm=512):
    m = h.shape[0]
    tm = min(tm, seq)
    tiles_per_seq = seq // tm
    tiles_per_mod = rows_per_mod // tm
    hb = tm // HALO
    in_specs = [pl.BlockSpec((tm, D_MODEL), lambda i: (i, 0)),
                pl.BlockSpec((HALO, D_MODEL), lambda i: (jnp.maximum(i * hb - 1, 0), 0)),
                pl.BlockSpec((HALO, D_MODEL), lambda i: (jnp.minimum((i + 1) * hb, m // HALO - 1), 0)),
                pl.BlockSpec((1, N_MOD, D_MODEL), lambda i: (i // tiles_per_mod, 0, 0)),
                pl.BlockSpec((1, D_MODEL), lambda i: (0, 0))]
    in_specs += [pl.BlockSpec(a.shape, lambda i: (0, 0)) for a in consts]
    return pl.pallas_call(
        functools.partial(body, 1, tiles_per_seq),
        grid=(m // tm,),
        in_specs=in_specs,
        out_specs=[pl.BlockSpec((tm // div, n), lambda i: (i, 0)) for div, n, _ in outs],
        out_shape=[jax.ShapeDtypeStruct((m // div, n), dt) for div, n, dt in outs],
        scratch_shapes=[pltpu.VMEM((2, tm + 2 * pad_margin, PROJ_TN), f32), pltpu.VMEM((tm, D_MODEL), bf16),
                        pltpu.VMEM((2 * HALO, D_MODEL), bf16)] + extra_scratch(tm),
        compiler_params=_params("parallel"),
        name=name,
    )(h, h, h, mods, g.reshape(1, -1), *consts)


def mixer_proj(h, mods, rows_per_mod, seq, g, w, w_dt, conv_w, conv_b):
    tm = min(512, seq)
    r = np.arange(tm)
    perm = np.zeros((tm, tm), np.float32)
    perm[r, (r % 8) * (tm // 8) + r // 8] = 1.0
    consts = [w, w_dt, conv_w.astype(f32), conv_b.reshape(1, -1).astype(f32), jnp.asarray(perm, bf16)]
    scratch = lambda tm: [pltpu.VMEM((tm, D_MODEL), bf16), pltpu.VMEM((PROJ_TN // LANES, tm, LANES), f32)]
    return _proj_call(_mixer_proj_kernel, "mixer_proj", h, mods, rows_per_mod, seq, g, consts,
                      [(1, XBC_DIM, bf16), (1, REST_COLS, bf16), (1, LANES, f32)], 8 * (SSD_CONV // 2), scratch)


def hyena_proj(h, mods, rows_per_mod, seq, g, w, conv_w, conv_b):
    consts = [w, conv_w.astype(f32), conv_b.reshape(1, -1).astype(f32)]
    scratch = lambda tm: [pltpu.VMEM((tm, PROJ_TN), f32), pltpu.VMEM((CONV_ROWS, LANES), f32)]
    return _proj_call(_hyena_proj_kernel, "hyena_proj", h, mods, rows_per_mod, seq, g, consts,
                      [(2, D_MODEL, bf16)] * 4, HALO, scratch)


def _head_norm(x, gain, bd):
    hi, lo = _split2(x * x)
    ms = _dot(hi, bd) + _dot(lo, bd)
    return x * lax.rsqrt(ms + RMS_EPS) * gain


def _rope(x, cos, sin_signed):
    lane = lax.broadcasted_iota(jnp.int32, x.shape, 1)
    partner = jnp.where((lane & 16) != 0, pltpu.roll(x, 16, 1), pltpu.roll(x, LANES - 16, 1))
    return x * cos + partner * sin_signed


def _t_bf16(x):
    return x.astype(f32).T.astype(bf16)


def _attn_kernel(seq, q_ref, k_ref, v_ref, kc_ref, vc_ref, qg_ref, kg_ref, cos_ref, sin_ref, bd_ref,
                 sink_ref, o_ref, qt_s, k_s, vt_s, kc_s, vct_s):
    j = pl.program_id(1)
    nblk = seq // ATTN_BLOCK
    nslab = ATTN_DIM // LANES
    blk = ATTN_BLOCK

    @pl.when(j == 0)
    def _():
        bd = bd_ref[...]
        cos, sin = cos_ref[...], sin_ref[...]
        scale = HEAD_DIM ** -0.5
        for p in range(nslab):
            qn = _head_norm(q_ref[:, p * LANES:(p + 1) * LANES].astype(f32), qg_ref[...], bd)
            qr = _rope(qn, cos, sin) * scale
            for jb in range(nblk):
                qt_s[jb, p * LANES:(p + 1) * LANES, :] = _t_bf16(qr[jb * blk:(jb + 1) * blk, :])
        kn = _head_norm(k_ref[...].astype(f32), kg_ref[...], bd)
        zeros = jnp.zeros((WINDOW, KV_DIM), bf16)
        k_s[0:WINDOW, :] = zeros
        k_s[WINDOW + seq:2 * WINDOW + seq, :] = zeros
        k_s[WINDOW:WINDOW + seq, :] = _rope(kn, cos, sin).astype(bf16)
        vt_s[0] = zeros
        vt_s[nblk + 1] = zeros
        for jb in range(nblk):
            vt_s[jb + 1] = _t_bf16(v_ref[jb * blk:(jb + 1) * blk, :])
        kc_s[...] = _head_norm(kc_ref[...].astype(f32), kg_ref[...], bd).astype(bf16)
        vct_s[...] = _t_bf16(vc_ref[...])

    band = blk + 2 * WINDOW
    start = pl.multiple_of(j * blk, blk)
    kb = k_s[pl.ds(start, band), :]
    kc = kc_s[...]
    vtb = jnp.concatenate([vt_s[j], vt_s[j + 1], vt_s[j + 2]], axis=1)
    vtc = vct_s[...]
    qt = qt_s[j]
    key = lax.broadcasted_iota(jnp.int32, (blk, 2 * blk), 0)
    qry = lax.broadcasted_iota(jnp.int32, (blk, 2 * blk), 1) & (blk - 1)
    ok_lo = (jnp.abs(qry - (key - WINDOW)) <= WINDOW) & (start - WINDOW + key >= 0)
    ok_hi = (jnp.abs(qry - (key + blk)) <= WINDOW) & (start + blk + key < seq)
    dim = lax.broadcasted_iota(jnp.int32, (LANES, blk), 0)
    lane2 = lax.broadcasted_iota(jnp.int32, (1, 2 * blk), 1)
    ones_b = jnp.ones((16, band), bf16)
    ones_c = jnp.ones((16, kc.shape[0]), bf16)
    def scores(p):
        qslab = qt[p * LANES:(p + 1) * LANES, :]
        zero = jnp.zeros_like(qslab)
        rhs = jnp.concatenate([jnp.where(dim < HEAD_DIM, qslab, zero), jnp.where(dim >= HEAD_DIM, qslab, zero)], axis=1)
        return _dot(kb, rhs), _dot(kc, rhs)

    def softmax(p, sb, sc):
        s_lo = jnp.where(ok_lo, sb[0:blk, :], -jnp.inf)
        s_mid = sb[blk:2 * blk, :]
        s_hi = jnp.where(ok_hi, sb[2 * blk:3 * blk, :], -jnp.inf)
        sink = jnp.where(lane2 < blk, sink_ref[p], sink_ref[p + N_Q_HEADS // N_KV_HEADS])
        colmax = lambda s: jnp.max(s, axis=0, keepdims=True)
        mx = jnp.maximum(jnp.maximum(jnp.maximum(colmax(s_lo), colmax(s_mid)), jnp.maximum(colmax(s_hi), colmax(sc))),
                         sink)
        pb = jnp.concatenate([jnp.exp(s_lo - mx).astype(bf16), jnp.exp(s_mid - mx).astype(bf16),
                              jnp.exp(s_hi - mx).astype(bf16)], axis=0)
        return pb, jnp.exp(sc - mx).astype(bf16), jnp.exp(sink - mx)

    def values(p, pb, pc, sink_term):
        den = (_dot(ones_b, pb) + _dot(ones_c, pc))[0:1, :] + sink_term
        ot = (_dot(vtb, pb) + _dot(vtc, pc)) / den
        both = jnp.where(dim < HEAD_DIM, ot[:, 0:blk], ot[:, blk:2 * blk])
        o_ref[:, p * LANES:(p + 1) * LANES] = both.T.astype(bf16)

    s_next = scores(0)
    for p in range(nslab):
        s_cur = s_next
        if p + 1 < nslab:
            s_next = scores(p + 1)
        values(p, *softmax(p, *s_cur))


def window_attention(proj_lat, proj_ctx, seq, ctx_len, q_gain, k_gain, sink, rope_cos, rope_sin):
    nb = proj_lat.shape[0] // seq
    nblk = seq // ATTN_BLOCK
    bd = np.kron(np.eye(LANES // HEAD_DIM), np.ones((HEAD_DIM, HEAD_DIM))) / HEAD_DIM
    gain2 = lambda g: jnp.tile(g, LANES // HEAD_DIM).reshape(1, LANES)
    const = lambda shape: pl.BlockSpec(shape, lambda b, j: (0, 0))
    return pl.pallas_call(
        functools.partial(_attn_kernel, seq),
        grid=(nb, nblk),
        in_specs=[pl.BlockSpec((seq, ATTN_DIM), lambda b, j: (b, COL_Q // ATTN_DIM)),
                  pl.BlockSpec((seq, KV_DIM), lambda b, j: (b, COL_K // KV_DIM)),
                  pl.BlockSpec((seq, KV_DIM), lambda b, j: (b, COL_V // KV_DIM)),
                  pl.BlockSpec((ctx_len, KV_DIM), lambda b, j: (b, COL_K // KV_DIM)),
                  pl.BlockSpec((ctx_len, KV_DIM), lambda b, j: (b, COL_V // KV_DIM)),
                  const((1, LANES)), const((1, LANES)),
                  const((seq, LANES)), const((seq, LANES)), const((LANES, LANES)),
                  pl.BlockSpec(memory_space=pltpu.SMEM)],
        out_specs=pl.BlockSpec((ATTN_BLOCK, ATTN_DIM), lambda b, j: (b * nblk + j, 0)),
        out_shape=jax.ShapeDtypeStruct((nb * seq, ATTN_DIM), bf16),
        scratch_shapes=[pltpu.VMEM((nblk, ATTN_DIM, ATTN_BLOCK), bf16),
                        pltpu.VMEM((seq + 2 * WINDOW, KV_DIM), bf16),
                        pltpu.VMEM((nblk + 2, KV_DIM, ATTN_BLOCK), bf16),
                        pltpu.VMEM((ctx_len, KV_DIM), bf16),
                        pltpu.VMEM((KV_DIM, ctx_len), bf16)],
        compiler_params=_params("parallel", "arbitrary"),
        name="window_attention",
    )(proj_lat, proj_lat, proj_lat, proj_ctx, proj_ctx, gain2(q_gain), gain2(k_gain),
      rope_cos, rope_sin, jnp.asarray(bd, bf16), sink)


def _rope_tables(seq):
    t = np.arange(seq)
    pos = np.stack([t // GRID_W, t % GRID_W], axis=1).astype(np.float32)
    axis_dim = HEAD_DIM // 2
    inv = (ROPE_THETA ** (-np.arange(0, axis_dim, 2, dtype=np.float32) / axis_dim)).astype(np.float32)
    lane = np.arange(LANES)
    d = lane % HEAD_DIM
    which = d // axis_dim
    ang = (pos[:, which] * inv[d % (axis_dim // 2)][None, :]).astype(np.float32)
    sign = np.where((d % axis_dim) < axis_dim // 2, -1.0, 1.0)
    return jnp.asarray(np.cos(ang), f32), jnp.asarray(np.sin(ang) * sign, f32)


def _softplus(x):
    return jnp.maximum(x, 0.0) + jnp.log1p(jnp.exp(-jnp.abs(x)))


def _expand_heads(v, e):
    return _dot(v.astype(bf16), e)


def _ssd_chunk(rev, lane0, want_y, x, bm, cm, dt_raw, dt_bias, a_neg, expand, state_ref, result):
    t = x.shape[0]
    dt = _softplus(dt_raw + dt_bias)
    a = dt * a_neg
    r = lax.broadcasted_iota(jnp.int32, (t, t), 0)
    c = lax.broadcasted_iota(jnp.int32, (t, t), 1)
    keep = (r <= c) if rev else (r >= c)
    tri = jnp.where(keep, 1.0, 0.0).astype(bf16)
    cs = sum(_dot(tri, part) for part in _split3(a))
    last = cs[0:1, :] if rev else cs[t - 1:t, :]
    e = jnp.exp(cs)
    w = dt * jnp.exp(last - cs)
    e_x = _expand_heads(e, expand)
    w_x = _expand_heads(w, expand)
    elast_x = e_x[0:1, :] if rev else e_x[t - 1:t, :]
    yield

    y = None
    if want_y:
        cs_t = cs.T
        dt_t = dt.T
        lane = lax.broadcasted_iota(jnp.int32, (t, LANES), 1)
        cb = [_dot_nt(cm[:, g * SSD_STATE:(g + 1) * SSD_STATE],
                      bm[:, g * SSD_STATE:(g + 1) * SSD_STATE]) for g in range(SSD_GROUPS)]
        yield
        pieces = []
        for p in range(SSD_HEADS // 2):
            xp = x[:, p * LANES:(p + 1) * LANES]
            ms = []
            for q in range(2):
                h = 2 * p + q
                g = h // (SSD_HEADS // SSD_GROUPS)
                seg = cs[:, lane0 + h:lane0 + h + 1] - cs_t[lane0 + h:lane0 + h + 1, :]
                dec = jnp.exp(jnp.where(keep, seg, -jnp.inf))
                ms.append((cb[g] * dec * dt_t[lane0 + h:lane0 + h + 1, :]).astype(bf16))
            zero = jnp.zeros_like(xp)
            xcat = jnp.concatenate([jnp.where(lane < SSD_HEAD_DIM, xp, zero),
                                    jnp.where(lane >= SSD_HEAD_DIM, xp, zero)], axis=0)
            pieces.append(_dot(jnp.concatenate(ms, axis=1), xcat))
            yield
        y = jnp.concatenate(pieces, axis=1)

    inter = []
    for g in range(SSD_GROUPS):
        gs = slice(g * GROUP_W, (g + 1) * GROUP_W)
        ss = slice(g * SSD_STATE, (g + 1) * SSD_STATE)
        h_t = state_ref[g]
        if want_y:
            inter.append(_dot(cm[:, ss], h_t.astype(bf16)) * e_x[:, gs])
        xw = (x[:, gs].astype(f32) * w_x[:, gs]).astype(bf16)
        state_ref[g] = h_t * elast_x[:, gs] + _dot(bm[:, ss].astype(f32).T.astype(bf16), xw)
        yield
    if want_y:
        result.append(y + jnp.concatenate(inter, axis=1))


def _interleave(*stage_generators):
    active = list(stage_generators)
    while active:
        for gen in list(active):
            if next(gen, StopIteration) is StopIteration:
                active.remove(gen)


def _ssd_ctx_kernel(nchunk, x_ref, bc_ref, dt_ref, bias_ref, alog_ref, ef_ref, eb_ref, hf_ref, hb_ref, sf, sb):
    sf[...] = jnp.zeros_like(sf)
    sb[...] = jnp.zeros_like(sb)
    a_neg = -jnp.exp(alog_ref[...])
    bias = bias_ref[...]
    t = SSD_CHUNK
    for ci in range(nchunk):
        rows = slice(ci * t, (ci + 1) * t)
        fwd = _ssd_chunk(False, 0, False, x_ref[rows, :], bc_ref[rows, 0:BC_DIM], None, dt_ref[rows, :],
                         bias, a_neg, ef_ref[...], sf, None)
        rows = slice((nchunk - 1 - ci) * t, (nchunk - ci) * t)
        bwd = _ssd_chunk(True, SSD_HEADS, False, x_ref[rows, :], bc_ref[rows, BC_DIM:2 * BC_DIM], None,
                         dt_ref[rows, :], bias, a_neg, eb_ref[...], sb, None)
        _interleave(fwd, bwd)
    hf_ref[0] = sf[...]
    hb_ref[0] = sb[...]


def _head_expanders():
    ef = np.zeros((LANES, D_SSM), np.float32)
    eb = np.zeros((LANES, D_SSM), np.float32)
    for h in range(SSD_HEADS):
        ef[h, h * SSD_HEAD_DIM:(h + 1) * SSD_HEAD_DIM] = 1.0
        eb[SSD_HEADS + h, h * SSD_HEAD_DIM:(h + 1) * SSD_HEAD_DIM] = 1.0
    return jnp.asarray(ef, bf16), jnp.asarray(eb, bf16)


def _pad_lanes(v):
    v = v.reshape(1, -1).astype(f32)
    return jnp.pad(v, ((0, 0), (0, LANES - v.shape[1])))


def ssd_ctx_states(xc, dt_raw, seq, dt_bias, a_log):
    nb = xc.shape[0] // seq
    ef, eb = _head_expanders()
    const = lambda shape: pl.BlockSpec(shape, lambda b: (0,) * len(shape))
    st = jax.ShapeDtypeStruct((nb, SSD_GROUPS, SSD_STATE, GROUP_W), f32)
    st_spec = pl.BlockSpec((1, SSD_GROUPS, SSD_STATE, GROUP_W), lambda b: (b, 0, 0, 0))
    return pl.pallas_call(
        functools.partial(_ssd_ctx_kernel, seq // SSD_CHUNK),
        grid=(nb,),
        in_specs=[pl.BlockSpec((seq, D_SSM), lambda b: (b, 0)),
                  pl.BlockSpec((seq, 2 * BC_DIM), lambda b: (b, D_SSM // (2 * BC_DIM))),
                  pl.BlockSpec((seq, LANES), lambda b: (b, 0)),
                  const((1, LANES)), const((1, LANES)), const((LANES, D_SSM)), const((LANES, D_SSM))],
        out_specs=[st_spec, st_spec],
        out_shape=[st, st],
        scratch_shapes=[pltpu.VMEM((SSD_GROUPS, SSD_STATE, GROUP_W), f32),
                        pltpu.VMEM((SSD_GROUPS, SSD_STATE, GROUP_W), f32)],
        compiler_params=_params("parallel"),
        name="ssd_ctx_states",
    )(xc, xc, dt_raw, _pad_lanes(dt_bias), _pad_lanes(a_log), ef, eb)


def _ssd_lat_kernel(nchunk, xf_ref, xb_ref, bf_ref, bb_ref, cf_ref, cb_ref, dtf_ref, dtb_ref, zf_ref, zb_ref,
                    hf0_ref, hb0_ref, bias_ref, alog_ref, dskip_ref, normw_ref, ef_ref, eb_ref,
                    o_ref, sf, sb, yacc):
    c = pl.program_id(1)
    t = SSD_CHUNK

    @pl.when(c == 0)
    def _():
        sf[...] = hf0_ref[0]
        sb[...] = hb0_ref[0]

    a_neg = -jnp.exp(alog_ref[...])
    bias = bias_ref[...]
    xf = xf_ref[...]
    yf, yb = [], []
    _interleave(
        _ssd_chunk(False, 0, True, xf, bf_ref[...], cf_ref[...], dtf_ref[...], bias, a_neg, ef_ref[...], sf, yf),
        _ssd_chunk(True, SSD_HEADS, True, xb_ref[...], bb_ref[...], cb_ref[...], dtb_ref[...], bias, a_neg,
                   eb_ref[...], sb, yb))
    yf = yf[0] + dskip_ref[...] * xf.astype(f32)
    yb = yb[0]
    rows_f = pl.ds(pl.multiple_of(c * t, t), t)
    rows_b = pl.ds(pl.multiple_of((nchunk - 1 - c) * t, t), t)

    @pl.when(c < nchunk // 2)
    def _():
        yacc[rows_f, :] = yf
        yacc[rows_b, :] = yb

    def finish(y, z):
        y = y * _silu(z.astype(f32))
        outs = []
        for g in range(SSD_GROUPS):
            yg = y[:, g * GROUP_W:(g + 1) * GROUP_W]
            outs.append(yg * lax.rsqrt(jnp.mean(yg * yg, axis=-1, keepdims=True) + RMS_EPS))
        return (jnp.concatenate(outs, axis=1) * normw_ref[...]).astype(bf16)

    @pl.when(c >= nchunk // 2)
    def _():
        o_ref[rows_f, :] = finish(yacc[rows_f, :] + yf, zf_ref[...])
        o_ref[rows_b, :] = finish(yacc[rows_b, :] + yb, zb_ref[...])


def ssd_latent(xc, proj, dt_raw, seq, hf0, hb0, dt_bias, a_log, d_skip, norm_w):
    nb = xc.shape[0] // seq
    nc = seq // SSD_CHUNK
    half = nc // 2
    ef, eb = _head_expanders()
    t = SSD_CHUNK
    fwd = lambda b, c: b * nc + c
    bwd = lambda b, c: b * nc + nc - 1 - c
    zfw = lambda b, c: b * nc + jnp.maximum(c, half)
    zbw = lambda b, c: b * nc + jnp.minimum(nc - 1 - c, half - 1)
    bc0 = D_SSM // SSD_STATE // SSD_GROUPS
    const = lambda shape: pl.BlockSpec(shape, lambda b, c: (0,) * len(shape))
    st_spec = pl.BlockSpec((1, SSD_GROUPS, SSD_STATE, GROUP_W), lambda b, c: (b, 0, 0, 0))
    dskip = jnp.repeat(d_skip.astype(f32), SSD_HEAD_DIM).reshape(1, D_SSM)
    return pl.pallas_call(
        functools.partial(_ssd_lat_kernel, nc),
        grid=(nb, nc),
        in_specs=[pl.BlockSpec((t, D_SSM), lambda b, c: (fwd(b, c), 0)),
                  pl.BlockSpec((t, D_SSM), lambda b, c: (bwd(b, c), 0)),
                  pl.BlockSpec((t, BC_DIM), lambda b, c: (fwd(b, c), bc0)),
                  pl.BlockSpec((t, BC_DIM), lambda b, c: (bwd(b, c), bc0 + 1)),
                  pl.BlockSpec((t, BC_DIM), lambda b, c: (fwd(b, c), bc0 + 2)),
                  pl.BlockSpec((t, BC_DIM), lambda b, c: (bwd(b, c), bc0 + 3)),
                  pl.BlockSpec((t, LANES), lambda b, c: (fwd(b, c), 0)),
                  pl.BlockSpec((t, LANES), lambda b, c: (bwd(b, c), 0)),
                  pl.BlockSpec((t, D_SSM), lambda b, c: (zfw(b, c), COL_Z // D_SSM)),
                  pl.BlockSpec((t, D_SSM), lambda b, c: (zbw(b, c), COL_Z // D_SSM)),
                  st_spec, st_spec,
                  const((1, LANES)), const((1, LANES)), const((1, D_SSM)), const((1, D_SSM)),
                  const((LANES, D_SSM)), const((LANES, D_SSM))],
        out_specs=pl.BlockSpec((seq, D_SSM), lambda b, c: (b, 0)),
        out_shape=jax.ShapeDtypeStruct((nb * seq, D_SSM), bf16),
        scratch_shapes=[pltpu.VMEM((SSD_GROUPS, SSD_STATE, GROUP_W), f32),
                        pltpu.VMEM((SSD_GROUPS, SSD_STATE, GROUP_W), f32),
                        pltpu.VMEM((seq, D_SSM), f32)],
        compiler_params=_params("parallel", "arbitrary"),
        name="ssd_latent",
    )(xc, xc, xc, xc, xc, xc, dt_raw, dt_raw, proj, proj, hf0, hb0,
      _pad_lanes(dt_bias), _pad_lanes(a_log), dskip, norm_w.reshape(1, -1).astype(f32), ef, eb)


def _filter_kernel(z_ref, w1_ref, b1_ref, wh_ref, bh_ref, freq_ref, wf_ref, wb_ref, delta_ref, ks_ref, kd_ref, h_ref):
    hp = lambda a, b: jnp.dot(a, b, preferred_element_type=f32, precision=lax.Precision.HIGHEST)
    z = z_ref[...]

    @pl.when(pl.program_id(0) == 0)
    def _():
        freq = freq_ref[...]
        h = jnp.sin(freq * (hp(z, w1_ref[...]) + b1_ref[...]))
        for n in range(HYENA_INNER):
            h = jnp.sin(freq * (hp(h, wh_ref[n]) + bh_ref[n]))
        h_ref[...] = h

    h = h_ref[...]
    window = jnp.exp(-z[:, 0:1] * delta_ref[...])
    hf = hp(h, wf_ref[...]) * window
    hb = hp(h, wb_ref[...]) * window
    row = lax.broadcasted_iota(jnp.int32, hb.shape, 0)
    hb = jnp.where(row == 0, 0.0, hb)
    norm = jnp.sum(jnp.abs(hf), axis=0, keepdims=True) + jnp.sum(jnp.abs(hb), axis=0, keepdims=True)
    ks_ref[...] = (hf + hb) / norm
    kd_ref[...] = (hf - hb) / norm


def hyena_filter_taps(seq, f_w1, f_b1, f_wh, f_bh, f_wout, freq, tc=256):
    fw = HYENA_FILTER_WIDTH
    t = np.arange(seq, dtype=np.float32)
    t_norm = t / np.float32(seq - 1)
    bands = np.linspace(1e-4, HYENA_BANDS - 1, HYENA_BANDS, dtype=np.float32)
    ang = np.float32(2.0 * math.pi / seq) * t[:, None] * bands
    z = np.concatenate([t_norm[:, None], np.cos(ang), -np.sin(ang)], axis=-1).astype(np.float32)
    z = np.pad(z, ((0, 0), (0, LANES - z.shape[1])))
    deltas = np.abs(np.linspace(math.log(HYENA_TARGET) / HYENA_SLOW_DECAY, math.log(HYENA_TARGET) / HYENA_FAST_DECAY,
                                D_MODEL, dtype=np.float32)).reshape(1, -1)
    padw = lambda a, r, c: jnp.pad(a.astype(f32), [(0, 0)] * (a.ndim - 2) + [(0, r - a.shape[-2]), (0, c - a.shape[-1])])
    w1 = padw(f_w1, LANES, LANES)
    wh = padw(f_wh, LANES, LANES)
    wout = padw(f_wout, LANES, 2 * D_MODEL)
    b1 = padw(f_b1.reshape(1, fw), 1, LANES)
    bh = padw(f_bh.reshape(HYENA_INNER, 1, fw), 1, LANES)
    fq = padw(freq.reshape(1, fw), 1, LANES)
    nt = D_MODEL // tc
    const = lambda shape: pl.BlockSpec(shape, lambda j: (0,) * len(shape))
    out = jax.ShapeDtypeStruct((seq, D_MODEL), f32)
    osp = pl.BlockSpec((seq, tc), lambda j: (0, j))
    return pl.pallas_call(
        _filter_kernel,
        grid=(nt,),
        in_specs=[const((seq, LANES)), const((LANES, LANES)), const((1, LANES)),
                  const((HYENA_INNER, LANES, LANES)), const((HYENA_INNER, 1, LANES)), const((1, LANES)),
                  pl.BlockSpec((LANES, tc), lambda j: (0, j)),
                  pl.BlockSpec((LANES, tc), lambda j: (0, j + nt)),
                  pl.BlockSpec((1, tc), lambda j: (0, j))],
        out_specs=[osp, osp],
        out_shape=[out, out],
        scratch_shapes=[pltpu.VMEM((seq, LANES), f32)],
        compiler_params=_params("arbitrary"),
        name="hyena_filter",
    )(jnp.asarray(z), w1, b1, wh, bh, fq, wout, wout, jnp.asarray(deltas))


def _dft_tables(seq):
    n = 2 * seq
    half = seq // 2

    def theta(f, t):
        return (2.0 * math.pi / (2 * n)) * (((2 * f[:, None] + 1) * t[None, :]) % (2 * n)).astype(np.float64)

    f_low = np.arange(half, dtype=np.int64)
    order = np.concatenate([f_low, seq - 1 - f_low])
    th_full = theta(order, np.arange(seq, dtype=np.int64))
    tp = np.arange(half, dtype=np.int64)
    th_e, th_o = theta(f_low, 2 * tp), theta(f_low, 2 * tp + 1)
    fwd = [np.cos(th_e), np.cos(th_o), np.sin(th_e), np.sin(th_o)]
    as_bf16 = lambda a: jnp.asarray(a, bf16)
    return (as_bf16(np.cos(th_full)), as_bf16(np.sin(th_full)),
            [as_bf16(a) for a in fwd], [as_bf16(a.T * (2.0 / n)) for a in fwd])


def _spectrum_kernel(c_ref, s_ref, ks_ref, kd_ref, kre_ref, kb_ref):
    kre_ref[...] = _dot(c_ref[...], ks_ref[...].astype(bf16))
    kb_ref[...] = _dot(s_ref[...], kd_ref[...].astype(bf16))


def filter_spectrum(cmat, smat, ks, kd, tmf=256, tc=256):
    seq = ks.shape[0]
    fsp = pl.BlockSpec((tmf, seq), lambda j, m: (m, 0))
    ksp = pl.BlockSpec((seq, tc), lambda j, m: (0, j))
    osp = pl.BlockSpec((tmf, tc), lambda j, m: (m, j))
    out = jax.ShapeDtypeStruct((seq, D_MODEL), f32)
    return pl.pallas_call(
        _spectrum_kernel,
        grid=(D_MODEL // tc, seq // tmf),
        in_specs=[fsp, fsp, ksp, ksp],
        out_specs=[osp, osp],
        out_shape=[out, out],
        compiler_params=_params("parallel", "parallel"),
        name="filter_spectrum",
    )(cmat, smat, ks, kd)


def _dft_fwd_kernel(ce_ref, co_ref, se_ref, so_ref, ve_ref, vo_ref, kre_ref, kb_ref, krem_ref, kbm_ref,
                    pp_ref, pm_ref, qp_ref, qm_ref):
    ve, vo = ve_ref[...], vo_ref[...]
    ec, oc = _dot(ce_ref[...], ve), _dot(co_ref[...], vo)
    es, os_ = _dot(se_ref[...], ve), _dot(so_ref[...], vo)

    def times_filter(vre, va, kre, kb):
        return vre * kre - va * kb, vre * kb + va * kre

    p, q = times_filter(ec + oc, es + os_, kre_ref[...], kb_ref[...])
    p_m, q_m = times_filter(ec - oc, os_ - es, krem_ref[...], kbm_ref[...])
    pp_ref[...] = (p + p_m).astype(bf16)
    pm_ref[...] = (p - p_m).astype(bf16)
    qp_ref[...] = (q + q_m).astype(bf16)
    qm_ref[...] = (q - q_m).astype(bf16)


def dft_forward(fwd_tables, vg_even, vg_odd, kre, kb, seq, tmf=512):
    half = seq // 2
    nb = vg_even.shape[0] // half
    tmf = min(tmf, half)
    nm = half // tmf
    fsp = pl.BlockSpec((tmf, half), lambda b, m: (m, 0))
    ksp = pl.BlockSpec((tmf, D_MODEL), lambda b, m: (m, 0))
    kmsp = pl.BlockSpec((tmf, D_MODEL), lambda b, m: (m + nm, 0))
    osp = pl.BlockSpec((tmf, D_MODEL), lambda b, m: (b * nm + m, 0))
    out = jax.ShapeDtypeStruct((nb * half, D_MODEL), bf16)
    return pl.pallas_call(
        _dft_fwd_kernel,
        grid=(nb, nm),
        in_specs=[fsp, fsp, fsp, fsp,
                  pl.BlockSpec((half, D_MODEL), lambda b, m: (b, 0)),
                  pl.BlockSpec((half, D_MODEL), lambda b, m: (b, 0)),
                  ksp, ksp, kmsp, kmsp],
        out_specs=[osp] * 4,
        out_shape=[out] * 4,
        compiler_params=_params("parallel", "parallel"),
        name="dft_forward",
    )(*fwd_tables, vg_even, vg_odd, kre, kb, kre, kb)


def _dft_inv_kernel(gate_row, cet_ref, cot_ref, set_ref, sot_ref, pp_ref, pm_ref, qp_ref, qm_ref,
                    vge_ref, vgo_ref, x0e_ref, x0o_ref, bias_ref, w_ref, h_ref, mod_ref, o_ref, stage_ref):
    gate = mod_ref[0, gate_row:gate_row + 1, :]
    tmt = vge_ref.shape[0]
    nlane = D_MODEL // LANES
    for c in range(nlane):
        stage_ref[c] = h_ref[:, c * LANES:(c + 1) * LANES]

    def finish(y, parity, vg_ref, x0_ref):
        y = y + vg_ref[...].astype(f32) * bias_ref[...]
        upd = gate * _dot((y * x0_ref[...].astype(f32)).astype(bf16), w_ref[...])
        rows = pl.ds(parity, tmt, stride=2)
        for c in range(nlane):
            chunk = stage_ref.at[c]
            chunk[rows, :] = chunk[rows, :] + upd[:, c * LANES:(c + 1) * LANES]

    finish(_dot(cet_ref[...], pp_ref[...]) + _dot(set_ref[...], qm_ref[...]), 0, vge_ref, x0e_ref)
    finish(_dot(cot_ref[...], pm_ref[...]) + _dot(sot_ref[...], qp_ref[...]), 1, vgo_ref, x0o_ref)
    for c in range(nlane):
        o_ref[:, c * LANES:(c + 1) * LANES] = stage_ref[c]


def dft_inverse_out(inv_tables, folded, vg_eo, x0_eo, bias, w_out, h, mods, seq, gate_row, tmt=256):
    half = seq // 2
    nb = h.shape[0] // seq
    tmt = min(tmt, half)
    nm = half // tmt
    gsp = pl.BlockSpec((tmt, half), lambda b, m: (m, 0))
    full = pl.BlockSpec((half, D_MODEL), lambda b, m: (b, 0))
    part = pl.BlockSpec((tmt, D_MODEL), lambda b, m: (b * nm + m, 0))
    tile = pl.BlockSpec((2 * tmt, D_MODEL), lambda b, m: (b * nm + m, 0))
    return pl.pallas_call(
        functools.partial(_dft_inv_kernel, gate_row),
        grid=(nb, nm),
        in_specs=[gsp, gsp, gsp, gsp, full, full, full, full, part, part, part, part,
                  pl.BlockSpec((1, D_MODEL), lambda b, m: (0, 0)),
                  pl.BlockSpec((D_MODEL, D_MODEL), lambda b, m: (0, 0)),
                  tile, pl.BlockSpec((1, N_MOD, D_MODEL), lambda b, m: (b, 0, 0))],
        out_specs=tile,
        out_shape=jax.ShapeDtypeStruct(h.shape, f32),
        scratch_shapes=[pltpu.VMEM((D_MODEL // LANES, 2 * tmt, LANES), f32)],
        compiler_params=_params("parallel", "parallel"),
        name="dft_inverse_out",
    )(*inv_tables, *folded, *vg_eo, *x0_eo, bias.reshape(1, -1).astype(f32), w_out, h, mods)


_Q_HEAD_ORDER = (0, 4, 1, 5, 2, 6, 3, 7)


def _mixer_in_weight(w_in):
    c = np.cumsum([ATTN_DIM, KV_DIM, KV_DIM, D_SSM, XBC_DIM])
    q, k, v, z, xbc, dt = (w_in[:, :c[0]], w_in[:, c[0]:c[1]], w_in[:, c[1]:c[2]], w_in[:, c[2]:c[3]],
                           w_in[:, c[3]:c[4]], w_in[:, c[4]:])
    q = q.reshape(D_MODEL, N_Q_HEADS, HEAD_DIM)[:, np.array(_Q_HEAD_ORDER), :].reshape(D_MODEL, ATTN_DIM)
    w = jnp.concatenate([xbc, z, q, k, v], axis=1).astype(bf16)
    return w, jnp.pad(dt, ((0, 0), (0, LANES - dt.shape[1]))).astype(bf16)


def _mixer_out_weights(w_out):
    wa = w_out[:ATTN_DIM].reshape(N_Q_HEADS, HEAD_DIM, D_MODEL)[np.array(_Q_HEAD_ORDER)].reshape(ATTN_DIM, D_MODEL)
    return wa.astype(bf16), w_out[ATTN_DIM:].astype(bf16)


def kernel(x, c, ctx, c_ctx, w_ada, b_ada, norm_g, ffn_w13, ffn_w2, mix_w_in, mix_w_out, q_norm, k_norm, attn_sink,
           ssd_conv_w, ssd_conv_b, ssd_dt_bias, ssd_a_log, ssd_d, ssd_norm, hy_w_in, hy_conv_w, hy_conv_b,
           hy_f_w1, hy_f_b1, hy_f_wh, hy_f_bh, hy_f_wout, hy_freq, hy_bias, hy_w_out):
    nb, seq, _ = x.shape
    ctx_len = ctx.shape[1]
    depth = w_ada.shape[0]
    assert depth == 2, "this kernel is written for the two-layer block"
    h_lat = x.reshape(nb * seq, D_MODEL)
    h_ctx = ctx.reshape(nb * ctx_len, D_MODEL)
    cond = jnp.concatenate([c, c_ctx[None]], axis=0)
    cond = jnp.pad(cond, ((0, -cond.shape[0] % 8), (0, 0)))
    ffn_w = [[_ffn_weights(ffn_w13[i, k], ffn_w2[i, k]) for k in range(2)] for i in range(depth)]

    mods = ada_mods(cond, w_ada[0], b_ada[0])
    m_lat, m_ctx = mods[:nb], mods[nb:nb + 1]
    h_lat = macaron_ffn(h_lat, m_lat, seq, norm_g[0, 0], 0, *ffn_w[0][0])
    h_ctx = macaron_ffn(h_ctx, m_ctx, nb * ctx_len, norm_g[0, 0], 0, *ffn_w[0][0])
    w_in, w_dt = _mixer_in_weight(mix_w_in[0])
    wa, ws = _mixer_out_weights(mix_w_out[0])
    xc_lat, p_lat, dt_lat = mixer_proj(h_lat, m_lat, seq, seq, norm_g[0, 1], w_in, w_dt, ssd_conv_w[0], ssd_conv_b[0])
    xc_ctx, p_ctx, dt_ctx = mixer_proj(h_ctx, m_ctx, nb * ctx_len, ctx_len, norm_g[0, 1], w_in, w_dt,
                                       ssd_conv_w[0], ssd_conv_b[0])
    cos, sin = _rope_tables(seq)
    a_lat = window_attention(p_lat, p_ctx, seq, ctx_len, q_norm[0], k_norm[0], attn_sink[0], cos, sin)
    hf0, hb0 = ssd_ctx_states(xc_ctx, dt_ctx, ctx_len, ssd_dt_bias[0], ssd_a_log[0])
    s_lat = ssd_latent(xc_lat, p_lat, dt_lat, seq, hf0, hb0, ssd_dt_bias[0], ssd_a_log[0], ssd_d[0], ssd_norm[0])
    h_lat = macaron_ffn(h_lat, m_lat, seq, norm_g[0, 2], 2, *ffn_w[0][1], mix=((a_lat, wa), (s_lat, ws)))

    m_lat = ada_mods(cond, w_ada[1], b_ada[1])[:nb]
    h_lat = macaron_ffn(h_lat, m_lat, seq, norm_g[1, 0], 0, *ffn_w[1][0])
    x0e, x0o, vge, vgo = hyena_proj(h_lat, m_lat, seq, seq, norm_g[1, 1], hy_w_in[0].astype(bf16),
                                    hy_conv_w[0], hy_conv_b[0])
    ks, kd = hyena_filter_taps(seq, hy_f_w1[0], hy_f_b1[0], hy_f_wh[0], hy_f_bh[0], hy_f_wout[0], hy_freq[0])
    c_full, s_full, fwd_tables, inv_tables = _dft_tables(seq)
    kre, kb = filter_spectrum(c_full, s_full, ks, kd)
    folded = dft_forward(fwd_tables, vge, vgo, kre, kb, seq)
    h_lat = dft_inverse_out(inv_tables, folded, (vge, vgo), (x0e, x0o), hy_bias[0], hy_w_out[0].astype(bf16),
                            h_lat, m_lat, seq, 5)
    h_lat = macaron_ffn(h_lat, m_lat, seq, norm_g[1, 2], 2, *ffn_w[1][1])
    return h_lat.reshape(nb, seq, D_MODEL)
```

```python
import functools
import math

import numpy as np
import jax
import jax.numpy as jnp
from jax import lax
from jax.experimental import pallas as pl
from jax.experimental.pallas import tpu as pltpu

f32 = jnp.float32
bf16 = jnp.bfloat16

D_MODEL = 1024
N_MOD = 9
RMS_EPS = 1e-6
GRID_W = 64

HEAD_DIM = 64
N_Q_HEADS = 8
N_KV_HEADS = 2
ATTN_DIM = N_Q_HEADS * HEAD_DIM
KV_DIM = N_KV_HEADS * HEAD_DIM
WINDOW = 128
ATTN_BLOCK = 128
ROPE_THETA = 10000.0

SSD_HEADS = 16
SSD_HEAD_DIM = 64
D_SSM = SSD_HEADS * SSD_HEAD_DIM
SSD_GROUPS = 2
SSD_STATE = 128
SSD_CONV = 7
SSD_CHUNK = 128
BC_DIM = SSD_GROUPS * SSD_STATE
XBC_DIM = D_SSM + 4 * BC_DIM
GROUP_W = D_SSM // SSD_GROUPS

HYENA_SHORT = 3
HYENA_BANDS = 8
HYENA_FILTER_WIDTH = 64
HYENA_INNER = 2
HYENA_FAST_DECAY = 0.3
HYENA_SLOW_DECAY = 1.5
HYENA_TARGET = 1e-2

D_FF = 2816
FFN_TF = 256
LANES = 128

COL_Z = 0
COL_Q = COL_Z + D_SSM
COL_K = COL_Q + ATTN_DIM
COL_V = COL_K + KV_DIM
REST_COLS = COL_V + KV_DIM
HALO = 16
PROJ_TN = 256
PROJ_TM = 256
CONV_ROWS = 64

VMEM_LIMIT = 56 * 1024 * 1024


def _params(*sem):
    return pltpu.CompilerParams(dimension_semantics=sem, vmem_limit_bytes=VMEM_LIMIT)


def _dot(a, b):
    return jnp.dot(a, b, preferred_element_type=f32)


def _dot_nt(a, b):
    return lax.dot_general(a, b, (((1,), (1,)), ((), ())), preferred_element_type=f32)


def _split2(x):
    hi = x.astype(bf16)
    lo = (x - hi.astype(f32)).astype(bf16)
    return hi, lo


def _split3(x):
    hi = x.astype(bf16)
    r = x - hi.astype(f32)
    mid = r.astype(bf16)
    lo = (r - mid.astype(f32)).astype(bf16)
    return hi, mid, lo


def _adaln(h, g, shift, scale):
    ms = jnp.mean(h * h, axis=-1, keepdims=True)
    return (h * lax.rsqrt(ms + RMS_EPS) * g) * (1.0 + scale) + shift


def _silu(x):
    return x * jax.nn.sigmoid(x)


def _mods_kernel(c_ref, w_ref, b_ref, o_ref):
    o_ref[...] = _dot(_silu(c_ref[...]).astype(bf16), w_ref[...].astype(bf16)) + b_ref[...]


def ada_mods(cond, w, b):
    r = cond.shape[0]
    tn = 1024
    out = pl.pallas_call(
        _mods_kernel,
        grid=(w.shape[1] // tn,),
        in_specs=[pl.BlockSpec((r, D_MODEL), lambda j: (0, 0)),
                  pl.BlockSpec((D_MODEL, tn), lambda j: (0, j)),
                  pl.BlockSpec((1, tn), lambda j: (0, j))],
        out_specs=pl.BlockSpec((r, tn), lambda j: (0, j)),
        out_shape=jax.ShapeDtypeStruct((r, w.shape[1]), f32),
        compiler_params=_params("parallel"),
        name="ada_mods",
    )(cond, w, b.reshape(1, -1))
    return out.reshape(r, N_MOD, D_MODEL)


def _ffn_kernel(s, nf, n_mix, h_ref, mod_ref, g_ref, w13_ref, w2_ref, *rest):
    o_ref = rest[-1]
    h = h_ref[...]
    if n_mix:
        mixed = _dot(rest[0][...], rest[n_mix][...])
        for x_ref, w_ref in zip(rest[1:n_mix], rest[n_mix + 1:2 * n_mix]):
            mixed += _dot(x_ref[...], w_ref[...])
        h = h + mod_ref[0, 3 * s - 1:3 * s, :] * mixed
    u = _adaln(h, g_ref[...], mod_ref[0, 3 * s:3 * s + 1, :], mod_ref[0, 3 * s + 1:3 * s + 2, :]).astype(bf16)
    acc = None
    tf = w2_ref.shape[1]
    for j in range(nf):
        a = _dot(u, w13_ref[:, j * tf:(j + 1) * tf])
        b = _dot(u, w13_ref[:, D_FF + j * tf:D_FF + (j + 1) * tf])
        part = _dot((_silu(a) * b).astype(bf16), w2_ref[j])
        acc = part if acc is None else acc + part
    o_ref[...] = h + 0.5 * mod_ref[0, 3 * s + 2:3 * s + 3, :] * acc


def _ffn_weights(w13, w2, tf=FFN_TF):
    return w13.astype(bf16), w2.astype(bf16).reshape(D_FF // tf, tf, D_MODEL)


def macaron_ffn(h, mods, rows_per_mod, g, s, w13c, w2c, mix=(), tm=512):
    m = h.shape[0]
    tm = min(tm, rows_per_mod)
    nf = w2c.shape[0]
    tiles_per_mod = rows_per_mod // tm
    assert not mix or s == 2, "the mixer's gate is the modulation row just before the second FFN's"
    xs = [x for x, _ in mix]
    ws = [w for _, w in mix]
    return pl.pallas_call(
        functools.partial(_ffn_kernel, s, nf, len(mix)),
        grid=(m // tm,),
        in_specs=[pl.BlockSpec((tm, D_MODEL), lambda i: (i, 0)),
                  pl.BlockSpec((1, N_MOD, D_MODEL), lambda i: (i // tiles_per_mod, 0, 0)),
                  pl.BlockSpec((1, D_MODEL), lambda i: (0, 0)),
                  pl.BlockSpec(w13c.shape, lambda i: (0, 0)),
                  pl.BlockSpec(w2c.shape, lambda i: (0, 0, 0))]
        + [pl.BlockSpec((tm, x.shape[1]), lambda i: (i, 0)) for x in xs]
        + [pl.BlockSpec(w.shape, lambda i: (0, 0)) for w in ws],
        out_specs=pl.BlockSpec((tm, D_MODEL), lambda i: (i, 0)),
        out_shape=jax.ShapeDtypeStruct((m, D_MODEL), f32),
        compiler_params=_params("parallel"),
        name="macaron_ffn",
    )(h, mods, g.reshape(1, -1), w13c, w2c, *xs, *ws)


def _tile_setup(s, tiles_per_seq, h_ref, hp_ref, hn_ref, mod_ref, g_ref, perm_ref, u_ref, uh_ref, up_ref):
    shift, scale = mod_ref[0, 3 * s:3 * s + 1, :], mod_ref[0, 3 * s + 1:3 * s + 2, :]
    gain = g_ref[...]
    u_ref[...] = _adaln(h_ref[...], gain, shift, scale).astype(bf16)
    uh_ref[...] = _adaln(jnp.concatenate([hp_ref[...], hn_ref[...]], axis=0), gain, shift, scale).astype(bf16)
    up_ref[...] = _dot(perm_ref[...], u_ref[...]).astype(bf16)
    t = pl.program_id(0) % tiles_per_seq
    keep_prev = jnp.where(t > 0, 1.0, 0.0)
    keep_next = jnp.where(t < tiles_per_seq - 1, 1.0, 0.0)
    return jnp.where(lax.broadcasted_iota(jnp.int32, (2 * HALO, 1), 0) < HALO, keep_prev, keep_next)


def _project_permuted(pad, up_ref, uh_ref, keep, w, half):
    tm = up_ref.shape[0]
    grp = tm // 8
    ext = 8 * half
    sub = lax.broadcasted_iota(jnp.int32, (8, w.shape[1]), 0)

    def tile_rows(r0):
        pad[ext + r0:ext + r0 + PROJ_TM, :] = _dot(up_ref[r0:r0 + PROJ_TM, :], w)

    def boundary_tiles():
        halo = _dot(uh_ref[...], w) * keep
        for j in range(half):
            src = ext + 8 * (grp - half + j)
            pad[8 * j:8 * j + 8, :] = jnp.where(sub == 0, halo[HALO - half + j:HALO - half + j + 1, :],
                                                pltpu.roll(pad[src:src + 8, :], 1, 0))
            src = ext + 8 * j
            dst = ext + tm + 8 * j
            pad[dst:dst + 8, :] = jnp.where(sub == 7, halo[HALO + j:HALO + j + 1, :],
                                            pltpu.roll(pad[src:src + 8, :], 7, 0))

    return [functools.partial(tile_rows, r0) for r0 in range(0, tm, PROJ_TM)] + [boundary_tiles]


def _conv_permuted(pad, cw_ref, cb_ref, col0, width, tm, emit, between):
    rows = CONV_ROWS
    steps = [(l0, r0) for l0 in range(0, pad.shape[1], LANES) for r0 in range(0, tm, rows)]
    every = max(1, len(steps) // max(1, len(between)))
    pending = list(between)
    for n, (l0, r0) in enumerate(steps):
        if pending and n % every == 0:
            pending.pop(0)()
        wcol = slice(col0 + l0, col0 + l0 + LANES)
        acc = cb_ref[:, wcol]
        for k in range(width):
            acc = acc + cw_ref[k:k + 1, wcol] * pad[8 * k + r0:8 * k + r0 + rows, l0:l0 + LANES]
        emit(r0, rows, l0, acc)
    for thunk in pending:
        thunk()


def _token_order(stage, r0, rows, grp):
    per = grp // 8
    tiles = [stage[pl.ds(64 * (j % per) + j // per, 8, stride=8), :] for j in range(r0 // 8, (r0 + rows) // 8)]
    return jnp.concatenate(tiles, axis=0)


def _mixer_proj_kernel(s, tiles_per_seq, h_ref, hp_ref, hn_ref, mod_ref, g_ref, w_ref, wdt_ref, cw_ref, cb_ref,
                       perm_ref, xc_ref, rest_ref, dt_ref, pad_ref, u_ref, uh_ref, up_ref, ystage_ref):
    keep = _tile_setup(s, tiles_per_seq, h_ref, hp_ref, hn_ref, mod_ref, g_ref, perm_ref, u_ref, uh_ref, up_ref)
    tm = h_ref.shape[0]
    tn = PROJ_TN
    n_conv = XBC_DIM // tn
    n_rest = REST_COLS // tn

    def project(c):
        return _project_permuted(pad_ref.at[c % 2], up_ref, uh_ref, keep, w_ref[:, c * tn:(c + 1) * tn], SSD_CONV // 2)

    def conv(c, between):
        def emit(r0, rows, l0, acc):
            ystage_ref[l0 // LANES, r0:r0 + rows, :] = _silu(acc)

        _conv_permuted(pad_ref.at[c % 2], cw_ref, cb_ref, c * tn, SSD_CONV, tm, emit, between)
        for l0 in range(0, tn, LANES):
            for r0 in range(0, tm, CONV_ROWS):
                value = _token_order(ystage_ref.at[l0 // LANES], r0, CONV_ROWS, tm // 8)
                xc_ref[r0:r0 + CONV_ROWS, c * tn + l0:c * tn + l0 + LANES] = value.astype(bf16)

    def plain_rows(c, r0):
        rows = slice(r0, r0 + PROJ_TM)
        if c < n_rest:
            w = w_ref[:, XBC_DIM + c * tn:XBC_DIM + (c + 1) * tn]
            rest_ref[rows, c * tn:(c + 1) * tn] = _dot(u_ref[rows, :], w).astype(bf16)
        elif c == n_rest:
            dt_ref[rows, :] = _dot(u_ref[rows, :], wdt_ref[...])

    def plain(c):
        return [functools.partial(plain_rows, c, r0) for r0 in range(0, tm, PROJ_TM)]

    for thunk in project(0):
        thunk()
    for c in range(n_conv):
        conv(c, (project(c + 1) if c + 1 < n_conv else []) + plain(c))
    for c in range(n_conv, n_rest + 1):
        for thunk in plain(c):
            thunk()


def _tile_adaln(s, tiles_per_seq, h_ref, hp_ref, hn_ref, mod_ref, g_ref, u_ref, uh_ref):
    shift, scale = mod_ref[0, 3 * s:3 * s + 1, :], mod_ref[0, 3 * s + 1:3 * s + 2, :]
    g = g_ref[...]
    u_ref[...] = _adaln(h_ref[...], g, shift, scale).astype(bf16)
    uh_ref[...] = _adaln(jnp.concatenate([hp_ref[...], hn_ref[...]], axis=0), g, shift, scale).astype(bf16)
    t = pl.program_id(0) % tiles_per_seq
    row = lax.broadcasted_iota(jnp.int32, (2 * HALO, 1), 0)
    keep_prev = jnp.where(t > 0, 1.0, 0.0)
    keep_next = jnp.where(t < tiles_per_seq - 1, 1.0, 0.0)
    return jnp.where(row < HALO, keep_prev, keep_next)


def _project_padded(pad_ref, u_ref, uh_ref, keep, w):
    tm = u_ref.shape[0]

    def halo_rows():
        halo = _dot(uh_ref[...], w) * keep
        pad_ref[0:HALO, :] = halo[0:HALO, :]
        pad_ref[HALO + tm:2 * HALO + tm, :] = halo[HALO:2 * HALO, :]

    def tile_rows(r0):
        pad_ref[HALO + r0:HALO + r0 + PROJ_TM, :] = _dot(u_ref[r0:r0 + PROJ_TM, :], w)

    return [halo_rows] + [functools.partial(tile_rows, r0) for r0 in range(0, tm, PROJ_TM)]


def _conv_padded(pad_ref, conv_w, conv_b, emit, between=()):
    tm = pad_ref.shape[0] - 2 * HALO
    width = conv_w.shape[0]
    rows = CONV_ROWS
    steps = [(l0, r0) for l0 in range(0, pad_ref.shape[1], LANES) for r0 in range(0, tm, rows)]
    every = max(1, len(steps) // max(1, len(between)))
    pending = list(between)
    half = width // 2
    for n, (l0, r0) in enumerate(steps):
        if pending and n % every == 0:
            pending.pop(0)()
        lanes = slice(l0, l0 + LANES)
        win = pad_ref[HALO + r0 - 8:HALO + r0 + rows + 8, lanes]
        acc = conv_b[:, lanes] + conv_w[half:half + 1, lanes] * win[8:8 + rows, :]
        for k in range(width):
            if k != half:
                acc += conv_w[k:k + 1, lanes] * pltpu.roll(win, (half - k) % (rows + 16), 0)[8:8 + rows, :]
        emit(r0, rows, l0, acc)
    for thunk in pending:
        thunk()


def _hyena_proj_kernel(s, tiles_per_seq, h_ref, hp_ref, hn_ref, mod_ref, g_ref, w_ref, cw_ref, cb_ref,
                       x0e_ref, x0o_ref, vge_ref, vgo_ref, pad_ref, u_ref, uh_ref, x1_ref, split_ref):
    keep = _tile_adaln(s, tiles_per_seq, h_ref, hp_ref, hn_ref, mod_ref, g_ref, u_ref, uh_ref)
    tn = PROJ_TN
    cols = [part * D_MODEL + c0 for c0 in range(0, D_MODEL, tn) for part in range(3)]

    def project(i):
        return _project_padded(pad_ref.at[i % 2], u_ref, uh_ref, keep, w_ref[:, cols[i]:cols[i] + tn])

    for thunk in project(0):
        thunk()
    for i, col in enumerate(cols):
        between = project(i + 1) if i + 1 < len(cols) else []
        c0 = col % D_MODEL

        def split_tokens(value, even_ref, odd_ref, r0, rows, lanes):
            split_ref[...] = value
            dst = slice(r0 // 2, (r0 + rows) // 2)
            even_ref[dst, lanes] = split_ref[pl.ds(0, rows // 2, stride=2), :].astype(bf16)
            odd_ref[dst, lanes] = split_ref[pl.ds(1, rows // 2, stride=2), :].astype(bf16)

        def emit_x0(r0, rows, l0, acc, c0=c0):
            split_tokens(acc, x0e_ref, x0o_ref, r0, rows, slice(c0 + l0, c0 + l0 + LANES))

        def emit_x1(r0, rows, l0, acc):
            x1_ref[r0:r0 + rows, l0:l0 + LANES] = acc

        def emit_v(r0, rows, l0, acc, c0=c0):
            vg = acc * x1_ref[r0:r0 + rows, l0:l0 + LANES]
            split_tokens(vg, vge_ref, vgo_ref, r0, rows, slice(c0 + l0, c0 + l0 + LANES))

        emit = (emit_x0, emit_x1, emit_v)[col // D_MODEL]
        _conv_padded(pad_ref.at[i % 2], cw_ref[:, col:col + tn], cb_ref[:, col:col + tn], emit, between)


def _proj_call(body, name, h, mods, rows_per_mod, seq, g, consts, outs, pad_margin, extra_scratch, tm=512):
    m = h.shape[0]
    tm = min(tm, seq)
    tiles_per_seq = seq // tm
    tiles_per_mod = rows_per_mod // tm
    hb = tm // HALO
    in_specs = [pl.BlockSpec((tm, D_MODEL), lambda i: (i, 0)),
                pl.BlockSpec((HALO, D_MODEL), lambda i: (jnp.maximum(i * hb - 1, 0), 0)),
                pl.BlockSpec((HALO, D_MODEL), lambda i: (jnp.minimum((i + 1) * hb, m // HALO - 1), 0)),
                pl.BlockSpec((1, N_MOD, D_MODEL), lambda i: (i // tiles_per_mod, 0, 0)),
                pl.BlockSpec((1, D_MODEL), lambda i: (0, 0))]
    in_specs += [pl.BlockSpec(a.shape, lambda i: (0, 0)) for a in consts]
    return pl.pallas_call(
        functools.partial(body, 1, tiles_per_seq),
        grid=(m // tm,),
        in_specs=in_specs,
        out_specs=[pl.BlockSpec((tm // div, n), lambda i: (i, 0)) for div, n, _ in outs],
        out_shape=[jax.ShapeDtypeStruct((m // div, n), dt) for div, n, dt in outs],
        scratch_shapes=[pltpu.VMEM((2, tm + 2 * pad_margin, PROJ_TN), f32), pltpu.VMEM((tm, D_MODEL), bf16),
                        pltpu.VMEM((2 * HALO, D_MODEL), bf16)] + extra_scratch(tm),
        compiler_params=_params("parallel"),
        name=name,
    )(h, h, h, mods, g.reshape(1, -1), *consts)


def _sublane_major_perm(tm):
    r = np.arange(tm)
    perm = np.zeros((tm, tm), np.float32)
    perm[r, (r % 8) * (tm // 8) + r // 8] = 1.0
    return jnp.asarray(perm, bf16)


def mixer_proj(h, mods, rows_per_mod, seq, g, w, w_dt, conv_w, conv_b, tm=512):
    tm = min(tm, seq)
    consts = [w, w_dt, conv_w.astype(f32), conv_b.reshape(1, -1).astype(f32), _sublane_major_perm(tm)]
    scratch = lambda tm: [pltpu.VMEM((tm, D_MODEL), bf16), pltpu.VMEM((PROJ_TN // LANES, tm, LANES), f32)]
    return _proj_call(_mixer_proj_kernel, "mixer_proj", h, mods, rows_per_mod, seq, g, consts,
                      [(1, XBC_DIM, bf16), (1, REST_COLS, bf16), (1, LANES, f32)], 8 * (SSD_CONV // 2), scratch, tm)


def hyena_proj(h, mods, rows_per_mod, seq, g, w, conv_w, conv_b, tm=512):
    consts = [w, conv_w.astype(f32), conv_b.reshape(1, -1).astype(f32)]
    scratch = lambda tm: [pltpu.VMEM((tm, PROJ_TN), f32), pltpu.VMEM((CONV_ROWS, LANES), f32)]
    return _proj_call(_hyena_proj_kernel, "hyena_proj", h, mods, rows_per_mod, seq, g, consts,
                      [(2, D_MODEL, bf16)] * 4, HALO, scratch, tm)


def _head_norm(x, gain, bd):
    hi, lo = _split2(x * x)
    ms = _dot(hi, bd) + _dot(lo, bd)
    return x * lax.rsqrt(ms + RMS_EPS) * gain


def _rope(x, cos, sin_signed):
    lane = lax.broadcasted_iota(jnp.int32, x.shape, 1)
    partner = jnp.where((lane & 16) != 0, pltpu.roll(x, 16, 1), pltpu.roll(x, LANES - 16, 1))
    return x * cos + partner * sin_signed


def _t_bf16(x):
    return x.astype(f32).T.astype(bf16)


def _attn_kernel(seq, q_ref, k_ref, v_ref, kc_ref, vc_ref, qg_ref, kg_ref, cos_ref, sin_ref, bd_ref,
                 sink_ref, o_ref, qt_s, k_s, vt_s, kc_s, vct_s):
    j = pl.program_id(1)
    nblk = seq // ATTN_BLOCK
    nslab = ATTN_DIM // LANES
    blk = ATTN_BLOCK

    @pl.when(j == 0)
    def _():
        bd = bd_ref[...]
        cos, sin = cos_ref[...], sin_ref[...]
        scale = HEAD_DIM ** -0.5
        for p in range(nslab):
            qn = _head_norm(q_ref[:, p * LANES:(p + 1) * LANES].astype(f32), qg_ref[...], bd)
            qr = _rope(qn, cos, sin) * scale
            for jb in range(nblk):
                qt_s[jb, p * LANES:(p + 1) * LANES, :] = _t_bf16(qr[jb * blk:(jb + 1) * blk, :])
        kn = _head_norm(k_ref[...].astype(f32), kg_ref[...], bd)
        zeros = jnp.zeros((WINDOW, KV_DIM), bf16)
        k_s[0:WINDOW, :] = zeros
        k_s[WINDOW + seq:2 * WINDOW + seq, :] = zeros
        k_s[WINDOW:WINDOW + seq, :] = _rope(kn, cos, sin).astype(bf16)
        vt_s[0] = zeros
        vt_s[nblk + 1] = zeros
        for jb in range(nblk):
            vt_s[jb + 1] = _t_bf16(v_ref[jb * blk:(jb + 1) * blk, :])
        kc_s[...] = _head_norm(kc_ref[...].astype(f32), kg_ref[...], bd).astype(bf16)
        vct_s[...] = _t_bf16(vc_ref[...])

    band = blk + 2 * WINDOW
    start = pl.multiple_of(j * blk, blk)
    kb = k_s[pl.ds(start, band), :]
    kc = kc_s[...]
    vtb = jnp.concatenate([vt_s[j], vt_s[j + 1], vt_s[j + 2]], axis=1)
    vtc = vct_s[...]
    qt = qt_s[j]
    key = lax.broadcasted_iota(jnp.int32, (blk, 2 * blk), 0)
    qry = lax.broadcasted_iota(jnp.int32, (blk, 2 * blk), 1) & (blk - 1)
    ok_lo = (jnp.abs(qry - (key - WINDOW)) <= WINDOW) & (start - WINDOW + key >= 0)
    ok_hi = (jnp.abs(qry - (key + blk)) <= WINDOW) & (start + blk + key < seq)
    dim = lax.broadcasted_iota(jnp.int32, (LANES, blk), 0)
    lane2 = lax.broadcasted_iota(jnp.int32, (1, 2 * blk), 1)
    ones_b = jnp.ones((16, band), bf16)
    ones_c = jnp.ones((16, kc.shape[0]), bf16)
    def scores(p):
        qslab = qt[p * LANES:(p + 1) * LANES, :]
        zero = jnp.zeros_like(qslab)
        rhs = jnp.concatenate([jnp.where(dim < HEAD_DIM, qslab, zero), jnp.where(dim >= HEAD_DIM, qslab, zero)], axis=1)
        return _dot(kb, rhs), _dot(kc, rhs)

    def softmax(p, sb, sc):
        s_lo = jnp.where(ok_lo, sb[0:blk, :], -jnp.inf)
        s_mid = sb[blk:2 * blk, :]
        s_hi = jnp.where(ok_hi, sb[2 * blk:3 * blk, :], -jnp.inf)
        sink = jnp.where(lane2 < blk, sink_ref[p], sink_ref[p + N_Q_HEADS // N_KV_HEADS])
        colmax = lambda s: jnp.max(s, axis=0, keepdims=True)
        mx = jnp.maximum(jnp.maximum(jnp.maximum(colmax(s_lo), colmax(s_mid)), jnp.maximum(colmax(s_hi), colmax(sc))),
                         sink)
        pb = jnp.concatenate([jnp.exp(s_lo - mx).astype(bf16), jnp.exp(s_mid - mx).astype(bf16),
                              jnp.exp(s_hi - mx).astype(bf16)], axis=0)
        return pb, jnp.exp(sc - mx).astype(bf16), jnp.exp(sink - mx)

    def values(p, pb, pc, sink_term):
        den = (_dot(ones_b, pb) + _dot(ones_c, pc))[0:1, :] + sink_term
        ot = (_dot(vtb, pb) + _dot(vtc, pc)) / den
        both = jnp.where(dim < HEAD_DIM, ot[:, 0:blk], ot[:, blk:2 * blk])
        o_ref[:, p * LANES:(p + 1) * LANES] = both.T.astype(bf16)

    s_next = scores(0)
    for p in range(nslab):
        s_cur = s_next
        if p + 1 < nslab:
            s_next = scores(p + 1)
        values(p, *softmax(p, *s_cur))


def window_attention(proj_lat, proj_ctx, seq, ctx_len, q_gain, k_gain, sink, rope_cos, rope_sin):
    nb = proj_lat.shape[0] // seq
    nblk = seq // ATTN_BLOCK
    bd = np.kron(np.eye(LANES // HEAD_DIM), np.ones((HEAD_DIM, HEAD_DIM))) / HEAD_DIM
    gain2 = lambda g: jnp.tile(g, LANES // HEAD_DIM).reshape(1, LANES)
    const = lambda shape: pl.BlockSpec(shape, lambda b, j: (0, 0))
    return pl.pallas_call(
        functools.partial(_attn_kernel, seq),
        grid=(nb, nblk),
        in_specs=[pl.BlockSpec((seq, ATTN_DIM), lambda b, j: (b, COL_Q // ATTN_DIM)),
                  pl.BlockSpec((seq, KV_DIM), lambda b, j: (b, COL_K // KV_DIM)),
                  pl.BlockSpec((seq, KV_DIM), lambda b, j: (b, COL_V // KV_DIM)),
                  pl.BlockSpec((ctx_len, KV_DIM), lambda b, j: (b, COL_K // KV_DIM)),
                  pl.BlockSpec((ctx_len, KV_DIM), lambda b, j: (b, COL_V // KV_DIM)),
                  const((1, LANES)), const((1, LANES)),
                  const((seq, LANES)), const((seq, LANES)), const((LANES, LANES)),
                  pl.BlockSpec(memory_space=pltpu.SMEM)],
        out_specs=pl.BlockSpec((ATTN_BLOCK, ATTN_DIM), lambda b, j: (b * nblk + j, 0)),
        out_shape=jax.ShapeDtypeStruct((nb * seq, ATTN_DIM), bf16),
        scratch_shapes=[pltpu.VMEM((nblk, ATTN_DIM, ATTN_BLOCK), bf16),
                        pltpu.VMEM((seq + 2 * WINDOW, KV_DIM), bf16),
                        pltpu.VMEM((nblk + 2, KV_DIM, ATTN_BLOCK), bf16),
                        pltpu.VMEM((ctx_len, KV_DIM), bf16),
                        pltpu.VMEM((KV_DIM, ctx_len), bf16)],
        compiler_params=_params("parallel", "arbitrary"),
        name="window_attention",
    )(proj_lat, proj_lat, proj_lat, proj_ctx, proj_ctx, gain2(q_gain), gain2(k_gain),
      rope_cos, rope_sin, jnp.asarray(bd, bf16), sink)


def _rope_tables(seq):
    t = np.arange(seq)
    pos = np.stack([t // GRID_W, t % GRID_W], axis=1).astype(np.float32)
    axis_dim = HEAD_DIM // 2
    inv = (ROPE_THETA ** (-np.arange(0, axis_dim, 2, dtype=np.float32) / axis_dim)).astype(np.float32)
    lane = np.arange(LANES)
    d = lane % HEAD_DIM
    which = d // axis_dim
    ang = (pos[:, which] * inv[d % (axis_dim // 2)][None, :]).astype(np.float32)
    sign = np.where((d % axis_dim) < axis_dim // 2, -1.0, 1.0)
    return jnp.asarray(np.cos(ang), f32), jnp.asarray(np.sin(ang) * sign, f32)


def _softplus(x):
    return jnp.maximum(x, 0.0) + jnp.log1p(jnp.exp(-jnp.abs(x)))


def _expand_heads(v, e):
    return _dot(v.astype(bf16), e)


def _ssd_chunk(rev, lane0, want_y, x, bm, cm, dt_raw, dt_bias, a_neg, expand, state_ref, result):
    t = x.shape[0]
    dt = _softplus(dt_raw + dt_bias)
    a = dt * a_neg
    r = lax.broadcasted_iota(jnp.int32, (t, t), 0)
    c = lax.broadcasted_iota(jnp.int32, (t, t), 1)
    keep = (r <= c) if rev else (r >= c)
    tri = jnp.where(keep, 1.0, 0.0).astype(bf16)
    cs = sum(_dot(tri, part) for part in _split3(a))
    last = cs[0:1, :] if rev else cs[t - 1:t, :]
    e = jnp.exp(cs)
    w = dt * jnp.exp(last - cs)
    e_x = _expand_heads(e, expand)
    w_x = _expand_heads(w, expand)
    elast_x = e_x[0:1, :] if rev else e_x[t - 1:t, :]
    yield

    y = None
    if want_y:
        cs_t = cs.T
        dt_t = dt.T
        lane = lax.broadcasted_iota(jnp.int32, (t, LANES), 1)
        cb = [_dot_nt(cm[:, g * SSD_STATE:(g + 1) * SSD_STATE],
                      bm[:, g * SSD_STATE:(g + 1) * SSD_STATE]) for g in range(SSD_GROUPS)]
        yield
        pieces = []
        for p in range(SSD_HEADS // 2):
            xp = x[:, p * LANES:(p + 1) * LANES]
            ms = []
            for q in range(2):
                h = 2 * p + q
                g = h // (SSD_HEADS // SSD_GROUPS)
                seg = cs[:, lane0 + h:lane0 + h + 1] - cs_t[lane0 + h:lane0 + h + 1, :]
                dec = jnp.exp(jnp.where(keep, seg, -jnp.inf))
                ms.append((cb[g] * dec * dt_t[lane0 + h:lane0 + h + 1, :]).astype(bf16))
            zero = jnp.zeros_like(xp)
            xcat = jnp.concatenate([jnp.where(lane < SSD_HEAD_DIM, xp, zero),
                                    jnp.where(lane >= SSD_HEAD_DIM, xp, zero)], axis=0)
            pieces.append(_dot(jnp.concatenate(ms, axis=1), xcat))
            yield
        y = jnp.concatenate(pieces, axis=1)

    inter = []
    for g in range(SSD_GROUPS):
        gs = slice(g * GROUP_W, (g + 1) * GROUP_W)
        ss = slice(g * SSD_STATE, (g + 1) * SSD_STATE)
        h_t = state_ref[g]
        if want_y:
            inter.append(_dot(cm[:, ss], h_t.astype(bf16)) * e_x[:, gs])
        xw = (x[:, gs].astype(f32) * w_x[:, gs]).astype(bf16)
        state_ref[g] = h_t * elast_x[:, gs] + _dot(bm[:, ss].astype(f32).T.astype(bf16), xw)
        yield
    if want_y:
        result.append(y + jnp.concatenate(inter, axis=1))


def _interleave(*stage_generators):
    active = list(stage_generators)
    while active:
        for gen in list(active):
            if next(gen, StopIteration) is StopIteration:
                active.remove(gen)


def _ssd_ctx_kernel(nchunk, x_ref, bc_ref, dt_ref, bias_ref, alog_ref, ef_ref, eb_ref, hf_ref, hb_ref, sf, sb):
    sf[...] = jnp.zeros_like(sf)
    sb[...] = jnp.zeros_like(sb)
    a_neg = -jnp.exp(alog_ref[...])
    bias = bias_ref[...]
    t = SSD_CHUNK
    for ci in range(nchunk):
        rows = slice(ci * t, (ci + 1) * t)
        fwd = _ssd_chunk(False, 0, False, x_ref[rows, :], bc_ref[rows, 0:BC_DIM], None, dt_ref[rows, :],
                         bias, a_neg, ef_ref[...], sf, None)
        rows = slice((nchunk - 1 - ci) * t, (nchunk - ci) * t)
        bwd = _ssd_chunk(True, SSD_HEADS, False, x_ref[rows, :], bc_ref[rows, BC_DIM:2 * BC_DIM], None,
                         dt_ref[rows, :], bias, a_neg, eb_ref[...], sb, None)
        _interleave(fwd, bwd)
    hf_ref[0] = sf[...]
    hb_ref[0] = sb[...]


def _head_expanders():
    ef = np.zeros((LANES, D_SSM), np.float32)
    eb = np.zeros((LANES, D_SSM), np.float32)
    for h in range(SSD_HEADS):
        ef[h, h * SSD_HEAD_DIM:(h + 1) * SSD_HEAD_DIM] = 1.0
        eb[SSD_HEADS + h, h * SSD_HEAD_DIM:(h + 1) * SSD_HEAD_DIM] = 1.0
    return jnp.asarray(ef, bf16), jnp.asarray(eb, bf16)


def _pad_lanes(v):
    v = v.reshape(1, -1).astype(f32)
    return jnp.pad(v, ((0, 0), (0, LANES - v.shape[1])))


def ssd_ctx_states(xc, dt_raw, seq, dt_bias, a_log):
    nb = xc.shape[0] // seq
    ef, eb = _head_expanders()
    const = lambda shape: pl.BlockSpec(shape, lambda b: (0,) * len(shape))
    st = jax.ShapeDtypeStruct((nb, SSD_GROUPS, SSD_STATE, GROUP_W), f32)
    st_spec = pl.BlockSpec((1, SSD_GROUPS, SSD_STATE, GROUP_W), lambda b: (b, 0, 0, 0))
    return pl.pallas_call(
        functools.partial(_ssd_ctx_kernel, seq // SSD_CHUNK),
        grid=(nb,),
        in_specs=[pl.BlockSpec((seq, D_SSM), lambda b: (b, 0)),
                  pl.BlockSpec((seq, 2 * BC_DIM), lambda b: (b, D_SSM // (2 * BC_DIM))),
                  pl.BlockSpec((seq, LANES), lambda b: (b, 0)),
                  const((1, LANES)), const((1, LANES)), const((LANES, D_SSM)), const((LANES, D_SSM))],
        out_specs=[st_spec, st_spec],
        out_shape=[st, st],
        scratch_shapes=[pltpu.VMEM((SSD_GROUPS, SSD_STATE, GROUP_W), f32),
                        pltpu.VMEM((SSD_GROUPS, SSD_STATE, GROUP_W), f32)],
        compiler_params=_params("parallel"),
        name="ssd_ctx_states",
    )(xc, xc, dt_raw, _pad_lanes(dt_bias), _pad_lanes(a_log), ef, eb)


def _ssd_lat_kernel(nchunk, xf_ref, xb_ref, bf_ref, bb_ref, cf_ref, cb_ref, dtf_ref, dtb_ref, zf_ref, zb_ref,
                    hf0_ref, hb0_ref, bias_ref, alog_ref, dskip_ref, normw_ref, ef_ref, eb_ref,
                    o_ref, sf, sb, yacc):
    c = pl.program_id(1)
    t = SSD_CHUNK

    @pl.when(c == 0)
    def _():
        sf[...] = hf0_ref[0]
        sb[...] = hb0_ref[0]

    a_neg = -jnp.exp(alog_ref[...])
    bias = bias_ref[...]
    xf = xf_ref[...]
    yf, yb = [], []
    _interleave(
        _ssd_chunk(False, 0, True, xf, bf_ref[...], cf_ref[...], dtf_ref[...], bias, a_neg, ef_ref[...], sf, yf),
        _ssd_chunk(True, SSD_HEADS, True, xb_ref[...], bb_ref[...], cb_ref[...], dtb_ref[...], bias, a_neg,
                   eb_ref[...], sb, yb))
    yf = yf[0] + dskip_ref[...] * xf.astype(f32)
    yb = yb[0]
    rows_f = pl.ds(pl.multiple_of(c * t, t), t)
    rows_b = pl.ds(pl.multiple_of((nchunk - 1 - c) * t, t), t)

    @pl.when(c < nchunk // 2)
    def _():
        yacc[rows_f, :] = yf
        yacc[rows_b, :] = yb

    def finish(y, z):
        y = y * _silu(z.astype(f32))
        outs = []
        for g in range(SSD_GROUPS):
            yg = y[:, g * GROUP_W:(g + 1) * GROUP_W]
            outs.append(yg * lax.rsqrt(jnp.mean(yg * yg, axis=-1, keepdims=True) + RMS_EPS))
        return (jnp.concatenate(outs, axis=1) * normw_ref[...]).astype(bf16)

    @pl.when(c >= nchunk // 2)
    def _():
        o_ref[rows_f, :] = finish(yacc[rows_f, :] + yf, zf_ref[...])
        o_ref[rows_b, :] = finish(yacc[rows_b, :] + yb, zb_ref[...])


def ssd_latent(xc, proj, dt_raw, seq, hf0, hb0, dt_bias, a_log, d_skip, norm_w):
    nb = xc.shape[0] // seq
    nc = seq // SSD_CHUNK
    half = nc // 2
    ef, eb = _head_expanders()
    t = SSD_CHUNK
    fwd = lambda b, c: b * nc + c
    bwd = lambda b, c: b * nc + nc - 1 - c
    zfw = lambda b, c: b * nc + jnp.maximum(c, half)
    zbw = lambda b, c: b * nc + jnp.minimum(nc - 1 - c, half - 1)
    bc0 = D_SSM // SSD_STATE // SSD_GROUPS
    const = lambda shape: pl.BlockSpec(shape, lambda b, c: (0,) * len(shape))
    st_spec = pl.BlockSpec((1, SSD_GROUPS, SSD_STATE, GROUP_W), lambda b, c: (b, 0, 0, 0))
    dskip = jnp.repeat(d_skip.astype(f32), SSD_HEAD_DIM).reshape(1, D_SSM)
    return pl.pallas_call(
        functools.partial(_ssd_lat_kernel, nc),
        grid=(nb, nc),
        in_specs=[pl.BlockSpec((t, D_SSM), lambda b, c: (fwd(b, c), 0)),
                  pl.BlockSpec((t, D_SSM), lambda b, c: (bwd(b, c), 0)),
                  pl.BlockSpec((t, BC_DIM), lambda b, c: (fwd(b, c), bc0)),
                  pl.BlockSpec((t, BC_DIM), lambda b, c: (bwd(b, c), bc0 + 1)),
                  pl.BlockSpec((t, BC_DIM), lambda b, c: (fwd(b, c), bc0 + 2)),
                  pl.BlockSpec((t, BC_DIM), lambda b, c: (bwd(b, c), bc0 + 3)),
                  pl.BlockSpec((t, LANES), lambda b, c: (fwd(b, c), 0)),
                  pl.BlockSpec((t, LANES), lambda b, c: (bwd(b, c), 0)),
                  pl.BlockSpec((t, D_SSM), lambda b, c: (zfw(b, c), COL_Z // D_SSM)),
                  pl.BlockSpec((t, D_SSM), lambda b, c: (zbw(b, c), COL_Z // D_SSM)),
                  st_spec, st_spec,
                  const((1, LANES)), const((1, LANES)), const((1, D_SSM)), const((1, D_SSM)),
                  const((LANES, D_SSM)), const((LANES, D_SSM))],
        out_specs=pl.BlockSpec((seq, D_SSM), lambda b, c: (b, 0)),
        out_shape=jax.ShapeDtypeStruct((nb * seq, D_SSM), bf16),
        scratch_shapes=[pltpu.VMEM((SSD_GROUPS, SSD_STATE, GROUP_W), f32),
                        pltpu.VMEM((SSD_GROUPS, SSD_STATE, GROUP_W), f32),
                        pltpu.VMEM((seq, D_SSM), f32)],
        compiler_params=_params("parallel", "arbitrary"),
        name="ssd_latent",
    )(xc, xc, xc, xc, xc, xc, dt_raw, dt_raw, proj, proj, hf0, hb0,
      _pad_lanes(dt_bias), _pad_lanes(a_log), dskip, norm_w.reshape(1, -1).astype(f32), ef, eb)


def _filter_kernel(z_ref, w1_ref, b1_ref, wh_ref, bh_ref, freq_ref, wf_ref, wb_ref, delta_ref, ks_ref, kd_ref, h_ref):
    hp = lambda a, b: jnp.dot(a, b, preferred_element_type=f32, precision=lax.Precision.HIGHEST)
    z = z_ref[...]

    @pl.when(pl.program_id(0) == 0)
    def _():
        freq = freq_ref[...]
        h = jnp.sin(freq * (hp(z, w1_ref[...]) + b1_ref[...]))
        for n in range(HYENA_INNER):
            h = jnp.sin(freq * (hp(h, wh_ref[n]) + bh_ref[n]))
        h_ref[...] = h

    h = h_ref[...]
    window = jnp.exp(-z[:, 0:1] * delta_ref[...])
    hf = hp(h, wf_ref[...]) * window
    hb = hp(h, wb_ref[...]) * window
    row = lax.broadcasted_iota(jnp.int32, hb.shape, 0)
    hb = jnp.where(row == 0, 0.0, hb)
    norm = jnp.sum(jnp.abs(hf), axis=0, keepdims=True) + jnp.sum(jnp.abs(hb), axis=0, keepdims=True)
    ks_ref[...] = (hf + hb) / norm
    kd_ref[...] = (hf - hb) / norm


def hyena_filter_taps(seq, f_w1, f_b1, f_wh, f_bh, f_wout, freq, tc=256):
    fw = HYENA_FILTER_WIDTH
    t = np.arange(seq, dtype=np.float32)
    t_norm = t / np.float32(seq - 1)
    bands = np.linspace(1e-4, HYENA_BANDS - 1, HYENA_BANDS, dtype=np.float32)
    ang = np.float32(2.0 * math.pi / seq) * t[:, None] * bands
    z = np.concatenate([t_norm[:, None], np.cos(ang), -np.sin(ang)], axis=-1).astype(np.float32)
    z = np.pad(z, ((0, 0), (0, LANES - z.shape[1])))
    deltas = np.abs(np.linspace(math.log(HYENA_TARGET) / HYENA_SLOW_DECAY, math.log(HYENA_TARGET) / HYENA_FAST_DECAY,
                                D_MODEL, dtype=np.float32)).reshape(1, -1)
    padw = lambda a, r, c: jnp.pad(a.astype(f32), [(0, 0)] * (a.ndim - 2) + [(0, r - a.shape[-2]), (0, c - a.shape[-1])])
    w1 = padw(f_w1, LANES, LANES)
    wh = padw(f_wh, LANES, LANES)
    wout = padw(f_wout, LANES, 2 * D_MODEL)
    b1 = padw(f_b1.reshape(1, fw), 1, LANES)
    bh = padw(f_bh.reshape(HYENA_INNER, 1, fw), 1, LANES)
    fq = padw(freq.reshape(1, fw), 1, LANES)
    nt = D_MODEL // tc
    const = lambda shape: pl.BlockSpec(shape, lambda j: (0,) * len(shape))
    out = jax.ShapeDtypeStruct((seq, D_MODEL), f32)
    osp = pl.BlockSpec((seq, tc), lambda j: (0, j))
    return pl.pallas_call(
        _filter_kernel,
        grid=(nt,),
        in_specs=[const((seq, LANES)), const((LANES, LANES)), const((1, LANES)),
                  const((HYENA_INNER, LANES, LANES)), const((HYENA_INNER, 1, LANES)), const((1, LANES)),
                  pl.BlockSpec((LANES, tc), lambda j: (0, j)),
                  pl.BlockSpec((LANES, tc), lambda j: (0, j + nt)),
                  pl.BlockSpec((1, tc), lambda j: (0, j))],
        out_specs=[osp, osp],
        out_shape=[out, out],
        scratch_shapes=[pltpu.VMEM((seq, LANES), f32)],
        compiler_params=_params("arbitrary"),
        name="hyena_filter",
    )(jnp.asarray(z), w1, b1, wh, bh, fq, wout, wout, jnp.asarray(deltas))


def _dft_tables(seq):
    n = 2 * seq
    half = seq // 2

    def theta(f, t):
        return (2.0 * math.pi / (2 * n)) * (((2 * f[:, None] + 1) * t[None, :]) % (2 * n)).astype(np.float64)

    f_low = np.arange(half, dtype=np.int64)
    order = np.concatenate([f_low, seq - 1 - f_low])
    th_full = theta(order, np.arange(seq, dtype=np.int64))
    tp = np.arange(half, dtype=np.int64)
    th_e, th_o = theta(f_low, 2 * tp), theta(f_low, 2 * tp + 1)
    fwd = [np.cos(th_e), np.cos(th_o), np.sin(th_e), np.sin(th_o)]
    as_bf16 = lambda a: jnp.asarray(a, bf16)
    return (as_bf16(np.cos(th_full)), as_bf16(np.sin(th_full)),
            [as_bf16(a) for a in fwd], [as_bf16(a.T * (2.0 / n)) for a in fwd])


def _spectrum_kernel(c_ref, s_ref, ks_ref, kd_ref, kre_ref, kb_ref):
    kre_ref[...] = _dot(c_ref[...], ks_ref[...].astype(bf16))
    kb_ref[...] = _dot(s_ref[...], kd_ref[...].astype(bf16))


def filter_spectrum(cmat, smat, ks, kd, tmf=256, tc=256):
    seq = ks.shape[0]
    fsp = pl.BlockSpec((tmf, seq), lambda j, m: (m, 0))
    ksp = pl.BlockSpec((seq, tc), lambda j, m: (0, j))
    osp = pl.BlockSpec((tmf, tc), lambda j, m: (m, j))
    out = jax.ShapeDtypeStruct((seq, D_MODEL), f32)
    return pl.pallas_call(
        _spectrum_kernel,
        grid=(D_MODEL // tc, seq // tmf),
        in_specs=[fsp, fsp, ksp, ksp],
        out_specs=[osp, osp],
        out_shape=[out, out],
        compiler_params=_params("parallel", "parallel"),
        name="filter_spectrum",
    )(cmat, smat, ks, kd)


def _dft_fwd_kernel(ce_ref, co_ref, se_ref, so_ref, ve_ref, vo_ref, kre_ref, kb_ref,
                    pp_ref, pm_ref, qp_ref, qm_ref):
    tmf = pp_ref.shape[0]
    half = ce_ref.shape[0]
    low = pl.ds(pl.multiple_of(pl.program_id(1) * tmf, tmf), tmf)
    mirrored = pl.ds(pl.multiple_of(half + pl.program_id(1) * tmf, tmf), tmf)
    ve, vo = ve_ref[...], vo_ref[...]
    ec, oc = _dot(ce_ref[low, :], ve), _dot(co_ref[low, :], vo)
    es, os_ = _dot(se_ref[low, :], ve), _dot(so_ref[low, :], vo)

    def times_filter(vre, va, kre, kb):
        return vre * kre - va * kb, vre * kb + va * kre

    p, q = times_filter(ec + oc, es + os_, kre_ref[low, :], kb_ref[low, :])
    p_m, q_m = times_filter(ec - oc, os_ - es, kre_ref[mirrored, :], kb_ref[mirrored, :])
    pp_ref[...] = (p + p_m).astype(bf16)
    pm_ref[...] = (p - p_m).astype(bf16)
    qp_ref[...] = (q + q_m).astype(bf16)
    qm_ref[...] = (q - q_m).astype(bf16)


def dft_forward(fwd_tables, vg_even, vg_odd, kre, kb, seq, tmf=256):
    half = seq // 2
    nb = vg_even.shape[0] // half
    tmf = min(tmf, half)
    nm = half // tmf
    whole = lambda a: pl.BlockSpec(a.shape, lambda b, m: (0, 0))
    osp = pl.BlockSpec((tmf, D_MODEL), lambda b, m: (b * nm + m, 0))
    out = jax.ShapeDtypeStruct((nb * half, D_MODEL), bf16)
    return pl.pallas_call(
        _dft_fwd_kernel,
        grid=(nb, nm),
        in_specs=[whole(t) for t in fwd_tables]
        + [pl.BlockSpec((half, D_MODEL), lambda b, m: (b, 0)),
           pl.BlockSpec((half, D_MODEL), lambda b, m: (b, 0)),
           whole(kre), whole(kb)],
        out_specs=[osp] * 4,
        out_shape=[out] * 4,
        compiler_params=_params("parallel", "parallel"),
        name="dft_forward",
    )(*fwd_tables, vg_even, vg_odd, kre, kb)


def _dft_inv_kernel(gate_row, cet_ref, cot_ref, set_ref, sot_ref, pp_ref, pm_ref, qp_ref, qm_ref,
                    vge_ref, vgo_ref, x0e_ref, x0o_ref, bias_ref, w_ref, h_ref, mod_ref, o_ref, stage_ref):
    gate = mod_ref[0, gate_row:gate_row + 1, :]
    tmt = vge_ref.shape[0]
    nlane = D_MODEL // LANES
    for c in range(nlane):
        stage_ref[c] = h_ref[:, c * LANES:(c + 1) * LANES]

    def finish(y, parity, vg_ref, x0_ref):
        y = y + vg_ref[...].astype(f32) * bias_ref[...]
        upd = gate * _dot((y * x0_ref[...].astype(f32)).astype(bf16), w_ref[...])
        rows = pl.ds(parity, tmt, stride=2)
        for c in range(nlane):
            chunk = stage_ref.at[c]
            chunk[rows, :] = chunk[rows, :] + upd[:, c * LANES:(c + 1) * LANES]

    mine = pl.ds(pl.multiple_of(pl.program_id(1) * tmt, tmt), tmt)
    finish(_dot(cet_ref[mine, :], pp_ref[...]) + _dot(set_ref[mine, :], qm_ref[...]), 0, vge_ref, x0e_ref)
    finish(_dot(cot_ref[mine, :], pm_ref[...]) + _dot(sot_ref[mine, :], qp_ref[...]), 1, vgo_ref, x0o_ref)
    for c in range(nlane):
        o_ref[:, c * LANES:(c + 1) * LANES] = stage_ref[c]


def dft_inverse_out(inv_tables, folded, vg_eo, x0_eo, bias, w_out, h, mods, seq, gate_row, tmt=256):
    half = seq // 2
    nb = h.shape[0] // seq
    tmt = min(tmt, half)
    nm = half // tmt
    gsp = pl.BlockSpec((half, half), lambda b, m: (0, 0))
    full = pl.BlockSpec((half, D_MODEL), lambda b, m: (b, 0))
    part = pl.BlockSpec((tmt, D_MODEL), lambda b, m: (b * nm + m, 0))
    tile = pl.BlockSpec((2 * tmt, D_MODEL), lambda b, m: (b * nm + m, 0))
    return pl.pallas_call(
        functools.partial(_dft_inv_kernel, gate_row),
        grid=(nb, nm),
        in_specs=[gsp, gsp, gsp, gsp, full, full, full, full, part, part, part, part,
                  pl.BlockSpec((1, D_MODEL), lambda b, m: (0, 0)),
                  pl.BlockSpec((D_MODEL, D_MODEL), lambda b, m: (0, 0)),
                  tile, pl.BlockSpec((1, N_MOD, D_MODEL), lambda b, m: (b, 0, 0))],
        out_specs=tile,
        out_shape=jax.ShapeDtypeStruct(h.shape, f32),
        scratch_shapes=[pltpu.VMEM((D_MODEL // LANES, 2 * tmt, LANES), f32)],
        compiler_params=_params("parallel", "parallel"),
        name="dft_inverse_out",
    )(*inv_tables, *folded, *vg_eo, *x0_eo, bias.reshape(1, -1).astype(f32), w_out, h, mods)


_Q_HEAD_ORDER = (0, 4, 1, 5, 2, 6, 3, 7)


def _mixer_in_weight(w_in):
    c = np.cumsum([ATTN_DIM, KV_DIM, KV_DIM, D_SSM, XBC_DIM])
    q, k, v, z, xbc, dt = (w_in[:, :c[0]], w_in[:, c[0]:c[1]], w_in[:, c[1]:c[2]], w_in[:, c[2]:c[3]],
                           w_in[:, c[3]:c[4]], w_in[:, c[4]:])
    q = q.reshape(D_MODEL, N_Q_HEADS, HEAD_DIM)[:, np.array(_Q_HEAD_ORDER), :].reshape(D_MODEL, ATTN_DIM)
    w = jnp.concatenate([xbc, z, q, k, v], axis=1).astype(bf16)
    return w, jnp.pad(dt, ((0, 0), (0, LANES - dt.shape[1]))).astype(bf16)


def _mixer_out_weights(w_out):
    wa = w_out[:ATTN_DIM].reshape(N_Q_HEADS, HEAD_DIM, D_MODEL)[np.array(_Q_HEAD_ORDER)].reshape(ATTN_DIM, D_MODEL)
    return wa.astype(bf16), w_out[ATTN_DIM:].astype(bf16)


def kernel(x, c, ctx, c_ctx, w_ada, b_ada, norm_g, ffn_w13, ffn_w2, mix_w_in, mix_w_out, q_norm, k_norm, attn_sink,
           ssd_conv_w, ssd_conv_b, ssd_dt_bias, ssd_a_log, ssd_d, ssd_norm, hy_w_in, hy_conv_w, hy_conv_b,
           hy_f_w1, hy_f_b1, hy_f_wh, hy_f_bh, hy_f_wout, hy_freq, hy_bias, hy_w_out):
    nb, seq, _ = x.shape
    ctx_len = ctx.shape[1]
    depth = w_ada.shape[0]
    assert depth == 2, "this kernel is written for the two-layer block"
    h_lat = x.reshape(nb * seq, D_MODEL)
    h_ctx = ctx.reshape(nb * ctx_len, D_MODEL)
    cond = jnp.concatenate([c, c_ctx[None]], axis=0)
    cond = jnp.pad(cond, ((0, -cond.shape[0] % 8), (0, 0)))
    ffn_w = [[_ffn_weights(ffn_w13[i, k], ffn_w2[i, k]) for k in range(2)] for i in range(depth)]

    mods = ada_mods(cond, w_ada[0], b_ada[0])
    m_lat, m_ctx = mods[:nb], mods[nb:nb + 1]
    h_lat = macaron_ffn(h_lat, m_lat, seq, norm_g[0, 0], 0, *ffn_w[0][0])
    h_ctx = macaron_ffn(h_ctx, m_ctx, nb * ctx_len, norm_g[0, 0], 0, *ffn_w[0][0])
    w_in, w_dt = _mixer_in_weight(mix_w_in[0])
    wa, ws = _mixer_out_weights(mix_w_out[0])
    xc_lat, p_lat, dt_lat = mixer_proj(h_lat, m_lat, seq, seq, norm_g[0, 1], w_in, w_dt, ssd_conv_w[0], ssd_conv_b[0])
    xc_ctx, p_ctx, dt_ctx = mixer_proj(h_ctx, m_ctx, nb * ctx_len, ctx_len, norm_g[0, 1], w_in, w_dt,
                                       ssd_conv_w[0], ssd_conv_b[0])
    cos, sin = _rope_tables(seq)
    a_lat = window_attention(p_lat, p_ctx, seq, ctx_len, q_norm[0], k_norm[0], attn_sink[0], cos, sin)
    hf0, hb0 = ssd_ctx_states(xc_ctx, dt_ctx, ctx_len, ssd_dt_bias[0], ssd_a_log[0])
    s_lat = ssd_latent(xc_lat, p_lat, dt_lat, seq, hf0, hb0, ssd_dt_bias[0], ssd_a_log[0], ssd_d[0], ssd_norm[0])
    h_lat = macaron_ffn(h_lat, m_lat, seq, norm_g[0, 2], 2, *ffn_w[0][1], mix=((a_lat, wa), (s_lat, ws)))

    m_lat = ada_mods(cond, w_ada[1], b_ada[1])[:nb]
    h_lat = macaron_ffn(h_lat, m_lat, seq, norm_g[1, 0], 0, *ffn_w[1][0])
    x0e, x0o, vge, vgo = hyena_proj(h_lat, m_lat, seq, seq, norm_g[1, 1], hy_w_in[0].astype(bf16),
                                    hy_conv_w[0], hy_conv_b[0])
    ks, kd = hyena_filter_taps(seq, hy_f_w1[0], hy_f_b1[0], hy_f_wh[0], hy_f_bh[0], hy_f_wout[0], hy_freq[0])
    c_full, s_full, fwd_tables, inv_tables = _dft_tables(seq)
    kre, kb = filter_spectrum(c_full, s_full, ks, kd)
    folded = dft_forward(fwd_tables, vge, vgo, kre, kb, seq)
    h_lat = dft_inverse_out(inv_tables, folded, (vge, vgo), (x0e, x0o), hy_bias[0], hy_w_out[0].astype(bf16),
                            h_lat, m_lat, seq, 5)
    h_lat = macaron_ffn(h_lat, m_lat, seq, norm_g[1, 2], 2, *ffn_w[1][1])
    return h_lat.reshape(nb, seq, D_MODEL)
```

```python
import functools
import math

import numpy as np
import jax
import jax.numpy as jnp
from jax import lax
from jax.experimental import pallas as pl
from jax.experimental.pallas import tpu as pltpu

f32 = jnp.float32
bf16 = jnp.bfloat16

D_MODEL = 1024
N_MOD = 9
RMS_EPS = 1e-6
GRID_W = 64

HEAD_DIM = 64
N_Q_HEADS = 8
N_KV_HEADS = 2
ATTN_DIM = N_Q_HEADS * HEAD_DIM
KV_DIM = N_KV_HEADS * HEAD_DIM
WINDOW = 128
ATTN_BLOCK = 128
ROPE_THETA = 10000.0

SSD_HEADS = 16
SSD_HEAD_DIM = 64
D_SSM = SSD_HEADS * SSD_HEAD_DIM
SSD_GROUPS = 2
SSD_STATE = 128
SSD_CONV = 7
SSD_CHUNK = 128
BC_DIM = SSD_GROUPS * SSD_STATE
XBC_DIM = D_SSM + 4 * BC_DIM
GROUP_W = D_SSM // SSD_GROUPS

HYENA_SHORT = 3
HYENA_BANDS = 8
HYENA_FILTER_WIDTH = 64
HYENA_INNER = 2
HYENA_FAST_DECAY = 0.3
HYENA_SLOW_DECAY = 1.5
HYENA_TARGET = 1e-2

D_FF = 2816
FFN_TF = 256
LANES = 128

COL_Z = 0
COL_Q = COL_Z + D_SSM
COL_K = COL_Q + ATTN_DIM
COL_V = COL_K + KV_DIM
REST_COLS = COL_V + KV_DIM
HALO = 16
PROJ_TN = 256
PROJ_TM = 256
CONV_ROWS = 64

VMEM_LIMIT = 56 * 1024 * 1024


def _params(*sem):
    return pltpu.CompilerParams(dimension_semantics=sem, vmem_limit_bytes=VMEM_LIMIT)


def _dot(a, b):
    return jnp.dot(a, b, preferred_element_type=f32)


def _dot_nt(a, b):
    return lax.dot_general(a, b, (((1,), (1,)), ((), ())), preferred_element_type=f32)


def _split2(x):
    hi = x.astype(bf16)
    lo = (x - hi.astype(f32)).astype(bf16)
    return hi, lo


def _split3(x):
    hi = x.astype(bf16)
    r = x - hi.astype(f32)
    mid = r.astype(bf16)
    lo = (r - mid.astype(f32)).astype(bf16)
    return hi, mid, lo


def _adaln(h, g, shift, scale):
    ms = jnp.mean(h * h, axis=-1, keepdims=True)
    return (h * lax.rsqrt(ms + RMS_EPS) * g) * (1.0 + scale) + shift


def _silu(x):
    return x * jax.nn.sigmoid(x)


def _mods_kernel(c_ref, w_ref, b_ref, o_ref):
    o_ref[...] = _dot(_silu(c_ref[...]).astype(bf16), w_ref[...].astype(bf16)) + b_ref[...]


def ada_mods(cond, w, b):
    r = cond.shape[0]
    tn = 1024
    out = pl.pallas_call(
        _mods_kernel,
        grid=(w.shape[1] // tn,),
        in_specs=[pl.BlockSpec((r, D_MODEL), lambda j: (0, 0)),
                  pl.BlockSpec((D_MODEL, tn), lambda j: (0, j)),
                  pl.BlockSpec((1, tn), lambda j: (0, j))],
        out_specs=pl.BlockSpec((r, tn), lambda j: (0, j)),
        out_shape=jax.ShapeDtypeStruct((r, w.shape[1]), f32),
        compiler_params=_params("parallel"),
        name="ada_mods",
    )(cond, w, b.reshape(1, -1))
    return out.reshape(r, N_MOD, D_MODEL)


def _ffn_kernel(s, nf, n_mix, h_ref, mod_ref, g_ref, w13_ref, w2_ref, *rest):
    o_ref = rest[-1]
    h = h_ref[...]
    if n_mix:
        mixed = _dot(rest[0][...], rest[n_mix][...])
        for x_ref, w_ref in zip(rest[1:n_mix], rest[n_mix + 1:2 * n_mix]):
            mixed += _dot(x_ref[...], w_ref[...])
        h = h + mod_ref[0, 3 * s - 1:3 * s, :] * mixed
    u = _adaln(h, g_ref[...], mod_ref[0, 3 * s:3 * s + 1, :], mod_ref[0, 3 * s + 1:3 * s + 2, :]).astype(bf16)
    acc = None
    tf = w2_ref.shape[1]
    for j in range(nf):
        a = _dot(u, w13_ref[:, j * tf:(j + 1) * tf])
        b = _dot(u, w13_ref[:, D_FF + j * tf:D_FF + (j + 1) * tf])
        part = _dot((_silu(a) * b).astype(bf16), w2_ref[j])
        acc = part if acc is None else acc + part
    o_ref[...] = h + 0.5 * mod_ref[0, 3 * s + 2:3 * s + 3, :] * acc


def _ffn_weights(w13, w2, tf=FFN_TF):
    return w13.astype(bf16), w2.astype(bf16).reshape(D_FF // tf, tf, D_MODEL)


def macaron_ffn(h, mods, rows_per_mod, g, s, w13c, w2c, mix=(), tm=1024):
    m = h.shape[0]
    tm = min(tm // 2 if mix else tm, rows_per_mod)
    nf = w2c.shape[0]
    tiles_per_mod = rows_per_mod // tm
    assert not mix or s == 2, "the mixer's gate is the modulation row just before the second FFN's"
    xs = [x for x, _ in mix]
    ws = [w for _, w in mix]
    return pl.pallas_call(
        functools.partial(_ffn_kernel, s, nf, len(mix)),
        grid=(m // tm,),
        in_specs=[pl.BlockSpec((tm, D_MODEL), lambda i: (i, 0)),
                  pl.BlockSpec((1, N_MOD, D_MODEL), lambda i: (i // tiles_per_mod, 0, 0)),
                  pl.BlockSpec((1, D_MODEL), lambda i: (0, 0)),
                  pl.BlockSpec(w13c.shape, lambda i: (0, 0)),
                  pl.BlockSpec(w2c.shape, lambda i: (0, 0, 0))]
        + [pl.BlockSpec((tm, x.shape[1]), lambda i: (i, 0)) for x in xs]
        + [pl.BlockSpec(w.shape, lambda i: (0, 0)) for w in ws],
        out_specs=pl.BlockSpec((tm, D_MODEL), lambda i: (i, 0)),
        out_shape=jax.ShapeDtypeStruct((m, D_MODEL), f32),
        compiler_params=_params("parallel"),
        name="macaron_ffn",
    )(h, mods, g.reshape(1, -1), w13c, w2c, *xs, *ws)


def _tile_setup(s, tiles_per_seq, h_ref, hp_ref, hn_ref, mod_ref, g_ref, perm_ref, u_ref, uh_ref, up_ref):
    shift, scale = mod_ref[0, 3 * s:3 * s + 1, :], mod_ref[0, 3 * s + 1:3 * s + 2, :]
    gain = g_ref[...]
    u_ref[...] = _adaln(h_ref[...], gain, shift, scale).astype(bf16)
    uh_ref[...] = _adaln(jnp.concatenate([hp_ref[...], hn_ref[...]], axis=0), gain, shift, scale).astype(bf16)
    up_ref[...] = _dot(perm_ref[...], u_ref[...]).astype(bf16)
    t = pl.program_id(0) % tiles_per_seq
    keep_prev = jnp.where(t > 0, 1.0, 0.0)
    keep_next = jnp.where(t < tiles_per_seq - 1, 1.0, 0.0)
    return jnp.where(lax.broadcasted_iota(jnp.int32, (2 * HALO, 1), 0) < HALO, keep_prev, keep_next)


def _project_permuted(pad, up_ref, uh_ref, keep, w, half):
    tm = up_ref.shape[0]
    grp = tm // 8
    ext = 8 * half
    sub = lax.broadcasted_iota(jnp.int32, (8, w.shape[1]), 0)

    def tile_rows(r0):
        pad[ext + r0:ext + r0 + PROJ_TM, :] = _dot(up_ref[r0:r0 + PROJ_TM, :], w)

    def boundary_tiles():
        halo = _dot(uh_ref[...], w) * keep
        for j in range(half):
            src = ext + 8 * (grp - half + j)
            pad[8 * j:8 * j + 8, :] = jnp.where(sub == 0, halo[HALO - half + j:HALO - half + j + 1, :],
                                                pltpu.roll(pad[src:src + 8, :], 1, 0))
            src = ext + 8 * j
            dst = ext + tm + 8 * j
            pad[dst:dst + 8, :] = jnp.where(sub == 7, halo[HALO + j:HALO + j + 1, :],
                                            pltpu.roll(pad[src:src + 8, :], 7, 0))

    return [functools.partial(tile_rows, r0) for r0 in range(0, tm, PROJ_TM)] + [boundary_tiles]


def _conv_permuted(pad, cw_ref, cb_ref, col0, width, tm, emit, between):
    rows = CONV_ROWS
    steps = [(l0, r0) for l0 in range(0, pad.shape[1], LANES) for r0 in range(0, tm, rows)]
    every = max(1, len(steps) // max(1, len(between)))
    pending = list(between)
    for n, (l0, r0) in enumerate(steps):
        if pending and n % every == 0:
            pending.pop(0)()
        wcol = slice(col0 + l0, col0 + l0 + LANES)
        acc = cb_ref[:, wcol]
        for k in range(width):
            acc = acc + cw_ref[k:k + 1, wcol] * pad[8 * k + r0:8 * k + r0 + rows, l0:l0 + LANES]
        emit(r0, rows, l0, acc)
    for thunk in pending:
        thunk()


def _token_order(stage, r0, rows, grp):
    per = grp // 8
    tiles = [stage[pl.ds(64 * (j % per) + j // per, 8, stride=8), :] for j in range(r0 // 8, (r0 + rows) // 8)]
    return jnp.concatenate(tiles, axis=0)


def _mixer_proj_kernel(s, tiles_per_seq, h_ref, hp_ref, hn_ref, mod_ref, g_ref, w_ref, wdt_ref, cw_ref, cb_ref,
                       perm_ref, xc_ref, rest_ref, dt_ref, pad_ref, u_ref, uh_ref, up_ref, ystage_ref):
    keep = _tile_setup(s, tiles_per_seq, h_ref, hp_ref, hn_ref, mod_ref, g_ref, perm_ref, u_ref, uh_ref, up_ref)
    tm = h_ref.shape[0]
    tn = PROJ_TN
    n_conv = XBC_DIM // tn
    n_rest = REST_COLS // tn

    def project(c):
        return _project_permuted(pad_ref.at[c % 2], up_ref, uh_ref, keep, w_ref[:, c * tn:(c + 1) * tn], SSD_CONV // 2)

    def conv(c, between):
        def emit(r0, rows, l0, acc):
            ystage_ref[l0 // LANES, r0:r0 + rows, :] = _silu(acc)

        _conv_permuted(pad_ref.at[c % 2], cw_ref, cb_ref, c * tn, SSD_CONV, tm, emit, between)
        for l0 in range(0, tn, LANES):
            for r0 in range(0, tm, CONV_ROWS):
                value = _token_order(ystage_ref.at[l0 // LANES], r0, CONV_ROWS, tm // 8)
                xc_ref[r0:r0 + CONV_ROWS, c * tn + l0:c * tn + l0 + LANES] = value.astype(bf16)

    def plain_rows(c, r0):
        rows = slice(r0, r0 + PROJ_TM)
        if c < n_rest:
            w = w_ref[:, XBC_DIM + c * tn:XBC_DIM + (c + 1) * tn]
            rest_ref[rows, c * tn:(c + 1) * tn] = _dot(u_ref[rows, :], w).astype(bf16)
        elif c == n_rest:
            dt_ref[rows, :] = _dot(u_ref[rows, :], wdt_ref[...])

    def plain(c):
        return [functools.partial(plain_rows, c, r0) for r0 in range(0, tm, PROJ_TM)]

    for thunk in project(0):
        thunk()
    for c in range(n_conv):
        conv(c, (project(c + 1) if c + 1 < n_conv else []) + plain(c))
    for c in range(n_conv, n_rest + 1):
        for thunk in plain(c):
            thunk()


def _tile_adaln(s, tiles_per_seq, h_ref, hp_ref, hn_ref, mod_ref, g_ref, u_ref, uh_ref):
    shift, scale = mod_ref[0, 3 * s:3 * s + 1, :], mod_ref[0, 3 * s + 1:3 * s + 2, :]
    g = g_ref[...]
    u_ref[...] = _adaln(h_ref[...], g, shift, scale).astype(bf16)
    uh_ref[...] = _adaln(jnp.concatenate([hp_ref[...], hn_ref[...]], axis=0), g, shift, scale).astype(bf16)
    t = pl.program_id(0) % tiles_per_seq
    row = lax.broadcasted_iota(jnp.int32, (2 * HALO, 1), 0)
    keep_prev = jnp.where(t > 0, 1.0, 0.0)
    keep_next = jnp.where(t < tiles_per_seq - 1, 1.0, 0.0)
    return jnp.where(row < HALO, keep_prev, keep_next)


def _project_padded(pad_ref, u_ref, uh_ref, keep, w):
    tm = u_ref.shape[0]

    def halo_rows():
        halo = _dot(uh_ref[...], w) * keep
        pad_ref[0:HALO, :] = halo[0:HALO, :]
        pad_ref[HALO + tm:2 * HALO + tm, :] = halo[HALO:2 * HALO, :]

    def tile_rows(r0):
        pad_ref[HALO + r0:HALO + r0 + PROJ_TM, :] = _dot(u_ref[r0:r0 + PROJ_TM, :], w)

    return [halo_rows] + [functools.partial(tile_rows, r0) for r0 in range(0, tm, PROJ_TM)]


def _conv_padded(pad_ref, conv_w, conv_b, emit, between=()):
    tm = pad_ref.shape[0] - 2 * HALO
    width = conv_w.shape[0]
    rows = CONV_ROWS
    steps = [(l0, r0) for l0 in range(0, pad_ref.shape[1], LANES) for r0 in range(0, tm, rows)]
    every = max(1, len(steps) // max(1, len(between)))
    pending = list(between)
    half = width // 2
    for n, (l0, r0) in enumerate(steps):
        if pending and n % every == 0:
            pending.pop(0)()
        lanes = slice(l0, l0 + LANES)
        win = pad_ref[HALO + r0 - 8:HALO + r0 + rows + 8, lanes]
        acc = conv_b[:, lanes] + conv_w[half:half + 1, lanes] * win[8:8 + rows, :]
        for k in range(width):
            if k != half:
                acc += conv_w[k:k + 1, lanes] * pltpu.roll(win, (half - k) % (rows + 16), 0)[8:8 + rows, :]
        emit(r0, rows, l0, acc)
    for thunk in pending:
        thunk()


def _hyena_proj_kernel(s, tiles_per_seq, h_ref, hp_ref, hn_ref, mod_ref, g_ref, w_ref, cw_ref, cb_ref,
                       x0e_ref, x0o_ref, vge_ref, vgo_ref, pad_ref, u_ref, uh_ref, x1_ref, split_ref):
    keep = _tile_adaln(s, tiles_per_seq, h_ref, hp_ref, hn_ref, mod_ref, g_ref, u_ref, uh_ref)
    tn = PROJ_TN
    cols = [part * D_MODEL + c0 for c0 in range(0, D_MODEL, tn) for part in range(3)]

    def project(i):
        return _project_padded(pad_ref.at[i % 2], u_ref, uh_ref, keep, w_ref[:, cols[i]:cols[i] + tn])

    for thunk in project(0):
        thunk()
    for i, col in enumerate(cols):
        between = project(i + 1) if i + 1 < len(cols) else []
        c0 = col % D_MODEL

        def split_tokens(value, even_ref, odd_ref, r0, rows, lanes):
            split_ref[...] = value
            dst = slice(r0 // 2, (r0 + rows) // 2)
            even_ref[dst, lanes] = split_ref[pl.ds(0, rows // 2, stride=2), :].astype(bf16)
            odd_ref[dst, lanes] = split_ref[pl.ds(1, rows // 2, stride=2), :].astype(bf16)

        def emit_x0(r0, rows, l0, acc, c0=c0):
            split_tokens(acc, x0e_ref, x0o_ref, r0, rows, slice(c0 + l0, c0 + l0 + LANES))

        def emit_x1(r0, rows, l0, acc):
            x1_ref[r0:r0 + rows, l0:l0 + LANES] = acc

        def emit_v(r0, rows, l0, acc, c0=c0):
            vg = acc * x1_ref[r0:r0 + rows, l0:l0 + LANES]
            split_tokens(vg, vge_ref, vgo_ref, r0, rows, slice(c0 + l0, c0 + l0 + LANES))

        emit = (emit_x0, emit_x1, emit_v)[col // D_MODEL]
        _conv_padded(pad_ref.at[i % 2], cw_ref[:, col:col + tn], cb_ref[:, col:col + tn], emit, between)


def _proj_call(body, name, h, mods, rows_per_mod, seq, g, consts, outs, pad_margin, extra_scratch, tm=512):
    m = h.shape[0]
    tm = min(tm, seq)
    tiles_per_seq = seq // tm
    tiles_per_mod = rows_per_mod // tm
    hb = tm // HALO
    in_specs = [pl.BlockSpec((tm, D_MODEL), lambda i: (i, 0)),
                pl.BlockSpec((HALO, D_MODEL), lambda i: (jnp.maximum(i * hb - 1, 0), 0)),
                pl.BlockSpec((HALO, D_MODEL), lambda i: (jnp.minimum((i + 1) * hb, m // HALO - 1), 0)),
                pl.BlockSpec((1, N_MOD, D_MODEL), lambda i: (i // tiles_per_mod, 0, 0)),
                pl.BlockSpec((1, D_MODEL), lambda i: (0, 0))]
    in_specs += [pl.BlockSpec(a.shape, lambda i: (0, 0)) for a in consts]
    return pl.pallas_call(
        functools.partial(body, 1, tiles_per_seq),
        grid=(m // tm,),
        in_specs=in_specs,
        out_specs=[pl.BlockSpec((tm // div, n), lambda i: (i, 0)) for div, n, _ in outs],
        out_shape=[jax.ShapeDtypeStruct((m // div, n), dt) for div, n, dt in outs],
        scratch_shapes=[pltpu.VMEM((2, tm + 2 * pad_margin, PROJ_TN), f32), pltpu.VMEM((tm, D_MODEL), bf16),
                        pltpu.VMEM((2 * HALO, D_MODEL), bf16)] + extra_scratch(tm),
        compiler_params=_params("parallel"),
        name=name,
    )(h, h, h, mods, g.reshape(1, -1), *consts)


def _sublane_major_perm(tm):
    r = np.arange(tm)
    perm = np.zeros((tm, tm), np.float32)
    perm[r, (r % 8) * (tm // 8) + r // 8] = 1.0
    return jnp.asarray(perm, bf16)


def mixer_proj(h, mods, rows_per_mod, seq, g, w, w_dt, conv_w, conv_b, tm=512):
    tm = min(tm, seq)
    consts = [w, w_dt, conv_w.astype(f32), conv_b.reshape(1, -1).astype(f32), _sublane_major_perm(tm)]
    scratch = lambda tm: [pltpu.VMEM((tm, D_MODEL), bf16), pltpu.VMEM((PROJ_TN // LANES, tm, LANES), f32)]
    return _proj_call(_mixer_proj_kernel, "mixer_proj", h, mods, rows_per_mod, seq, g, consts,
                      [(1, XBC_DIM, bf16), (1, REST_COLS, bf16), (1, LANES, f32)], 8 * (SSD_CONV // 2), scratch, tm)


def hyena_proj(h, mods, rows_per_mod, seq, g, w, conv_w, conv_b, tm=512):
    consts = [w, conv_w.astype(f32), conv_b.reshape(1, -1).astype(f32)]
    scratch = lambda tm: [pltpu.VMEM((tm, PROJ_TN), f32), pltpu.VMEM((CONV_ROWS, LANES), f32)]
    return _proj_call(_hyena_proj_kernel, "hyena_proj", h, mods, rows_per_mod, seq, g, consts,
                      [(2, D_MODEL, bf16)] * 4, HALO, scratch, tm)


def _head_norm(x, gain, bd):
    hi, lo = _split2(x * x)
    ms = _dot(hi, bd) + _dot(lo, bd)
    return x * lax.rsqrt(ms + RMS_EPS) * gain


def _rope(x, cos, sin_signed):
    lane = lax.broadcasted_iota(jnp.int32, x.shape, 1)
    partner = jnp.where((lane & 16) != 0, pltpu.roll(x, 16, 1), pltpu.roll(x, LANES - 16, 1))
    return x * cos + partner * sin_signed


def _t_bf16(x):
    return x.astype(f32).T.astype(bf16)


def _attn_kernel(seq, q_ref, k_ref, v_ref, kc_ref, vc_ref, qg_ref, kg_ref, cos_ref, sin_ref, bd_ref,
                 sink_ref, o_ref, qt_s, k_s, vt_s, kc_s, vct_s):
    j = pl.program_id(1)
    nblk = seq // ATTN_BLOCK
    nslab = ATTN_DIM // LANES
    blk = ATTN_BLOCK

    @pl.when(j == 0)
    def _():
        bd = bd_ref[...]
        cos, sin = cos_ref[...], sin_ref[...]
        scale = HEAD_DIM ** -0.5
        for p in range(nslab):
            qn = _head_norm(q_ref[:, p * LANES:(p + 1) * LANES].astype(f32), qg_ref[...], bd)
            qr = _rope(qn, cos, sin) * scale
            for jb in range(nblk):
                qt_s[jb, p * LANES:(p + 1) * LANES, :] = _t_bf16(qr[jb * blk:(jb + 1) * blk, :])
        kn = _head_norm(k_ref[...].astype(f32), kg_ref[...], bd)
        zeros = jnp.zeros((WINDOW, KV_DIM), bf16)
        k_s[0:WINDOW, :] = zeros
        k_s[WINDOW + seq:2 * WINDOW + seq, :] = zeros
        k_s[WINDOW:WINDOW + seq, :] = _rope(kn, cos, sin).astype(bf16)
        vt_s[0] = zeros
        vt_s[nblk + 1] = zeros
        for jb in range(nblk):
            vt_s[jb + 1] = _t_bf16(v_ref[jb * blk:(jb + 1) * blk, :])
        kc_s[...] = _head_norm(kc_ref[...].astype(f32), kg_ref[...], bd).astype(bf16)
        vct_s[...] = _t_bf16(vc_ref[...])

    band = blk + 2 * WINDOW
    start = pl.multiple_of(j * blk, blk)
    kb = k_s[pl.ds(start, band), :]
    kc = kc_s[...]
    vtb = jnp.concatenate([vt_s[j], vt_s[j + 1], vt_s[j + 2]], axis=1)
    vtc = vct_s[...]
    qt = qt_s[j]
    key = lax.broadcasted_iota(jnp.int32, (blk, 2 * blk), 0)
    qry = lax.broadcasted_iota(jnp.int32, (blk, 2 * blk), 1) & (blk - 1)
    ok_lo = (jnp.abs(qry - (key - WINDOW)) <= WINDOW) & (start - WINDOW + key >= 0)
    ok_hi = (jnp.abs(qry - (key + blk)) <= WINDOW) & (start + blk + key < seq)
    dim = lax.broadcasted_iota(jnp.int32, (LANES, blk), 0)
    lane2 = lax.broadcasted_iota(jnp.int32, (1, 2 * blk), 1)
    ones_b = jnp.ones((16, band), bf16)
    ones_c = jnp.ones((16, kc.shape[0]), bf16)
    def scores(p):
        qslab = qt[p * LANES:(p + 1) * LANES, :]
        zero = jnp.zeros_like(qslab)
        rhs = jnp.concatenate([jnp.where(dim < HEAD_DIM, qslab, zero), jnp.where(dim >= HEAD_DIM, qslab, zero)], axis=1)
        return _dot(kb, rhs), _dot(kc, rhs)

    def softmax(p, sb, sc):
        s_lo = jnp.where(ok_lo, sb[0:blk, :], -jnp.inf)
        s_mid = sb[blk:2 * blk, :]
        s_hi = jnp.where(ok_hi, sb[2 * blk:3 * blk, :], -jnp.inf)
        sink = jnp.where(lane2 < blk, sink_ref[p], sink_ref[p + N_Q_HEADS // N_KV_HEADS])
        colmax = lambda s: jnp.max(s, axis=0, keepdims=True)
        mx = jnp.maximum(jnp.maximum(jnp.maximum(colmax(s_lo), colmax(s_mid)), jnp.maximum(colmax(s_hi), colmax(sc))),
                         sink)
        pb = jnp.concatenate([jnp.exp(s_lo - mx).astype(bf16), jnp.exp(s_mid - mx).astype(bf16),
                              jnp.exp(s_hi - mx).astype(bf16)], axis=0)
        return pb, jnp.exp(sc - mx).astype(bf16), jnp.exp(sink - mx)

    def values(p, pb, pc, sink_term):
        den = (_dot(ones_b, pb) + _dot(ones_c, pc))[0:1, :] + sink_term
        ot = (_dot(vtb, pb) + _dot(vtc, pc)) / den
        both = jnp.where(dim < HEAD_DIM, ot[:, 0:blk], ot[:, blk:2 * blk])
        o_ref[:, p * LANES:(p + 1) * LANES] = both.T.astype(bf16)

    s_next = scores(0)
    for p in range(nslab):
        s_cur = s_next
        if p + 1 < nslab:
            s_next = scores(p + 1)
        values(p, *softmax(p, *s_cur))


def window_attention(proj_lat, proj_ctx, seq, ctx_len, q_gain, k_gain, sink, rope_cos, rope_sin):
    nb = proj_lat.shape[0] // seq
    nblk = seq // ATTN_BLOCK
    bd = np.kron(np.eye(LANES // HEAD_DIM), np.ones((HEAD_DIM, HEAD_DIM))) / HEAD_DIM
    gain2 = lambda g: jnp.tile(g, LANES // HEAD_DIM).reshape(1, LANES)
    const = lambda shape: pl.BlockSpec(shape, lambda b, j: (0, 0))
    return pl.pallas_call(
        functools.partial(_attn_kernel, seq),
        grid=(nb, nblk),
        in_specs=[pl.BlockSpec((seq, ATTN_DIM), lambda b, j: (b, COL_Q // ATTN_DIM)),
                  pl.BlockSpec((seq, KV_DIM), lambda b, j: (b, COL_K // KV_DIM)),
                  pl.BlockSpec((seq, KV_DIM), lambda b, j: (b, COL_V // KV_DIM)),
                  pl.BlockSpec((ctx_len, KV_DIM), lambda b, j: (b, COL_K // KV_DIM)),
                  pl.BlockSpec((ctx_len, KV_DIM), lambda b, j: (b, COL_V // KV_DIM)),
                  const((1, LANES)), const((1, LANES)),
                  const((seq, LANES)), const((seq, LANES)), const((LANES, LANES)),
                  pl.BlockSpec(memory_space=pltpu.SMEM)],
        out_specs=pl.BlockSpec((ATTN_BLOCK, ATTN_DIM), lambda b, j: (b * nblk + j, 0)),
        out_shape=jax.ShapeDtypeStruct((nb * seq, ATTN_DIM), bf16),
        scratch_shapes=[pltpu.VMEM((nblk, ATTN_DIM, ATTN_BLOCK), bf16),
                        pltpu.VMEM((seq + 2 * WINDOW, KV_DIM), bf16),
                        pltpu.VMEM((nblk + 2, KV_DIM, ATTN_BLOCK), bf16),
                        pltpu.VMEM((ctx_len, KV_DIM), bf16),
                        pltpu.VMEM((KV_DIM, ctx_len), bf16)],
        compiler_params=_params("parallel", "arbitrary"),
        name="window_attention",
    )(proj_lat, proj_lat, proj_lat, proj_ctx, proj_ctx, gain2(q_gain), gain2(k_gain),
      rope_cos, rope_sin, jnp.asarray(bd, bf16), sink)


def _rope_tables(seq):
    t = np.arange(seq)
    pos = np.stack([t // GRID_W, t % GRID_W], axis=1).astype(np.float32)
    axis_dim = HEAD_DIM // 2
    inv = (ROPE_THETA ** (-np.arange(0, axis_dim, 2, dtype=np.float32) / axis_dim)).astype(np.float32)
    lane = np.arange(LANES)
    d = lane % HEAD_DIM
    which = d // axis_dim
    ang = (pos[:, which] * inv[d % (axis_dim // 2)][None, :]).astype(np.float32)
    sign = np.where((d % axis_dim) < axis_dim // 2, -1.0, 1.0)
    return jnp.asarray(np.cos(ang), f32), jnp.asarray(np.sin(ang) * sign, f32)


def _softplus(x):
    return jnp.maximum(x, 0.0) + jnp.log1p(jnp.exp(-jnp.abs(x)))


def _expand_heads(v, e):
    return _dot(v.astype(bf16), e)


def _ssd_chunk(rev, lane0, want_y, x, bm, cm, dt_raw, dt_bias, a_neg, expand, state_ref, result):
    t = x.shape[0]
    dt = _softplus(dt_raw + dt_bias)
    a = dt * a_neg
    r = lax.broadcasted_iota(jnp.int32, (t, t), 0)
    c = lax.broadcasted_iota(jnp.int32, (t, t), 1)
    keep = (r <= c) if rev else (r >= c)
    tri = jnp.where(keep, 1.0, 0.0).astype(bf16)
    cs = sum(_dot(tri, part) for part in _split3(a))
    last = cs[0:1, :] if rev else cs[t - 1:t, :]
    e = jnp.exp(cs)
    w = dt * jnp.exp(last - cs)
    e_x = _expand_heads(e, expand)
    w_x = _expand_heads(w, expand)
    elast_x = e_x[0:1, :] if rev else e_x[t - 1:t, :]
    yield

    y = None
    if want_y:
        src_t = (cs - jnp.log(dt)).T
        lane = lax.broadcasted_iota(jnp.int32, (t, LANES), 1)
        cb = [_dot_nt(cm[:, g * SSD_STATE:(g + 1) * SSD_STATE],
                      bm[:, g * SSD_STATE:(g + 1) * SSD_STATE]) for g in range(SSD_GROUPS)]
        yield
        pieces = []
        for p in range(SSD_HEADS // 2):
            xp = x[:, p * LANES:(p + 1) * LANES]
            ms = []
            for q in range(2):
                h = 2 * p + q
                g = h // (SSD_HEADS // SSD_GROUPS)
                seg = cs[:, lane0 + h:lane0 + h + 1] - src_t[lane0 + h:lane0 + h + 1, :]
                ms.append((cb[g] * jnp.exp(jnp.where(keep, seg, -jnp.inf))).astype(bf16))
            zero = jnp.zeros_like(xp)
            xcat = jnp.concatenate([jnp.where(lane < SSD_HEAD_DIM, xp, zero),
                                    jnp.where(lane >= SSD_HEAD_DIM, xp, zero)], axis=0)
            pieces.append(_dot(jnp.concatenate(ms, axis=1), xcat))
            yield
        y = jnp.concatenate(pieces, axis=1)

    inter = []
    for g in range(SSD_GROUPS):
        gs = slice(g * GROUP_W, (g + 1) * GROUP_W)
        ss = slice(g * SSD_STATE, (g + 1) * SSD_STATE)
        h_t = state_ref[g]
        if want_y:
            inter.append(_dot(cm[:, ss], h_t.astype(bf16)) * e_x[:, gs])
        xw = (x[:, gs].astype(f32) * w_x[:, gs]).astype(bf16)
        state_ref[g] = h_t * elast_x[:, gs] + _dot(bm[:, ss].astype(f32).T.astype(bf16), xw)
        yield
    if want_y:
        result.append(y + jnp.concatenate(inter, axis=1))


def _interleave(*stage_generators):
    active = list(stage_generators)
    while active:
        for gen in list(active):
            if next(gen, StopIteration) is StopIteration:
                active.remove(gen)


def _ssd_ctx_kernel(nchunk, x_ref, bc_ref, dt_ref, bias_ref, alog_ref, ef_ref, eb_ref, hf_ref, hb_ref, sf, sb):
    sf[...] = jnp.zeros_like(sf)
    sb[...] = jnp.zeros_like(sb)
    a_neg = -jnp.exp(alog_ref[...])
    bias = bias_ref[...]
    t = SSD_CHUNK
    for ci in range(nchunk):
        rows = slice(ci * t, (ci + 1) * t)
        fwd = _ssd_chunk(False, 0, False, x_ref[rows, :], bc_ref[rows, 0:BC_DIM], None, dt_ref[rows, :],
                         bias, a_neg, ef_ref[...], sf, None)
        rows = slice((nchunk - 1 - ci) * t, (nchunk - ci) * t)
        bwd = _ssd_chunk(True, SSD_HEADS, False, x_ref[rows, :], bc_ref[rows, BC_DIM:2 * BC_DIM], None,
                         dt_ref[rows, :], bias, a_neg, eb_ref[...], sb, None)
        _interleave(fwd, bwd)
    hf_ref[0] = sf[...]
    hb_ref[0] = sb[...]


def _head_expanders():
    ef = np.zeros((LANES, D_SSM), np.float32)
    eb = np.zeros((LANES, D_SSM), np.float32)
    for h in range(SSD_HEADS):
        ef[h, h * SSD_HEAD_DIM:(h + 1) * SSD_HEAD_DIM] = 1.0
        eb[SSD_HEADS + h, h * SSD_HEAD_DIM:(h + 1) * SSD_HEAD_DIM] = 1.0
    return jnp.asarray(ef, bf16), jnp.asarray(eb, bf16)


def _pad_lanes(v):
    v = v.reshape(1, -1).astype(f32)
    return jnp.pad(v, ((0, 0), (0, LANES - v.shape[1])))


def ssd_ctx_states(xc, dt_raw, seq, dt_bias, a_log):
    nb = xc.shape[0] // seq
    ef, eb = _head_expanders()
    const = lambda shape: pl.BlockSpec(shape, lambda b: (0,) * len(shape))
    st = jax.ShapeDtypeStruct((nb, SSD_GROUPS, SSD_STATE, GROUP_W), f32)
    st_spec = pl.BlockSpec((1, SSD_GROUPS, SSD_STATE, GROUP_W), lambda b: (b, 0, 0, 0))
    return pl.pallas_call(
        functools.partial(_ssd_ctx_kernel, seq // SSD_CHUNK),
        grid=(nb,),
        in_specs=[pl.BlockSpec((seq, D_SSM), lambda b: (b, 0)),
                  pl.BlockSpec((seq, 2 * BC_DIM), lambda b: (b, D_SSM // (2 * BC_DIM))),
                  pl.BlockSpec((seq, LANES), lambda b: (b, 0)),
                  const((1, LANES)), const((1, LANES)), const((LANES, D_SSM)), const((LANES, D_SSM))],
        out_specs=[st_spec, st_spec],
        out_shape=[st, st],
        scratch_shapes=[pltpu.VMEM((SSD_GROUPS, SSD_STATE, GROUP_W), f32),
                        pltpu.VMEM((SSD_GROUPS, SSD_STATE, GROUP_W), f32)],
        compiler_params=_params("parallel"),
        name="ssd_ctx_states",
    )(xc, xc, dt_raw, _pad_lanes(dt_bias), _pad_lanes(a_log), ef, eb)


def _ssd_lat_kernel(nchunk, xf_ref, xb_ref, bf_ref, bb_ref, cf_ref, cb_ref, dtf_ref, dtb_ref, zf_ref, zb_ref,
                    hf0_ref, hb0_ref, bias_ref, alog_ref, dskip_ref, normw_ref, ef_ref, eb_ref,
                    o_ref, sf, sb, yacc):
    c = pl.program_id(1)
    t = SSD_CHUNK

    @pl.when(c == 0)
    def _():
        sf[...] = hf0_ref[0]
        sb[...] = hb0_ref[0]

    a_neg = -jnp.exp(alog_ref[...])
    bias = bias_ref[...]
    xf = xf_ref[...]
    yf, yb = [], []
    _interleave(
        _ssd_chunk(False, 0, True, xf, bf_ref[...], cf_ref[...], dtf_ref[...], bias, a_neg, ef_ref[...], sf, yf),
        _ssd_chunk(True, SSD_HEADS, True, xb_ref[...], bb_ref[...], cb_ref[...], dtb_ref[...], bias, a_neg,
                   eb_ref[...], sb, yb))
    yf = yf[0] + dskip_ref[...] * xf.astype(f32)
    yb = yb[0]
    rows_f = pl.ds(pl.multiple_of(c * t, t), t)
    rows_b = pl.ds(pl.multiple_of((nchunk - 1 - c) * t, t), t)

    @pl.when(c < nchunk // 2)
    def _():
        yacc[rows_f, :] = yf
        yacc[rows_b, :] = yb

    def finish(y, z):
        y = y * _silu(z.astype(f32))
        outs = []
        for g in range(SSD_GROUPS):
            yg = y[:, g * GROUP_W:(g + 1) * GROUP_W]
            outs.append(yg * lax.rsqrt(jnp.mean(yg * yg, axis=-1, keepdims=True) + RMS_EPS))
        return (jnp.concatenate(outs, axis=1) * normw_ref[...]).astype(bf16)

    @pl.when(c >= nchunk // 2)
    def _():
        o_ref[rows_f, :] = finish(yacc[rows_f, :] + yf, zf_ref[...])
        o_ref[rows_b, :] = finish(yacc[rows_b, :] + yb, zb_ref[...])


def ssd_latent(xc, proj, dt_raw, seq, hf0, hb0, dt_bias, a_log, d_skip, norm_w):
    nb = xc.shape[0] // seq
    nc = seq // SSD_CHUNK
    half = nc // 2
    ef, eb = _head_expanders()
    t = SSD_CHUNK
    fwd = lambda b, c: b * nc + c
    bwd = lambda b, c: b * nc + nc - 1 - c
    zfw = lambda b, c: b * nc + jnp.maximum(c, half)
    zbw = lambda b, c: b * nc + jnp.minimum(nc - 1 - c, half - 1)
    bc0 = D_SSM // SSD_STATE // SSD_GROUPS
    const = lambda shape: pl.BlockSpec(shape, lambda b, c: (0,) * len(shape))
    st_spec = pl.BlockSpec((1, SSD_GROUPS, SSD_STATE, GROUP_W), lambda b, c: (b, 0, 0, 0))
    dskip = jnp.repeat(d_skip.astype(f32), SSD_HEAD_DIM).reshape(1, D_SSM)
    return pl.pallas_call(
        functools.partial(_ssd_lat_kernel, nc),
        grid=(nb, nc),
        in_specs=[pl.BlockSpec((t, D_SSM), lambda b, c: (fwd(b, c), 0)),
                  pl.BlockSpec((t, D_SSM), lambda b, c: (bwd(b, c), 0)),
                  pl.BlockSpec((t, BC_DIM), lambda b, c: (fwd(b, c), bc0)),
                  pl.BlockSpec((t, BC_DIM), lambda b, c: (bwd(b, c), bc0 + 1)),
                  pl.BlockSpec((t, BC_DIM), lambda b, c: (fwd(b, c), bc0 + 2)),
                  pl.BlockSpec((t, BC_DIM), lambda b, c: (bwd(b, c), bc0 + 3)),
                  pl.BlockSpec((t, LANES), lambda b, c: (fwd(b, c), 0)),
                  pl.BlockSpec((t, LANES), lambda b, c: (bwd(b, c), 0)),
                  pl.BlockSpec((t, D_SSM), lambda b, c: (zfw(b, c), COL_Z // D_SSM)),
                  pl.BlockSpec((t, D_SSM), lambda b, c: (zbw(b, c), COL_Z // D_SSM)),
                  st_spec, st_spec,
                  const((1, LANES)), const((1, LANES)), const((1, D_SSM)), const((1, D_SSM)),
                  const((LANES, D_SSM)), const((LANES, D_SSM))],
        out_specs=pl.BlockSpec((seq, D_SSM), lambda b, c: (b, 0)),
        out_shape=jax.ShapeDtypeStruct((nb * seq, D_SSM), bf16),
        scratch_shapes=[pltpu.VMEM((SSD_GROUPS, SSD_STATE, GROUP_W), f32),
                        pltpu.VMEM((SSD_GROUPS, SSD_STATE, GROUP_W), f32),
                        pltpu.VMEM((seq, D_SSM), f32)],
        compiler_params=_params("parallel", "arbitrary"),
        name="ssd_latent",
    )(xc, xc, xc, xc, xc, xc, dt_raw, dt_raw, proj, proj, hf0, hb0,
      _pad_lanes(dt_bias), _pad_lanes(a_log), dskip, norm_w.reshape(1, -1).astype(f32), ef, eb)


def _filter_kernel(z_ref, w1_ref, b1_ref, wh_ref, bh_ref, freq_ref, wf_ref, wb_ref, delta_ref, ks_ref, kd_ref, h_ref):
    hp = lambda a, b: jnp.dot(a, b, preferred_element_type=f32, precision=lax.Precision.HIGHEST)
    z = z_ref[...]

    @pl.when(pl.program_id(0) == 0)
    def _():
        freq = freq_ref[...]
        h = jnp.sin(freq * (hp(z, w1_ref[...]) + b1_ref[...]))
        for n in range(HYENA_INNER):
            h = jnp.sin(freq * (hp(h, wh_ref[n]) + bh_ref[n]))
        h_ref[...] = h

    h = h_ref[...]
    window = jnp.exp(-z[:, 0:1] * delta_ref[...])
    hf = hp(h, wf_ref[...]) * window
    hb = hp(h, wb_ref[...]) * window
    row = lax.broadcasted_iota(jnp.int32, hb.shape, 0)
    hb = jnp.where(row == 0, 0.0, hb)
    norm = jnp.sum(jnp.abs(hf), axis=0, keepdims=True) + jnp.sum(jnp.abs(hb), axis=0, keepdims=True)
    ks_ref[...] = (hf + hb) / norm
    kd_ref[...] = (hf - hb) / norm


def hyena_filter_taps(seq, f_w1, f_b1, f_wh, f_bh, f_wout, freq, tc=256):
    fw = HYENA_FILTER_WIDTH
    t = np.arange(seq, dtype=np.float32)
    t_norm = t / np.float32(seq - 1)
    bands = np.linspace(1e-4, HYENA_BANDS - 1, HYENA_BANDS, dtype=np.float32)
    ang = np.float32(2.0 * math.pi / seq) * t[:, None] * bands
    z = np.concatenate([t_norm[:, None], np.cos(ang), -np.sin(ang)], axis=-1).astype(np.float32)
    z = np.pad(z, ((0, 0), (0, LANES - z.shape[1])))
    deltas = np.abs(np.linspace(math.log(HYENA_TARGET) / HYENA_SLOW_DECAY, math.log(HYENA_TARGET) / HYENA_FAST_DECAY,
                                D_MODEL, dtype=np.float32)).reshape(1, -1)
    padw = lambda a, r, c: jnp.pad(a.astype(f32), [(0, 0)] * (a.ndim - 2) + [(0, r - a.shape[-2]), (0, c - a.shape[-1])])
    w1 = padw(f_w1, LANES, LANES)
    wh = padw(f_wh, LANES, LANES)
    wout = padw(f_wout, LANES, 2 * D_MODEL)
    b1 = padw(f_b1.reshape(1, fw), 1, LANES)
    bh = padw(f_bh.reshape(HYENA_INNER, 1, fw), 1, LANES)
    fq = padw(freq.reshape(1, fw), 1, LANES)
    nt = D_MODEL // tc
    const = lambda shape: pl.BlockSpec(shape, lambda j: (0,) * len(shape))
    out = jax.ShapeDtypeStruct((seq, D_MODEL), f32)
    osp = pl.BlockSpec((seq, tc), lambda j: (0, j))
    return pl.pallas_call(
        _filter_kernel,
        grid=(nt,),
        in_specs=[const((seq, LANES)), const((LANES, LANES)), const((1, LANES)),
                  const((HYENA_INNER, LANES, LANES)), const((HYENA_INNER, 1, LANES)), const((1, LANES)),
                  pl.BlockSpec((LANES, tc), lambda j: (0, j)),
                  pl.BlockSpec((LANES, tc), lambda j: (0, j + nt)),
                  pl.BlockSpec((1, tc), lambda j: (0, j))],
        out_specs=[osp, osp],
        out_shape=[out, out],
        scratch_shapes=[pltpu.VMEM((seq, LANES), f32)],
        compiler_params=_params("arbitrary"),
        name="hyena_filter",
    )(jnp.asarray(z), w1, b1, wh, bh, fq, wout, wout, jnp.asarray(deltas))


def _dft_tables(seq):
    n = 2 * seq
    half = seq // 2

    def theta(f, t):
        return (2.0 * math.pi / (2 * n)) * (((2 * f[:, None] + 1) * t[None, :]) % (2 * n)).astype(np.float64)

    f_low = np.arange(half, dtype=np.int64)
    order = np.concatenate([f_low, seq - 1 - f_low])
    th_full = theta(order, np.arange(seq, dtype=np.int64))
    tp = np.arange(half, dtype=np.int64)
    th_e, th_o = theta(f_low, 2 * tp), theta(f_low, 2 * tp + 1)
    fwd = [np.cos(th_e), np.cos(th_o), np.sin(th_e), np.sin(th_o)]
    as_bf16 = lambda a: jnp.asarray(a, bf16)
    return (as_bf16(np.cos(th_full)), as_bf16(np.sin(th_full)),
            [as_bf16(a) for a in fwd], [as_bf16(a.T * (2.0 / n)) for a in fwd])


def _spectrum_kernel(c_ref, s_ref, ks_ref, kd_ref, kre_ref, kb_ref):
    kre_ref[...] = _dot(c_ref[...], ks_ref[...].astype(bf16))
    kb_ref[...] = _dot(s_ref[...], kd_ref[...].astype(bf16))


def filter_spectrum(cmat, smat, ks, kd, tmf=256, tc=256):
    seq = ks.shape[0]
    fsp = pl.BlockSpec((tmf, seq), lambda j, m: (m, 0))
    ksp = pl.BlockSpec((seq, tc), lambda j, m: (0, j))
    osp = pl.BlockSpec((tmf, tc), lambda j, m: (m, j))
    out = jax.ShapeDtypeStruct((seq, D_MODEL), f32)
    return pl.pallas_call(
        _spectrum_kernel,
        grid=(D_MODEL // tc, seq // tmf),
        in_specs=[fsp, fsp, ksp, ksp],
        out_specs=[osp, osp],
        out_shape=[out, out],
        compiler_params=_params("parallel", "parallel"),
        name="filter_spectrum",
    )(cmat, smat, ks, kd)


def _dft_fwd_kernel(ce_ref, co_ref, se_ref, so_ref, ve_ref, vo_ref, kre_ref, kb_ref,
                    pp_ref, pm_ref, qp_ref, qm_ref):
    tmf = pp_ref.shape[0]
    half = ce_ref.shape[0]
    low = pl.ds(pl.multiple_of(pl.program_id(1) * tmf, tmf), tmf)
    mirrored = pl.ds(pl.multiple_of(half + pl.program_id(1) * tmf, tmf), tmf)
    ve, vo = ve_ref[...], vo_ref[...]
    ec, oc = _dot(ce_ref[low, :], ve), _dot(co_ref[low, :], vo)
    es, os_ = _dot(se_ref[low, :], ve), _dot(so_ref[low, :], vo)

    def times_filter(vre, va, kre, kb):
        return vre * kre - va * kb, vre * kb + va * kre

    p, q = times_filter(ec + oc, es + os_, kre_ref[low, :], kb_ref[low, :])
    p_m, q_m = times_filter(ec - oc, os_ - es, kre_ref[mirrored, :], kb_ref[mirrored, :])
    pp_ref[...] = (p + p_m).astype(bf16)
    pm_ref[...] = (p - p_m).astype(bf16)
    qp_ref[...] = (q + q_m).astype(bf16)
    qm_ref[...] = (q - q_m).astype(bf16)


def dft_forward(fwd_tables, vg_even, vg_odd, kre, kb, seq, tmf=256):
    half = seq // 2
    nb = vg_even.shape[0] // half
    tmf = min(tmf, half)
    nm = half // tmf
    whole = lambda a: pl.BlockSpec(a.shape, lambda b, m: (0, 0))
    osp = pl.BlockSpec((tmf, D_MODEL), lambda b, m: (b * nm + m, 0))
    out = jax.ShapeDtypeStruct((nb * half, D_MODEL), bf16)
    return pl.pallas_call(
        _dft_fwd_kernel,
        grid=(nb, nm),
        in_specs=[whole(t) for t in fwd_tables]
        + [pl.BlockSpec((half, D_MODEL), lambda b, m: (b, 0)),
           pl.BlockSpec((half, D_MODEL), lambda b, m: (b, 0)),
           whole(kre), whole(kb)],
        out_specs=[osp] * 4,
        out_shape=[out] * 4,
        compiler_params=_params("parallel", "parallel"),
        name="dft_forward",
    )(*fwd_tables, vg_even, vg_odd, kre, kb)


def _dft_inv_kernel(gate_row, cet_ref, cot_ref, set_ref, sot_ref, pp_ref, pm_ref, qp_ref, qm_ref,
                    vge_ref, vgo_ref, x0e_ref, x0o_ref, bias_ref, w_ref, h_ref, mod_ref, o_ref, stage_ref):
    gate = mod_ref[0, gate_row:gate_row + 1, :]
    tmt = vge_ref.shape[0]
    nlane = D_MODEL // LANES
    for c in range(nlane):
        stage_ref[c] = h_ref[:, c * LANES:(c + 1) * LANES]

    def finish(y, parity, vg_ref, x0_ref):
        y = y + vg_ref[...].astype(f32) * bias_ref[...]
        upd = gate * _dot((y * x0_ref[...].astype(f32)).astype(bf16), w_ref[...])
        rows = pl.ds(parity, tmt, stride=2)
        for c in range(nlane):
            chunk = stage_ref.at[c]
            chunk[rows, :] = chunk[rows, :] + upd[:, c * LANES:(c + 1) * LANES]

    mine = pl.ds(pl.multiple_of(pl.program_id(1) * tmt, tmt), tmt)
    finish(_dot(cet_ref[mine, :], pp_ref[...]) + _dot(set_ref[mine, :], qm_ref[...]), 0, vge_ref, x0e_ref)
    finish(_dot(cot_ref[mine, :], pm_ref[...]) + _dot(sot_ref[mine, :], qp_ref[...]), 1, vgo_ref, x0o_ref)
    for c in range(nlane):
        o_ref[:, c * LANES:(c + 1) * LANES] = stage_ref[c]


def dft_inverse_out(inv_tables, folded, vg_eo, x0_eo, bias, w_out, h, mods, seq, gate_row, tmt=256):
    half = seq // 2
    nb = h.shape[0] // seq
    tmt = min(tmt, half)
    nm = half // tmt
    gsp = pl.BlockSpec((half, half), lambda b, m: (0, 0))
    full = pl.BlockSpec((half, D_MODEL), lambda b, m: (b, 0))
    part = pl.BlockSpec((tmt, D_MODEL), lambda b, m: (b * nm + m, 0))
    tile = pl.BlockSpec((2 * tmt, D_MODEL), lambda b, m: (b * nm + m, 0))
    return pl.pallas_call(
        functools.partial(_dft_inv_kernel, gate_row),
        grid=(nb, nm),
        in_specs=[gsp, gsp, gsp, gsp, full, full, full, full, part, part, part, part,
                  pl.BlockSpec((1, D_MODEL), lambda b, m: (0, 0)),
                  pl.BlockSpec((D_MODEL, D_MODEL), lambda b, m: (0, 0)),
                  tile, pl.BlockSpec((1, N_MOD, D_MODEL), lambda b, m: (b, 0, 0))],
        out_specs=tile,
        out_shape=jax.ShapeDtypeStruct(h.shape, f32),
        scratch_shapes=[pltpu.VMEM((D_MODEL // LANES, 2 * tmt, LANES), f32)],
        compiler_params=_params("parallel", "parallel"),
        name="dft_inverse_out",
    )(*inv_tables, *folded, *vg_eo, *x0_eo, bias.reshape(1, -1).astype(f32), w_out, h, mods)


_Q_HEAD_ORDER = (0, 4, 1, 5, 2, 6, 3, 7)


def _mixer_in_weight(w_in):
    c = np.cumsum([ATTN_DIM, KV_DIM, KV_DIM, D_SSM, XBC_DIM])
    q, k, v, z, xbc, dt = (w_in[:, :c[0]], w_in[:, c[0]:c[1]], w_in[:, c[1]:c[2]], w_in[:, c[2]:c[3]],
                           w_in[:, c[3]:c[4]], w_in[:, c[4]:])
    q = q.reshape(D_MODEL, N_Q_HEADS, HEAD_DIM)[:, np.array(_Q_HEAD_ORDER), :].reshape(D_MODEL, ATTN_DIM)
    w = jnp.concatenate([xbc, z, q, k, v], axis=1).astype(bf16)
    return w, jnp.pad(dt, ((0, 0), (0, LANES - dt.shape[1]))).astype(bf16)


def _mixer_out_weights(w_out):
    wa = w_out[:ATTN_DIM].reshape(N_Q_HEADS, HEAD_DIM, D_MODEL)[np.array(_Q_HEAD_ORDER)].reshape(ATTN_DIM, D_MODEL)
    return wa.astype(bf16), w_out[ATTN_DIM:].astype(bf16)


def kernel(x, c, ctx, c_ctx, w_ada, b_ada, norm_g, ffn_w13, ffn_w2, mix_w_in, mix_w_out, q_norm, k_norm, attn_sink,
           ssd_conv_w, ssd_conv_b, ssd_dt_bias, ssd_a_log, ssd_d, ssd_norm, hy_w_in, hy_conv_w, hy_conv_b,
           hy_f_w1, hy_f_b1, hy_f_wh, hy_f_bh, hy_f_wout, hy_freq, hy_bias, hy_w_out):
    nb, seq, _ = x.shape
    ctx_len = ctx.shape[1]
    depth = w_ada.shape[0]
    assert depth == 2, "this kernel is written for the two-layer block"
    h_lat = x.reshape(nb * seq, D_MODEL)
    h_ctx = ctx.reshape(nb * ctx_len, D_MODEL)
    cond = jnp.concatenate([c, c_ctx[None]], axis=0)
    cond = jnp.pad(cond, ((0, -cond.shape[0] % 8), (0, 0)))
    ffn_w = [[_ffn_weights(ffn_w13[i, k], ffn_w2[i, k]) for k in range(2)] for i in range(depth)]

    mods = ada_mods(cond, w_ada[0], b_ada[0])
    m_lat, m_ctx = mods[:nb], mods[nb:nb + 1]
    h_lat = macaron_ffn(h_lat, m_lat, seq, norm_g[0, 0], 0, *ffn_w[0][0])
    h_ctx = macaron_ffn(h_ctx, m_ctx, nb * ctx_len, norm_g[0, 0], 0, *ffn_w[0][0])
    w_in, w_dt = _mixer_in_weight(mix_w_in[0])
    wa, ws = _mixer_out_weights(mix_w_out[0])
    xc_lat, p_lat, dt_lat = mixer_proj(h_lat, m_lat, seq, seq, norm_g[0, 1], w_in, w_dt, ssd_conv_w[0], ssd_conv_b[0])
    xc_ctx, p_ctx, dt_ctx = mixer_proj(h_ctx, m_ctx, nb * ctx_len, ctx_len, norm_g[0, 1], w_in, w_dt,
                                       ssd_conv_w[0], ssd_conv_b[0])
    cos, sin = _rope_tables(seq)
    a_lat = window_attention(p_lat, p_ctx, seq, ctx_len, q_norm[0], k_norm[0], attn_sink[0], cos, sin)
    hf0, hb0 = ssd_ctx_states(xc_ctx, dt_ctx, ctx_len, ssd_dt_bias[0], ssd_a_log[0])
    s_lat = ssd_latent(xc_lat, p_lat, dt_lat, seq, hf0, hb0, ssd_dt_bias[0], ssd_a_log[0], ssd_d[0], ssd_norm[0])
    h_lat = macaron_ffn(h_lat, m_lat, seq, norm_g[0, 2], 2, *ffn_w[0][1], mix=((a_lat, wa), (s_lat, ws)))

    m_lat = ada_mods(cond, w_ada[1], b_ada[1])[:nb]
    h_lat = macaron_ffn(h_lat, m_lat, seq, norm_g[1, 0], 0, *ffn_w[1][0])
    x0e, x0o, vge, vgo = hyena_proj(h_lat, m_lat, seq, seq, norm_g[1, 1], hy_w_in[0].astype(bf16),
                                    hy_conv_w[0], hy_conv_b[0])
    ks, kd = hyena_filter_taps(seq, hy_f_w1[0], hy_f_b1[0], hy_f_wh[0], hy_f_bh[0], hy_f_wout[0], hy_freq[0])
    c_full, s_full, fwd_tables, inv_tables = _dft_tables(seq)
    kre, kb = filter_spectrum(c_full, s_full, ks, kd)
    folded = dft_forward(fwd_tables, vge, vgo, kre, kb, seq)
    h_lat = dft_inverse_out(inv_tables, folded, (vge, vgo), (x0e, x0o), hy_bias[0], hy_w_out[0].astype(bf16),
                            h_lat, m_lat, seq, 5)
    h_lat = macaron_ffn(h_lat, m_lat, seq, norm_g[1, 2], 2, *ffn_w[1][1])
    return h_lat.reshape(nb, seq, D_MODEL)
```

```python
import functools
import math

import numpy as np
import jax
import jax.numpy as jnp
from jax import lax
from jax.experimental import pallas as pl
from jax.experimental.pallas import tpu as pltpu

f32 = jnp.float32
bf16 = jnp.bfloat16

D_MODEL = 1024
N_MOD = 9
RMS_EPS = 1e-6
GRID_W = 64

HEAD_DIM = 64
N_Q_HEADS = 8
N_KV_HEADS = 2
ATTN_DIM = N_Q_HEADS * HEAD_DIM
KV_DIM = N_KV_HEADS * HEAD_DIM
WINDOW = 128
ATTN_BLOCK = 128
ROPE_THETA = 10000.0

SSD_HEADS = 16
SSD_HEAD_DIM = 64
D_SSM = SSD_HEADS * SSD_HEAD_DIM
SSD_GROUPS = 2
SSD_STATE = 128
SSD_CONV = 7
SSD_CHUNK = 128
BC_DIM = SSD_GROUPS * SSD_STATE
XBC_DIM = D_SSM + 4 * BC_DIM
GROUP_W = D_SSM // SSD_GROUPS

HYENA_SHORT = 3
HYENA_BANDS = 8
HYENA_FILTER_WIDTH = 64
HYENA_INNER = 2
HYENA_FAST_DECAY = 0.3
HYENA_SLOW_DECAY = 1.5
HYENA_TARGET = 1e-2

D_FF = 2816
FFN_TF = 256
LANES = 128

COL_Z = 0
COL_Q = COL_Z + D_SSM
COL_K = COL_Q + ATTN_DIM
COL_V = COL_K + KV_DIM
REST_COLS = COL_V + KV_DIM
HALO = 16
PROJ_TN = 256
PROJ_TM = 256
CONV_ROWS = 64

VMEM_LIMIT = 56 * 1024 * 1024


def _params(*sem):
    return pltpu.CompilerParams(dimension_semantics=sem, vmem_limit_bytes=VMEM_LIMIT)


def _dot(a, b):
    return jnp.dot(a, b, preferred_element_type=f32)


def _dot_nt(a, b):
    return lax.dot_general(a, b, (((1,), (1,)), ((), ())), preferred_element_type=f32)


def _split2(x):
    hi = x.astype(bf16)
    lo = (x - hi.astype(f32)).astype(bf16)
    return hi, lo


def _split3(x):
    hi = x.astype(bf16)
    r = x - hi.astype(f32)
    mid = r.astype(bf16)
    lo = (r - mid.astype(f32)).astype(bf16)
    return hi, mid, lo


def _adaln(h, g, shift, scale):
    ms = jnp.mean(h * h, axis=-1, keepdims=True)
    return (h * lax.rsqrt(ms + RMS_EPS) * g) * (1.0 + scale) + shift


def _silu(x):
    return x * jax.nn.sigmoid(x)


def _mods_kernel(c_ref, w_ref, b_ref, o_ref):
    o_ref[...] = _dot(_silu(c_ref[...]).astype(bf16), w_ref[...].astype(bf16)) + b_ref[...]


def ada_mods(cond, w, b):
    r = cond.shape[0]
    tn = 1024
    out = pl.pallas_call(
        _mods_kernel,
        grid=(w.shape[1] // tn,),
        in_specs=[pl.BlockSpec((r, D_MODEL), lambda j: (0, 0)),
                  pl.BlockSpec((D_MODEL, tn), lambda j: (0, j)),
                  pl.BlockSpec((1, tn), lambda j: (0, j))],
        out_specs=pl.BlockSpec((r, tn), lambda j: (0, j)),
        out_shape=jax.ShapeDtypeStruct((r, w.shape[1]), f32),
        compiler_params=_params("parallel"),
        name="ada_mods",
    )(cond, w, b.reshape(1, -1))
    return out.reshape(r, N_MOD, D_MODEL)


def _ffn_kernel(s, nf, n_mix, h_ref, mod_ref, g_ref, w13_ref, w2_ref, *rest):
    o_ref = rest[-1]
    h = h_ref[...]
    if n_mix:
        mixed = _dot(rest[0][...], rest[n_mix][...])
        for x_ref, w_ref in zip(rest[1:n_mix], rest[n_mix + 1:2 * n_mix]):
            mixed += _dot(x_ref[...], w_ref[...])
        h = h + mod_ref[0, 3 * s - 1:3 * s, :] * mixed
    u = _adaln(h, g_ref[...], mod_ref[0, 3 * s:3 * s + 1, :], mod_ref[0, 3 * s + 1:3 * s + 2, :]).astype(bf16)
    acc = None
    tf = w2_ref.shape[1]
    for j in range(nf):
        a = _dot(u, w13_ref[:, j * tf:(j + 1) * tf])
        b = _dot(u, w13_ref[:, D_FF + j * tf:D_FF + (j + 1) * tf])
        part = _dot((_silu(a) * b).astype(bf16), w2_ref[j])
        acc = part if acc is None else acc + part
    o_ref[...] = h + 0.5 * mod_ref[0, 3 * s + 2:3 * s + 3, :] * acc


def _ffn_weights(w13, w2, tf=FFN_TF):
    return w13.astype(bf16), w2.astype(bf16).reshape(D_FF // tf, tf, D_MODEL)


def macaron_ffn(h, mods, rows_per_mod, g, s, w13c, w2c, mix=(), tm=1024):
    m = h.shape[0]
    tm = min(tm // 2 if mix else tm, rows_per_mod)
    nf = w2c.shape[0]
    tiles_per_mod = rows_per_mod // tm
    assert not mix or s == 2, "the mixer's gate is the modulation row just before the second FFN's"
    xs = [x for x, _ in mix]
    ws = [w for _, w in mix]
    return pl.pallas_call(
        functools.partial(_ffn_kernel, s, nf, len(mix)),
        grid=(m // tm,),
        in_specs=[pl.BlockSpec((tm, D_MODEL), lambda i: (i, 0)),
                  pl.BlockSpec((1, N_MOD, D_MODEL), lambda i: (i // tiles_per_mod, 0, 0)),
                  pl.BlockSpec((1, D_MODEL), lambda i: (0, 0)),
                  pl.BlockSpec(w13c.shape, lambda i: (0, 0)),
                  pl.BlockSpec(w2c.shape, lambda i: (0, 0, 0))]
        + [pl.BlockSpec((tm, x.shape[1]), lambda i: (i, 0)) for x in xs]
        + [pl.BlockSpec(w.shape, lambda i: (0, 0)) for w in ws],
        out_specs=pl.BlockSpec((tm, D_MODEL), lambda i: (i, 0)),
        out_shape=jax.ShapeDtypeStruct((m, D_MODEL), f32),
        compiler_params=_params("parallel"),
        name="macaron_ffn",
    )(h, mods, g.reshape(1, -1), w13c, w2c, *xs, *ws)


def _tile_setup(s, tiles_per_seq, h_ref, hp_ref, hn_ref, mod_ref, g_ref, perm_ref, u_ref, uh_ref, up_ref):
    shift, scale = mod_ref[0, 3 * s:3 * s + 1, :], mod_ref[0, 3 * s + 1:3 * s + 2, :]
    gain = g_ref[...]
    u_ref[...] = _adaln(h_ref[...], gain, shift, scale).astype(bf16)
    uh_ref[...] = _adaln(jnp.concatenate([hp_ref[...], hn_ref[...]], axis=0), gain, shift, scale).astype(bf16)
    up_ref[...] = _dot(perm_ref[...], u_ref[...]).astype(bf16)
    t = pl.program_id(0) % tiles_per_seq
    keep_prev = jnp.where(t > 0, 1.0, 0.0)
    keep_next = jnp.where(t < tiles_per_seq - 1, 1.0, 0.0)
    return jnp.where(lax.broadcasted_iota(jnp.int32, (2 * HALO, 1), 0) < HALO, keep_prev, keep_next)


def _project_permuted(pad, up_ref, uh_ref, keep, w, half):
    tm = up_ref.shape[0]
    grp = tm // 8
    ext = 8 * half
    sub = lax.broadcasted_iota(jnp.int32, (8, w.shape[1]), 0)

    def tile_rows(r0):
        pad[ext + r0:ext + r0 + PROJ_TM, :] = _dot(up_ref[r0:r0 + PROJ_TM, :], w)

    def boundary_tiles():
        halo = _dot(uh_ref[...], w) * keep
        for j in range(half):
            src = ext + 8 * (grp - half + j)
            pad[8 * j:8 * j + 8, :] = jnp.where(sub == 0, halo[HALO - half + j:HALO - half + j + 1, :],
                                                pltpu.roll(pad[src:src + 8, :], 1, 0))
            src = ext + 8 * j
            dst = ext + tm + 8 * j
            pad[dst:dst + 8, :] = jnp.where(sub == 7, halo[HALO + j:HALO + j + 1, :],
                                            pltpu.roll(pad[src:src + 8, :], 7, 0))

    return [functools.partial(tile_rows, r0) for r0 in range(0, tm, PROJ_TM)] + [boundary_tiles]


def _conv_permuted(pad, cw_ref, cb_ref, col0, width, tm, emit, between):
    rows = CONV_ROWS
    steps = [(l0, r0) for l0 in range(0, pad.shape[1], LANES) for r0 in range(0, tm, rows)]
    every = max(1, len(steps) // max(1, len(between)))
    pending = list(between)
    for n, (l0, r0) in enumerate(steps):
        if pending and n % every == 0:
            pending.pop(0)()
        wcol = slice(col0 + l0, col0 + l0 + LANES)
        acc = cb_ref[:, wcol]
        for k in range(width):
            acc = acc + cw_ref[k:k + 1, wcol] * pad[8 * k + r0:8 * k + r0 + rows, l0:l0 + LANES]
        emit(r0, rows, l0, acc)
    for thunk in pending:
        thunk()


def _token_order(stage, r0, rows, grp):
    per = grp // 8
    tiles = [stage[pl.ds(64 * (j % per) + j // per, 8, stride=8), :] for j in range(r0 // 8, (r0 + rows) // 8)]
    return jnp.concatenate(tiles, axis=0)


def _mixer_proj_kernel(s, tiles_per_seq, h_ref, hp_ref, hn_ref, mod_ref, g_ref, w_ref, wdt_ref, cw_ref, cb_ref,
                       perm_ref, xc_ref, rest_ref, dt_ref, pad_ref, u_ref, uh_ref, up_ref, ystage_ref):
    keep = _tile_setup(s, tiles_per_seq, h_ref, hp_ref, hn_ref, mod_ref, g_ref, perm_ref, u_ref, uh_ref, up_ref)
    tm = h_ref.shape[0]
    tn = PROJ_TN
    n_conv = XBC_DIM // tn
    n_rest = REST_COLS // tn

    def project(c):
        return _project_permuted(pad_ref.at[c % 2], up_ref, uh_ref, keep, w_ref[:, c * tn:(c + 1) * tn], SSD_CONV // 2)

    def conv(c, between):
        def emit(r0, rows, l0, acc):
            ystage_ref[l0 // LANES, r0:r0 + rows, :] = _silu(acc)

        _conv_permuted(pad_ref.at[c % 2], cw_ref, cb_ref, c * tn, SSD_CONV, tm, emit, between)
        for l0 in range(0, tn, LANES):
            for r0 in range(0, tm, CONV_ROWS):
                value = _token_order(ystage_ref.at[l0 // LANES], r0, CONV_ROWS, tm // 8)
                xc_ref[r0:r0 + CONV_ROWS, c * tn + l0:c * tn + l0 + LANES] = value.astype(bf16)

    def plain_rows(c, r0):
        rows = slice(r0, r0 + PROJ_TM)
        if c < n_rest:
            w = w_ref[:, XBC_DIM + c * tn:XBC_DIM + (c + 1) * tn]
            rest_ref[rows, c * tn:(c + 1) * tn] = _dot(u_ref[rows, :], w).astype(bf16)
        elif c == n_rest:
            dt_ref[rows, :] = _dot(u_ref[rows, :], wdt_ref[...])

    def plain(c):
        return [functools.partial(plain_rows, c, r0) for r0 in range(0, tm, PROJ_TM)]

    for thunk in project(0):
        thunk()
    for c in range(n_conv):
        conv(c, (project(c + 1) if c + 1 < n_conv else []) + plain(c))
    for c in range(n_conv, n_rest + 1):
        for thunk in plain(c):
            thunk()


def _tile_adaln(s, tiles_per_seq, h_ref, hp_ref, hn_ref, mod_ref, g_ref, u_ref, uh_ref):
    shift, scale = mod_ref[0, 3 * s:3 * s + 1, :], mod_ref[0, 3 * s + 1:3 * s + 2, :]
    g = g_ref[...]
    u_ref[...] = _adaln(h_ref[...], g, shift, scale).astype(bf16)
    uh_ref[...] = _adaln(jnp.concatenate([hp_ref[...], hn_ref[...]], axis=0), g, shift, scale).astype(bf16)
    t = pl.program_id(0) % tiles_per_seq
    row = lax.broadcasted_iota(jnp.int32, (2 * HALO, 1), 0)
    keep_prev = jnp.where(t > 0, 1.0, 0.0)
    keep_next = jnp.where(t < tiles_per_seq - 1, 1.0, 0.0)
    return jnp.where(row < HALO, keep_prev, keep_next)


def _project_padded(pad_ref, u_ref, uh_ref, keep, w):
    tm = u_ref.shape[0]

    def halo_rows():
        halo = _dot(uh_ref[...], w) * keep
        pad_ref[0:HALO, :] = halo[0:HALO, :]
        pad_ref[HALO + tm:2 * HALO + tm, :] = halo[HALO:2 * HALO, :]

    def tile_rows(r0):
        pad_ref[HALO + r0:HALO + r0 + PROJ_TM, :] = _dot(u_ref[r0:r0 + PROJ_TM, :], w)

    return [halo_rows] + [functools.partial(tile_rows, r0) for r0 in range(0, tm, PROJ_TM)]


def _conv_padded(pad_ref, conv_w, conv_b, emit, between=()):
    tm = pad_ref.shape[0] - 2 * HALO
    width = conv_w.shape[0]
    rows = CONV_ROWS
    steps = [(l0, r0) for l0 in range(0, pad_ref.shape[1], LANES) for r0 in range(0, tm, rows)]
    every = max(1, len(steps) // max(1, len(between)))
    pending = list(between)
    half = width // 2
    for n, (l0, r0) in enumerate(steps):
        if pending and n % every == 0:
            pending.pop(0)()
        lanes = slice(l0, l0 + LANES)
        win = pad_ref[HALO + r0 - 8:HALO + r0 + rows + 8, lanes]
        acc = conv_b[:, lanes] + conv_w[half:half + 1, lanes] * win[8:8 + rows, :]
        for k in range(width):
            if k != half:
                acc += conv_w[k:k + 1, lanes] * pltpu.roll(win, (half - k) % (rows + 16), 0)[8:8 + rows, :]
        emit(r0, rows, l0, acc)
    for thunk in pending:
        thunk()


def _hyena_proj_kernel(s, tiles_per_seq, h_ref, hp_ref, hn_ref, mod_ref, g_ref, w_ref, cw_ref, cb_ref,
                       x0e_ref, x0o_ref, vge_ref, vgo_ref, pad_ref, u_ref, uh_ref, x1_ref, split_ref):
    keep = _tile_adaln(s, tiles_per_seq, h_ref, hp_ref, hn_ref, mod_ref, g_ref, u_ref, uh_ref)
    tn = PROJ_TN
    cols = [part * D_MODEL + c0 for c0 in range(0, D_MODEL, tn) for part in range(3)]

    def project(i):
        return _project_padded(pad_ref.at[i % 2], u_ref, uh_ref, keep, w_ref[:, cols[i]:cols[i] + tn])

    for thunk in project(0):
        thunk()
    for i, col in enumerate(cols):
        between = project(i + 1) if i + 1 < len(cols) else []
        c0 = col % D_MODEL

        def split_tokens(value, even_ref, odd_ref, r0, rows, lanes):
            split_ref[...] = value
            dst = slice(r0 // 2, (r0 + rows) // 2)
            even_ref[dst, lanes] = split_ref[pl.ds(0, rows // 2, stride=2), :].astype(bf16)
            odd_ref[dst, lanes] = split_ref[pl.ds(1, rows // 2, stride=2), :].astype(bf16)

        def emit_x0(r0, rows, l0, acc, c0=c0):
            split_tokens(acc, x0e_ref, x0o_ref, r0, rows, slice(c0 + l0, c0 + l0 + LANES))

        def emit_x1(r0, rows, l0, acc):
            x1_ref[r0:r0 + rows, l0:l0 + LANES] = acc

        def emit_v(r0, rows, l0, acc, c0=c0):
            vg = acc * x1_ref[r0:r0 + rows, l0:l0 + LANES]
            split_tokens(vg, vge_ref, vgo_ref, r0, rows, slice(c0 + l0, c0 + l0 + LANES))

        emit = (emit_x0, emit_x1, emit_v)[col // D_MODEL]
        _conv_padded(pad_ref.at[i % 2], cw_ref[:, col:col + tn], cb_ref[:, col:col + tn], emit, between)


def _proj_call(body, name, h, mods, rows_per_mod, seq, g, consts, outs, pad_margin, extra_scratch, tm=512):
    m = h.shape[0]
    tm = min(tm, seq)
    tiles_per_seq = seq // tm
    tiles_per_mod = rows_per_mod // tm
    hb = tm // HALO
    in_specs = [pl.BlockSpec((tm, D_MODEL), lambda i: (i, 0)),
                pl.BlockSpec((HALO, D_MODEL), lambda i: (jnp.maximum(i * hb - 1, 0), 0)),
                pl.BlockSpec((HALO, D_MODEL), lambda i: (jnp.minimum((i + 1) * hb, m // HALO - 1), 0)),
                pl.BlockSpec((1, N_MOD, D_MODEL), lambda i: (i // tiles_per_mod, 0, 0)),
                pl.BlockSpec((1, D_MODEL), lambda i: (0, 0))]
    in_specs += [pl.BlockSpec(a.shape, lambda i: (0, 0)) for a in consts]
    return pl.pallas_call(
        functools.partial(body, 1, tiles_per_seq),
        grid=(m // tm,),
        in_specs=in_specs,
        out_specs=[pl.BlockSpec((tm // div, n), lambda i: (i, 0)) for div, n, _ in outs],
        out_shape=[jax.ShapeDtypeStruct((m // div, n), dt) for div, n, dt in outs],
        scratch_shapes=[pltpu.VMEM((2, tm + 2 * pad_margin, PROJ_TN), f32), pltpu.VMEM((tm, D_MODEL), bf16),
                        pltpu.VMEM((2 * HALO, D_MODEL), bf16)] + extra_scratch(tm),
        compiler_params=_params("parallel"),
        name=name,
    )(h, h, h, mods, g.reshape(1, -1), *consts)


def _sublane_major_perm(tm):
    r = np.arange(tm)
    perm = np.zeros((tm, tm), np.float32)
    perm[r, (r % 8) * (tm // 8) + r // 8] = 1.0
    return jnp.asarray(perm, bf16)


def mixer_proj(h, mods, rows_per_mod, seq, g, w, w_dt, conv_w, conv_b, tm=512):
    tm = min(tm, seq)
    consts = [w, w_dt, conv_w.astype(f32), conv_b.reshape(1, -1).astype(f32), _sublane_major_perm(tm)]
    scratch = lambda tm: [pltpu.VMEM((tm, D_MODEL), bf16), pltpu.VMEM((PROJ_TN // LANES, tm, LANES), f32)]
    return _proj_call(_mixer_proj_kernel, "mixer_proj", h, mods, rows_per_mod, seq, g, consts,
                      [(1, XBC_DIM, bf16), (1, REST_COLS, bf16), (1, LANES, f32)], 8 * (SSD_CONV // 2), scratch, tm)


def hyena_proj(h, mods, rows_per_mod, seq, g, w, conv_w, conv_b, tm=512):
    consts = [w, conv_w.astype(f32), conv_b.reshape(1, -1).astype(f32)]
    scratch = lambda tm: [pltpu.VMEM((tm, PROJ_TN), f32), pltpu.VMEM((CONV_ROWS, LANES), f32)]
    return _proj_call(_hyena_proj_kernel, "hyena_proj", h, mods, rows_per_mod, seq, g, consts,
                      [(2, D_MODEL, bf16)] * 4, HALO, scratch, tm)


def _head_norm(x, gain, bd):
    hi, lo = _split2(x * x)
    ms = _dot(hi, bd) + _dot(lo, bd)
    return x * lax.rsqrt(ms + RMS_EPS) * gain


def _rope(x, cos, sin_signed):
    lane = lax.broadcasted_iota(jnp.int32, x.shape, 1)
    partner = jnp.where((lane & 16) != 0, pltpu.roll(x, 16, 1), pltpu.roll(x, LANES - 16, 1))
    return x * cos + partner * sin_signed


def _t_bf16(x):
    return x.astype(f32).T.astype(bf16)


def _attn_kernel(seq, q_ref, k_ref, v_ref, kc_ref, vc_ref, qg_ref, kg_ref, cos_ref, sin_ref, bd_ref,
                 sink_ref, o_ref, qt_s, k_s, vt_s, kc_s, vct_s):
    j = pl.program_id(1)
    nblk = seq // ATTN_BLOCK
    nslab = ATTN_DIM // LANES
    blk = ATTN_BLOCK

    @pl.when(j == 0)
    def _():
        bd = bd_ref[...]
        cos, sin = cos_ref[...], sin_ref[...]
        scale = HEAD_DIM ** -0.5
        for p in range(nslab):
            qn = _head_norm(q_ref[:, p * LANES:(p + 1) * LANES].astype(f32), qg_ref[...], bd)
            qr = _rope(qn, cos, sin) * scale
            for jb in range(nblk):
                qt_s[jb, p * LANES:(p + 1) * LANES, :] = _t_bf16(qr[jb * blk:(jb + 1) * blk, :])
        kn = _head_norm(k_ref[...].astype(f32), kg_ref[...], bd)
        zeros = jnp.zeros((WINDOW, KV_DIM), bf16)
        k_s[0:WINDOW, :] = zeros
        k_s[WINDOW + seq:2 * WINDOW + seq, :] = zeros
        k_s[WINDOW:WINDOW + seq, :] = _rope(kn, cos, sin).astype(bf16)
        vt_s[0] = zeros
        vt_s[nblk + 1] = zeros
        for jb in range(nblk):
            vt_s[jb + 1] = _t_bf16(v_ref[jb * blk:(jb + 1) * blk, :])
        kc_s[...] = _head_norm(kc_ref[...].astype(f32), kg_ref[...], bd).astype(bf16)
        vct_s[...] = _t_bf16(vc_ref[...])

    band = blk + 2 * WINDOW
    start = pl.multiple_of(j * blk, blk)
    kb = k_s[pl.ds(start, band), :]
    kc = kc_s[...]
    vtb = jnp.concatenate([vt_s[j], vt_s[j + 1], vt_s[j + 2]], axis=1)
    vtc = vct_s[...]
    qt = qt_s[j]
    key = lax.broadcasted_iota(jnp.int32, (blk, 2 * blk), 0)
    qry = lax.broadcasted_iota(jnp.int32, (blk, 2 * blk), 1) & (blk - 1)
    ok_lo = (jnp.abs(qry - (key - WINDOW)) <= WINDOW) & (start - WINDOW + key >= 0)
    ok_hi = (jnp.abs(qry - (key + blk)) <= WINDOW) & (start + blk + key < seq)
    dim = lax.broadcasted_iota(jnp.int32, (LANES, blk), 0)
    lane2 = lax.broadcasted_iota(jnp.int32, (1, 2 * blk), 1)
    ones_b = jnp.ones((16, band), bf16)
    ones_c = jnp.ones((16, kc.shape[0]), bf16)
    def scores(p):
        qslab = qt[p * LANES:(p + 1) * LANES, :]
        zero = jnp.zeros_like(qslab)
        rhs = jnp.concatenate([jnp.where(dim < HEAD_DIM, qslab, zero), jnp.where(dim >= HEAD_DIM, qslab, zero)], axis=1)
        return _dot(kb, rhs), _dot(kc, rhs)

    def softmax(p, sb, sc):
        s_lo = jnp.where(ok_lo, sb[0:blk, :], -jnp.inf)
        s_mid = sb[blk:2 * blk, :]
        s_hi = jnp.where(ok_hi, sb[2 * blk:3 * blk, :], -jnp.inf)
        sink = jnp.where(lane2 < blk, sink_ref[p], sink_ref[p + N_Q_HEADS // N_KV_HEADS])
        colmax = lambda s: jnp.max(s, axis=0, keepdims=True)
        mx = jnp.maximum(jnp.maximum(jnp.maximum(colmax(s_lo), colmax(s_mid)), jnp.maximum(colmax(s_hi), colmax(sc))),
                         sink)
        pb = jnp.concatenate([jnp.exp(s_lo - mx).astype(bf16), jnp.exp(s_mid - mx).astype(bf16),
                              jnp.exp(s_hi - mx).astype(bf16)], axis=0)
        return pb, jnp.exp(sc - mx).astype(bf16), jnp.exp(sink - mx)

    def values(p, pb, pc, sink_term):
        den = (_dot(ones_b, pb) + _dot(ones_c, pc))[0:1, :] + sink_term
        ot = (_dot(vtb, pb) + _dot(vtc, pc)) / den
        both = jnp.where(dim < HEAD_DIM, ot[:, 0:blk], ot[:, blk:2 * blk])
        o_ref[:, p * LANES:(p + 1) * LANES] = both.T.astype(bf16)

    s_next = scores(0)
    for p in range(nslab):
        s_cur = s_next
        if p + 1 < nslab:
            s_next = scores(p + 1)
        values(p, *softmax(p, *s_cur))


def window_attention(proj_lat, proj_ctx, seq, ctx_len, q_gain, k_gain, sink, rope_cos, rope_sin):
    nb = proj_lat.shape[0] // seq
    nblk = seq // ATTN_BLOCK
    bd = np.kron(np.eye(LANES // HEAD_DIM), np.ones((HEAD_DIM, HEAD_DIM))) / HEAD_DIM
    gain2 = lambda g: jnp.tile(g, LANES // HEAD_DIM).reshape(1, LANES)
    const = lambda shape: pl.BlockSpec(shape, lambda b, j: (0, 0))
    return pl.pallas_call(
        functools.partial(_attn_kernel, seq),
        grid=(nb, nblk),
        in_specs=[pl.BlockSpec((seq, ATTN_DIM), lambda b, j: (b, COL_Q // ATTN_DIM)),
                  pl.BlockSpec((seq, KV_DIM), lambda b, j: (b, COL_K // KV_DIM)),
                  pl.BlockSpec((seq, KV_DIM), lambda b, j: (b, COL_V // KV_DIM)),
                  pl.BlockSpec((ctx_len, KV_DIM), lambda b, j: (b, COL_K // KV_DIM)),
                  pl.BlockSpec((ctx_len, KV_DIM), lambda b, j: (b, COL_V // KV_DIM)),
                  const((1, LANES)), const((1, LANES)),
                  const((seq, LANES)), const((seq, LANES)), const((LANES, LANES)),
                  pl.BlockSpec(memory_space=pltpu.SMEM)],
        out_specs=pl.BlockSpec((ATTN_BLOCK, ATTN_DIM), lambda b, j: (b * nblk + j, 0)),
        out_shape=jax.ShapeDtypeStruct((nb * seq, ATTN_DIM), bf16),
        scratch_shapes=[pltpu.VMEM((nblk, ATTN_DIM, ATTN_BLOCK), bf16),
                        pltpu.VMEM((seq + 2 * WINDOW, KV_DIM), bf16),
                        pltpu.VMEM((nblk + 2, KV_DIM, ATTN_BLOCK), bf16),
                        pltpu.VMEM((ctx_len, KV_DIM), bf16),
                        pltpu.VMEM((KV_DIM, ctx_len), bf16)],
        compiler_params=_params("parallel", "arbitrary"),
        name="window_attention",
    )(proj_lat, proj_lat, proj_lat, proj_ctx, proj_ctx, gain2(q_gain), gain2(k_gain),
      rope_cos, rope_sin, jnp.asarray(bd, bf16), sink)


def _rope_tables(seq):
    t = np.arange(seq)
    pos = np.stack([t // GRID_W, t % GRID_W], axis=1).astype(np.float32)
    axis_dim = HEAD_DIM // 2
    inv = (ROPE_THETA ** (-np.arange(0, axis_dim, 2, dtype=np.float32) / axis_dim)).astype(np.float32)
    lane = np.arange(LANES)
    d = lane % HEAD_DIM
    which = d // axis_dim
    ang = (pos[:, which] * inv[d % (axis_dim // 2)][None, :]).astype(np.float32)
    sign = np.where((d % axis_dim) < axis_dim // 2, -1.0, 1.0)
    return jnp.asarray(np.cos(ang), f32), jnp.asarray(np.sin(ang) * sign, f32)


def _softplus(x):
    return jnp.maximum(x, 0.0) + jnp.log1p(jnp.exp(-jnp.abs(x)))


def _expand_heads(v, e):
    return _dot(v.astype(bf16), e)


def _ssd_chunk(rev, lane0, want_y, x, bm, cm, dt_raw, dt_bias, a_neg, expand, state_ref, result):
    t = x.shape[0]
    dt = _softplus(dt_raw + dt_bias)
    a = dt * a_neg
    r = lax.broadcasted_iota(jnp.int32, (t, t), 0)
    c = lax.broadcasted_iota(jnp.int32, (t, t), 1)
    keep = (r <= c) if rev else (r >= c)
    tri = jnp.where(keep, 1.0, 0.0).astype(bf16)
    cs = sum(_dot(tri, part) for part in _split3(a))
    last = cs[0:1, :] if rev else cs[t - 1:t, :]
    e = jnp.exp(cs)
    w = dt * jnp.exp(last - cs)
    e_x = _expand_heads(e, expand)
    w_x = _expand_heads(w, expand)
    elast_x = e_x[0:1, :] if rev else e_x[t - 1:t, :]
    yield

    y = None
    if want_y:
        src_t = (cs - jnp.log(dt)).T
        lane = lax.broadcasted_iota(jnp.int32, (t, LANES), 1)
        cb = [_dot_nt(cm[:, g * SSD_STATE:(g + 1) * SSD_STATE],
                      bm[:, g * SSD_STATE:(g + 1) * SSD_STATE]) for g in range(SSD_GROUPS)]
        yield
        pieces = []
        for p in range(SSD_HEADS // 2):
            xp = x[:, p * LANES:(p + 1) * LANES]
            ms = []
            for q in range(2):
                h = 2 * p + q
                g = h // (SSD_HEADS // SSD_GROUPS)
                seg = cs[:, lane0 + h:lane0 + h + 1] - src_t[lane0 + h:lane0 + h + 1, :]
                ms.append((cb[g] * jnp.exp(jnp.where(keep, seg, -jnp.inf))).astype(bf16))
            zero = jnp.zeros_like(xp)
            xcat = jnp.concatenate([jnp.where(lane < SSD_HEAD_DIM, xp, zero),
                                    jnp.where(lane >= SSD_HEAD_DIM, xp, zero)], axis=0)
            pieces.append(_dot(jnp.concatenate(ms, axis=1), xcat))
            yield
        y = jnp.concatenate(pieces, axis=1)

    inter = []
    for g in range(SSD_GROUPS):
        gs = slice(g * GROUP_W, (g + 1) * GROUP_W)
        ss = slice(g * SSD_STATE, (g + 1) * SSD_STATE)
        h_t = state_ref[g]
        if want_y:
            inter.append(_dot(cm[:, ss], h_t.astype(bf16)) * e_x[:, gs])
        xw = (x[:, gs].astype(f32) * w_x[:, gs]).astype(bf16)
        state_ref[g] = h_t * elast_x[:, gs] + _dot(bm[:, ss].astype(f32).T.astype(bf16), xw)
        yield
    if want_y:
        result.append(y + jnp.concatenate(inter, axis=1))


def _interleave(*stage_generators):
    active = list(stage_generators)
    while active:
        for gen in list(active):
            if next(gen, StopIteration) is StopIteration:
                active.remove(gen)


def _ssd_ctx_kernel(nchunk, x_ref, bc_ref, dt_ref, bias_ref, alog_ref, ef_ref, eb_ref, hf_ref, hb_ref, sf, sb):
    sf[...] = jnp.zeros_like(sf)
    sb[...] = jnp.zeros_like(sb)
    a_neg = -jnp.exp(alog_ref[...])
    bias = bias_ref[...]
    t = SSD_CHUNK
    for ci in range(nchunk):
        rows = slice(ci * t, (ci + 1) * t)
        fwd = _ssd_chunk(False, 0, False, x_ref[rows, :], bc_ref[rows, 0:BC_DIM], None, dt_ref[rows, :],
                         bias, a_neg, ef_ref[...], sf, None)
        rows = slice((nchunk - 1 - ci) * t, (nchunk - ci) * t)
        bwd = _ssd_chunk(True, SSD_HEADS, False, x_ref[rows, :], bc_ref[rows, BC_DIM:2 * BC_DIM], None,
                         dt_ref[rows, :], bias, a_neg, eb_ref[...], sb, None)
        _interleave(fwd, bwd)
    hf_ref[0] = sf[...]
    hb_ref[0] = sb[...]


def _head_expanders():
    ef = np.zeros((LANES, D_SSM), np.float32)
    eb = np.zeros((LANES, D_SSM), np.float32)
    for h in range(SSD_HEADS):
        ef[h, h * SSD_HEAD_DIM:(h + 1) * SSD_HEAD_DIM] = 1.0
        eb[SSD_HEADS + h, h * SSD_HEAD_DIM:(h + 1) * SSD_HEAD_DIM] = 1.0
    return jnp.asarray(ef, bf16), jnp.asarray(eb, bf16)


def _pad_lanes(v):
    v = v.reshape(1, -1).astype(f32)
    return jnp.pad(v, ((0, 0), (0, LANES - v.shape[1])))


def ssd_ctx_states(xc, dt_raw, seq, dt_bias, a_log):
    nb = xc.shape[0] // seq
    ef, eb = _head_expanders()
    const = lambda shape: pl.BlockSpec(shape, lambda b: (0,) * len(shape))
    st = jax.ShapeDtypeStruct((nb, SSD_GROUPS, SSD_STATE, GROUP_W), f32)
    st_spec = pl.BlockSpec((1, SSD_GROUPS, SSD_STATE, GROUP_W), lambda b: (b, 0, 0, 0))
    return pl.pallas_call(
        functools.partial(_ssd_ctx_kernel, seq // SSD_CHUNK),
        grid=(nb,),
        in_specs=[pl.BlockSpec((seq, D_SSM), lambda b: (b, 0)),
                  pl.BlockSpec((seq, 2 * BC_DIM), lambda b: (b, D_SSM // (2 * BC_DIM))),
                  pl.BlockSpec((seq, LANES), lambda b: (b, 0)),
                  const((1, LANES)), const((1, LANES)), const((LANES, D_SSM)), const((LANES, D_SSM))],
        out_specs=[st_spec, st_spec],
        out_shape=[st, st],
        scratch_shapes=[pltpu.VMEM((SSD_GROUPS, SSD_STATE, GROUP_W), f32),
                        pltpu.VMEM((SSD_GROUPS, SSD_STATE, GROUP_W), f32)],
        compiler_params=_params("parallel"),
        name="ssd_ctx_states",
    )(xc, xc, dt_raw, _pad_lanes(dt_bias), _pad_lanes(a_log), ef, eb)


def _ssd_lat_kernel(nchunk, xf_ref, xb_ref, bf_ref, bb_ref, cf_ref, cb_ref, dtf_ref, dtb_ref, zf_ref, zb_ref,
                    hf0_ref, hb0_ref, bias_ref, alog_ref, dskip_ref, normw_ref, ef_ref, eb_ref,
                    o_ref, sf, sb, yacc):
    c = pl.program_id(1)
    t = SSD_CHUNK

    @pl.when(c == 0)
    def _():
        sf[...] = hf0_ref[0]
        sb[...] = hb0_ref[0]

    a_neg = -jnp.exp(alog_ref[...])
    bias = bias_ref[...]
    xf = xf_ref[...]
    yf, yb = [], []
    _interleave(
        _ssd_chunk(False, 0, True, xf, bf_ref[...], cf_ref[...], dtf_ref[...], bias, a_neg, ef_ref[...], sf, yf),
        _ssd_chunk(True, SSD_HEADS, True, xb_ref[...], bb_ref[...], cb_ref[...], dtb_ref[...], bias, a_neg,
                   eb_ref[...], sb, yb))
    yf = yf[0] + dskip_ref[...] * xf.astype(f32)
    yb = yb[0]
    rows_f = pl.ds(pl.multiple_of(c * t, t), t)
    rows_b = pl.ds(pl.multiple_of((nchunk - 1 - c) * t, t), t)

    @pl.when(c < nchunk // 2)
    def _():
        yacc[rows_f, :] = yf
        yacc[rows_b, :] = yb

    def finish(y, z):
        y = y * _silu(z.astype(f32))
        outs = []
        for g in range(SSD_GROUPS):
            yg = y[:, g * GROUP_W:(g + 1) * GROUP_W]
            outs.append(yg * lax.rsqrt(jnp.mean(yg * yg, axis=-1, keepdims=True) + RMS_EPS))
        return (jnp.concatenate(outs, axis=1) * normw_ref[...]).astype(bf16)

    @pl.when(c >= nchunk // 2)
    def _():
        o_ref[rows_f, :] = finish(yacc[rows_f, :] + yf, zf_ref[...])
        o_ref[rows_b, :] = finish(yacc[rows_b, :] + yb, zb_ref[...])


def ssd_latent(xc, proj, dt_raw, seq, hf0, hb0, dt_bias, a_log, d_skip, norm_w):
    nb = xc.shape[0] // seq
    nc = seq // SSD_CHUNK
    half = nc // 2
    ef, eb = _head_expanders()
    t = SSD_CHUNK
    fwd = lambda b, c: b * nc + c
    bwd = lambda b, c: b * nc + nc - 1 - c
    zfw = lambda b, c: b * nc + jnp.maximum(c, half)
    zbw = lambda b, c: b * nc + jnp.minimum(nc - 1 - c, half - 1)
    bc0 = D_SSM // SSD_STATE // SSD_GROUPS
    const = lambda shape: pl.BlockSpec(shape, lambda b, c: (0,) * len(shape))
    st_spec = pl.BlockSpec((1, SSD_GROUPS, SSD_STATE, GROUP_W), lambda b, c: (b, 0, 0, 0))
    dskip = jnp.repeat(d_skip.astype(f32), SSD_HEAD_DIM).reshape(1, D_SSM)
    return pl.pallas_call(
        functools.partial(_ssd_lat_kernel, nc),
        grid=(nb, nc),
        in_specs=[pl.BlockSpec((t, D_SSM), lambda b, c: (fwd(b, c), 0)),
                  pl.BlockSpec((t, D_SSM), lambda b, c: (bwd(b, c), 0)),
                  pl.BlockSpec((t, BC_DIM), lambda b, c: (fwd(b, c), bc0)),
                  pl.BlockSpec((t, BC_DIM), lambda b, c: (bwd(b, c), bc0 + 1)),
                  pl.BlockSpec((t, BC_DIM), lambda b, c: (fwd(b, c), bc0 + 2)),
                  pl.BlockSpec((t, BC_DIM), lambda b, c: (bwd(b, c), bc0 + 3)),
                  pl.BlockSpec((t, LANES), lambda b, c: (fwd(b, c), 0)),
                  pl.BlockSpec((t, LANES), lambda b, c: (bwd(b, c), 0)),
                  pl.BlockSpec((t, D_SSM), lambda b, c: (zfw(b, c), COL_Z // D_SSM)),
                  pl.BlockSpec((t, D_SSM), lambda b, c: (zbw(b, c), COL_Z // D_SSM)),
                  st_spec, st_spec,
                  const((1, LANES)), const((1, LANES)), const((1, D_SSM)), const((1, D_SSM)),
                  const((LANES, D_SSM)), const((LANES, D_SSM))],
        out_specs=pl.BlockSpec((seq, D_SSM), lambda b, c: (b, 0)),
        out_shape=jax.ShapeDtypeStruct((nb * seq, D_SSM), bf16),
        scratch_shapes=[pltpu.VMEM((SSD_GROUPS, SSD_STATE, GROUP_W), f32),
                        pltpu.VMEM((SSD_GROUPS, SSD_STATE, GROUP_W), f32),
                        pltpu.VMEM((seq, D_SSM), f32)],
        compiler_params=_params("parallel", "arbitrary"),
        name="ssd_latent",
    )(xc, xc, xc, xc, xc, xc, dt_raw, dt_raw, proj, proj, hf0, hb0,
      _pad_lanes(dt_bias), _pad_lanes(a_log), dskip, norm_w.reshape(1, -1).astype(f32), ef, eb)


def _filter_kernel(z_ref, w1_ref, b1_ref, wh_ref, bh_ref, freq_ref, wf_ref, wb_ref, delta_ref, ks_ref, kd_ref, h_ref):
    hp = lambda a, b: jnp.dot(a, b, preferred_element_type=f32, precision=lax.Precision.HIGHEST)
    z = z_ref[...]

    @pl.when(pl.program_id(0) == 0)
    def _():
        freq = freq_ref[...]
        h = jnp.sin(freq * (hp(z, w1_ref[...]) + b1_ref[...]))
        for n in range(HYENA_INNER):
            h = jnp.sin(freq * (hp(h, wh_ref[n]) + bh_ref[n]))
        h_ref[...] = h

    h = h_ref[...]
    window = jnp.exp(-z[:, 0:1] * delta_ref[...])
    hf = hp(h, wf_ref[...]) * window
    hb = hp(h, wb_ref[...]) * window
    row = lax.broadcasted_iota(jnp.int32, hb.shape, 0)
    hb = jnp.where(row == 0, 0.0, hb)
    norm = jnp.sum(jnp.abs(hf), axis=0, keepdims=True) + jnp.sum(jnp.abs(hb), axis=0, keepdims=True)
    ks_ref[...] = (hf + hb) / norm
    kd_ref[...] = (hf - hb) / norm


def hyena_filter_taps(seq, f_w1, f_b1, f_wh, f_bh, f_wout, freq, tc=256):
    fw = HYENA_FILTER_WIDTH
    t = np.arange(seq, dtype=np.float32)
    t_norm = t / np.float32(seq - 1)
    bands = np.linspace(1e-4, HYENA_BANDS - 1, HYENA_BANDS, dtype=np.float32)
    ang = np.float32(2.0 * math.pi / seq) * t[:, None] * bands
    z = np.concatenate([t_norm[:, None], np.cos(ang), -np.sin(ang)], axis=-1).astype(np.float32)
    z = np.pad(z, ((0, 0), (0, LANES - z.shape[1])))
    deltas = np.abs(np.linspace(math.log(HYENA_TARGET) / HYENA_SLOW_DECAY, math.log(HYENA_TARGET) / HYENA_FAST_DECAY,
                                D_MODEL, dtype=np.float32)).reshape(1, -1)
    padw = lambda a, r, c: jnp.pad(a.astype(f32), [(0, 0)] * (a.ndim - 2) + [(0, r - a.shape[-2]), (0, c - a.shape[-1])])
    w1 = padw(f_w1, LANES, LANES)
    wh = padw(f_wh, LANES, LANES)
    wout = padw(f_wout, LANES, 2 * D_MODEL)
    b1 = padw(f_b1.reshape(1, fw), 1, LANES)
    bh = padw(f_bh.reshape(HYENA_INNER, 1, fw), 1, LANES)
    fq = padw(freq.reshape(1, fw), 1, LANES)
    nt = D_MODEL // tc
    const = lambda shape: pl.BlockSpec(shape, lambda j: (0,) * len(shape))
    out = jax.ShapeDtypeStruct((seq, D_MODEL), f32)
    osp = pl.BlockSpec((seq, tc), lambda j: (0, j))
    return pl.pallas_call(
        _filter_kernel,
        grid=(nt,),
        in_specs=[const((seq, LANES)), const((LANES, LANES)), const((1, LANES)),
                  const((HYENA_INNER, LANES, LANES)), const((HYENA_INNER, 1, LANES)), const((1, LANES)),
                  pl.BlockSpec((LANES, tc), lambda j: (0, j)),
                  pl.BlockSpec((LANES, tc), lambda j: (0, j + nt)),
                  pl.BlockSpec((1, tc), lambda j: (0, j))],
        out_specs=[osp, osp],
        out_shape=[out, out],
        scratch_shapes=[pltpu.VMEM((seq, LANES), f32)],
        compiler_params=_params("arbitrary"),
        name="hyena_filter",
    )(jnp.asarray(z), w1, b1, wh, bh, fq, wout, wout, jnp.asarray(deltas))


def _dft_tables(seq):
    n = 2 * seq
    half = seq // 2

    def theta(f, t):
        return (2.0 * math.pi / (2 * n)) * (((2 * f[:, None] + 1) * t[None, :]) % (2 * n)).astype(np.float64)

    f_low = np.arange(half, dtype=np.int64)
    order = np.concatenate([f_low, seq - 1 - f_low])
    th_full = theta(order, np.arange(seq, dtype=np.int64))
    tp = np.arange(half, dtype=np.int64)
    th_e, th_o = theta(f_low, 2 * tp), theta(f_low, 2 * tp + 1)
    fwd = [np.cos(th_e), np.cos(th_o), np.sin(th_e), np.sin(th_o)]
    as_bf16 = lambda a: jnp.asarray(a, bf16)
    return (as_bf16(np.cos(th_full)), as_bf16(np.sin(th_full)),
            [as_bf16(a) for a in fwd], [as_bf16(a.T * (2.0 / n)) for a in fwd])


def _spectrum_kernel(c_ref, s_ref, ks_ref, kd_ref, kre_ref, kb_ref):
    kre_ref[...] = _dot(c_ref[...], ks_ref[...].astype(bf16))
    kb_ref[...] = _dot(s_ref[...], kd_ref[...].astype(bf16))


def filter_spectrum(cmat, smat, ks, kd, tmf=256, tc=256):
    seq = ks.shape[0]
    fsp = pl.BlockSpec((tmf, seq), lambda j, m: (m, 0))
    ksp = pl.BlockSpec((seq, tc), lambda j, m: (0, j))
    osp = pl.BlockSpec((tmf, tc), lambda j, m: (m, j))
    out = jax.ShapeDtypeStruct((seq, D_MODEL), f32)
    return pl.pallas_call(
        _spectrum_kernel,
        grid=(D_MODEL // tc, seq // tmf),
        in_specs=[fsp, fsp, ksp, ksp],
        out_specs=[osp, osp],
        out_shape=[out, out],
        compiler_params=_params("parallel", "parallel"),
        name="filter_spectrum",
    )(cmat, smat, ks, kd)


def _dft_fwd_kernel(ce_ref, co_ref, se_ref, so_ref, ve_ref, vo_ref, kre_ref, kb_ref,
                    pp_ref, pm_ref, qp_ref, qm_ref):
    tmf = pp_ref.shape[0]
    half = ce_ref.shape[0]
    low = pl.ds(pl.multiple_of(pl.program_id(1) * tmf, tmf), tmf)
    mirrored = pl.ds(pl.multiple_of(half + pl.program_id(1) * tmf, tmf), tmf)
    ve, vo = ve_ref[...], vo_ref[...]
    ec, oc = _dot(ce_ref[low, :], ve), _dot(co_ref[low, :], vo)
    es, os_ = _dot(se_ref[low, :], ve), _dot(so_ref[low, :], vo)

    def times_filter(vre, va, kre, kb):
        return vre * kre - va * kb, vre * kb + va * kre

    p, q = times_filter(ec + oc, es + os_, kre_ref[low, :], kb_ref[low, :])
    p_m, q_m = times_filter(ec - oc, os_ - es, kre_ref[mirrored, :], kb_ref[mirrored, :])
    pp_ref[...] = (p + p_m).astype(bf16)
    pm_ref[...] = (p - p_m).astype(bf16)
    qp_ref[...] = (q + q_m).astype(bf16)
    qm_ref[...] = (q - q_m).astype(bf16)


def dft_forward(fwd_tables, vg_even, vg_odd, kre, kb, seq, tmf=256):
    half = seq // 2
    nb = vg_even.shape[0] // half
    tmf = min(tmf, half)
    nm = half // tmf
    whole = lambda a: pl.BlockSpec(a.shape, lambda b, m: (0, 0))
    osp = pl.BlockSpec((tmf, D_MODEL), lambda b, m: (b * nm + m, 0))
    out = jax.ShapeDtypeStruct((nb * half, D_MODEL), bf16)
    return pl.pallas_call(
        _dft_fwd_kernel,
        grid=(nb, nm),
        in_specs=[whole(t) for t in fwd_tables]
        + [pl.BlockSpec((half, D_MODEL), lambda b, m: (b, 0)),
           pl.BlockSpec((half, D_MODEL), lambda b, m: (b, 0)),
           whole(kre), whole(kb)],
        out_specs=[osp] * 4,
        out_shape=[out] * 4,
        compiler_params=_params("parallel", "parallel"),
        name="dft_forward",
    )(*fwd_tables, vg_even, vg_odd, kre, kb)


def _dft_inv_kernel(gate_row, cet_ref, cot_ref, set_ref, sot_ref, pp_ref, pm_ref, qp_ref, qm_ref,
                    vge_ref, vgo_ref, x0e_ref, x0o_ref, bias_ref, w_ref, h_ref, mod_ref, o_ref, stage_ref):
    gate = mod_ref[0, gate_row:gate_row + 1, :]
    tmt = vge_ref.shape[0]
    nlane = D_MODEL // LANES
    def finish(y, parity, vg_ref, x0_ref):
        y = y + vg_ref[...].astype(f32) * bias_ref[...]
        upd = gate * _dot((y * x0_ref[...].astype(f32)).astype(bf16), w_ref[...])
        rows = pl.ds(parity, tmt, stride=2)
        for c in range(nlane):
            stage_ref.at[c][rows, :] = upd[:, c * LANES:(c + 1) * LANES]

    mine = pl.ds(pl.multiple_of(pl.program_id(1) * tmt, tmt), tmt)
    finish(_dot(cet_ref[mine, :], pp_ref[...]) + _dot(set_ref[mine, :], qm_ref[...]), 0, vge_ref, x0e_ref)
    finish(_dot(cot_ref[mine, :], pm_ref[...]) + _dot(sot_ref[mine, :], qp_ref[...]), 1, vgo_ref, x0o_ref)
    for c in range(nlane):
        o_ref[:, c * LANES:(c + 1) * LANES] = h_ref[:, c * LANES:(c + 1) * LANES] + stage_ref[c]


def dft_inverse_out(inv_tables, folded, vg_eo, x0_eo, bias, w_out, h, mods, seq, gate_row, tmt=256):
    half = seq // 2
    nb = h.shape[0] // seq
    tmt = min(tmt, half)
    nm = half // tmt
    gsp = pl.BlockSpec((half, half), lambda b, m: (0, 0))
    full = pl.BlockSpec((half, D_MODEL), lambda b, m: (b, 0))
    part = pl.BlockSpec((tmt, D_MODEL), lambda b, m: (b * nm + m, 0))
    tile = pl.BlockSpec((2 * tmt, D_MODEL), lambda b, m: (b * nm + m, 0))
    return pl.pallas_call(
        functools.partial(_dft_inv_kernel, gate_row),
        grid=(nb, nm),
        in_specs=[gsp, gsp, gsp, gsp, full, full, full, full, part, part, part, part,
                  pl.BlockSpec((1, D_MODEL), lambda b, m: (0, 0)),
                  pl.BlockSpec((D_MODEL, D_MODEL), lambda b, m: (0, 0)),
                  tile, pl.BlockSpec((1, N_MOD, D_MODEL), lambda b, m: (b, 0, 0))],
        out_specs=tile,
        out_shape=jax.ShapeDtypeStruct(h.shape, f32),
        scratch_shapes=[pltpu.VMEM((D_MODEL // LANES, 2 * tmt, LANES), f32)],
        compiler_params=_params("parallel", "parallel"),
        name="dft_inverse_out",
    )(*inv_tables, *folded, *vg_eo, *x0_eo, bias.reshape(1, -1).astype(f32), w_out, h, mods)


_Q_HEAD_ORDER = (0, 4, 1, 5, 2, 6, 3, 7)


def _mixer_in_weight(w_in):
    c = np.cumsum([ATTN_DIM, KV_DIM, KV_DIM, D_SSM, XBC_DIM])
    q, k, v, z, xbc, dt = (w_in[:, :c[0]], w_in[:, c[0]:c[1]], w_in[:, c[1]:c[2]], w_in[:, c[2]:c[3]],
                           w_in[:, c[3]:c[4]], w_in[:, c[4]:])
    q = q.reshape(D_MODEL, N_Q_HEADS, HEAD_DIM)[:, np.array(_Q_HEAD_ORDER), :].reshape(D_MODEL, ATTN_DIM)
    w = jnp.concatenate([xbc, z, q, k, v], axis=1).astype(bf16)
    return w, jnp.pad(dt, ((0, 0), (0, LANES - dt.shape[1]))).astype(bf16)


def _mixer_out_weights(w_out):
    wa = w_out[:ATTN_DIM].reshape(N_Q_HEADS, HEAD_DIM, D_MODEL)[np.array(_Q_HEAD_ORDER)].reshape(ATTN_DIM, D_MODEL)
    return wa.astype(bf16), w_out[ATTN_DIM:].astype(bf16)


def kernel(x, c, ctx, c_ctx, w_ada, b_ada, norm_g, ffn_w13, ffn_w2, mix_w_in, mix_w_out, q_norm, k_norm, attn_sink,
           ssd_conv_w, ssd_conv_b, ssd_dt_bias, ssd_a_log, ssd_d, ssd_norm, hy_w_in, hy_conv_w, hy_conv_b,
           hy_f_w1, hy_f_b1, hy_f_wh, hy_f_bh, hy_f_wout, hy_freq, hy_bias, hy_w_out):
    nb, seq, _ = x.shape
    ctx_len = ctx.shape[1]
    depth = w_ada.shape[0]
    assert depth == 2, "this kernel is written for the two-layer block"
    h_lat = x.reshape(nb * seq, D_MODEL)
    h_ctx = ctx.reshape(nb * ctx_len, D_MODEL)
    cond = jnp.concatenate([c, c_ctx[None]], axis=0)
    cond = jnp.pad(cond, ((0, -cond.shape[0] % 8), (0, 0)))
    ffn_w = [[_ffn_weights(ffn_w13[i, k], ffn_w2[i, k]) for k in range(2)] for i in range(depth)]

    mods = ada_mods(cond, w_ada[0], b_ada[0])
    m_lat, m_ctx = mods[:nb], mods[nb:nb + 1]
    h_lat = macaron_ffn(h_lat, m_lat, seq, norm_g[0, 0], 0, *ffn_w[0][0])
    h_ctx = macaron_ffn(h_ctx, m_ctx, nb * ctx_len, norm_g[0, 0], 0, *ffn_w[0][0])
    w_in, w_dt = _mixer_in_weight(mix_w_in[0])
    wa, ws = _mixer_out_weights(mix_w_out[0])
    xc_lat, p_lat, dt_lat = mixer_proj(h_lat, m_lat, seq, seq, norm_g[0, 1], w_in, w_dt, ssd_conv_w[0], ssd_conv_b[0])
    xc_ctx, p_ctx, dt_ctx = mixer_proj(h_ctx, m_ctx, nb * ctx_len, ctx_len, norm_g[0, 1], w_in, w_dt,
                                       ssd_conv_w[0], ssd_conv_b[0])
    cos, sin = _rope_tables(seq)
    a_lat = window_attention(p_lat, p_ctx, seq, ctx_len, q_norm[0], k_norm[0], attn_sink[0], cos, sin)
    hf0, hb0 = ssd_ctx_states(xc_ctx, dt_ctx, ctx_len, ssd_dt_bias[0], ssd_a_log[0])
    s_lat = ssd_latent(xc_lat, p_lat, dt_lat, seq, hf0, hb0, ssd_dt_bias[0], ssd_a_log[0], ssd_d[0], ssd_norm[0])
    h_lat = macaron_ffn(h_lat, m_lat, seq, norm_g[0, 2], 2, *ffn_w[0][1], mix=((a_lat, wa), (s_lat, ws)))

    m_lat = ada_mods(cond, w_ada[1], b_ada[1])[:nb]
    h_lat = macaron_ffn(h_lat, m_lat, seq, norm_g[1, 0], 0, *ffn_w[1][0])
    x0e, x0o, vge, vgo = hyena_proj(h_lat, m_lat, seq, seq, norm_g[1, 1], hy_w_in[0].astype(bf16),
                                    hy_conv_w[0], hy_conv_b[0])
    ks, kd = hyena_filter_taps(seq, hy_f_w1[0], hy_f_b1[0], hy_f_wh[0], hy_f_bh[0], hy_f_wout[0], hy_freq[0])
    c_full, s_full, fwd_tables, inv_tables = _dft_tables(seq)
    kre, kb = filter_spectrum(c_full, s_full, ks, kd)
    folded = dft_forward(fwd_tables, vge, vgo, kre, kb, seq)
    h_lat = dft_inverse_out(inv_tables, folded, (vge, vgo), (x0e, x0o), hy_bias[0], hy_w_out[0].astype(bf16),
                            h_lat, m_lat, seq, 5)
    h_lat = macaron_ffn(h_lat, m_lat, seq, norm_g[1, 2], 2, *ffn_w[1][1])
    return h_lat.reshape(nb, seq, D_MODEL)
```

```python
import functools
import math

import numpy as np
import jax
import jax.numpy as jnp
from jax import lax
from jax.experimental import pallas as pl
from jax.experimental.pallas import tpu as pltpu

f32 = jnp.float32
bf16 = jnp.bfloat16

D_MODEL = 1024
N_MOD = 9
RMS_EPS = 1e-6
GRID_W = 64

HEAD_DIM = 64
N_Q_HEADS = 8
N_KV_HEADS = 2
ATTN_DIM = N_Q_HEADS * HEAD_DIM
KV_DIM = N_KV_HEADS * HEAD_DIM
WINDOW = 128
ATTN_BLOCK = 128
ROPE_THETA = 10000.0

SSD_HEADS = 16
SSD_HEAD_DIM = 64
D_SSM = SSD_HEADS * SSD_HEAD_DIM
SSD_GROUPS = 2
SSD_STATE = 128
SSD_CONV = 7
SSD_CHUNK = 128
BC_DIM = SSD_GROUPS * SSD_STATE
XBC_DIM = D_SSM + 4 * BC_DIM
GROUP_W = D_SSM // SSD_GROUPS

HYENA_SHORT = 3
HYENA_BANDS = 8
HYENA_FILTER_WIDTH = 64
HYENA_INNER = 2
HYENA_FAST_DECAY = 0.3
HYENA_SLOW_DECAY = 1.5
HYENA_TARGET = 1e-2

D_FF = 2816
FFN_TF = 256
LANES = 128

COL_Z = 0
COL_Q = COL_Z + D_SSM
COL_K = COL_Q + ATTN_DIM
COL_V = COL_K + KV_DIM
REST_COLS = COL_V + KV_DIM
HALO = 16
PROJ_TN = 256
PROJ_TM = 256
CONV_ROWS = 64

VMEM_LIMIT = 56 * 1024 * 1024


def _params(*sem):
    return pltpu.CompilerParams(dimension_semantics=sem, vmem_limit_bytes=VMEM_LIMIT)


def _dot(a, b):
    return jnp.dot(a, b, preferred_element_type=f32)


def _dot_nt(a, b):
    return lax.dot_general(a, b, (((1,), (1,)), ((), ())), preferred_element_type=f32)


def _split2(x):
    hi = x.astype(bf16)
    lo = (x - hi.astype(f32)).astype(bf16)
    return hi, lo


def _split3(x):
    hi = x.astype(bf16)
    r = x - hi.astype(f32)
    mid = r.astype(bf16)
    lo = (r - mid.astype(f32)).astype(bf16)
    return hi, mid, lo


def _adaln(h, g, shift, scale):
    ms = jnp.mean(h * h, axis=-1, keepdims=True)
    return (h * lax.rsqrt(ms + RMS_EPS) * g) * (1.0 + scale) + shift


def _silu(x):
    return x * jax.nn.sigmoid(x)


def _mods_kernel(c_ref, w_ref, b_ref, o_ref):
    o_ref[...] = _dot(_silu(c_ref[...]).astype(bf16), w_ref[...].astype(bf16)) + b_ref[...]


def ada_mods(cond, w, b):
    r = cond.shape[0]
    tn = 1024
    out = pl.pallas_call(
        _mods_kernel,
        grid=(w.shape[1] // tn,),
        in_specs=[pl.BlockSpec((r, D_MODEL), lambda j: (0, 0)),
                  pl.BlockSpec((D_MODEL, tn), lambda j: (0, j)),
                  pl.BlockSpec((1, tn), lambda j: (0, j))],
        out_specs=pl.BlockSpec((r, tn), lambda j: (0, j)),
        out_shape=jax.ShapeDtypeStruct((r, w.shape[1]), f32),
        compiler_params=_params("parallel"),
        name="ada_mods",
    )(cond, w, b.reshape(1, -1))
    return out.reshape(r, N_MOD, D_MODEL)


def _ffn_kernel(s, nf, n_mix, h_ref, mod_ref, g_ref, w13_ref, w2_ref, *rest):
    o_ref = rest[-1]
    h = h_ref[...]
    if n_mix:
        mixed = _dot(rest[0][...], rest[n_mix][...])
        for x_ref, w_ref in zip(rest[1:n_mix], rest[n_mix + 1:2 * n_mix]):
            mixed += _dot(x_ref[...], w_ref[...])
        h = h + mod_ref[0, 3 * s - 1:3 * s, :] * mixed
    u = _adaln(h, g_ref[...], mod_ref[0, 3 * s:3 * s + 1, :], mod_ref[0, 3 * s + 1:3 * s + 2, :]).astype(bf16)
    acc = None
    tf = w2_ref.shape[1]
    for j in range(nf):
        a = _dot(u, w13_ref[:, j * tf:(j + 1) * tf])
        b = _dot(u, w13_ref[:, D_FF + j * tf:D_FF + (j + 1) * tf])
        part = _dot((_silu(a) * b).astype(bf16), w2_ref[j])
        acc = part if acc is None else acc + part
    o_ref[...] = h + 0.5 * mod_ref[0, 3 * s + 2:3 * s + 3, :] * acc


def _ffn_weights(w13, w2, tf=FFN_TF):
    return w13.astype(bf16), w2.astype(bf16).reshape(D_FF // tf, tf, D_MODEL)


def macaron_ffn(h, mods, rows_per_mod, g, s, w13c, w2c, mix=(), tm=1024):
    m = h.shape[0]
    tm = min(tm // 2 if mix else tm, rows_per_mod)
    nf = w2c.shape[0]
    tiles_per_mod = rows_per_mod // tm
    assert not mix or s == 2, "the mixer's gate is the modulation row just before the second FFN's"
    xs = [x for x, _ in mix]
    ws = [w for _, w in mix]
    return pl.pallas_call(
        functools.partial(_ffn_kernel, s, nf, len(mix)),
        grid=(m // tm,),
        in_specs=[pl.BlockSpec((tm, D_MODEL), lambda i: (i, 0)),
                  pl.BlockSpec((1, N_MOD, D_MODEL), lambda i: (i // tiles_per_mod, 0, 0)),
                  pl.BlockSpec((1, D_MODEL), lambda i: (0, 0)),
                  pl.BlockSpec(w13c.shape, lambda i: (0, 0)),
                  pl.BlockSpec(w2c.shape, lambda i: (0, 0, 0))]
        + [pl.BlockSpec((tm, x.shape[1]), lambda i: (i, 0)) for x in xs]
        + [pl.BlockSpec(w.shape, lambda i: (0, 0)) for w in ws],
        out_specs=pl.BlockSpec((tm, D_MODEL), lambda i: (i, 0)),
        out_shape=jax.ShapeDtypeStruct((m, D_MODEL), f32),
        compiler_params=_params("parallel"),
        name="macaron_ffn",
    )(h, mods, g.reshape(1, -1), w13c, w2c, *xs, *ws)


def _tile_setup(s, tiles_per_seq, h_ref, hp_ref, hn_ref, mod_ref, g_ref, perm_ref, u_ref, uh_ref, up_ref):
    shift, scale = mod_ref[0, 3 * s:3 * s + 1, :], mod_ref[0, 3 * s + 1:3 * s + 2, :]
    gain = g_ref[...]
    u_ref[...] = _adaln(h_ref[...], gain, shift, scale).astype(bf16)
    uh_ref[...] = _adaln(jnp.concatenate([hp_ref[...], hn_ref[...]], axis=0), gain, shift, scale).astype(bf16)
    up_ref[...] = _dot(perm_ref[...], u_ref[...]).astype(bf16)
    t = pl.program_id(0) % tiles_per_seq
    keep_prev = jnp.where(t > 0, 1.0, 0.0)
    keep_next = jnp.where(t < tiles_per_seq - 1, 1.0, 0.0)
    return jnp.where(lax.broadcasted_iota(jnp.int32, (2 * HALO, 1), 0) < HALO, keep_prev, keep_next)


def _project_permuted(pad, up_ref, uh_ref, keep, w, half):
    tm = up_ref.shape[0]
    grp = tm // 8
    ext = 8 * half
    sub = lax.broadcasted_iota(jnp.int32, (8, w.shape[1]), 0)

    def tile_rows(r0):
        pad[ext + r0:ext + r0 + PROJ_TM, :] = _dot(up_ref[r0:r0 + PROJ_TM, :], w)

    def boundary_tiles():
        halo = _dot(uh_ref[...], w) * keep
        for j in range(half):
            src = ext + 8 * (grp - half + j)
            pad[8 * j:8 * j + 8, :] = jnp.where(sub == 0, halo[HALO - half + j:HALO - half + j + 1, :],
                                                pltpu.roll(pad[src:src + 8, :], 1, 0))
            src = ext + 8 * j
            dst = ext + tm + 8 * j
            pad[dst:dst + 8, :] = jnp.where(sub == 7, halo[HALO + j:HALO + j + 1, :],
                                            pltpu.roll(pad[src:src + 8, :], 7, 0))

    return [functools.partial(tile_rows, r0) for r0 in range(0, tm, PROJ_TM)] + [boundary_tiles]


def _conv_permuted(pad, cw_ref, cb_ref, col0, width, tm, emit, between):
    rows = CONV_ROWS
    steps = [(l0, r0) for l0 in range(0, pad.shape[1], LANES) for r0 in range(0, tm, rows)]
    every = max(1, len(steps) // max(1, len(between)))
    pending = list(between)
    for n, (l0, r0) in enumerate(steps):
        if pending and n % every == 0:
            pending.pop(0)()
        wcol = slice(col0 + l0, col0 + l0 + LANES)
        acc = cb_ref[:, wcol]
        for k in range(width):
            acc = acc + cw_ref[k:k + 1, wcol] * pad[8 * k + r0:8 * k + r0 + rows, l0:l0 + LANES]
        emit(r0, rows, l0, acc)
    for thunk in pending:
        thunk()


def _token_order(stage, r0, rows, grp):
    per = grp // 8
    tiles = [stage[pl.ds(64 * (j % per) + j // per, 8, stride=8), :] for j in range(r0 // 8, (r0 + rows) // 8)]
    return jnp.concatenate(tiles, axis=0)


def _mixer_proj_kernel(s, tiles_per_seq, h_ref, hp_ref, hn_ref, mod_ref, g_ref, w_ref, wdt_ref, cw_ref, cb_ref,
                       perm_ref, xc_ref, rest_ref, dt_ref, pad_ref, u_ref, uh_ref, up_ref, ystage_ref):
    keep = _tile_setup(s, tiles_per_seq, h_ref, hp_ref, hn_ref, mod_ref, g_ref, perm_ref, u_ref, uh_ref, up_ref)
    tm = h_ref.shape[0]
    tn = PROJ_TN
    n_conv = XBC_DIM // tn
    n_rest = REST_COLS // tn

    def project(c):
        return _project_permuted(pad_ref.at[c % 2], up_ref, uh_ref, keep, w_ref[:, c * tn:(c + 1) * tn], SSD_CONV // 2)

    def conv(c, between):
        def emit(r0, rows, l0, acc):
            ystage_ref[l0 // LANES, r0:r0 + rows, :] = _silu(acc)

        _conv_permuted(pad_ref.at[c % 2], cw_ref, cb_ref, c * tn, SSD_CONV, tm, emit, between)
        for l0 in range(0, tn, LANES):
            for r0 in range(0, tm, CONV_ROWS):
                value = _token_order(ystage_ref.at[l0 // LANES], r0, CONV_ROWS, tm // 8)
                xc_ref[r0:r0 + CONV_ROWS, c * tn + l0:c * tn + l0 + LANES] = value.astype(bf16)

    def plain_rows(c, r0):
        rows = slice(r0, r0 + PROJ_TM)
        if c < n_rest:
            w = w_ref[:, XBC_DIM + c * tn:XBC_DIM + (c + 1) * tn]
            rest_ref[rows, c * tn:(c + 1) * tn] = _dot(u_ref[rows, :], w).astype(bf16)
        elif c == n_rest:
            dt_ref[rows, :] = _dot(u_ref[rows, :], wdt_ref[...])

    def plain(c):
        return [functools.partial(plain_rows, c, r0) for r0 in range(0, tm, PROJ_TM)]

    for thunk in project(0):
        thunk()
    for c in range(n_conv):
        conv(c, (project(c + 1) if c + 1 < n_conv else []) + plain(c))
    for c in range(n_conv, n_rest + 1):
        for thunk in plain(c):
            thunk()


def _tile_adaln(s, tiles_per_seq, h_ref, hp_ref, hn_ref, mod_ref, g_ref, u_ref, uh_ref):
    shift, scale = mod_ref[0, 3 * s:3 * s + 1, :], mod_ref[0, 3 * s + 1:3 * s + 2, :]
    g = g_ref[...]
    u_ref[...] = _adaln(h_ref[...], g, shift, scale).astype(bf16)
    uh_ref[...] = _adaln(jnp.concatenate([hp_ref[...], hn_ref[...]], axis=0), g, shift, scale).astype(bf16)
    t = pl.program_id(0) % tiles_per_seq
    row = lax.broadcasted_iota(jnp.int32, (2 * HALO, 1), 0)
    keep_prev = jnp.where(t > 0, 1.0, 0.0)
    keep_next = jnp.where(t < tiles_per_seq - 1, 1.0, 0.0)
    return jnp.where(row < HALO, keep_prev, keep_next)


def _project_padded(pad_ref, u_ref, uh_ref, keep, w):
    tm = u_ref.shape[0]

    def halo_rows():
        halo = _dot(uh_ref[...], w) * keep
        pad_ref[0:HALO, :] = halo[0:HALO, :]
        pad_ref[HALO + tm:2 * HALO + tm, :] = halo[HALO:2 * HALO, :]

    def tile_rows(r0):
        pad_ref[HALO + r0:HALO + r0 + PROJ_TM, :] = _dot(u_ref[r0:r0 + PROJ_TM, :], w)

    return [halo_rows] + [functools.partial(tile_rows, r0) for r0 in range(0, tm, PROJ_TM)]


def _conv_padded(pad_ref, conv_w, conv_b, emit, between=()):
    tm = pad_ref.shape[0] - 2 * HALO
    width = conv_w.shape[0]
    rows = CONV_ROWS
    steps = [(l0, r0) for l0 in range(0, pad_ref.shape[1], LANES) for r0 in range(0, tm, rows)]
    every = max(1, len(steps) // max(1, len(between)))
    pending = list(between)
    half = width // 2
    for n, (l0, r0) in enumerate(steps):
        if pending and n % every == 0:
            pending.pop(0)()
        lanes = slice(l0, l0 + LANES)
        win = pad_ref[HALO + r0 - 8:HALO + r0 + rows + 8, lanes]
        acc = conv_b[:, lanes] + conv_w[half:half + 1, lanes] * win[8:8 + rows, :]
        for k in range(width):
            if k != half:
                acc += conv_w[k:k + 1, lanes] * pltpu.roll(win, (half - k) % (rows + 16), 0)[8:8 + rows, :]
        emit(r0, rows, l0, acc)
    for thunk in pending:
        thunk()


def _hyena_proj_kernel(s, tiles_per_seq, h_ref, hp_ref, hn_ref, mod_ref, g_ref, w_ref, cw_ref, cb_ref,
                       x0e_ref, x0o_ref, vge_ref, vgo_ref, pad_ref, u_ref, uh_ref, x1_ref, split_ref):
    keep = _tile_adaln(s, tiles_per_seq, h_ref, hp_ref, hn_ref, mod_ref, g_ref, u_ref, uh_ref)
    tn = PROJ_TN
    cols = [part * D_MODEL + c0 for c0 in range(0, D_MODEL, tn) for part in range(3)]

    def project(i):
        return _project_padded(pad_ref.at[i % 2], u_ref, uh_ref, keep, w_ref[:, cols[i]:cols[i] + tn])

    for thunk in project(0):
        thunk()
    for i, col in enumerate(cols):
        between = project(i + 1) if i + 1 < len(cols) else []
        c0 = col % D_MODEL

        def split_tokens(value, even_ref, odd_ref, r0, rows, lanes):
            split_ref[...] = value
            dst = slice(r0 // 2, (r0 + rows) // 2)
            even_ref[dst, lanes] = split_ref[pl.ds(0, rows // 2, stride=2), :].astype(bf16)
            odd_ref[dst, lanes] = split_ref[pl.ds(1, rows // 2, stride=2), :].astype(bf16)

        def emit_x0(r0, rows, l0, acc, c0=c0):
            split_tokens(acc, x0e_ref, x0o_ref, r0, rows, slice(c0 + l0, c0 + l0 + LANES))

        def emit_x1(r0, rows, l0, acc):
            x1_ref[r0:r0 + rows, l0:l0 + LANES] = acc

        def emit_v(r0, rows, l0, acc, c0=c0):
            vg = acc * x1_ref[r0:r0 + rows, l0:l0 + LANES]
            split_tokens(vg, vge_ref, vgo_ref, r0, rows, slice(c0 + l0, c0 + l0 + LANES))

        emit = (emit_x0, emit_x1, emit_v)[col // D_MODEL]
        _conv_padded(pad_ref.at[i % 2], cw_ref[:, col:col + tn], cb_ref[:, col:col + tn], emit, between)


def _proj_call(body, name, h, mods, rows_per_mod, seq, g, consts, outs, pad_margin, extra_scratch, tm=512):
    m = h.shape[0]
    tm = min(tm, seq)
    tiles_per_seq = seq // tm
    tiles_per_mod = rows_per_mod // tm
    hb = tm // HALO
    in_specs = [pl.BlockSpec((tm, D_MODEL), lambda i: (i, 0)),
                pl.BlockSpec((HALO, D_MODEL), lambda i: (jnp.maximum(i * hb - 1, 0), 0)),
                pl.BlockSpec((HALO, D_MODEL), lambda i: (jnp.minimum((i + 1) * hb, m // HALO - 1), 0)),
                pl.BlockSpec((1, N_MOD, D_MODEL), lambda i: (i // tiles_per_mod, 0, 0)),
                pl.BlockSpec((1, D_MODEL), lambda i: (0, 0))]
    in_specs += [pl.BlockSpec(a.shape, lambda i: (0, 0)) for a in consts]
    return pl.pallas_call(
        functools.partial(body, 1, tiles_per_seq),
        grid=(m // tm,),
        in_specs=in_specs,
        out_specs=[pl.BlockSpec((tm // div, n), lambda i: (i, 0)) for div, n, _ in outs],
        out_shape=[jax.ShapeDtypeStruct((m // div, n), dt) for div, n, dt in outs],
        scratch_shapes=[pltpu.VMEM((2, tm + 2 * pad_margin, PROJ_TN), f32), pltpu.VMEM((tm, D_MODEL), bf16),
                        pltpu.VMEM((2 * HALO, D_MODEL), bf16)] + extra_scratch(tm),
        compiler_params=_params("parallel"),
        name=name,
    )(h, h, h, mods, g.reshape(1, -1), *consts)


def _sublane_major_perm(tm):
    r = np.arange(tm)
    perm = np.zeros((tm, tm), np.float32)
    perm[r, (r % 8) * (tm // 8) + r // 8] = 1.0
    return jnp.asarray(perm, bf16)


def mixer_proj(h, mods, rows_per_mod, seq, g, w, w_dt, conv_w, conv_b, tm=512):
    tm = min(tm, seq)
    consts = [w, w_dt, conv_w.astype(f32), conv_b.reshape(1, -1).astype(f32), _sublane_major_perm(tm)]
    scratch = lambda tm: [pltpu.VMEM((tm, D_MODEL), bf16), pltpu.VMEM((PROJ_TN // LANES, tm, LANES), f32)]
    return _proj_call(_mixer_proj_kernel, "mixer_proj", h, mods, rows_per_mod, seq, g, consts,
                      [(1, XBC_DIM, bf16), (1, REST_COLS, bf16), (1, LANES, f32)], 8 * (SSD_CONV // 2), scratch, tm)


def hyena_proj(h, mods, rows_per_mod, seq, g, w, conv_w, conv_b, tm=1024):
    consts = [w, conv_w.astype(f32), conv_b.reshape(1, -1).astype(f32)]
    scratch = lambda tm: [pltpu.VMEM((tm, PROJ_TN), f32), pltpu.VMEM((CONV_ROWS, LANES), f32)]
    return _proj_call(_hyena_proj_kernel, "hyena_proj", h, mods, rows_per_mod, seq, g, consts,
                      [(2, D_MODEL, bf16)] * 4, HALO, scratch, tm)


def _head_norm(x, gain, bd):
    hi, lo = _split2(x * x)
    ms = _dot(hi, bd) + _dot(lo, bd)
    return x * lax.rsqrt(ms + RMS_EPS) * gain


def _rope(x, cos, sin_signed):
    lane = lax.broadcasted_iota(jnp.int32, x.shape, 1)
    partner = jnp.where((lane & 16) != 0, pltpu.roll(x, 16, 1), pltpu.roll(x, LANES - 16, 1))
    return x * cos + partner * sin_signed


def _t_bf16(x):
    return x.astype(f32).T.astype(bf16)


def _attn_kernel(seq, q_ref, k_ref, v_ref, kc_ref, vc_ref, qg_ref, kg_ref, cos_ref, sin_ref, bd_ref,
                 sink_ref, o_ref, qt_s, k_s, vt_s, kc_s, vct_s):
    j = pl.program_id(1)
    nblk = seq // ATTN_BLOCK
    nslab = ATTN_DIM // LANES
    blk = ATTN_BLOCK

    @pl.when(j == 0)
    def _():
        bd = bd_ref[...]
        cos, sin = cos_ref[...], sin_ref[...]
        scale = HEAD_DIM ** -0.5
        for p in range(nslab):
            qn = _head_norm(q_ref[:, p * LANES:(p + 1) * LANES].astype(f32), qg_ref[...], bd)
            qr = _rope(qn, cos, sin) * scale
            for jb in range(nblk):
                qt_s[jb, p * LANES:(p + 1) * LANES, :] = _t_bf16(qr[jb * blk:(jb + 1) * blk, :])
        kn = _head_norm(k_ref[...].astype(f32), kg_ref[...], bd)
        zeros = jnp.zeros((WINDOW, KV_DIM), bf16)
        k_s[0:WINDOW, :] = zeros
        k_s[WINDOW + seq:2 * WINDOW + seq, :] = zeros
        k_s[WINDOW:WINDOW + seq, :] = _rope(kn, cos, sin).astype(bf16)
        vt_s[0] = zeros
        vt_s[nblk + 1] = zeros
        for jb in range(nblk):
            vt_s[jb + 1] = _t_bf16(v_ref[jb * blk:(jb + 1) * blk, :])
        kc_s[...] = _head_norm(kc_ref[...].astype(f32), kg_ref[...], bd).astype(bf16)
        vct_s[...] = _t_bf16(vc_ref[...])

    band = blk + 2 * WINDOW
    start = pl.multiple_of(j * blk, blk)
    kb = k_s[pl.ds(start, band), :]
    kc = kc_s[...]
    vtb = jnp.concatenate([vt_s[j], vt_s[j + 1], vt_s[j + 2]], axis=1)
    vtc = vct_s[...]
    qt = qt_s[j]
    key = lax.broadcasted_iota(jnp.int32, (blk, 2 * blk), 0)
    qry = lax.broadcasted_iota(jnp.int32, (blk, 2 * blk), 1) & (blk - 1)
    ok_lo = (jnp.abs(qry - (key - WINDOW)) <= WINDOW) & (start - WINDOW + key >= 0)
    ok_hi = (jnp.abs(qry - (key + blk)) <= WINDOW) & (start + blk + key < seq)
    dim = lax.broadcasted_iota(jnp.int32, (LANES, blk), 0)
    lane2 = lax.broadcasted_iota(jnp.int32, (1, 2 * blk), 1)
    ones_b = jnp.ones((16, band), bf16)
    ones_c = jnp.ones((16, kc.shape[0]), bf16)
    def scores(p):
        qslab = qt[p * LANES:(p + 1) * LANES, :]
        zero = jnp.zeros_like(qslab)
        rhs = jnp.concatenate([jnp.where(dim < HEAD_DIM, qslab, zero), jnp.where(dim >= HEAD_DIM, qslab, zero)], axis=1)
        return _dot(kb, rhs), _dot(kc, rhs)

    def softmax(p, sb, sc):
        s_lo = jnp.where(ok_lo, sb[0:blk, :], -jnp.inf)
        s_mid = sb[blk:2 * blk, :]
        s_hi = jnp.where(ok_hi, sb[2 * blk:3 * blk, :], -jnp.inf)
        sink = jnp.where(lane2 < blk, sink_ref[p], sink_ref[p + N_Q_HEADS // N_KV_HEADS])
        colmax = lambda s: jnp.max(s, axis=0, keepdims=True)
        mx = jnp.maximum(jnp.maximum(jnp.maximum(colmax(s_lo), colmax(s_mid)), jnp.maximum(colmax(s_hi), colmax(sc))),
                         sink)
        pb = jnp.concatenate([jnp.exp(s_lo - mx).astype(bf16), jnp.exp(s_mid - mx).astype(bf16),
                              jnp.exp(s_hi - mx).astype(bf16)], axis=0)
        return pb, jnp.exp(sc - mx).astype(bf16), jnp.exp(sink - mx)

    def values(p, pb, pc, sink_term):
        den = (_dot(ones_b, pb) + _dot(ones_c, pc))[0:1, :] + sink_term
        ot = (_dot(vtb, pb) + _dot(vtc, pc)) / den
        both = jnp.where(dim < HEAD_DIM, ot[:, 0:blk], ot[:, blk:2 * blk])
        o_ref[:, p * LANES:(p + 1) * LANES] = both.T.astype(bf16)

    s_next = scores(0)
    for p in range(nslab):
        s_cur = s_next
        if p + 1 < nslab:
            s_next = scores(p + 1)
        values(p, *softmax(p, *s_cur))


def window_attention(proj_lat, proj_ctx, seq, ctx_len, q_gain, k_gain, sink, rope_cos, rope_sin):
    nb = proj_lat.shape[0] // seq
    nblk = seq // ATTN_BLOCK
    bd = np.kron(np.eye(LANES // HEAD_DIM), np.ones((HEAD_DIM, HEAD_DIM))) / HEAD_DIM
    gain2 = lambda g: jnp.tile(g, LANES // HEAD_DIM).reshape(1, LANES)
    const = lambda shape: pl.BlockSpec(shape, lambda b, j: (0, 0))
    return pl.pallas_call(
        functools.partial(_attn_kernel, seq),
        grid=(nb, nblk),
        in_specs=[pl.BlockSpec((seq, ATTN_DIM), lambda b, j: (b, COL_Q // ATTN_DIM)),
                  pl.BlockSpec((seq, KV_DIM), lambda b, j: (b, COL_K // KV_DIM)),
                  pl.BlockSpec((seq, KV_DIM), lambda b, j: (b, COL_V // KV_DIM)),
                  pl.BlockSpec((ctx_len, KV_DIM), lambda b, j: (b, COL_K // KV_DIM)),
                  pl.BlockSpec((ctx_len, KV_DIM), lambda b, j: (b, COL_V // KV_DIM)),
                  const((1, LANES)), const((1, LANES)),
                  const((seq, LANES)), const((seq, LANES)), const((LANES, LANES)),
                  pl.BlockSpec(memory_space=pltpu.SMEM)],
        out_specs=pl.BlockSpec((ATTN_BLOCK, ATTN_DIM), lambda b, j: (b * nblk + j, 0)),
        out_shape=jax.ShapeDtypeStruct((nb * seq, ATTN_DIM), bf16),
        scratch_shapes=[pltpu.VMEM((nblk, ATTN_DIM, ATTN_BLOCK), bf16),
                        pltpu.VMEM((seq + 2 * WINDOW, KV_DIM), bf16),
                        pltpu.VMEM((nblk + 2, KV_DIM, ATTN_BLOCK), bf16),
                        pltpu.VMEM((ctx_len, KV_DIM), bf16),
                        pltpu.VMEM((KV_DIM, ctx_len), bf16)],
        compiler_params=_params("parallel", "arbitrary"),
        name="window_attention",
    )(proj_lat, proj_lat, proj_lat, proj_ctx, proj_ctx, gain2(q_gain), gain2(k_gain),
      rope_cos, rope_sin, jnp.asarray(bd, bf16), sink)


def _rope_tables(seq):
    t = np.arange(seq)
    pos = np.stack([t // GRID_W, t % GRID_W], axis=1).astype(np.float32)
    axis_dim = HEAD_DIM // 2
    inv = (ROPE_THETA ** (-np.arange(0, axis_dim, 2, dtype=np.float32) / axis_dim)).astype(np.float32)
    lane = np.arange(LANES)
    d = lane % HEAD_DIM
    which = d // axis_dim
    ang = (pos[:, which] * inv[d % (axis_dim // 2)][None, :]).astype(np.float32)
    sign = np.where((d % axis_dim) < axis_dim // 2, -1.0, 1.0)
    return jnp.asarray(np.cos(ang), f32), jnp.asarray(np.sin(ang) * sign, f32)


def _softplus(x):
    return jnp.maximum(x, 0.0) + jnp.log1p(jnp.exp(-jnp.abs(x)))


def _expand_heads(v, e):
    return _dot(v.astype(bf16), e)


def _ssd_chunk(rev, lane0, want_y, x, bm, cm, dt_raw, dt_bias, a_neg, expand, state_ref, result):
    t = x.shape[0]
    dt = _softplus(dt_raw + dt_bias)
    a = dt * a_neg
    r = lax.broadcasted_iota(jnp.int32, (t, t), 0)
    c = lax.broadcasted_iota(jnp.int32, (t, t), 1)
    keep = (r <= c) if rev else (r >= c)
    tri = jnp.where(keep, 1.0, 0.0).astype(bf16)
    cs = sum(_dot(tri, part) for part in _split3(a))
    last = cs[0:1, :] if rev else cs[t - 1:t, :]
    e = jnp.exp(cs)
    w = dt * jnp.exp(last - cs)
    e_x = _expand_heads(e, expand)
    w_x = _expand_heads(w, expand)
    elast_x = e_x[0:1, :] if rev else e_x[t - 1:t, :]
    yield

    y = None
    if want_y:
        src_t = (cs - jnp.log(dt)).T
        lane = lax.broadcasted_iota(jnp.int32, (t, LANES), 1)
        cb = [_dot_nt(cm[:, g * SSD_STATE:(g + 1) * SSD_STATE],
                      bm[:, g * SSD_STATE:(g + 1) * SSD_STATE]) for g in range(SSD_GROUPS)]
        yield
        pieces = []
        for p in range(SSD_HEADS // 2):
            xp = x[:, p * LANES:(p + 1) * LANES]
            ms = []
            for q in range(2):
                h = 2 * p + q
                g = h // (SSD_HEADS // SSD_GROUPS)
                seg = cs[:, lane0 + h:lane0 + h + 1] - src_t[lane0 + h:lane0 + h + 1, :]
                ms.append((cb[g] * jnp.exp(jnp.where(keep, seg, -jnp.inf))).astype(bf16))
            zero = jnp.zeros_like(xp)
            xcat = jnp.concatenate([jnp.where(lane < SSD_HEAD_DIM, xp, zero),
                                    jnp.where(lane >= SSD_HEAD_DIM, xp, zero)], axis=0)
            pieces.append(_dot(jnp.concatenate(ms, axis=1), xcat))
            yield
        y = jnp.concatenate(pieces, axis=1)

    inter = []
    for g in range(SSD_GROUPS):
        gs = slice(g * GROUP_W, (g + 1) * GROUP_W)
        ss = slice(g * SSD_STATE, (g + 1) * SSD_STATE)
        h_t = state_ref[g]
        if want_y:
            inter.append(_dot(cm[:, ss], h_t.astype(bf16)) * e_x[:, gs])
        xw = (x[:, gs].astype(f32) * w_x[:, gs]).astype(bf16)
        state_ref[g] = h_t * elast_x[:, gs] + _dot(bm[:, ss].astype(f32).T.astype(bf16), xw)
        yield
    if want_y:
        result.append(y + jnp.concatenate(inter, axis=1))


def _interleave(*stage_generators):
    active = list(stage_generators)
    while active:
        for gen in list(active):
            if next(gen, StopIteration) is StopIteration:
                active.remove(gen)


def _ssd_ctx_kernel(nchunk, x_ref, bc_ref, dt_ref, bias_ref, alog_ref, ef_ref, eb_ref, hf_ref, hb_ref, sf, sb):
    sf[...] = jnp.zeros_like(sf)
    sb[...] = jnp.zeros_like(sb)
    a_neg = -jnp.exp(alog_ref[...])
    bias = bias_ref[...]
    t = SSD_CHUNK
    for ci in range(nchunk):
        rows = slice(ci * t, (ci + 1) * t)
        fwd = _ssd_chunk(False, 0, False, x_ref[rows, :], bc_ref[rows, 0:BC_DIM], None, dt_ref[rows, :],
                         bias, a_neg, ef_ref[...], sf, None)
        rows = slice((nchunk - 1 - ci) * t, (nchunk - ci) * t)
        bwd = _ssd_chunk(True, SSD_HEADS, False, x_ref[rows, :], bc_ref[rows, BC_DIM:2 * BC_DIM], None,
                         dt_ref[rows, :], bias, a_neg, eb_ref[...], sb, None)
        _interleave(fwd, bwd)
    hf_ref[0] = sf[...]
    hb_ref[0] = sb[...]


def _head_expanders():
    ef = np.zeros((LANES, D_SSM), np.float32)
    eb = np.zeros((LANES, D_SSM), np.float32)
    for h in range(SSD_HEADS):
        ef[h, h * SSD_HEAD_DIM:(h + 1) * SSD_HEAD_DIM] = 1.0
        eb[SSD_HEADS + h, h * SSD_HEAD_DIM:(h + 1) * SSD_HEAD_DIM] = 1.0
    return jnp.asarray(ef, bf16), jnp.asarray(eb, bf16)


def _pad_lanes(v):
    v = v.reshape(1, -1).astype(f32)
    return jnp.pad(v, ((0, 0), (0, LANES - v.shape[1])))


def ssd_ctx_states(xc, dt_raw, seq, dt_bias, a_log):
    nb = xc.shape[0] // seq
    ef, eb = _head_expanders()
    const = lambda shape: pl.BlockSpec(shape, lambda b: (0,) * len(shape))
    st = jax.ShapeDtypeStruct((nb, SSD_GROUPS, SSD_STATE, GROUP_W), f32)
    st_spec = pl.BlockSpec((1, SSD_GROUPS, SSD_STATE, GROUP_W), lambda b: (b, 0, 0, 0))
    return pl.pallas_call(
        functools.partial(_ssd_ctx_kernel, seq // SSD_CHUNK),
        grid=(nb,),
        in_specs=[pl.BlockSpec((seq, D_SSM), lambda b: (b, 0)),
                  pl.BlockSpec((seq, 2 * BC_DIM), lambda b: (b, D_SSM // (2 * BC_DIM))),
                  pl.BlockSpec((seq, LANES), lambda b: (b, 0)),
                  const((1, LANES)), const((1, LANES)), const((LANES, D_SSM)), const((LANES, D_SSM))],
        out_specs=[st_spec, st_spec],
        out_shape=[st, st],
        scratch_shapes=[pltpu.VMEM((SSD_GROUPS, SSD_STATE, GROUP_W), f32),
                        pltpu.VMEM((SSD_GROUPS, SSD_STATE, GROUP_W), f32)],
        compiler_params=_params("parallel"),
        name="ssd_ctx_states",
    )(xc, xc, dt_raw, _pad_lanes(dt_bias), _pad_lanes(a_log), ef, eb)


def _ssd_lat_kernel(nchunk, xf_ref, xb_ref, bf_ref, bb_ref, cf_ref, cb_ref, dtf_ref, dtb_ref, zf_ref, zb_ref,
                    hf0_ref, hb0_ref, bias_ref, alog_ref, dskip_ref, normw_ref, ef_ref, eb_ref,
                    o_ref, sf, sb, yacc):
    c = pl.program_id(1)
    t = SSD_CHUNK

    @pl.when(c == 0)
    def _():
        sf[...] = hf0_ref[0]
        sb[...] = hb0_ref[0]

    a_neg = -jnp.exp(alog_ref[...])
    bias = bias_ref[...]
    xf = xf_ref[...]
    yf, yb = [], []
    _interleave(
        _ssd_chunk(False, 0, True, xf, bf_ref[...], cf_ref[...], dtf_ref[...], bias, a_neg, ef_ref[...], sf, yf),
        _ssd_chunk(True, SSD_HEADS, True, xb_ref[...], bb_ref[...], cb_ref[...], dtb_ref[...], bias, a_neg,
                   eb_ref[...], sb, yb))
    yf = yf[0] + dskip_ref[...] * xf.astype(f32)
    yb = yb[0]
    rows_f = pl.ds(pl.multiple_of(c * t, t), t)
    rows_b = pl.ds(pl.multiple_of((nchunk - 1 - c) * t, t), t)

    @pl.when(c < nchunk // 2)
    def _():
        yacc[rows_f, :] = yf
        yacc[rows_b, :] = yb

    def finish(y, z):
        y = y * _silu(z.astype(f32))
        outs = []
        for g in range(SSD_GROUPS):
            yg = y[:, g * GROUP_W:(g + 1) * GROUP_W]
            outs.append(yg * lax.rsqrt(jnp.mean(yg * yg, axis=-1, keepdims=True) + RMS_EPS))
        return (jnp.concatenate(outs, axis=1) * normw_ref[...]).astype(bf16)

    @pl.when(c >= nchunk // 2)
    def _():
        o_ref[rows_f, :] = finish(yacc[rows_f, :] + yf, zf_ref[...])
        o_ref[rows_b, :] = finish(yacc[rows_b, :] + yb, zb_ref[...])


def ssd_latent(xc, proj, dt_raw, seq, hf0, hb0, dt_bias, a_log, d_skip, norm_w):
    nb = xc.shape[0] // seq
    nc = seq // SSD_CHUNK
    half = nc // 2
    ef, eb = _head_expanders()
    t = SSD_CHUNK
    fwd = lambda b, c: b * nc + c
    bwd = lambda b, c: b * nc + nc - 1 - c
    zfw = lambda b, c: b * nc + jnp.maximum(c, half)
    zbw = lambda b, c: b * nc + jnp.minimum(nc - 1 - c, half - 1)
    bc0 = D_SSM // SSD_STATE // SSD_GROUPS
    const = lambda shape: pl.BlockSpec(shape, lambda b, c: (0,) * len(shape))
    st_spec = pl.BlockSpec((1, SSD_GROUPS, SSD_STATE, GROUP_W), lambda b, c: (b, 0, 0, 0))
    dskip = jnp.repeat(d_skip.astype(f32), SSD_HEAD_DIM).reshape(1, D_SSM)
    return pl.pallas_call(
        functools.partial(_ssd_lat_kernel, nc),
        grid=(nb, nc),
        in_specs=[pl.BlockSpec((t, D_SSM), lambda b, c: (fwd(b, c), 0)),
                  pl.BlockSpec((t, D_SSM), lambda b, c: (bwd(b, c), 0)),
                  pl.BlockSpec((t, BC_DIM), lambda b, c: (fwd(b, c), bc0)),
                  pl.BlockSpec((t, BC_DIM), lambda b, c: (bwd(b, c), bc0 + 1)),
                  pl.BlockSpec((t, BC_DIM), lambda b, c: (fwd(b, c), bc0 + 2)),
                  pl.BlockSpec((t, BC_DIM), lambda b, c: (bwd(b, c), bc0 + 3)),
                  pl.BlockSpec((t, LANES), lambda b, c: (fwd(b, c), 0)),
                  pl.BlockSpec((t, LANES), lambda b, c: (bwd(b, c), 0)),
                  pl.BlockSpec((t, D_SSM), lambda b, c: (zfw(b, c), COL_Z // D_SSM)),
                  pl.BlockSpec((t, D_SSM), lambda b, c: (zbw(b, c), COL_Z // D_SSM)),
                  st_spec, st_spec,
                  const((1, LANES)), const((1, LANES)), const((1, D_SSM)), const((1, D_SSM)),
                  const((LANES, D_SSM)), const((LANES, D_SSM))],
        out_specs=pl.BlockSpec((seq, D_SSM), lambda b, c: (b, 0)),
        out_shape=jax.ShapeDtypeStruct((nb * seq, D_SSM), bf16),
        scratch_shapes=[pltpu.VMEM((SSD_GROUPS, SSD_STATE, GROUP_W), f32),
                        pltpu.VMEM((SSD_GROUPS, SSD_STATE, GROUP_W), f32),
                        pltpu.VMEM((seq, D_SSM), f32)],
        compiler_params=_params("parallel", "arbitrary"),
        name="ssd_latent",
    )(xc, xc, xc, xc, xc, xc, dt_raw, dt_raw, proj, proj, hf0, hb0,
      _pad_lanes(dt_bias), _pad_lanes(a_log), dskip, norm_w.reshape(1, -1).astype(f32), ef, eb)


def _filter_kernel(z_ref, w1_ref, b1_ref, wh_ref, bh_ref, freq_ref, wf_ref, wb_ref, delta_ref, ks_ref, kd_ref, h_ref):
    hp = lambda a, b: jnp.dot(a, b, preferred_element_type=f32, precision=lax.Precision.HIGHEST)
    z = z_ref[...]

    @pl.when(pl.program_id(0) == 0)
    def _():
        freq = freq_ref[...]
        h = jnp.sin(freq * (hp(z, w1_ref[...]) + b1_ref[...]))
        for n in range(HYENA_INNER):
            h = jnp.sin(freq * (hp(h, wh_ref[n]) + bh_ref[n]))
        h_ref[...] = h

    h = h_ref[...]
    window = jnp.exp(-z[:, 0:1] * delta_ref[...])
    hf = hp(h, wf_ref[...]) * window
    hb = hp(h, wb_ref[...]) * window
    row = lax.broadcasted_iota(jnp.int32, hb.shape, 0)
    hb = jnp.where(row == 0, 0.0, hb)
    norm = jnp.sum(jnp.abs(hf), axis=0, keepdims=True) + jnp.sum(jnp.abs(hb), axis=0, keepdims=True)
    ks_ref[...] = (hf + hb) / norm
    kd_ref[...] = (hf - hb) / norm


def hyena_filter_taps(seq, f_w1, f_b1, f_wh, f_bh, f_wout, freq, tc=256):
    fw = HYENA_FILTER_WIDTH
    t = np.arange(seq, dtype=np.float32)
    t_norm = t / np.float32(seq - 1)
    bands = np.linspace(1e-4, HYENA_BANDS - 1, HYENA_BANDS, dtype=np.float32)
    ang = np.float32(2.0 * math.pi / seq) * t[:, None] * bands
    z = np.concatenate([t_norm[:, None], np.cos(ang), -np.sin(ang)], axis=-1).astype(np.float32)
    z = np.pad(z, ((0, 0), (0, LANES - z.shape[1])))
    deltas = np.abs(np.linspace(math.log(HYENA_TARGET) / HYENA_SLOW_DECAY, math.log(HYENA_TARGET) / HYENA_FAST_DECAY,
                                D_MODEL, dtype=np.float32)).reshape(1, -1)
    padw = lambda a, r, c: jnp.pad(a.astype(f32), [(0, 0)] * (a.ndim - 2) + [(0, r - a.shape[-2]), (0, c - a.shape[-1])])
    w1 = padw(f_w1, LANES, LANES)
    wh = padw(f_wh, LANES, LANES)
    wout = padw(f_wout, LANES, 2 * D_MODEL)
    b1 = padw(f_b1.reshape(1, fw), 1, LANES)
    bh = padw(f_bh.reshape(HYENA_INNER, 1, fw), 1, LANES)
    fq = padw(freq.reshape(1, fw), 1, LANES)
    nt = D_MODEL // tc
    const = lambda shape: pl.BlockSpec(shape, lambda j: (0,) * len(shape))
    out = jax.ShapeDtypeStruct((seq, D_MODEL), f32)
    osp = pl.BlockSpec((seq, tc), lambda j: (0, j))
    return pl.pallas_call(
        _filter_kernel,
        grid=(nt,),
        in_specs=[const((seq, LANES)), const((LANES, LANES)), const((1, LANES)),
                  const((HYENA_INNER, LANES, LANES)), const((HYENA_INNER, 1, LANES)), const((1, LANES)),
                  pl.BlockSpec((LANES, tc), lambda j: (0, j)),
                  pl.BlockSpec((LANES, tc), lambda j: (0, j + nt)),
                  pl.BlockSpec((1, tc), lambda j: (0, j))],
        out_specs=[osp, osp],
        out_shape=[out, out],
        scratch_shapes=[pltpu.VMEM((seq, LANES), f32)],
        compiler_params=_params("arbitrary"),
        name="hyena_filter",
    )(jnp.asarray(z), w1, b1, wh, bh, fq, wout, wout, jnp.asarray(deltas))


def _dft_tables(seq):
    n = 2 * seq
    half = seq // 2

    def theta(f, t):
        return (2.0 * math.pi / (2 * n)) * (((2 * f[:, None] + 1) * t[None, :]) % (2 * n)).astype(np.float64)

    f_low = np.arange(half, dtype=np.int64)
    order = np.concatenate([f_low, seq - 1 - f_low])
    th_full = theta(order, np.arange(seq, dtype=np.int64))
    tp = np.arange(half, dtype=np.int64)
    th_e, th_o = theta(f_low, 2 * tp), theta(f_low, 2 * tp + 1)
    fwd = [np.cos(th_e), np.cos(th_o), np.sin(th_e), np.sin(th_o)]
    as_bf16 = lambda a: jnp.asarray(a, bf16)
    return (as_bf16(np.cos(th_full)), as_bf16(np.sin(th_full)),
            [as_bf16(a) for a in fwd], [as_bf16(a.T * (2.0 / n)) for a in fwd])


def _spectrum_kernel(c_ref, s_ref, ks_ref, kd_ref, kre_ref, kb_ref):
    kre_ref[...] = _dot(c_ref[...], ks_ref[...].astype(bf16))
    kb_ref[...] = _dot(s_ref[...], kd_ref[...].astype(bf16))


def filter_spectrum(cmat, smat, ks, kd, tmf=256, tc=256):
    seq = ks.shape[0]
    fsp = pl.BlockSpec((tmf, seq), lambda j, m: (m, 0))
    ksp = pl.BlockSpec((seq, tc), lambda j, m: (0, j))
    osp = pl.BlockSpec((tmf, tc), lambda j, m: (m, j))
    out = jax.ShapeDtypeStruct((seq, D_MODEL), f32)
    return pl.pallas_call(
        _spectrum_kernel,
        grid=(D_MODEL // tc, seq // tmf),
        in_specs=[fsp, fsp, ksp, ksp],
        out_specs=[osp, osp],
        out_shape=[out, out],
        compiler_params=_params("parallel", "parallel"),
        name="filter_spectrum",
    )(cmat, smat, ks, kd)


def _dft_fwd_kernel(ce_ref, co_ref, se_ref, so_ref, ve_ref, vo_ref, kre_ref, kb_ref,
                    pp_ref, pm_ref, qp_ref, qm_ref):
    tmf = pp_ref.shape[0]
    half = ce_ref.shape[0]
    low = pl.ds(pl.multiple_of(pl.program_id(1) * tmf, tmf), tmf)
    mirrored = pl.ds(pl.multiple_of(half + pl.program_id(1) * tmf, tmf), tmf)
    ve, vo = ve_ref[...], vo_ref[...]
    ec, oc = _dot(ce_ref[low, :], ve), _dot(co_ref[low, :], vo)
    es, os_ = _dot(se_ref[low, :], ve), _dot(so_ref[low, :], vo)

    def times_filter(vre, va, kre, kb):
        return vre * kre - va * kb, vre * kb + va * kre

    p, q = times_filter(ec + oc, es + os_, kre_ref[low, :], kb_ref[low, :])
    p_m, q_m = times_filter(ec - oc, os_ - es, kre_ref[mirrored, :], kb_ref[mirrored, :])
    pp_ref[...] = (p + p_m).astype(bf16)
    pm_ref[...] = (p - p_m).astype(bf16)
    qp_ref[...] = (q + q_m).astype(bf16)
    qm_ref[...] = (q - q_m).astype(bf16)


def dft_forward(fwd_tables, vg_even, vg_odd, kre, kb, seq, tmf=256):
    half = seq // 2
    nb = vg_even.shape[0] // half
    tmf = min(tmf, half)
    nm = half // tmf
    whole = lambda a: pl.BlockSpec(a.shape, lambda b, m: (0, 0))
    osp = pl.BlockSpec((tmf, D_MODEL), lambda b, m: (b * nm + m, 0))
    out = jax.ShapeDtypeStruct((nb * half, D_MODEL), bf16)
    return pl.pallas_call(
        _dft_fwd_kernel,
        grid=(nb, nm),
        in_specs=[whole(t) for t in fwd_tables]
        + [pl.BlockSpec((half, D_MODEL), lambda b, m: (b, 0)),
           pl.BlockSpec((half, D_MODEL), lambda b, m: (b, 0)),
           whole(kre), whole(kb)],
        out_specs=[osp] * 4,
        out_shape=[out] * 4,
        compiler_params=_params("parallel", "parallel"),
        name="dft_forward",
    )(*fwd_tables, vg_even, vg_odd, kre, kb)


def _dft_inv_kernel(gate_row, cet_ref, cot_ref, set_ref, sot_ref, pp_ref, pm_ref, qp_ref, qm_ref,
                    vge_ref, vgo_ref, x0e_ref, x0o_ref, bias_ref, w_ref, h_ref, mod_ref, o_ref, stage_ref):
    gate = mod_ref[0, gate_row:gate_row + 1, :]
    tmt = vge_ref.shape[0]
    nlane = D_MODEL // LANES
    def finish(y, parity, vg_ref, x0_ref):
        y = y + vg_ref[...].astype(f32) * bias_ref[...]
        upd = gate * _dot((y * x0_ref[...].astype(f32)).astype(bf16), w_ref[...])
        rows = pl.ds(parity, tmt, stride=2)
        for c in range(nlane):
            stage_ref.at[c][rows, :] = upd[:, c * LANES:(c + 1) * LANES]

    mine = pl.ds(pl.multiple_of(pl.program_id(1) * tmt, tmt), tmt)
    finish(_dot(cet_ref[mine, :], pp_ref[...]) + _dot(set_ref[mine, :], qm_ref[...]), 0, vge_ref, x0e_ref)
    finish(_dot(cot_ref[mine, :], pm_ref[...]) + _dot(sot_ref[mine, :], qp_ref[...]), 1, vgo_ref, x0o_ref)
    for c in range(nlane):
        o_ref[:, c * LANES:(c + 1) * LANES] = h_ref[:, c * LANES:(c + 1) * LANES] + stage_ref[c]


def dft_inverse_out(inv_tables, folded, vg_eo, x0_eo, bias, w_out, h, mods, seq, gate_row, tmt=256):
    half = seq // 2
    nb = h.shape[0] // seq
    tmt = min(tmt, half)
    nm = half // tmt
    gsp = pl.BlockSpec((half, half), lambda b, m: (0, 0))
    full = pl.BlockSpec((half, D_MODEL), lambda b, m: (b, 0))
    part = pl.BlockSpec((tmt, D_MODEL), lambda b, m: (b * nm + m, 0))
    tile = pl.BlockSpec((2 * tmt, D_MODEL), lambda b, m: (b * nm + m, 0))
    return pl.pallas_call(
        functools.partial(_dft_inv_kernel, gate_row),
        grid=(nb, nm),
        in_specs=[gsp, gsp, gsp, gsp, full, full, full, full, part, part, part, part,
                  pl.BlockSpec((1, D_MODEL), lambda b, m: (0, 0)),
                  pl.BlockSpec((D_MODEL, D_MODEL), lambda b, m: (0, 0)),
                  tile, pl.BlockSpec((1, N_MOD, D_MODEL), lambda b, m: (b, 0, 0))],
        out_specs=tile,
        out_shape=jax.ShapeDtypeStruct(h.shape, f32),
        scratch_shapes=[pltpu.VMEM((D_MODEL // LANES, 2 * tmt, LANES), f32)],
        compiler_params=_params("parallel", "parallel"),
        name="dft_inverse_out",
    )(*inv_tables, *folded, *vg_eo, *x0_eo, bias.reshape(1, -1).astype(f32), w_out, h, mods)


_Q_HEAD_ORDER = (0, 4, 1, 5, 2, 6, 3, 7)


def _mixer_in_weight(w_in):
    c = np.cumsum([ATTN_DIM, KV_DIM, KV_DIM, D_SSM, XBC_DIM])
    q, k, v, z, xbc, dt = (w_in[:, :c[0]], w_in[:, c[0]:c[1]], w_in[:, c[1]:c[2]], w_in[:, c[2]:c[3]],
                           w_in[:, c[3]:c[4]], w_in[:, c[4]:])
    q = q.reshape(D_MODEL, N_Q_HEADS, HEAD_DIM)[:, np.array(_Q_HEAD_ORDER), :].reshape(D_MODEL, ATTN_DIM)
    w = jnp.concatenate([xbc, z, q, k, v], axis=1).astype(bf16)
    return w, jnp.pad(dt, ((0, 0), (0, LANES - dt.shape[1]))).astype(bf16)


def _mixer_out_weights(w_out):
    wa = w_out[:ATTN_DIM].reshape(N_Q_HEADS, HEAD_DIM, D_MODEL)[np.array(_Q_HEAD_ORDER)].reshape(ATTN_DIM, D_MODEL)
    return wa.astype(bf16), w_out[ATTN_DIM:].astype(bf16)


def kernel(x, c, ctx, c_ctx, w_ada, b_ada, norm_g, ffn_w13, ffn_w2, mix_w_in, mix_w_out, q_norm, k_norm, attn_sink,
           ssd_conv_w, ssd_conv_b, ssd_dt_bias, ssd_a_log, ssd_d, ssd_norm, hy_w_in, hy_conv_w, hy_conv_b,
           hy_f_w1, hy_f_b1, hy_f_wh, hy_f_bh, hy_f_wout, hy_freq, hy_bias, hy_w_out):
    nb, seq, _ = x.shape
    ctx_len = ctx.shape[1]
    depth = w_ada.shape[0]
    assert depth == 2, "this kernel is written for the two-layer block"
    h_lat = x.reshape(nb * seq, D_MODEL)
    h_ctx = ctx.reshape(nb * ctx_len, D_MODEL)
    cond = jnp.concatenate([c, c_ctx[None]], axis=0)
    cond = jnp.pad(cond, ((0, -cond.shape[0] % 8), (0, 0)))
    ffn_w = [[_ffn_weights(ffn_w13[i, k], ffn_w2[i, k]) for k in range(2)] for i in range(depth)]

    mods = ada_mods(cond, w_ada[0], b_ada[0])
    m_lat, m_ctx = mods[:nb], mods[nb:nb + 1]
    h_lat = macaron_ffn(h_lat, m_lat, seq, norm_g[0, 0], 0, *ffn_w[0][0])
    h_ctx = macaron_ffn(h_ctx, m_ctx, nb * ctx_len, norm_g[0, 0], 0, *ffn_w[0][0])
    w_in, w_dt = _mixer_in_weight(mix_w_in[0])
    wa, ws = _mixer_out_weights(mix_w_out[0])
    xc_lat, p_lat, dt_lat = mixer_proj(h_lat, m_lat, seq, seq, norm_g[0, 1], w_in, w_dt, ssd_conv_w[0], ssd_conv_b[0])
    xc_ctx, p_ctx, dt_ctx = mixer_proj(h_ctx, m_ctx, nb * ctx_len, ctx_len, norm_g[0, 1], w_in, w_dt,
                                       ssd_conv_w[0], ssd_conv_b[0])
    cos, sin = _rope_tables(seq)
    a_lat = window_attention(p_lat, p_ctx, seq, ctx_len, q_norm[0], k_norm[0], attn_sink[0], cos, sin)
    hf0, hb0 = ssd_ctx_states(xc_ctx, dt_ctx, ctx_len, ssd_dt_bias[0], ssd_a_log[0])
    s_lat = ssd_latent(xc_lat, p_lat, dt_lat, seq, hf0, hb0, ssd_dt_bias[0], ssd_a_log[0], ssd_d[0], ssd_norm[0])
    h_lat = macaron_ffn(h_lat, m_lat, seq, norm_g[0, 2], 2, *ffn_w[0][1], mix=((a_lat, wa), (s_lat, ws)))

    m_lat = ada_mods(cond, w_ada[1], b_ada[1])[:nb]
    h_lat = macaron_ffn(h_lat, m_lat, seq, norm_g[1, 0], 0, *ffn_w[1][0])
    x0e, x0o, vge, vgo = hyena_proj(h_lat, m_lat, seq, seq, norm_g[1, 1], hy_w_in[0].astype(bf16),
                                    hy_conv_w[0], hy_conv_b[0])
    ks, kd = hyena_filter_taps(seq, hy_f_w1[0], hy_f_b1[0], hy_f_wh[0], hy_f_bh[0], hy_f_wout[0], hy_freq[0])
    c_full, s_full, fwd_tables, inv_tables = _dft_tables(seq)
    kre, kb = filter_spectrum(c_full, s_full, ks, kd)
    folded = dft_forward(fwd_tables, vge, vgo, kre, kb, seq)
    h_lat = dft_inverse_out(inv_tables, folded, (vge, vgo), (x0e, x0o), hy_bias[0], hy_w_out[0].astype(bf16),
                            h_lat, m_lat, seq, 5)
    h_lat = macaron_ffn(h_lat, m_lat, seq, norm_g[1, 2], 2, *ffn_w[1][1])
    return h_lat.reshape(nb, seq, D_MODEL)
```
